```python
import jax, jax.numpy as jnp
from jax import lax
import numpy as np

D_MODEL = 1024
BATCH = 8
SEQ = 4096
DEPTH = 2
DEC_BATCH = 128
DEC_SEQ = 8
PAST_LEN = 16384
PAGE_SIZE = 128

N_MIXERS = 2
N_ATTN_LAYERS = (DEPTH + 1) // 2
N_CONV_LAYERS = DEPTH // 2
HEAD_DIM = 64
N_HEADS = D_MODEL // HEAD_DIM
N_KV_HEADS = 4
GROUP = N_HEADS // N_KV_HEADS
QKV_DIM = (N_HEADS + 2 * N_KV_HEADS) * HEAD_DIM
ROT_DIM = HEAD_DIM // 4
ROPE_THETA = 500000.0
WINDOW = 128
ATTN_BLOCK = 128
ATTN_SCALE = HEAD_DIM ** -0.5
META_LEN = 16
CONV_WIDTH = 31
CONV_CTX = CONV_WIDTH - 1
N_EXPERTS = 32
TOP_K = 4
MOE_FF = D_MODEL
SWIGLU_LIMIT = 7.0
SWIGLU_ALPHA = 1.702
MOE_BLOCK = 256
LN_EPS = 1e-5
DEEPNORM_ALPHA = (2 * DEPTH) ** 0.25
DEEPNORM_BETA = (8 * DEPTH) ** -0.25
NEG_INF = -1e30

kernel_name = 'hybrid_swa_sink_conformer_moe_decode_step'


def layer_norm(x, g, b):
    xf = x.astype(jnp.float32)
    mu = xf.mean(-1, keepdims=True)
    var = jnp.square(xf - mu).mean(-1, keepdims=True)
    y = (xf - mu) * lax.rsqrt(var + LN_EPS) * g.astype(jnp.float32) + b.astype(jnp.float32)
    return y.astype(x.dtype)


def apply_rope(x, pos):
    inv = 1.0 / (ROPE_THETA ** (jnp.arange(0, ROT_DIM, 2, dtype=jnp.float32) / ROT_DIM))
    ang = pos.astype(jnp.float32)[:, None] * inv[None, :]
    cos, sin = jnp.cos(ang)[:, None, :], jnp.sin(ang)[:, None, :]
    xr = x[..., :ROT_DIM].astype(jnp.float32)
    x1, x2 = xr[..., :ROT_DIM // 2], xr[..., ROT_DIM // 2:]
    rot = jnp.concatenate([x1 * cos - x2 * sin, x2 * cos + x1 * sin], -1).astype(x.dtype)
    return jnp.concatenate([rot, x[..., ROT_DIM:]], -1)


def qkv_proj(h, w_qkv, b_qkv, pos):
    n, t = h.shape[0], h.shape[1]
    qkv = h @ w_qkv + b_qkv
    q = qkv[..., :N_HEADS * HEAD_DIM].reshape(n, t, N_HEADS, HEAD_DIM)
    k = qkv[..., N_HEADS * HEAD_DIM:(N_HEADS + N_KV_HEADS) * HEAD_DIM].reshape(n, t, N_KV_HEADS, HEAD_DIM)
    v = qkv[..., (N_HEADS + N_KV_HEADS) * HEAD_DIM:].reshape(n, t, N_KV_HEADS, HEAD_DIM)
    q = apply_rope(q, pos).reshape(n, t, N_KV_HEADS, GROUP, HEAD_DIM)
    k = apply_rope(k, pos)
    return q, k, v


def sink_softmax(s, sink):
    m = jnp.maximum(s.max(-1, keepdims=True), sink)
    p = jnp.exp(s - m)
    return p / (p.sum(-1, keepdims=True) + jnp.exp(sink - m))


def attend(q, k, v, mask, sinks):
    s = jnp.einsum('...qhgd,...khd->...hgqk', q, k).astype(jnp.float32) * ATTN_SCALE
    s = jnp.where(mask[..., None, None, :, :], s, NEG_INF)
    p = sink_softmax(s, sinks.astype(jnp.float32).reshape(N_KV_HEADS, GROUP, 1, 1))
    return jnp.einsum('...hgqk,...khd->...qhgd', p.astype(v.dtype), v)


def swa_prompt(h, w_qkv, b_qkv, sinks, w_o, b_o):
    n, length, _ = h.shape
    seq = length - META_LEN
    n_blk = seq // ATTN_BLOCK
    q, k, v = qkv_proj(h, w_qkv, b_qkv, jnp.arange(length))
    qm, km, vm = q[:, :META_LEN], k[:, :META_LEN], v[:, :META_LEN]
    o_meta = attend(qm, km, vm, jnp.tril(jnp.ones((META_LEN, META_LEN), bool)), sinks)

    def blocks(a):
        return a[:, META_LEN:].reshape((n, n_blk, ATTN_BLOCK) + a.shape[2:])

    def band(own, meta):
        prev = jnp.concatenate([jnp.zeros_like(own[:, :1]), own[:, :-1]], axis=1)
        meta_b = jnp.broadcast_to(meta[:, None], (n, n_blk) + meta.shape[1:])
        return jnp.concatenate([meta_b, prev, own], axis=2)

    r = jnp.arange(ATTN_BLOCK)[:, None]
    c = jnp.arange(ATTN_BLOCK)[None, :]
    m_meta = jnp.ones((n_blk, ATTN_BLOCK, META_LEN), bool)
    m_prev = (ATTN_BLOCK + r - c <= WINDOW)[None] & (jnp.arange(n_blk) > 0)[:, None, None]
    m_own = jnp.broadcast_to(c <= r, (n_blk, ATTN_BLOCK, ATTN_BLOCK))
    mask = jnp.concatenate([m_meta, m_prev, m_own], -1)
    o_real = attend(blocks(q), band(blocks(k), km), band(blocks(v), vm), mask, sinks)
    o = jnp.concatenate([o_meta.reshape(n, META_LEN, N_HEADS * HEAD_DIM),
                         o_real.reshape(n, seq, N_HEADS * HEAD_DIM)], axis=1)
    y = o @ w_o + b_o
    return y, km, vm, k[:, -WINDOW:], v[:, -WINDOW:]


def swa_sample(h, meta_k, meta_v, win_k, win_v, w_qkv, b_qkv, sinks, w_o, b_o):
    n, t, _ = h.shape
    w_len = win_k.shape[1]
    pos = PAST_LEN + jnp.arange(t)
    q, k, v = qkv_proj(h, w_qkv, b_qkv, pos)
    k_all = jnp.concatenate([meta_k, win_k, k], axis=1)
    v_all = jnp.concatenate([meta_v, win_v, v], axis=1)
    buf_pos = PAST_LEN - w_len + jnp.arange(w_len)
    m_buf = (pos[:, None] - buf_pos[None, :] <= WINDOW) & (buf_pos >= META_LEN)[None, :]
    m_meta = jnp.ones((t, META_LEN), bool)
    m_new = jnp.tril(jnp.ones((t, t), bool))
    mask = jnp.concatenate([m_meta, m_buf, m_new], -1)
    o = attend(q, k_all, v_all, mask, sinks).reshape(n, t, N_HEADS * HEAD_DIM)
    y = o @ w_o + b_o
    new_k = jnp.concatenate([win_k, k], axis=1)[:, -w_len:]
    new_v = jnp.concatenate([win_v, v], axis=1)[:, -w_len:]
    return y, new_k, new_v


def conv_module(h, ctx, w_pw1, b_pw1, w_dw, b_dw, ln_g, ln_b, w_pw2, b_pw2):
    a = h @ w_pw1 + b_pw1
    u = a[..., :D_MODEL] * jax.nn.sigmoid(a[..., D_MODEL:])
    full = jnp.concatenate([ctx, u], axis=1)
    d = lax.conv_general_dilated(full, w_dw[:, None, :], window_strides=(1,), padding='VALID',
                                 dimension_numbers=('NWC', 'WIO', 'NWC'),
                                 feature_group_count=D_MODEL) + b_dw
    z = jax.nn.silu(layer_norm(d, ln_g, ln_b))
    y = z @ w_pw2 + b_pw2
    return y, full[:, -CONV_CTX:]


def expert_ffn(xb, w_gu, b_gu, w_dn, b_dn):
    gu = xb @ w_gu + b_gu
    gate = jnp.minimum(gu[..., :MOE_FF], SWIGLU_LIMIT)
    up = jnp.clip(gu[..., MOE_FF:], -SWIGLU_LIMIT, SWIGLU_LIMIT)
    glu = gate * jax.nn.sigmoid(SWIGLU_ALPHA * gate)
    return ((up + 1.0) * glu) @ w_dn + b_dn


def moe(x2d, w_r, b_r, w_gu, b_gu, w_dn, b_dn):
    n_tok, d = x2d.shape
    logits = (x2d @ w_r + b_r).astype(jnp.float32)
    top_val, top_idx = lax.top_k(logits, TOP_K)
    gates = jax.nn.softmax(top_val, axis=-1)
    n_assign = n_tok * TOP_K
    flat_e = top_idx.reshape(-1)
    order = jnp.argsort(flat_e)
    sorted_e = flat_e[order]
    counts = jnp.bincount(flat_e, length=N_EXPERTS)
    padded = ((counts + MOE_BLOCK - 1) // MOE_BLOCK) * MOE_BLOCK
    start = jnp.cumsum(counts) - counts
    pstart = jnp.cumsum(padded) - padded
    dest = pstart[sorted_e] + (jnp.arange(n_assign) - start[sorted_e])
    n_blocks = -(-n_assign // MOE_BLOCK) + N_EXPERTS
    n_rows = n_blocks * MOE_BLOCK
    row_token = jnp.full((n_rows,), n_tok, jnp.int32).at[dest].set((order // TOP_K).astype(jnp.int32))
    block_expert = jnp.minimum(
        jnp.searchsorted(jnp.cumsum(padded), jnp.arange(n_blocks) * MOE_BLOCK, side='right'),
        N_EXPERTS - 1)
    x_pad = jnp.concatenate([x2d, jnp.zeros((1, d), x2d.dtype)], axis=0)

    def run_block(args):
        tok, e = args
        return expert_ffn(x_pad[tok], w_gu[e], b_gu[e], w_dn[e], b_dn[e])

    y_rows = lax.map(run_block, (row_token.reshape(n_blocks, MOE_BLOCK), block_expert))
    y_sorted = y_rows.reshape(n_rows, d)[dest]
    y_assign = jnp.zeros_like(y_sorted).at[order].set(y_sorted).reshape(n_tok, TOP_K, d)
    return jnp.einsum('tk,tkd->td', gates.astype(x2d.dtype), y_assign)


def setup_inputs(seed: int = 0) -> dict:
    key = jax.random.key(seed)
    ks = jax.random.split(key, 32)

    def nrm(k, shape, scale):
        return scale * jax.random.normal(k, shape, jnp.float32)

    win_len = min(WINDOW, PAST_LEN)
    return {
        'x_prompt': nrm(ks[0], (BATCH, SEQ, D_MODEL), 1.0),
        'x_sample': nrm(ks[1], (DEC_BATCH, DEC_SEQ, D_MODEL), 1.0),
        'cache_attn_meta_k': nrm(ks[2], (N_ATTN_LAYERS, DEC_BATCH, META_LEN, N_KV_HEADS, HEAD_DIM), 1.0),
        'cache_attn_meta_v': nrm(ks[3], (N_ATTN_LAYERS, DEC_BATCH, META_LEN, N_KV_HEADS, HEAD_DIM), 1.0),
        'cache_attn_win_k': nrm(ks[4], (N_ATTN_LAYERS, DEC_BATCH, win_len, N_KV_HEADS, HEAD_DIM), 1.0),
        'cache_attn_win_v': nrm(ks[5], (N_ATTN_LAYERS, DEC_BATCH, win_len, N_KV_HEADS, HEAD_DIM), 1.0),
        'state_conv': nrm(ks[6], (N_CONV_LAYERS, DEC_BATCH, CONV_CTX, D_MODEL), 0.5),
        'meta_tokens': nrm(ks[7], (META_LEN, D_MODEL), 1.0),
        'attn_w_qkv': nrm(ks[8], (N_ATTN_LAYERS, D_MODEL, QKV_DIM), D_MODEL ** -0.5),
        'attn_b_qkv': nrm(ks[9], (N_ATTN_LAYERS, QKV_DIM), 0.01),
        'attn_sinks': nrm(ks[10], (N_ATTN_LAYERS, N_HEADS), 0.5),
        'attn_w_o': nrm(ks[11], (N_ATTN_LAYERS, N_HEADS * HEAD_DIM, D_MODEL), (N_HEADS * HEAD_DIM) ** -0.5 * DEEPNORM_BETA),
        'attn_b_o': nrm(ks[12], (N_ATTN_LAYERS, D_MODEL), 0.01),
        'conv_w_pw1': nrm(ks[13], (N_CONV_LAYERS, D_MODEL, 2 * D_MODEL), D_MODEL ** -0.5),
        'conv_b_pw1': nrm(ks[14], (N_CONV_LAYERS, 2 * D_MODEL), 0.01),
        'conv_w_dw': nrm(ks[15], (N_CONV_LAYERS, CONV_WIDTH, D_MODEL), CONV_WIDTH ** -0.5),
        'conv_b_dw': nrm(ks[16], (N_CONV_LAYERS, D_MODEL), 0.01),
        'conv_ln_g': 1.0 + nrm(ks[17], (N_CONV_LAYERS, D_MODEL), 0.05),
        'conv_ln_b': nrm(ks[18], (N_CONV_LAYERS, D_MODEL), 0.01),
        'conv_w_pw2': nrm(ks[19], (N_CONV_LAYERS, D_MODEL, D_MODEL), D_MODEL ** -0.5 * DEEPNORM_BETA),
        'conv_b_pw2': nrm(ks[20], (N_CONV_LAYERS, D_MODEL), 0.01),
        'ln_mix_g': 1.0 + nrm(ks[21], (DEPTH, D_MODEL), 0.05),
        'ln_mix_b': nrm(ks[22], (DEPTH, D_MODEL), 0.01),
        'ln_ffn_g': 1.0 + nrm(ks[23], (DEPTH, D_MODEL), 0.05),
        'ln_ffn_b': nrm(ks[24], (DEPTH, D_MODEL), 0.01),
        'moe_w_router': nrm(ks[25], (DEPTH, D_MODEL, N_EXPERTS), D_MODEL ** -0.5),
        'moe_b_router': nrm(ks[26], (DEPTH, N_EXPERTS), 0.01),
        'moe_w_gate_up': nrm(ks[27], (DEPTH, N_EXPERTS, D_MODEL, 2 * MOE_FF), D_MODEL ** -0.5),
        'moe_b_gate_up': nrm(ks[28], (DEPTH, N_EXPERTS, 2 * MOE_FF), 0.01),
        'moe_w_down': nrm(ks[29], (DEPTH, N_EXPERTS, MOE_FF, D_MODEL), MOE_FF ** -0.5 * DEEPNORM_BETA),
        'moe_b_down': nrm(ks[30], (DEPTH, N_EXPERTS, D_MODEL), 0.01),
    }


def reference(x_prompt, x_sample, cache_attn_meta_k, cache_attn_meta_v, cache_attn_win_k,
              cache_attn_win_v, state_conv, meta_tokens, attn_w_qkv, attn_b_qkv, attn_sinks,
              attn_w_o, attn_b_o, conv_w_pw1, conv_b_pw1, conv_w_dw, conv_b_dw, conv_ln_g,
              conv_ln_b, conv_w_pw2, conv_b_pw2, ln_mix_g, ln_mix_b, ln_ffn_g, ln_ffn_b,
              moe_w_router, moe_b_router, moe_w_gate_up, moe_b_gate_up, moe_w_down, moe_b_down):
    n_prompt = x_prompt.shape[0]
    meta = jnp.broadcast_to(meta_tokens[None].astype(x_prompt.dtype), (n_prompt, META_LEN, D_MODEL))
    xp = jnp.concatenate([meta, x_prompt], axis=1)
    xs = x_sample
    p_meta_k, p_meta_v, p_win_k, p_win_v, p_conv = [], [], [], [], []
    s_win_k, s_win_v, s_conv = [], [], []
    for i in range(DEPTH):
        j = i // N_MIXERS
        if i % N_MIXERS == 0:
            mix_p, mk, mv, wk, wv = swa_prompt(xp, attn_w_qkv[j], attn_b_qkv[j], attn_sinks[j],
                                               attn_w_o[j], attn_b_o[j])
            mix_s, swk, swv = swa_sample(xs, cache_attn_meta_k[j], cache_attn_meta_v[j],
                                         cache_attn_win_k[j], cache_attn_win_v[j],
                                         attn_w_qkv[j], attn_b_qkv[j], attn_sinks[j],
                                         attn_w_o[j], attn_b_o[j])
            p_meta_k.append(mk); p_meta_v.append(mv); p_win_k.append(wk); p_win_v.append(wv)
            s_win_k.append(swk); s_win_v.append(swv)
        else:
            conv_args = (conv_w_pw1[j], conv_b_pw1[j], conv_w_dw[j], conv_b_dw[j], conv_ln_g[j],
                         conv_ln_b[j], conv_w_pw2[j], conv_b_pw2[j])
            mix_p, cp = conv_module(xp, jnp.zeros((n_prompt, CONV_CTX, D_MODEL), xp.dtype), *conv_args)
            mix_s, cs = conv_module(xs, state_conv[j], *conv_args)
            p_conv.append(cp); s_conv.append(cs)
        xp = layer_norm(DEEPNORM_ALPHA * xp + mix_p, ln_mix_g[i], ln_mix_b[i])
        xs = layer_norm(DEEPNORM_ALPHA * xs + mix_s, ln_mix_g[i], ln_mix_b[i])
        n_p = xp.shape[0] * xp.shape[1]
        ff = moe(jnp.concatenate([xp.reshape(n_p, D_MODEL), xs.reshape(-1, D_MODEL)], axis=0),
                 moe_w_router[i], moe_b_router[i], moe_w_gate_up[i], moe_b_gate_up[i],
                 moe_w_down[i], moe_b_down[i])
        xp = layer_norm(DEEPNORM_ALPHA * xp + ff[:n_p].reshape(xp.shape), ln_ffn_g[i], ln_ffn_b[i])
        xs = layer_norm(DEEPNORM_ALPHA * xs + ff[n_p:].reshape(xs.shape), ln_ffn_g[i], ln_ffn_b[i])
    return (xp[:, META_LEN:], xs, jnp.stack(p_meta_k), jnp.stack(p_meta_v), jnp.stack(p_win_k),
            jnp.stack(p_win_v), jnp.stack(p_conv), jnp.stack(s_win_k), jnp.stack(s_win_v),
            jnp.stack(s_conv))
```

```python
import functools

import jax
import jax.numpy as jnp
from jax import lax
from jax.experimental import pallas as pl
from jax.experimental.pallas import tpu as pltpu

F32 = jnp.float32
BF16 = jnp.bfloat16

D_MODEL = 1024
HEAD_DIM = 64
N_HEADS = 16
N_KV_HEADS = 4
GROUP = N_HEADS // N_KV_HEADS
KV_DIM = N_KV_HEADS * HEAD_DIM
QKV_DIM = D_MODEL + 2 * KV_DIM
ROT_DIM = 16
ROPE_THETA = 500000.0
WINDOW = 128
ATTN_BLOCK = 128
ATTN_SCALE = HEAD_DIM ** -0.5
META_LEN = 16
CONV_WIDTH = 31
CONV_CTX = CONV_WIDTH - 1
N_EXPERTS = 32
TOP_K = 4
MOE_FF = 1024
SWIGLU_LIMIT = 7.0
SWIGLU_ALPHA = 1.702
LN_EPS = 1e-5
DEPTH = 2
DEEPNORM_ALPHA = (2 * DEPTH) ** 0.25
PAST_LEN = 16384
NEG_INF = -1e30

LANES = 128
SUBLANES = 8
GMM_TILE = 256
CONV_HALO = 32
CONV_BLOCK = 32
N_SLAB = D_MODEL // LANES
VMEM_LIMIT = 56 * 1024 * 1024


def _cparams(n_axes):
    return pltpu.CompilerParams(dimension_semantics=("arbitrary",) * n_axes,
                                vmem_limit_bytes=VMEM_LIMIT)


def _pick_tile(n, cap):
    best = None
    t = LANES
    while t <= cap:
        if n % t == 0:
            best = t
        t += LANES
    assert best is not None, n
    return best


def _layer_norm(h, g, b):
    mu = jnp.mean(h, axis=-1, keepdims=True)
    hc = h - mu
    var = jnp.mean(hc * hc, axis=-1, keepdims=True)
    return hc * lax.rsqrt(var + LN_EPS) * g + b


def _qkv_kernel(x_ref, w_ref, b_ref, c_ref, a_ref, s_ref, q_ref, k_ref, v_ref):
    x = x_ref[...].astype(BF16)
    acc = jnp.dot(x, w_ref[...], preferred_element_type=F32) + b_ref[...]
    c = c_ref[...]
    a = a_ref[...]
    s = s_ref[...]

    def rope(t):
        return t * c + pltpu.roll(t, LANES - ROT_DIM // 2, 1) * a + pltpu.roll(t, ROT_DIM // 2, 1) * s

    for j in range(D_MODEL // LANES):
        sl = slice(j * LANES, (j + 1) * LANES)
        q_ref[:, sl] = (rope(acc[:, sl]) * ATTN_SCALE).astype(BF16)
    for j in range(KV_DIM // LANES):
        sl = slice(D_MODEL + j * LANES, D_MODEL + (j + 1) * LANES)
        k_ref[:, j * LANES:(j + 1) * LANES] = rope(acc[:, sl])
    v_ref[...] = acc[:, D_MODEL + KV_DIM:]


def _qkv_rope(x, w_bf, b, cos_t, sa_t, sb_t, tm):
    t = x.shape[0]
    row = lambda i: (i, 0)
    fixed = lambda i: (0, 0)
    return pl.pallas_call(
        _qkv_kernel,
        grid=(t // tm,),
        in_specs=[pl.BlockSpec((tm, D_MODEL), row),
                  pl.BlockSpec((D_MODEL, QKV_DIM), fixed),
                  pl.BlockSpec((1, QKV_DIM), fixed),
                  pl.BlockSpec((tm, LANES), row),
                  pl.BlockSpec((tm, LANES), row),
                  pl.BlockSpec((tm, LANES), row)],
        out_specs=[pl.BlockSpec((tm, D_MODEL), row),
                   pl.BlockSpec((tm, KV_DIM), row),
                   pl.BlockSpec((tm, KV_DIM), row)],
        out_shape=[jax.ShapeDtypeStruct((t, D_MODEL), BF16),
                   jax.ShapeDtypeStruct((t, KV_DIM), F32),
                   jax.ShapeDtypeStruct((t, KV_DIM), F32)],
        compiler_params=_cparams(1),
        name="qkv_rope",
    )(x, w_bf, b, cos_t, sa_t, sb_t)


def _attend(q, kcat, vcat, mask, sink_ref, write):
    for h in range(N_HEADS):
        g = h // GROUP
        qh = q[:, h * HEAD_DIM:(h + 1) * HEAD_DIM]
        kg = kcat[:, g * HEAD_DIM:(g + 1) * HEAD_DIM]
        vg = vcat[:, g * HEAD_DIM:(g + 1) * HEAD_DIM]
        s = lax.dot_general(qh, kg, (((1,), (1,)), ((), ())), preferred_element_type=F32)
        s = jnp.where(mask, s, NEG_INF)
        sink = sink_ref[h]
        m = jnp.maximum(jnp.max(s, axis=-1, keepdims=True), sink)
        p = jnp.exp(s - m)
        denom = jnp.sum(p, axis=-1, keepdims=True) + jnp.exp(sink - m)
        o = jnp.dot(p.astype(BF16), vg, preferred_element_type=F32)
        write(h, o / denom)


def _attn_prompt_kernel(sink_ref, q_ref, km_ref, kp_ref, ko_ref, vm_ref, vp_ref, vo_ref, o_ref):
    i = pl.program_id(1)
    kcat = jnp.concatenate([km_ref[...], kp_ref[...], ko_ref[...]], axis=0).astype(BF16)
    vcat = jnp.concatenate([vm_ref[...], vp_ref[...], vo_ref[...]], axis=0).astype(BF16)
    nk = META_LEN + 2 * ATTN_BLOCK
    r = lax.broadcasted_iota(jnp.int32, (ATTN_BLOCK, nk), 0)
    c = lax.broadcasted_iota(jnp.int32, (ATTN_BLOCK, nk), 1)
    first = jnp.where(i > 0, 0, 2 * ATTN_BLOCK)
    prev_ok = (c >= META_LEN + first) & (c < META_LEN + ATTN_BLOCK) & (c - META_LEN >= r)
    own_ok = (c >= META_LEN + ATTN_BLOCK) & (c - (META_LEN + ATTN_BLOCK) <= r)
    mask = (c < META_LEN) | prev_ok | own_ok

    def write(h, o):
        o_ref[:, h * HEAD_DIM:(h + 1) * HEAD_DIM] = o.astype(o_ref.dtype)

    _attend(q_ref[...], kcat, vcat, mask, sink_ref, write)


def _attn_prompt(q, k, v, sinks, n_batch, seq):
    nblk = seq // ATTN_BLOCK
    meta_blk0 = n_batch * seq // META_LEN
    qmap = lambda b, i: (b * nblk + i, 0)
    pmap = lambda b, i: (b * nblk + jnp.maximum(i - 1, 0), 0)
    mmap = lambda b, i: (meta_blk0 + b, 0)
    kspec = lambda m: pl.BlockSpec((ATTN_BLOCK, KV_DIM), m)
    mspec = pl.BlockSpec((META_LEN, KV_DIM), mmap)
    return pl.pallas_call(
        _attn_prompt_kernel,
        grid=(n_batch, nblk),
        in_specs=[pl.BlockSpec(memory_space=pltpu.SMEM),
                  pl.BlockSpec((ATTN_BLOCK, D_MODEL), qmap),
                  mspec, kspec(pmap), kspec(qmap),
                  mspec, kspec(pmap), kspec(qmap)],
        out_specs=pl.BlockSpec((ATTN_BLOCK, D_MODEL), qmap),
        out_shape=jax.ShapeDtypeStruct((n_batch * seq, D_MODEL), BF16),
        compiler_params=_cparams(2),
        name="attn_prompt",
    )(sinks, q, k, k, k, v, v, v)


def _attn_meta_kernel(sink_ref, q_ref, k_ref, v_ref, o_ref):
    r = lax.broadcasted_iota(jnp.int32, (META_LEN, META_LEN), 0)
    c = lax.broadcasted_iota(jnp.int32, (META_LEN, META_LEN), 1)

    def write(h, o):
        o_ref[:, h * HEAD_DIM:(h + 1) * HEAD_DIM] = o.astype(o_ref.dtype)

    _attend(q_ref[...], k_ref[...].astype(BF16), v_ref[...].astype(BF16), c <= r, sink_ref, write)


def _attn_meta(q, k, v, sinks, n_batch, seq):
    blk0 = n_batch * seq // META_LEN
    imap = lambda b: (blk0 + b, 0)
    return pl.pallas_call(
        _attn_meta_kernel,
        grid=(n_batch,),
        in_specs=[pl.BlockSpec(memory_space=pltpu.SMEM),
                  pl.BlockSpec((META_LEN, D_MODEL), imap),
                  pl.BlockSpec((META_LEN, KV_DIM), imap),
                  pl.BlockSpec((META_LEN, KV_DIM), imap)],
        out_specs=pl.BlockSpec((META_LEN, D_MODEL), lambda b: (b, 0)),
        out_shape=jax.ShapeDtypeStruct((n_batch * META_LEN, D_MODEL), BF16),
        compiler_params=_cparams(1),
        name="attn_meta",
    )(sinks, q, k, v)


SAMPLE_GROUP = 2


def _attn_sample_kernel(sink_ref, q_ref, kn_ref, vn_ref, cmk_ref, cmv_ref, cwk_ref, cwv_ref,
                        o_ref, acc_ref, *, dec_seq):
    nq = SAMPLE_GROUP * dec_seq
    nk = META_LEN + WINDOW + nq
    q = q_ref[...]
    kn = kn_ref[...]
    vn = vn_ref[...]
    r = lax.broadcasted_iota(jnp.int32, (nq, nk), 0)
    c = lax.broadcasted_iota(jnp.int32, (nq, nk), 1)
    acc_ref[...] = jnp.zeros_like(acc_ref)

    def write(h, o):
        acc_ref[:, h * HEAD_DIM:(h + 1) * HEAD_DIM] += o

    for j in range(SAMPLE_GROUP):
        kcat = jnp.concatenate([cmk_ref[j], cwk_ref[j], kn], axis=0).astype(BF16)
        vcat = jnp.concatenate([cmv_ref[j], cwv_ref[j], vn], axis=0).astype(BF16)
        tq = r - j * dec_seq
        mine = (tq >= 0) & (tq < dec_seq)
        cn = c - (META_LEN + WINDOW) - j * dec_seq
        win_ok = (c >= META_LEN) & (c < META_LEN + WINDOW) & (c - META_LEN >= tq)
        new_ok = (cn >= 0) & (cn <= tq)
        vis = (c < META_LEN) | win_ok | new_ok
        _attend(q, kcat, vcat, mine & vis, sink_ref, write)
    o_ref[...] = acc_ref[...].astype(o_ref.dtype)


def _attn_sample(q, k, v, cmk, cmv, cwk, cwv, sinks, row0, dec_batch, dec_seq):
    nq = SAMPLE_GROUP * dec_seq
    blk0 = row0 // nq
    qmap = lambda n: (blk0 + n, 0)
    cmap = lambda n: (n, 0, 0)
    return pl.pallas_call(
        functools.partial(_attn_sample_kernel, dec_seq=dec_seq),
        grid=(dec_batch // SAMPLE_GROUP,),
        in_specs=[pl.BlockSpec(memory_space=pltpu.SMEM),
                  pl.BlockSpec((nq, D_MODEL), qmap),
                  pl.BlockSpec((nq, KV_DIM), qmap),
                  pl.BlockSpec((nq, KV_DIM), qmap),
                  pl.BlockSpec((SAMPLE_GROUP, META_LEN, KV_DIM), cmap),
                  pl.BlockSpec((SAMPLE_GROUP, META_LEN, KV_DIM), cmap),
                  pl.BlockSpec((SAMPLE_GROUP, WINDOW, KV_DIM), cmap),
                  pl.BlockSpec((SAMPLE_GROUP, WINDOW, KV_DIM), cmap)],
        out_specs=pl.BlockSpec((nq, D_MODEL), lambda n: (n, 0)),
        out_shape=jax.ShapeDtypeStruct((dec_batch * dec_seq, D_MODEL), BF16),
        scratch_shapes=[pltpu.VMEM((nq, D_MODEL), F32)],
        compiler_params=_cparams(1),
        name="attn_sample",
    )(sinks, q, k, v, cmk, cmv, cwk, cwv)


def _proj_ln_route_kernel(a_ref, w_ref, b_ref, x_ref, g_ref, bb_ref, wr_ref, br_ref, tri_ref,
                          x1_ref, ids_ref, gates_ref, rank_ref, cnt_ref, carry_ref):
    @pl.when(pl.program_id(0) == 0)
    def _():
        carry_ref[...] = jnp.zeros_like(carry_ref)

    y = jnp.dot(a_ref[...], w_ref[...], preferred_element_type=F32) + b_ref[...]
    x1 = _layer_norm(DEEPNORM_ALPHA * x_ref[...] + y, g_ref[...], bb_ref[...])
    x1_ref[...] = x1

    logits = lax.dot_general(wr_ref[...], x1.astype(BF16), (((1,), (1,)), ((), ())),
                             preferred_element_type=F32) + br_ref[...]
    tm = logits.shape[1]
    eidx = lax.broadcasted_iota(jnp.int32, (N_EXPERTS, tm), 0).astype(F32)
    cur = logits
    vals, idxs, sels = [], [], []
    for _ in range(TOP_K):
        m = jnp.max(cur, axis=0, keepdims=True)
        idx = jnp.min(jnp.where(cur == m, eidx, float(N_EXPERTS)), axis=0, keepdims=True)
        sel = eidx == idx
        vals.append(m)
        idxs.append(idx)
        sels.append(sel)
        cur = jnp.where(sel, -jnp.inf, cur)
    exps = [jnp.exp(v - vals[0]) for v in vals]
    tot = exps[0] + exps[1] + exps[2] + exps[3]
    gates_ref[...] = jnp.concatenate([e / tot for e in exps], axis=0)
    ids_ref[...] = jnp.concatenate(idxs, axis=0).astype(jnp.int32)

    chosen = jnp.where(sels[0] | sels[1] | sels[2] | sels[3], 1.0, 0.0)
    before = jnp.dot(chosen.astype(BF16), tri_ref[...], preferred_element_type=F32)
    before = before + carry_ref[:, 0:1]
    ranks = [jnp.sum(jnp.where(s, before, 0.0), axis=0, keepdims=True) for s in sels]
    rank_ref[...] = jnp.concatenate(ranks, axis=0).astype(jnp.int32)
    carry_ref[...] = carry_ref[...] + jnp.sum(chosen, axis=1, keepdims=True)
    cnt_ref[...] = carry_ref[...].astype(jnp.int32)


def _proj_ln_route(a_bf, w_bf, b, x, g, bb, wr_t_bf, br_col, tm):
    t = x.shape[0]
    row = lambda i: (i, 0)
    col = lambda i: (0, i)
    fixed = lambda i: (0, 0)
    tri = (jnp.arange(tm)[:, None] < jnp.arange(tm)[None, :]).astype(BF16)
    return pl.pallas_call(
        _proj_ln_route_kernel,
        grid=(t // tm,),
        in_specs=[pl.BlockSpec((tm, D_MODEL), row),
                  pl.BlockSpec((D_MODEL, D_MODEL), fixed),
                  pl.BlockSpec((1, D_MODEL), fixed),
                  pl.BlockSpec((tm, D_MODEL), row),
                  pl.BlockSpec((1, D_MODEL), fixed),
                  pl.BlockSpec((1, D_MODEL), fixed),
                  pl.BlockSpec((N_EXPERTS, D_MODEL), fixed),
                  pl.BlockSpec((N_EXPERTS, 1), fixed),
                  pl.BlockSpec((tm, tm), fixed)],
        out_specs=[pl.BlockSpec((tm, D_MODEL), row),
                   pl.BlockSpec((TOP_K, tm), col),
                   pl.BlockSpec((TOP_K, tm), col),
                   pl.BlockSpec((TOP_K, tm), col),
                   pl.BlockSpec((N_EXPERTS, LANES), fixed)],
        out_shape=[jax.ShapeDtypeStruct((t, D_MODEL), F32),
                   jax.ShapeDtypeStruct((TOP_K, t), jnp.int32),
                   jax.ShapeDtypeStruct((TOP_K, t), F32),
                   jax.ShapeDtypeStruct((TOP_K, t), jnp.int32),
                   jax.ShapeDtypeStruct((N_EXPERTS, LANES), jnp.int32)],
        scratch_shapes=[pltpu.VMEM((N_EXPERTS, LANES), F32)],
        compiler_params=_cparams(1),
        name="proj_ln_route",
    )(a_bf, w_bf, b, x, g, bb, wr_t_bf, br_col, tri)


def _idx_copy(dest_hbm, dsm, sem, step, stride):
    slot = step % 2
    return pltpu.make_async_copy(dest_hbm.at[pl.ds(pl.multiple_of(step * stride, stride), stride)],
                                 dsm.at[pl.ds(pl.multiple_of(slot * stride, stride), stride)],
                                 sem)


def _dispatch_kernel(pstart_ref, padded_ref, nact_ref, dest_hbm, x_ref, xs_hbm,
                     dsm, zbuf, sems, *, tm, stride, n_tiles):
    i = pl.program_id(0)
    nsteps = pl.num_programs(0)

    def zero_tile(row0):
        return pltpu.make_async_copy(zbuf, xs_hbm.at[pl.ds(pl.multiple_of(row0, GMM_TILE), GMM_TILE)],
                                     sems.at[2])

    @pl.when(i == 0)
    def _():
        zbuf[...] = jnp.zeros_like(zbuf)

        def each_expert(fn):
            def body(e, _):
                @pl.when(padded_ref[e] > 0)
                def _():
                    fn(zero_tile(pstart_ref[e] + padded_ref[e] - GMM_TILE))
                return 0
            lax.fori_loop(0, N_EXPERTS, body, 0)

        def each_tail(fn):
            def body(t, _):
                fn(zero_tile(t * GMM_TILE))
                return 0
            lax.fori_loop(nact_ref[0], n_tiles, body, 0)

        each_expert(lambda cp: cp.start())
        each_tail(lambda cp: cp.start())
        each_expert(lambda cp: cp.wait())
        each_tail(lambda cp: cp.wait())
        _idx_copy(dest_hbm, dsm, sems.at[0], i, stride).start()

    _idx_copy(dest_hbm, dsm, sems.at[0], i, stride).wait()

    @pl.when(i + 1 < nsteps)
    def _():
        _idx_copy(dest_hbm, dsm, sems.at[0], i + 1, stride).start()

    base = (i % 2) * stride

    def row_body(r, _):
        for k in range(TOP_K):
            d = dsm[base + k * tm + r]
            pltpu.make_async_copy(x_ref.at[pl.ds(r, 1)], xs_hbm.at[pl.ds(d, 1)], sems.at[1]).start()
        return 0

    lax.fori_loop(0, tm, row_body, 0)
    for k in range(TOP_K):
        pltpu.make_async_copy(x_ref, xs_hbm.at[pl.ds(0, tm)], sems.at[1]).wait()


def _dispatch(x, dest_steps, pstart, padded, nact, tm, stride, n_tiles):
    t = x.shape[0]
    grid_spec = pltpu.PrefetchScalarGridSpec(
        num_scalar_prefetch=3,
        grid=(t // tm,),
        in_specs=[pl.BlockSpec(memory_space=pl.ANY),
                  pl.BlockSpec((tm, D_MODEL), lambda i, *_: (i, 0))],
        out_specs=pl.BlockSpec(memory_space=pl.ANY),
        scratch_shapes=[pltpu.SMEM((2 * stride,), jnp.int32),
                        pltpu.VMEM((GMM_TILE, D_MODEL), F32),
                        pltpu.SemaphoreType.DMA((3,))],
    )
    return pl.pallas_call(
        functools.partial(_dispatch_kernel, tm=tm, stride=stride, n_tiles=n_tiles),
        grid_spec=grid_spec,
        out_shape=jax.ShapeDtypeStruct((n_tiles * GMM_TILE, D_MODEL), F32),
        compiler_params=_cparams(1),
        name="moe_dispatch",
    )(pstart, padded, nact, dest_steps, x)


def _gmm_kernel(te_ref, tsrc_ref, nact_ref, x_ref, wgu_ref, bgu_ref, wdn_ref, bdn_ref, y_ref):
    i = pl.program_id(0)

    @pl.when(i < nact_ref[0])
    def _():
        gu = jnp.dot(x_ref[...].astype(BF16), wgu_ref[...], preferred_element_type=F32) + bgu_ref[...]
        gate = jnp.minimum(gu[:, :MOE_FF], SWIGLU_LIMIT)
        up = jnp.clip(gu[:, MOE_FF:], -SWIGLU_LIMIT, SWIGLU_LIMIT)
        glu = gate * jax.nn.sigmoid(SWIGLU_ALPHA * gate)
        h = ((up + 1.0) * glu).astype(BF16)
        y_ref[...] = jnp.dot(h, wdn_ref[...], preferred_element_type=F32) + bdn_ref[...]

    @pl.when(i >= nact_ref[0])
    def _():
        y_ref[...] = jnp.zeros_like(y_ref)


def _gmm(xs, te, tsrc, nact, wgu_bf, bgu, wdn_bf, bdn, n_tiles):
    emap = lambda i, te, ts, na: (te[i], 0, 0)
    grid_spec = pltpu.PrefetchScalarGridSpec(
        num_scalar_prefetch=3,
        grid=(n_tiles,),
        in_specs=[pl.BlockSpec((GMM_TILE, D_MODEL), lambda i, te, ts, na: (ts[i], 0)),
                  pl.BlockSpec((None, D_MODEL, 2 * MOE_FF), emap),
                  pl.BlockSpec((None, 1, 2 * MOE_FF), emap),
                  pl.BlockSpec((None, MOE_FF, D_MODEL), emap),
                  pl.BlockSpec((None, 1, D_MODEL), emap)],
        out_specs=pl.BlockSpec((GMM_TILE, D_MODEL), lambda i, *_: (i, 0)),
    )
    return pl.pallas_call(
        _gmm_kernel,
        grid_spec=grid_spec,
        out_shape=jax.ShapeDtypeStruct((n_tiles * GMM_TILE, D_MODEL), F32),
        compiler_params=_cparams(1),
        name="moe_gmm",
    )(te, tsrc, nact, xs, wgu_bf, bgu, wdn_bf, bdn)


def _combine_kernel(dest_hbm, ys_hbm, x_ref, gt_ref, g_ref, bb_ref, o_ref, dsm, buf, sems, *, tm, stride):
    i = pl.program_id(0)
    nsteps = pl.num_programs(0)

    @pl.when(i == 0)
    def _():
        _idx_copy(dest_hbm, dsm, sems.at[0], i, stride).start()

    _idx_copy(dest_hbm, dsm, sems.at[0], i, stride).wait()

    @pl.when(i + 1 < nsteps)
    def _():
        _idx_copy(dest_hbm, dsm, sems.at[0], i + 1, stride).start()

    base = (i % 2) * stride

    def row_body(r, _):
        for k in range(TOP_K):
            d = dsm[base + k * tm + r]
            pltpu.make_async_copy(ys_hbm.at[pl.ds(d, 1)], buf.at[k, pl.ds(r, 1)], sems.at[1]).start()
        return 0

    lax.fori_loop(0, tm, row_body, 0)
    for k in range(TOP_K):
        pltpu.make_async_copy(ys_hbm.at[pl.ds(0, tm)], buf.at[k], sems.at[1]).wait()

    gt = gt_ref[...]
    ff = gt[:, 0:1] * buf[0]
    for k in range(1, TOP_K):
        ff = ff + gt[:, k:k + 1] * buf[k]
    o_ref[...] = _layer_norm(DEEPNORM_ALPHA * x_ref[...] + ff, g_ref[...], bb_ref[...])


def _combine(ys, dest_steps, x, gates_t, g, bb, tm, stride):
    t = x.shape[0]
    row = lambda i: (i, 0)
    fixed = lambda i: (0, 0)
    return pl.pallas_call(
        functools.partial(_combine_kernel, tm=tm, stride=stride),
        grid=(t // tm,),
        in_specs=[pl.BlockSpec(memory_space=pl.ANY),
                  pl.BlockSpec(memory_space=pl.ANY),
                  pl.BlockSpec((tm, D_MODEL), row),
                  pl.BlockSpec((tm, TOP_K), row),
                  pl.BlockSpec((1, D_MODEL), fixed),
                  pl.BlockSpec((1, D_MODEL), fixed)],
        out_specs=pl.BlockSpec((tm, D_MODEL), row),
        out_shape=jax.ShapeDtypeStruct((t, D_MODEL), F32),
        scratch_shapes=[pltpu.SMEM((2 * stride,), jnp.int32),
                        pltpu.VMEM((TOP_K, tm, D_MODEL), F32),
                        pltpu.SemaphoreType.DMA((2,))],
        compiler_params=_cparams(1),
        name="moe_combine",
    )(dest_steps, ys, x, gates_t, g, bb)


def _moe(x1, ids, gates, rank, counts, wgu_bf, bgu, wdn_bf, bdn, g, bb, tm):
    t = x1.shape[0]
    n_tiles = -(-(t * TOP_K) // GMM_TILE) + N_EXPERTS
    counts = counts[:, 0]
    padded = ((counts + GMM_TILE - 1) // GMM_TILE) * GMM_TILE
    pend = jnp.cumsum(padded)
    pstart = pend - padded
    nact = (pend[-1] // GMM_TILE).astype(jnp.int32).reshape(1)
    tile_row = jnp.arange(n_tiles, dtype=jnp.int32) * GMM_TILE
    te_raw = jnp.minimum(jnp.searchsorted(pend, tile_row, side="right"), N_EXPERTS - 1).astype(jnp.int32)
    last = jnp.maximum(nact[0] - 1, 0)
    tsrc = jnp.minimum(jnp.arange(n_tiles, dtype=jnp.int32), last)
    te = te_raw[tsrc]
    dest = pstart[ids] + rank
    nsteps = t // tm
    stride = -(-(TOP_K * tm) // 1024) * 1024
    dest_steps = dest.reshape(TOP_K, nsteps, tm).transpose(1, 0, 2).reshape(nsteps, TOP_K * tm)
    dest_steps = jnp.pad(dest_steps, ((0, 0), (0, stride - TOP_K * tm))).reshape(-1)
    xs = _dispatch(x1, dest_steps, pstart.astype(jnp.int32), padded.astype(jnp.int32), nact,
                   tm, stride, n_tiles)
    ys = _gmm(xs, te, tsrc, nact, wgu_bf, bgu, wdn_bf, bdn, n_tiles)
    return _combine(ys, dest_steps, x1, gates.T, g, bb, tm, stride)


def _pw1_glu_kernel(x_ref, w_ref, b_ref, u_ref):
    a = jnp.dot(x_ref[...].astype(BF16), w_ref[...], preferred_element_type=F32) + b_ref[...]
    u_ref[...] = a[:, :D_MODEL] * jax.nn.sigmoid(a[:, D_MODEL:])


def _pw1_glu(x, w_bf, b, tm):
    t = x.shape[0]
    row = lambda i: (i, 0)
    fixed = lambda i: (0, 0)
    return pl.pallas_call(
        _pw1_glu_kernel,
        grid=(t // tm,),
        in_specs=[pl.BlockSpec((tm, D_MODEL), row),
                  pl.BlockSpec((D_MODEL, 2 * D_MODEL), fixed),
                  pl.BlockSpec((1, 2 * D_MODEL), fixed)],
        out_specs=pl.BlockSpec((tm, D_MODEL), row),
        out_shape=jax.ShapeDtypeStruct((t, D_MODEL), F32),
        compiler_params=_cparams(1),
        name="pw1_glu",
    )(x, w_bf, b)


def _conv_rows(win_ref, zs_ref, base, offs, w_ref, b_ref, g_ref, bb_ref):
    shift = CONV_HALO - CONV_CTX
    accs = [[None] * N_SLAB for _ in offs]
    for c in range(N_SLAB):
        bias = b_ref[c:c + 1, :]
        for j in range(CONV_WIDTH):
            w = w_ref[c, j:j + 1, :]
            for q, o in enumerate(offs):
                x = win_ref[c, pl.ds(base + (shift + j + o), SUBLANES, stride=2), :]
                accs[q][c] = (bias if j == 0 else accs[q][c]) + w * x
    for q, o in enumerate(offs):
        tot = accs[q][0]
        for c in range(1, N_SLAB):
            tot = tot + accs[q][c]
        mu = jnp.sum(tot, axis=1, keepdims=True) * (1.0 / D_MODEL)
        cen = [a - mu for a in accs[q]]
        sq = cen[0] * cen[0]
        for c in range(1, N_SLAB):
            sq = sq + cen[c] * cen[c]
        inv = lax.rsqrt(jnp.sum(sq, axis=1, keepdims=True) * (1.0 / D_MODEL) + LN_EPS)
        for c in range(N_SLAB):
            z = cen[c] * inv * g_ref[c:c + 1, :] + bb_ref[c:c + 1, :]
            zs_ref[c, pl.ds(base + o, SUBLANES, stride=2), :] = z * jax.nn.sigmoid(z)


def _conv_prompt_kernel(um_ref, up_ref, uc_ref, w_ref, b_ref, g_ref, bb_ref, z_ref, win_ref, zs_ref, *, tr):
    i = pl.program_id(1)
    gap = CONV_HALO - META_LEN

    @pl.when(i == 0)
    def _():
        for c in range(N_SLAB):
            win_ref[c, 0:gap, :] = jnp.zeros((gap, LANES), F32)
            win_ref[c, gap:CONV_HALO, :] = um_ref[:, c * LANES:(c + 1) * LANES]

    @pl.when(i > 0)
    def _():
        for c in range(N_SLAB):
            win_ref[c, 0:CONV_HALO, :] = up_ref[:, c * LANES:(c + 1) * LANES]

    for c in range(N_SLAB):
        win_ref[c, CONV_HALO:, :] = uc_ref[:, c * LANES:(c + 1) * LANES]

    def block(bi, _):
        base = pl.multiple_of(bi * CONV_BLOCK, CONV_BLOCK)
        _conv_rows(win_ref, zs_ref, base, (0, 1, 16, 17), w_ref, b_ref, g_ref, bb_ref)
        return 0

    lax.fori_loop(0, tr // CONV_BLOCK, block, 0)
    for c in range(N_SLAB):
        z_ref[:, c * LANES:(c + 1) * LANES] = zs_ref[c].astype(z_ref.dtype)


def _conv_prompt(u, w3, b3, g3, bb3, n_batch, seq, tr):
    nt = seq // tr
    meta_blk0 = n_batch * seq // META_LEN
    cur = lambda b, i: (b * nt + i, 0)
    prev = lambda b, i: (jnp.maximum((b * seq + i * tr) // CONV_HALO - 1, 0), 0)
    fixed2 = lambda b, i: (0, 0)
    fixed3 = lambda b, i: (0, 0, 0)
    return pl.pallas_call(
        functools.partial(_conv_prompt_kernel, tr=tr),
        grid=(n_batch, nt),
        in_specs=[pl.BlockSpec((META_LEN, D_MODEL), lambda b, i: (meta_blk0 + b, 0)),
                  pl.BlockSpec((CONV_HALO, D_MODEL), prev),
                  pl.BlockSpec((tr, D_MODEL), cur),
                  pl.BlockSpec((N_SLAB, CONV_HALO, LANES), fixed3),
                  pl.BlockSpec((N_SLAB, LANES), fixed2),
                  pl.BlockSpec((N_SLAB, LANES), fixed2),
                  pl.BlockSpec((N_SLAB, LANES), fixed2)],
        out_specs=pl.BlockSpec((tr, D_MODEL), cur),
        out_shape=jax.ShapeDtypeStruct((n_batch * seq, D_MODEL), BF16),
        scratch_shapes=[pltpu.VMEM((N_SLAB, CONV_HALO + tr, LANES), F32),
                        pltpu.VMEM((N_SLAB, tr, LANES), F32)],
        compiler_params=_cparams(2),
        name="conv_prompt",
    )(u, u, u, w3, b3, g3, bb3)


SHORT_ROWS = 16


def _conv_short_kernel(ctx_ref, u_ref, w_ref, b_ref, g_ref, bb_ref, z_ref, win_ref, zs_ref, *, n_seq, t_len):
    for n in range(n_seq):
        for c in range(N_SLAB):
            sl = slice(c * LANES, (c + 1) * LANES)
            win_ref[c, 0:CONV_HALO, :] = ctx_ref[n, :, sl]
            win_ref[c, CONV_HALO:CONV_HALO + t_len, :] = u_ref[n * t_len:(n + 1) * t_len, sl]
            if t_len < SHORT_ROWS:
                win_ref[c, CONV_HALO + t_len:, :] = jnp.zeros((SHORT_ROWS - t_len, LANES), F32)
        _conv_rows(win_ref, zs_ref, 0, (0, 1), w_ref, b_ref, g_ref, bb_ref)
        for c in range(N_SLAB):
            z_ref[n * t_len:(n + 1) * t_len, c * LANES:(c + 1) * LANES] = zs_ref[c, 0:t_len, :]


def _conv_short(ctx_pad, u, row0, w3, b3, g3, bb3, t_len, n_seq):
    n_total = ctx_pad.shape[0]
    rows = n_seq * t_len
    blk0 = row0 // rows
    fixed2 = lambda n: (0, 0)
    fixed3 = lambda n: (0, 0, 0)
    return pl.pallas_call(
        functools.partial(_conv_short_kernel, n_seq=n_seq, t_len=t_len),
        grid=(n_total // n_seq,),
        in_specs=[pl.BlockSpec((n_seq, CONV_HALO, D_MODEL), lambda n: (n, 0, 0)),
                  pl.BlockSpec((rows, D_MODEL), lambda n: (blk0 + n, 0)),
                  pl.BlockSpec((N_SLAB, CONV_HALO, LANES), fixed3),
                  pl.BlockSpec((N_SLAB, LANES), fixed2),
                  pl.BlockSpec((N_SLAB, LANES), fixed2),
                  pl.BlockSpec((N_SLAB, LANES), fixed2)],
        out_specs=pl.BlockSpec((rows, D_MODEL), lambda n: (n, 0)),
        out_shape=jax.ShapeDtypeStruct((n_total * t_len, D_MODEL), F32),
        scratch_shapes=[pltpu.VMEM((N_SLAB, CONV_HALO + SHORT_ROWS, LANES), F32),
                        pltpu.VMEM((N_SLAB, SHORT_ROWS, LANES), F32)],
        compiler_params=_cparams(1),
        name="conv_short",
    )(ctx_pad, u, w3, b3, g3, bb3)


def _rope_tables(pos):
    half = ROT_DIM // 2
    inv = 1.0 / (ROPE_THETA ** (jnp.arange(0, ROT_DIM, 2, dtype=F32) / ROT_DIM))
    ang = pos.astype(F32)[:, None] * inv[None, :]
    cos, sin = jnp.cos(ang), jnp.sin(ang)
    n = pos.shape[0]
    ones = jnp.ones((n, HEAD_DIM - ROT_DIM), F32)
    zeros = jnp.zeros((n, HEAD_DIM - ROT_DIM), F32)
    zh = jnp.zeros((n, half), F32)
    c = jnp.concatenate([cos, cos, ones], axis=1)
    a = jnp.concatenate([-sin, zh, zeros], axis=1)
    s = jnp.concatenate([zh, sin, zeros], axis=1)
    rep = LANES // HEAD_DIM
    return jnp.tile(c, (1, rep)), jnp.tile(a, (1, rep)), jnp.tile(s, (1, rep))


def kernel(x_prompt, x_sample, cache_attn_meta_k, cache_attn_meta_v, cache_attn_win_k, cache_attn_win_v, state_conv, meta_tokens, attn_w_qkv, attn_b_qkv, attn_sinks, attn_w_o, attn_b_o, conv_w_pw1, conv_b_pw1, conv_w_dw, conv_b_dw, conv_ln_g, conv_ln_b, conv_w_pw2, conv_b_pw2, ln_mix_g, ln_mix_b, ln_ffn_g, ln_ffn_b, moe_w_router, moe_b_router, moe_w_gate_up, moe_b_gate_up, moe_w_down, moe_b_down):
    n_batch, seq, _ = x_prompt.shape
    dec_batch, dec_seq, _ = x_sample.shape
    n_real = n_batch * seq
    n_meta = n_batch * META_LEN
    n_samp = dec_batch * dec_seq
    t = n_real + n_meta + n_samp
    tm = _pick_tile(t, 640)
    row2 = lambda v: v.reshape(1, -1)

    meta_rows = jnp.broadcast_to(meta_tokens[None], (n_batch, META_LEN, D_MODEL)).reshape(n_meta, D_MODEL)
    x = jnp.concatenate([x_prompt.reshape(n_real, D_MODEL), meta_rows.astype(F32),
                         x_sample.reshape(n_samp, D_MODEL)], axis=0)

    def moe_layer(i, a_bf, w, b, x_res):
        x1, ids, gates, rank, counts = _proj_ln_route(
            a_bf, w.astype(BF16), row2(b), x_res, row2(ln_mix_g[i]), row2(ln_mix_b[i]),
            moe_w_router[i].T.astype(BF16), moe_b_router[i].reshape(N_EXPERTS, 1), tm)
        return _moe(x1, ids, gates, rank, counts,
                    moe_w_gate_up[i].astype(BF16), moe_b_gate_up[i].reshape(N_EXPERTS, 1, 2 * MOE_FF),
                    moe_w_down[i].astype(BF16), moe_b_down[i].reshape(N_EXPERTS, 1, D_MODEL),
                    row2(ln_ffn_g[i]), row2(ln_ffn_b[i]), tm)

    pos = jnp.concatenate([jnp.tile(META_LEN + jnp.arange(seq), n_batch),
                           jnp.tile(jnp.arange(META_LEN), n_batch),
                           jnp.tile(PAST_LEN + jnp.arange(dec_seq), dec_batch)])
    cos_t, sa_t, sb_t = _rope_tables(pos)
    q, k, v = _qkv_rope(x, attn_w_qkv[0].astype(BF16), row2(attn_b_qkv[0]), cos_t, sa_t, sb_t, tm)
    sinks = attn_sinks[0]
    cmk = cache_attn_meta_k[0].reshape(dec_batch, META_LEN, KV_DIM)
    cmv = cache_attn_meta_v[0].reshape(dec_batch, META_LEN, KV_DIM)
    cwk = cache_attn_win_k[0].reshape(dec_batch, WINDOW, KV_DIM)
    cwv = cache_attn_win_v[0].reshape(dec_batch, WINDOW, KV_DIM)
    o = jnp.concatenate([
        _attn_prompt(q, k, v, sinks, n_batch, seq),
        _attn_meta(q, k, v, sinks, n_batch, seq),
        _attn_sample(q, k, v, cmk, cmv, cwk, cwv, sinks, n_real + n_meta, dec_batch, dec_seq)], axis=0)
    x = moe_layer(0, o, attn_w_o[0], attn_b_o[0], x)

    k_real = k[:n_real].reshape(n_batch, seq, N_KV_HEADS, HEAD_DIM)
    v_real = v[:n_real].reshape(n_batch, seq, N_KV_HEADS, HEAD_DIM)
    p_meta_k = k[n_real:n_real + n_meta].reshape(1, n_batch, META_LEN, N_KV_HEADS, HEAD_DIM)
    p_meta_v = v[n_real:n_real + n_meta].reshape(1, n_batch, META_LEN, N_KV_HEADS, HEAD_DIM)
    p_win_k = k_real[:, seq - WINDOW:][None]
    p_win_v = v_real[:, seq - WINDOW:][None]
    k_new = k[n_real + n_meta:].reshape(dec_batch, dec_seq, N_KV_HEADS, HEAD_DIM)
    v_new = v[n_real + n_meta:].reshape(dec_batch, dec_seq, N_KV_HEADS, HEAD_DIM)
    s_win_k = jnp.concatenate([cache_attn_win_k[0], k_new], axis=1)[:, -WINDOW:][None]
    s_win_v = jnp.concatenate([cache_attn_win_v[0], v_new], axis=1)[:, -WINDOW:][None]

    u = _pw1_glu(x, conv_w_pw1[0].astype(BF16), row2(conv_b_pw1[0]), tm)
    slab = lambda v: v.reshape(N_SLAB, LANES)
    w_dw = jnp.pad(conv_w_dw[0], ((0, CONV_HALO - CONV_WIDTH), (0, 0)))
    w3 = w_dw.reshape(CONV_HALO, N_SLAB, LANES).transpose(1, 0, 2)
    conv_args = (w3, slab(conv_b_dw[0]), slab(conv_ln_g[0]), slab(conv_ln_b[0]))
    tr = _pick_tile(seq, 512)
    z_real = _conv_prompt(u, *conv_args, n_batch, seq, tr)
    z_meta = _conv_short(jnp.zeros((n_batch, CONV_HALO, D_MODEL), F32), u, n_real, *conv_args,
                         META_LEN, 1)
    ctx_pad = jnp.pad(state_conv[0], ((0, 0), (CONV_HALO - CONV_CTX, 0), (0, 0)))
    z_samp = _conv_short(ctx_pad, u, n_real + n_meta, *conv_args, dec_seq, 2)
    z = jnp.concatenate([z_real, z_meta.astype(BF16), z_samp.astype(BF16)], axis=0)
    x = moe_layer(1, z, conv_w_pw2[0], conv_b_pw2[0], x)

    u_real = u[:n_real].reshape(n_batch, seq, D_MODEL)
    p_conv = u_real[:, seq - CONV_CTX:][None]
    u_samp = u[n_real + n_meta:].reshape(dec_batch, dec_seq, D_MODEL)
    s_conv = jnp.concatenate([state_conv[0], u_samp], axis=1)[:, -CONV_CTX:][None]

    y_prompt = x[:n_real].reshape(n_batch, seq, D_MODEL)
    y_sample = x[n_real + n_meta:].reshape(dec_batch, dec_seq, D_MODEL)
    return (y_prompt, y_sample, p_meta_k, p_meta_v, p_win_k, p_win_v, p_conv,
            s_win_k, s_win_v, s_conv)
```

```python
import functools

import jax
import jax.numpy as jnp
from jax import lax
from jax.experimental import pallas as pl
from jax.experimental.pallas import tpu as pltpu

F32 = jnp.float32
BF16 = jnp.bfloat16

D_MODEL = 1024
HEAD_DIM = 64
N_HEADS = 16
N_KV_HEADS = 4
GROUP = N_HEADS // N_KV_HEADS
KV_DIM = N_KV_HEADS * HEAD_DIM
QKV_DIM = D_MODEL + 2 * KV_DIM
ROT_DIM = 16
ROPE_THETA = 500000.0
WINDOW = 128
ATTN_BLOCK = 128
ATTN_SCALE = HEAD_DIM ** -0.5
META_LEN = 16
CONV_WIDTH = 31
CONV_CTX = CONV_WIDTH - 1
N_EXPERTS = 32
TOP_K = 4
MOE_FF = 1024
SWIGLU_LIMIT = 7.0
SWIGLU_ALPHA = 1.702
LN_EPS = 1e-5
DEPTH = 2
DEEPNORM_ALPHA = (2 * DEPTH) ** 0.25
PAST_LEN = 16384
NEG_INF = -1e30

LANES = 128
SUBLANES = 8
GMM_TILE = 256
CONV_HALO = 32
CONV_BLOCK = 32
N_SLAB = D_MODEL // LANES
VMEM_LIMIT = 56 * 1024 * 1024


def _cparams(n_axes):
    return pltpu.CompilerParams(dimension_semantics=("arbitrary",) * n_axes,
                                vmem_limit_bytes=VMEM_LIMIT)


def _pick_tile(n, cap):
    best = None
    t = LANES
    while t <= cap:
        if n % t == 0:
            best = t
        t += LANES
    assert best is not None, n
    return best


def _layer_norm(h, g, b):
    mu = jnp.mean(h, axis=-1, keepdims=True)
    hc = h - mu
    var = jnp.mean(hc * hc, axis=-1, keepdims=True)
    return hc * lax.rsqrt(var + LN_EPS) * g + b


def _qkv_kernel(x_ref, w_ref, b_ref, c_ref, a_ref, s_ref, q_ref, k_ref, v_ref):
    x = x_ref[...].astype(BF16)
    acc = jnp.dot(x, w_ref[...], preferred_element_type=F32) + b_ref[...]
    c = c_ref[...]
    a = a_ref[...]
    s = s_ref[...]

    def rope(t):
        return t * c + pltpu.roll(t, LANES - ROT_DIM // 2, 1) * a + pltpu.roll(t, ROT_DIM // 2, 1) * s

    for j in range(D_MODEL // LANES):
        sl = slice(j * LANES, (j + 1) * LANES)
        q_ref[:, sl] = (rope(acc[:, sl]) * ATTN_SCALE).astype(BF16)
    for j in range(KV_DIM // LANES):
        sl = slice(D_MODEL + j * LANES, D_MODEL + (j + 1) * LANES)
        k_ref[:, j * LANES:(j + 1) * LANES] = rope(acc[:, sl])
    v_ref[...] = acc[:, D_MODEL + KV_DIM:]


def _qkv_rope(x, w_bf, b, cos_t, sa_t, sb_t, tm):
    t = x.shape[0]
    period = cos_t.shape[0] // tm
    row = lambda i: (i, 0)
    tab = lambda i: (i % period, 0)
    fixed = lambda i: (0, 0)
    return pl.pallas_call(
        _qkv_kernel,
        grid=(t // tm,),
        in_specs=[pl.BlockSpec((tm, D_MODEL), row),
                  pl.BlockSpec((D_MODEL, QKV_DIM), fixed),
                  pl.BlockSpec((1, QKV_DIM), fixed),
                  pl.BlockSpec((tm, LANES), tab),
                  pl.BlockSpec((tm, LANES), tab),
                  pl.BlockSpec((tm, LANES), tab)],
        out_specs=[pl.BlockSpec((tm, D_MODEL), row),
                   pl.BlockSpec((tm, KV_DIM), row),
                   pl.BlockSpec((tm, KV_DIM), row)],
        out_shape=[jax.ShapeDtypeStruct((t, D_MODEL), BF16),
                   jax.ShapeDtypeStruct((t, KV_DIM), F32),
                   jax.ShapeDtypeStruct((t, KV_DIM), F32)],
        compiler_params=_cparams(1),
        name="qkv_rope",
    )(x, w_bf, b, cos_t, sa_t, sb_t)


def _attend(q, kcat, vcat, mask, sink_ref, write):
    for h in range(N_HEADS):
        g = h // GROUP
        qh = q[:, h * HEAD_DIM:(h + 1) * HEAD_DIM]
        kg = kcat[:, g * HEAD_DIM:(g + 1) * HEAD_DIM]
        vg = vcat[:, g * HEAD_DIM:(g + 1) * HEAD_DIM]
        s = lax.dot_general(qh, kg, (((1,), (1,)), ((), ())), preferred_element_type=F32)
        s = jnp.where(mask, s, NEG_INF)
        sink = sink_ref[h]
        m = jnp.maximum(jnp.max(s, axis=-1, keepdims=True), sink)
        p = jnp.exp(s - m)
        denom = jnp.sum(p, axis=-1, keepdims=True) + jnp.exp(sink - m)
        o = jnp.dot(p.astype(BF16), vg, preferred_element_type=F32)
        write(h, o / denom)


def _attn_prompt_kernel(sink_ref, q_ref, km_ref, kp_ref, ko_ref, vm_ref, vp_ref, vo_ref, o_ref):
    i = pl.program_id(1)
    kcat = jnp.concatenate([km_ref[...], kp_ref[...], ko_ref[...]], axis=0).astype(BF16)
    vcat = jnp.concatenate([vm_ref[...], vp_ref[...], vo_ref[...]], axis=0).astype(BF16)
    nk = META_LEN + 2 * ATTN_BLOCK
    r = lax.broadcasted_iota(jnp.int32, (ATTN_BLOCK, nk), 0)
    c = lax.broadcasted_iota(jnp.int32, (ATTN_BLOCK, nk), 1)
    first = jnp.where(i > 0, 0, 2 * ATTN_BLOCK)
    prev_ok = (c >= META_LEN + first) & (c < META_LEN + ATTN_BLOCK) & (c - META_LEN >= r)
    own_ok = (c >= META_LEN + ATTN_BLOCK) & (c - (META_LEN + ATTN_BLOCK) <= r)
    mask = (c < META_LEN) | prev_ok | own_ok

    def write(h, o):
        o_ref[:, h * HEAD_DIM:(h + 1) * HEAD_DIM] = o.astype(o_ref.dtype)

    _attend(q_ref[...], kcat, vcat, mask, sink_ref, write)


def _attn_prompt(q, k, v, k_meta, v_meta, sinks, n_batch, seq):
    nblk = seq // ATTN_BLOCK
    qmap = lambda b, i: (b * nblk + i, 0)
    pmap = lambda b, i: (b * nblk + jnp.maximum(i - 1, 0), 0)
    mmap = lambda b, i: (b, 0)
    kspec = lambda m: pl.BlockSpec((ATTN_BLOCK, KV_DIM), m)
    mspec = pl.BlockSpec((META_LEN, KV_DIM), mmap)
    return pl.pallas_call(
        _attn_prompt_kernel,
        grid=(n_batch, nblk),
        in_specs=[pl.BlockSpec(memory_space=pltpu.SMEM),
                  pl.BlockSpec((ATTN_BLOCK, D_MODEL), qmap),
                  mspec, kspec(pmap), kspec(qmap),
                  mspec, kspec(pmap), kspec(qmap)],
        out_specs=pl.BlockSpec((ATTN_BLOCK, D_MODEL), qmap),
        out_shape=jax.ShapeDtypeStruct((n_batch * seq, D_MODEL), BF16),
        compiler_params=_cparams(2),
        name="attn_prompt",
    )(sinks, q, k_meta, k, k, v_meta, v, v)


def _attn_meta_kernel(sink_ref, q_ref, k_ref, v_ref, o_ref):
    r = lax.broadcasted_iota(jnp.int32, (META_LEN, META_LEN), 0)
    c = lax.broadcasted_iota(jnp.int32, (META_LEN, META_LEN), 1)

    def write(h, o):
        o_ref[:, h * HEAD_DIM:(h + 1) * HEAD_DIM] = o.astype(o_ref.dtype)

    _attend(q_ref[...], k_ref[...].astype(BF16), v_ref[...].astype(BF16), c <= r, sink_ref, write)


def _attn_meta(q, k, v, sinks, n_batch):
    imap = lambda b: (b, 0)
    return pl.pallas_call(
        _attn_meta_kernel,
        grid=(n_batch,),
        in_specs=[pl.BlockSpec(memory_space=pltpu.SMEM),
                  pl.BlockSpec((META_LEN, D_MODEL), imap),
                  pl.BlockSpec((META_LEN, KV_DIM), imap),
                  pl.BlockSpec((META_LEN, KV_DIM), imap)],
        out_specs=pl.BlockSpec((META_LEN, D_MODEL), lambda b: (b, 0)),
        out_shape=jax.ShapeDtypeStruct((n_batch * META_LEN, D_MODEL), BF16),
        compiler_params=_cparams(1),
        name="attn_meta",
    )(sinks, q, k, v)


SAMPLE_GROUP = 2


def _attn_sample_kernel(sink_ref, q_ref, kn_ref, vn_ref, cmk_ref, cmv_ref, cwk_ref, cwv_ref,
                        o_ref, acc_ref, *, dec_seq):
    nq = SAMPLE_GROUP * dec_seq
    nk = META_LEN + WINDOW + nq
    q = q_ref[...]
    kn = kn_ref[...]
    vn = vn_ref[...]
    r = lax.broadcasted_iota(jnp.int32, (nq, nk), 0)
    c = lax.broadcasted_iota(jnp.int32, (nq, nk), 1)
    acc_ref[...] = jnp.zeros_like(acc_ref)

    def write(h, o):
        acc_ref[:, h * HEAD_DIM:(h + 1) * HEAD_DIM] += o

    for j in range(SAMPLE_GROUP):
        kcat = jnp.concatenate([cmk_ref[j], cwk_ref[j], kn], axis=0).astype(BF16)
        vcat = jnp.concatenate([cmv_ref[j], cwv_ref[j], vn], axis=0).astype(BF16)
        tq = r - j * dec_seq
        mine = (tq >= 0) & (tq < dec_seq)
        cn = c - (META_LEN + WINDOW) - j * dec_seq
        win_ok = (c >= META_LEN) & (c < META_LEN + WINDOW) & (c - META_LEN >= tq)
        new_ok = (cn >= 0) & (cn <= tq)
        vis = (c < META_LEN) | win_ok | new_ok
        _attend(q, kcat, vcat, mine & vis, sink_ref, write)
    o_ref[...] = acc_ref[...].astype(o_ref.dtype)


def _attn_sample(q, k, v, cmk, cmv, cwk, cwv, sinks, row0, dec_batch, dec_seq):
    nq = SAMPLE_GROUP * dec_seq
    blk0 = row0 // nq
    qmap = lambda n: (blk0 + n, 0)
    cmap = lambda n: (n, 0, 0)
    return pl.pallas_call(
        functools.partial(_attn_sample_kernel, dec_seq=dec_seq),
        grid=(dec_batch // SAMPLE_GROUP,),
        in_specs=[pl.BlockSpec(memory_space=pltpu.SMEM),
                  pl.BlockSpec((nq, D_MODEL), qmap),
                  pl.BlockSpec((nq, KV_DIM), qmap),
                  pl.BlockSpec((nq, KV_DIM), qmap),
                  pl.BlockSpec((SAMPLE_GROUP, META_LEN, KV_DIM), cmap),
                  pl.BlockSpec((SAMPLE_GROUP, META_LEN, KV_DIM), cmap),
                  pl.BlockSpec((SAMPLE_GROUP, WINDOW, KV_DIM), cmap),
                  pl.BlockSpec((SAMPLE_GROUP, WINDOW, KV_DIM), cmap)],
        out_specs=pl.BlockSpec((nq, D_MODEL), lambda n: (n, 0)),
        out_shape=jax.ShapeDtypeStruct((dec_batch * dec_seq, D_MODEL), BF16),
        scratch_shapes=[pltpu.VMEM((nq, D_MODEL), F32)],
        compiler_params=_cparams(1),
        name="attn_sample",
    )(sinks, q, k, v, cmk, cmv, cwk, cwv)


def _proj_ln_route_kernel(a_ref, w_ref, b_ref, x_ref, g_ref, bb_ref, wr_ref, br_ref, tri_ref, cin_ref,
                          x1_ref, ids_ref, gates_ref, rank_ref, cnt_ref, carry_ref):
    @pl.when(pl.program_id(0) == 0)
    def _():
        carry_ref[...] = cin_ref[...].astype(F32)

    y = jnp.dot(a_ref[...], w_ref[...], preferred_element_type=F32) + b_ref[...]
    x1 = _layer_norm(DEEPNORM_ALPHA * x_ref[...] + y, g_ref[...], bb_ref[...])
    x1_ref[...] = x1

    logits = lax.dot_general(wr_ref[...], x1.astype(BF16), (((1,), (1,)), ((), ())),
                             preferred_element_type=F32) + br_ref[...]
    tm = logits.shape[1]
    eidx = lax.broadcasted_iota(jnp.int32, (N_EXPERTS, tm), 0).astype(F32)
    cur = logits
    vals, idxs, sels = [], [], []
    for _ in range(TOP_K):
        m = jnp.max(cur, axis=0, keepdims=True)
        idx = jnp.min(jnp.where(cur == m, eidx, float(N_EXPERTS)), axis=0, keepdims=True)
        sel = eidx == idx
        vals.append(m)
        idxs.append(idx)
        sels.append(sel)
        cur = jnp.where(sel, -jnp.inf, cur)
    exps = [jnp.exp(v - vals[0]) for v in vals]
    tot = exps[0] + exps[1] + exps[2] + exps[3]
    gates_ref[...] = jnp.concatenate([e / tot for e in exps], axis=0)
    ids_ref[...] = jnp.concatenate(idxs, axis=0).astype(jnp.int32)

    chosen = jnp.where(sels[0] | sels[1] | sels[2] | sels[3], 1.0, 0.0)
    before = jnp.dot(chosen.astype(BF16), tri_ref[...], preferred_element_type=F32)
    before = before + carry_ref[:, 0:1]
    ranks = [jnp.sum(jnp.where(s, before, 0.0), axis=0, keepdims=True) for s in sels]
    rank_ref[...] = jnp.concatenate(ranks, axis=0).astype(jnp.int32)
    carry_ref[...] = carry_ref[...] + jnp.sum(chosen, axis=1, keepdims=True)
    cnt_ref[...] = carry_ref[...].astype(jnp.int32)


def _proj_ln_route(a_bf, w_bf, b, x, g, bb, wr_t_bf, br_col, counts_in, tm):
    t = x.shape[0]
    row = lambda i: (i, 0)
    col = lambda i: (0, i)
    fixed = lambda i: (0, 0)
    tri = (jnp.arange(tm)[:, None] < jnp.arange(tm)[None, :]).astype(BF16)
    return pl.pallas_call(
        _proj_ln_route_kernel,
        grid=(t // tm,),
        in_specs=[pl.BlockSpec((tm, D_MODEL), row),
                  pl.BlockSpec((D_MODEL, D_MODEL), fixed),
                  pl.BlockSpec((1, D_MODEL), fixed),
                  pl.BlockSpec((tm, D_MODEL), row),
                  pl.BlockSpec((1, D_MODEL), fixed),
                  pl.BlockSpec((1, D_MODEL), fixed),
                  pl.BlockSpec((N_EXPERTS, D_MODEL), fixed),
                  pl.BlockSpec((N_EXPERTS, 1), fixed),
                  pl.BlockSpec((tm, tm), fixed),
                  pl.BlockSpec((N_EXPERTS, LANES), fixed)],
        out_specs=[pl.BlockSpec((tm, D_MODEL), row),
                   pl.BlockSpec((TOP_K, tm), col),
                   pl.BlockSpec((TOP_K, tm), col),
                   pl.BlockSpec((TOP_K, tm), col),
                   pl.BlockSpec((N_EXPERTS, LANES), fixed)],
        out_shape=[jax.ShapeDtypeStruct((t, D_MODEL), F32),
                   jax.ShapeDtypeStruct((TOP_K, t), jnp.int32),
                   jax.ShapeDtypeStruct((TOP_K, t), F32),
                   jax.ShapeDtypeStruct((TOP_K, t), jnp.int32),
                   jax.ShapeDtypeStruct((N_EXPERTS, LANES), jnp.int32)],
        scratch_shapes=[pltpu.VMEM((N_EXPERTS, LANES), F32)],
        compiler_params=_cparams(1),
        name="proj_ln_route",
    )(a_bf, w_bf, b, x, g, bb, wr_t_bf, br_col, tri, counts_in)


def _idx_copy(dest_hbm, dsm, sem, step, stride):
    slot = step % 2
    return pltpu.make_async_copy(dest_hbm.at[pl.ds(pl.multiple_of(step * stride, stride), stride)],
                                 dsm.at[pl.ds(pl.multiple_of(slot * stride, stride), stride)],
                                 sem)


def _dispatch_kernel(pstart_ref, padded_ref, nact_ref, dest_hbm, *rest, tiles, steps, stride, n_tiles):
    x_refs = rest[:len(tiles)]
    xs_hbm, dsm, zbuf, sems = rest[len(tiles):]
    i = pl.program_id(0)
    nsteps = pl.num_programs(0)

    def zero_tile(row0):
        return pltpu.make_async_copy(zbuf, xs_hbm.at[pl.ds(pl.multiple_of(row0, GMM_TILE), GMM_TILE)],
                                     sems.at[2])

    @pl.when(i == 0)
    def _():
        zbuf[...] = jnp.zeros_like(zbuf)

        def each_expert(fn):
            def body(e, _):
                @pl.when(padded_ref[e] > 0)
                def _():
                    fn(zero_tile(pstart_ref[e] + padded_ref[e] - GMM_TILE))
                return 0
            lax.fori_loop(0, N_EXPERTS, body, 0)

        def each_tail(fn):
            def body(t, _):
                fn(zero_tile(t * GMM_TILE))
                return 0
            lax.fori_loop(nact_ref[0], n_tiles, body, 0)

        each_expert(lambda cp: cp.start())
        each_tail(lambda cp: cp.start())
        each_expert(lambda cp: cp.wait())
        each_tail(lambda cp: cp.wait())
        _idx_copy(dest_hbm, dsm, sems.at[0], i, stride).start()

    _idx_copy(dest_hbm, dsm, sems.at[0], i, stride).wait()

    @pl.when(i + 1 < nsteps)
    def _():
        _idx_copy(dest_hbm, dsm, sems.at[0], i + 1, stride).start()

    base = (i % 2) * stride
    step0 = 0
    for x_ref, tm, n in zip(x_refs, tiles, steps):
        @pl.when((i >= step0) & (i < step0 + n))
        def _(x_ref=x_ref, tm=tm):
            def row_body(r, _):
                for k in range(TOP_K):
                    d = dsm[base + k * tm + r]
                    pltpu.make_async_copy(x_ref.at[pl.ds(r, 1)], xs_hbm.at[pl.ds(d, 1)], sems.at[1]).start()
                return 0

            lax.fori_loop(0, tm, row_body, 0)
            for k in range(TOP_K):
                pltpu.make_async_copy(x_ref, xs_hbm.at[pl.ds(0, tm)], sems.at[1]).wait()
        step0 += n


def _seg_map(step0, n):
    return lambda i, *_: (jnp.clip(i - step0, 0, n - 1), 0)


def _dispatch(xs_list, tiles, dest_steps, pstart, padded, nact, stride, n_tiles):
    steps = [x.shape[0] // tm for x, tm in zip(xs_list, tiles)]
    in_specs = [pl.BlockSpec(memory_space=pl.ANY)]
    step0 = 0
    for tm, n in zip(tiles, steps):
        in_specs.append(pl.BlockSpec((tm, D_MODEL), _seg_map(step0, n)))
        step0 += n
    grid_spec = pltpu.PrefetchScalarGridSpec(
        num_scalar_prefetch=3,
        grid=(sum(steps),),
        in_specs=in_specs,
        out_specs=pl.BlockSpec(memory_space=pl.ANY),
        scratch_shapes=[pltpu.SMEM((2 * stride,), jnp.int32),
                        pltpu.VMEM((GMM_TILE, D_MODEL), F32),
                        pltpu.SemaphoreType.DMA((3,))],
    )
    return pl.pallas_call(
        functools.partial(_dispatch_kernel, tiles=tuple(tiles), steps=tuple(steps), stride=stride,
                          n_tiles=n_tiles),
        grid_spec=grid_spec,
        out_shape=jax.ShapeDtypeStruct((n_tiles * GMM_TILE, D_MODEL), F32),
        compiler_params=_cparams(1),
        name="moe_dispatch",
    )(pstart, padded, nact, dest_steps, *xs_list)


def _gmm_kernel(te_ref, tsrc_ref, tfirst_ref, nact_ref, x_ref, wgu_ref, bgu_ref, wdn_ref, bdn_ref, y_ref,
                wgu_bf, wdn_bf):
    i = pl.program_id(0)

    @pl.when(tfirst_ref[i] == 1)
    def _():
        wgu_bf[...] = wgu_ref[...].astype(BF16)
        wdn_bf[...] = wdn_ref[...].astype(BF16)

    @pl.when(i < nact_ref[0])
    def _():
        gu = jnp.dot(x_ref[...].astype(BF16), wgu_bf[...], preferred_element_type=F32) + bgu_ref[...]
        gate = jnp.minimum(gu[:, :MOE_FF], SWIGLU_LIMIT)
        up = jnp.clip(gu[:, MOE_FF:], -SWIGLU_LIMIT, SWIGLU_LIMIT)
        glu = gate * jax.nn.sigmoid(SWIGLU_ALPHA * gate)
        h = ((up + 1.0) * glu).astype(BF16)
        y_ref[...] = jnp.dot(h, wdn_bf[...], preferred_element_type=F32) + bdn_ref[...]

    @pl.when(i >= nact_ref[0])
    def _():
        y_ref[...] = jnp.zeros_like(y_ref)


def _gmm(xs, te, tsrc, tfirst, nact, wgu, bgu, wdn, bdn, n_tiles):
    emap = lambda i, te, *_: (te[i], 0, 0)
    grid_spec = pltpu.PrefetchScalarGridSpec(
        num_scalar_prefetch=4,
        grid=(n_tiles,),
        in_specs=[pl.BlockSpec((GMM_TILE, D_MODEL), lambda i, te, ts, *_: (ts[i], 0)),
                  pl.BlockSpec((None, D_MODEL, 2 * MOE_FF), emap),
                  pl.BlockSpec((None, 1, 2 * MOE_FF), emap),
                  pl.BlockSpec((None, MOE_FF, D_MODEL), emap),
                  pl.BlockSpec((None, 1, D_MODEL), emap)],
        out_specs=pl.BlockSpec((GMM_TILE, D_MODEL), lambda i, *_: (i, 0)),
        scratch_shapes=[pltpu.VMEM((D_MODEL, 2 * MOE_FF), BF16),
                        pltpu.VMEM((MOE_FF, D_MODEL), BF16)],
    )
    return pl.pallas_call(
        _gmm_kernel,
        grid_spec=grid_spec,
        out_shape=jax.ShapeDtypeStruct((n_tiles * GMM_TILE, D_MODEL), F32),
        compiler_params=_cparams(1),
        name="moe_gmm",
    )(te, tsrc, tfirst, nact, xs, wgu, bgu, wdn, bdn)


def _combine_kernel(dest_hbm, ys_hbm, x_ref, gt_ref, g_ref, bb_ref, o_ref, dsm, buf, sems, *, tm, stride):
    i = pl.program_id(0)
    nsteps = pl.num_programs(0)

    @pl.when(i == 0)
    def _():
        _idx_copy(dest_hbm, dsm, sems.at[0], i, stride).start()

    _idx_copy(dest_hbm, dsm, sems.at[0], i, stride).wait()

    @pl.when(i + 1 < nsteps)
    def _():
        _idx_copy(dest_hbm, dsm, sems.at[0], i + 1, stride).start()

    base = (i % 2) * stride

    def row_body(r, _):
        for k in range(TOP_K):
            d = dsm[base + k * tm + r]
            pltpu.make_async_copy(ys_hbm.at[pl.ds(d, 1)], buf.at[k, pl.ds(r, 1)], sems.at[1]).start()
        return 0

    lax.fori_loop(0, tm, row_body, 0)
    for k in range(TOP_K):
        pltpu.make_async_copy(ys_hbm.at[pl.ds(0, tm)], buf.at[k], sems.at[1]).wait()

    gt = gt_ref[...]
    ff = gt[:, 0:1] * buf[0]
    for k in range(1, TOP_K):
        ff = ff + gt[:, k:k + 1] * buf[k]
    o_ref[...] = _layer_norm(DEEPNORM_ALPHA * x_ref[...] + ff, g_ref[...], bb_ref[...])


def _combine(ys, dest_steps, x, gates_t, g, bb, tm, stride):
    t = x.shape[0]
    row = lambda i: (i, 0)
    fixed = lambda i: (0, 0)
    return pl.pallas_call(
        functools.partial(_combine_kernel, tm=tm, stride=stride),
        grid=(t // tm,),
        in_specs=[pl.BlockSpec(memory_space=pl.ANY),
                  pl.BlockSpec(memory_space=pl.ANY),
                  pl.BlockSpec((tm, D_MODEL), row),
                  pl.BlockSpec((tm, TOP_K), row),
                  pl.BlockSpec((1, D_MODEL), fixed),
                  pl.BlockSpec((1, D_MODEL), fixed)],
        out_specs=pl.BlockSpec((tm, D_MODEL), row),
        out_shape=jax.ShapeDtypeStruct((t, D_MODEL), F32),
        scratch_shapes=[pltpu.SMEM((2 * stride,), jnp.int32),
                        pltpu.VMEM((TOP_K, tm, D_MODEL), F32),
                        pltpu.SemaphoreType.DMA((2,))],
        compiler_params=_cparams(1),
        name="moe_combine",
    )(dest_steps, ys, x, gates_t, g, bb)


def _moe(segs, counts, wgu, bgu, wdn, bdn, g, bb):
    t = sum(seg[0].shape[0] for seg in segs)
    n_tiles = -(-(t * TOP_K) // GMM_TILE) + N_EXPERTS
    counts = counts[:, 0]
    padded = ((counts + GMM_TILE - 1) // GMM_TILE) * GMM_TILE
    pend = jnp.cumsum(padded)
    pstart = (pend - padded).astype(jnp.int32)
    padded = padded.astype(jnp.int32)
    nact = (pend[-1] // GMM_TILE).astype(jnp.int32).reshape(1)
    tile = jnp.arange(n_tiles, dtype=jnp.int32)
    tsrc = jnp.minimum(tile, jnp.maximum(nact[0] - 1, 0))
    te = jnp.sum((pend[None, :] <= (tsrc * GMM_TILE)[:, None]).astype(jnp.int32), axis=1)
    te = jnp.minimum(te, N_EXPERTS - 1).astype(jnp.int32)
    tfirst = jnp.concatenate([jnp.ones((1,), jnp.int32), (te[1:] != te[:-1]).astype(jnp.int32)])
    eids = jnp.arange(N_EXPERTS, dtype=jnp.int32)[:, None, None]

    stride = -(-(TOP_K * max(seg[4] for seg in segs)) // 1024) * 1024
    plans = []
    for x1, ids, gates, rank, tm in segs:
        n = x1.shape[0]
        dest = jnp.sum(jnp.where(ids[None] == eids, pstart[:, None, None], 0), axis=0) + rank
        nsteps = n // tm
        dest_steps = dest.reshape(TOP_K, nsteps, tm).transpose(1, 0, 2).reshape(nsteps, TOP_K * tm)
        plans.append(jnp.pad(dest_steps, ((0, 0), (0, stride - TOP_K * tm))).reshape(-1))
    xs = _dispatch([seg[0] for seg in segs], [seg[4] for seg in segs], jnp.concatenate(plans),
                   pstart, padded, nact, stride, n_tiles)
    ys = _gmm(xs, te, tsrc, tfirst, nact, wgu, bgu, wdn, bdn, n_tiles)
    return [_combine(ys, dest_steps, x1, gates.T, g, bb, tm, stride)
            for (x1, ids, gates, rank, tm), dest_steps in zip(segs, plans)]


def _pw1_glu_kernel(x_ref, w_ref, b_ref, u_ref):
    a = jnp.dot(x_ref[...].astype(BF16), w_ref[...], preferred_element_type=F32) + b_ref[...]
    u_ref[...] = a[:, :D_MODEL] * jax.nn.sigmoid(a[:, D_MODEL:])


def _pw1_glu(x, w_bf, b, tm):
    t = x.shape[0]
    row = lambda i: (i, 0)
    fixed = lambda i: (0, 0)
    return pl.pallas_call(
        _pw1_glu_kernel,
        grid=(t // tm,),
        in_specs=[pl.BlockSpec((tm, D_MODEL), row),
                  pl.BlockSpec((D_MODEL, 2 * D_MODEL), fixed),
                  pl.BlockSpec((1, 2 * D_MODEL), fixed)],
        out_specs=pl.BlockSpec((tm, D_MODEL), row),
        out_shape=jax.ShapeDtypeStruct((t, D_MODEL), F32),
        compiler_params=_cparams(1),
        name="pw1_glu",
    )(x, w_bf, b)


def _conv_rows(win_ref, zs_ref, base, offs, w_ref, b_ref, g_ref, bb_ref):
    shift = CONV_HALO - CONV_CTX
    accs = [[None] * N_SLAB for _ in offs]
    for c in range(N_SLAB):
        bias = b_ref[c:c + 1, :]
        for j in range(CONV_WIDTH):
            w = w_ref[c, j:j + 1, :]
            for q, o in enumerate(offs):
                x = win_ref[c, pl.ds(base + (shift + j + o), SUBLANES, stride=2), :]
                accs[q][c] = (bias if j == 0 else accs[q][c]) + w * x
    for q, o in enumerate(offs):
        tot = accs[q][0]
        for c in range(1, N_SLAB):
            tot = tot + accs[q][c]
        mu = jnp.sum(tot, axis=1, keepdims=True) * (1.0 / D_MODEL)
        cen = [a - mu for a in accs[q]]
        sq = cen[0] * cen[0]
        for c in range(1, N_SLAB):
            sq = sq + cen[c] * cen[c]
        inv = lax.rsqrt(jnp.sum(sq, axis=1, keepdims=True) * (1.0 / D_MODEL) + LN_EPS)
        for c in range(N_SLAB):
            z = cen[c] * inv * g_ref[c:c + 1, :] + bb_ref[c:c + 1, :]
            zs_ref[c, pl.ds(base + o, SUBLANES, stride=2), :] = z * jax.nn.sigmoid(z)


def _conv_prompt_kernel(um_ref, up_ref, uc_ref, w_ref, b_ref, g_ref, bb_ref, z_ref, win_ref, zs_ref, *, tr):
    i = pl.program_id(1)
    gap = CONV_HALO - META_LEN

    @pl.when(i == 0)
    def _():
        for c in range(N_SLAB):
            win_ref[c, 0:gap, :] = jnp.zeros((gap, LANES), F32)
            win_ref[c, gap:CONV_HALO, :] = um_ref[:, c * LANES:(c + 1) * LANES]

    @pl.when(i > 0)
    def _():
        for c in range(N_SLAB):
            win_ref[c, 0:CONV_HALO, :] = up_ref[:, c * LANES:(c + 1) * LANES]

    for c in range(N_SLAB):
        win_ref[c, CONV_HALO:, :] = uc_ref[:, c * LANES:(c + 1) * LANES]

    def block(bi, _):
        base = pl.multiple_of(bi * CONV_BLOCK, CONV_BLOCK)
        _conv_rows(win_ref, zs_ref, base, (0, 1, 16, 17), w_ref, b_ref, g_ref, bb_ref)
        return 0

    lax.fori_loop(0, tr // CONV_BLOCK, block, 0)
    for c in range(N_SLAB):
        z_ref[:, c * LANES:(c + 1) * LANES] = zs_ref[c].astype(z_ref.dtype)


def _conv_prompt(u, u_meta, w3, b3, g3, bb3, n_batch, seq, tr):
    nt = seq // tr
    cur = lambda b, i: (b * nt + i, 0)
    prev = lambda b, i: (jnp.maximum((b * seq + i * tr) // CONV_HALO - 1, 0), 0)
    fixed2 = lambda b, i: (0, 0)
    fixed3 = lambda b, i: (0, 0, 0)
    return pl.pallas_call(
        functools.partial(_conv_prompt_kernel, tr=tr),
        grid=(n_batch, nt),
        in_specs=[pl.BlockSpec((META_LEN, D_MODEL), lambda b, i: (b, 0)),
                  pl.BlockSpec((CONV_HALO, D_MODEL), prev),
                  pl.BlockSpec((tr, D_MODEL), cur),
                  pl.BlockSpec((N_SLAB, CONV_HALO, LANES), fixed3),
                  pl.BlockSpec((N_SLAB, LANES), fixed2),
                  pl.BlockSpec((N_SLAB, LANES), fixed2),
                  pl.BlockSpec((N_SLAB, LANES), fixed2)],
        out_specs=pl.BlockSpec((tr, D_MODEL), cur),
        out_shape=jax.ShapeDtypeStruct((n_batch * seq, D_MODEL), BF16),
        scratch_shapes=[pltpu.VMEM((N_SLAB, CONV_HALO + tr, LANES), F32),
                        pltpu.VMEM((N_SLAB, tr, LANES), F32)],
        compiler_params=_cparams(2),
        name="conv_prompt",
    )(u_meta, u, u, w3, b3, g3, bb3)


SHORT_ROWS = 16


def _conv_short_kernel(ctx_ref, u_ref, w_ref, b_ref, g_ref, bb_ref, z_ref, win_ref, zs_ref, *, n_seq, t_len):
    for n in range(n_seq):
        for c in range(N_SLAB):
            sl = slice(c * LANES, (c + 1) * LANES)
            win_ref[c, 0:CONV_HALO, :] = ctx_ref[n, :, sl]
            win_ref[c, CONV_HALO:CONV_HALO + t_len, :] = u_ref[n * t_len:(n + 1) * t_len, sl]
            if t_len < SHORT_ROWS:
                win_ref[c, CONV_HALO + t_len:, :] = jnp.zeros((SHORT_ROWS - t_len, LANES), F32)
        _conv_rows(win_ref, zs_ref, 0, (0, 1), w_ref, b_ref, g_ref, bb_ref)
        for c in range(N_SLAB):
            z_ref[n * t_len:(n + 1) * t_len, c * LANES:(c + 1) * LANES] = zs_ref[c, 0:t_len, :]


def _conv_short(ctx_pad, u, row0, w3, b3, g3, bb3, t_len, n_seq):
    n_total = ctx_pad.shape[0]
    rows = n_seq * t_len
    blk0 = row0 // rows
    fixed2 = lambda n: (0, 0)
    fixed3 = lambda n: (0, 0, 0)
    return pl.pallas_call(
        functools.partial(_conv_short_kernel, n_seq=n_seq, t_len=t_len),
        grid=(n_total // n_seq,),
        in_specs=[pl.BlockSpec((n_seq, CONV_HALO, D_MODEL), lambda n: (n, 0, 0)),
                  pl.BlockSpec((rows, D_MODEL), lambda n: (blk0 + n, 0)),
                  pl.BlockSpec((N_SLAB, CONV_HALO, LANES), fixed3),
                  pl.BlockSpec((N_SLAB, LANES), fixed2),
                  pl.BlockSpec((N_SLAB, LANES), fixed2),
                  pl.BlockSpec((N_SLAB, LANES), fixed2)],
        out_specs=pl.BlockSpec((rows, D_MODEL), lambda n: (n, 0)),
        out_shape=jax.ShapeDtypeStruct((n_total * t_len, D_MODEL), F32),
        scratch_shapes=[pltpu.VMEM((N_SLAB, CONV_HALO + SHORT_ROWS, LANES), F32),
                        pltpu.VMEM((N_SLAB, SHORT_ROWS, LANES), F32)],
        compiler_params=_cparams(1),
        name="conv_short",
    )(ctx_pad, u, w3, b3, g3, bb3)


def _rope_tables(pos):
    half = ROT_DIM // 2
    inv = 1.0 / (ROPE_THETA ** (jnp.arange(0, ROT_DIM, 2, dtype=F32) / ROT_DIM))
    ang = pos.astype(F32)[:, None] * inv[None, :]
    cos, sin = jnp.cos(ang), jnp.sin(ang)
    n = pos.shape[0]
    ones = jnp.ones((n, HEAD_DIM - ROT_DIM), F32)
    zeros = jnp.zeros((n, HEAD_DIM - ROT_DIM), F32)
    zh = jnp.zeros((n, half), F32)
    c = jnp.concatenate([cos, cos, ones], axis=1)
    a = jnp.concatenate([-sin, zh, zeros], axis=1)
    s = jnp.concatenate([zh, sin, zeros], axis=1)
    rep = LANES // HEAD_DIM
    return jnp.tile(c, (1, rep)), jnp.tile(a, (1, rep)), jnp.tile(s, (1, rep))


def kernel(x_prompt, x_sample, cache_attn_meta_k, cache_attn_meta_v, cache_attn_win_k, cache_attn_win_v, state_conv, meta_tokens, attn_w_qkv, attn_b_qkv, attn_sinks, attn_w_o, attn_b_o, conv_w_pw1, conv_b_pw1, conv_w_dw, conv_b_dw, conv_ln_g, conv_ln_b, conv_w_pw2, conv_b_pw2, ln_mix_g, ln_mix_b, ln_ffn_g, ln_ffn_b, moe_w_router, moe_b_router, moe_w_gate_up, moe_b_gate_up, moe_w_down, moe_b_down):
    n_batch, seq, _ = x_prompt.shape
    dec_batch, dec_seq, _ = x_sample.shape
    n_real = n_batch * seq
    n_meta = n_batch * META_LEN
    n_samp = dec_batch * dec_seq
    n_small = n_meta + n_samp
    tb = _pick_tile(seq, 512)
    ts = _pick_tile(n_small, 512)
    row2 = lambda v: v.reshape(1, -1)

    meta_rows = jnp.broadcast_to(meta_tokens[None], (n_batch, META_LEN, D_MODEL)).reshape(n_meta, D_MODEL)
    xb = x_prompt.reshape(n_real, D_MODEL)
    xs = jnp.concatenate([meta_rows.astype(F32), x_sample.reshape(n_samp, D_MODEL)], axis=0)

    def moe_layer(i, a_big, a_small, w, b, xb, xs):
        w_bf = w.astype(BF16)
        wr = moe_w_router[i].T.astype(BF16)
        br = moe_b_router[i].reshape(N_EXPERTS, 1)
        lng, lnb = row2(ln_mix_g[i]), row2(ln_mix_b[i])
        zero_counts = jnp.zeros((N_EXPERTS, LANES), jnp.int32)
        x1b, idb, gab, rab, cnt = _proj_ln_route(a_big, w_bf, row2(b), xb, lng, lnb, wr, br, zero_counts, tb)
        x1s, ids_, gas, ras, cnt = _proj_ln_route(a_small, w_bf, row2(b), xs, lng, lnb, wr, br, cnt, ts)
        return _moe([(x1b, idb, gab, rab, tb), (x1s, ids_, gas, ras, ts)], cnt,
                    moe_w_gate_up[i], moe_b_gate_up[i].reshape(N_EXPERTS, 1, 2 * MOE_FF),
                    moe_w_down[i], moe_b_down[i].reshape(N_EXPERTS, 1, D_MODEL),
                    row2(ln_ffn_g[i]), row2(ln_ffn_b[i]))

    w_qkv = attn_w_qkv[0].astype(BF16)
    b_qkv = row2(attn_b_qkv[0])
    pos_small = jnp.concatenate([jnp.tile(jnp.arange(META_LEN), n_batch),
                                 jnp.tile(PAST_LEN + jnp.arange(dec_seq), dec_batch)])
    qb, kb, vb = _qkv_rope(xb, w_qkv, b_qkv, *_rope_tables(META_LEN + jnp.arange(seq)), tb)
    qs, ks, vs = _qkv_rope(xs, w_qkv, b_qkv, *_rope_tables(pos_small), ts)
    sinks = attn_sinks[0]
    cmk = cache_attn_meta_k[0].reshape(dec_batch, META_LEN, KV_DIM)
    cmv = cache_attn_meta_v[0].reshape(dec_batch, META_LEN, KV_DIM)
    cwk = cache_attn_win_k[0].reshape(dec_batch, WINDOW, KV_DIM)
    cwv = cache_attn_win_v[0].reshape(dec_batch, WINDOW, KV_DIM)
    o_big = _attn_prompt(qb, kb, vb, ks, vs, sinks, n_batch, seq)
    o_small = jnp.concatenate([
        _attn_meta(qs, ks, vs, sinks, n_batch),
        _attn_sample(qs, ks, vs, cmk, cmv, cwk, cwv, sinks, n_meta, dec_batch, dec_seq)], axis=0)
    xb, xs = moe_layer(0, o_big, o_small, attn_w_o[0], attn_b_o[0], xb, xs)

    kv4 = lambda a, n, t_len: a.reshape(n, t_len, N_KV_HEADS, HEAD_DIM)
    p_meta_k = kv4(ks[:n_meta], n_batch, META_LEN)[None]
    p_meta_v = kv4(vs[:n_meta], n_batch, META_LEN)[None]
    p_win_k = kv4(kb, n_batch, seq)[:, seq - WINDOW:][None]
    p_win_v = kv4(vb, n_batch, seq)[:, seq - WINDOW:][None]
    k_new = kv4(ks[n_meta:], dec_batch, dec_seq)
    v_new = kv4(vs[n_meta:], dec_batch, dec_seq)
    s_win_k = jnp.concatenate([cache_attn_win_k[0], k_new], axis=1)[:, -WINDOW:][None]
    s_win_v = jnp.concatenate([cache_attn_win_v[0], v_new], axis=1)[:, -WINDOW:][None]

    w_pw1 = conv_w_pw1[0].astype(BF16)
    ub = _pw1_glu(xb, w_pw1, row2(conv_b_pw1[0]), tb)
    us = _pw1_glu(xs, w_pw1, row2(conv_b_pw1[0]), ts)
    slab = lambda v: v.reshape(N_SLAB, LANES)
    w_dw = jnp.pad(conv_w_dw[0], ((0, CONV_HALO - CONV_WIDTH), (0, 0)))
    w3 = w_dw.reshape(CONV_HALO, N_SLAB, LANES).transpose(1, 0, 2)
    conv_args = (w3, slab(conv_b_dw[0]), slab(conv_ln_g[0]), slab(conv_ln_b[0]))
    z_big = _conv_prompt(ub, us, *conv_args, n_batch, seq, tb)
    z_meta = _conv_short(jnp.zeros((n_batch, CONV_HALO, D_MODEL), F32), us, 0, *conv_args, META_LEN, 1)
    ctx_pad = jnp.pad(state_conv[0], ((0, 0), (CONV_HALO - CONV_CTX, 0), (0, 0)))
    z_samp = _conv_short(ctx_pad, us, n_meta, *conv_args, dec_seq, 2)
    z_small = jnp.concatenate([z_meta, z_samp], axis=0).astype(BF16)
    xb, xs = moe_layer(1, z_big, z_small, conv_w_pw2[0], conv_b_pw2[0], xb, xs)

    p_conv = ub.reshape(n_batch, seq, D_MODEL)[:, seq - CONV_CTX:][None]
    u_samp = us[n_meta:].reshape(dec_batch, dec_seq, D_MODEL)
    s_conv = jnp.concatenate([state_conv[0], u_samp], axis=1)[:, -CONV_CTX:][None]

    y_prompt = xb.reshape(n_batch, seq, D_MODEL)
    y_sample = xs[n_meta:].reshape(dec_batch, dec_seq, D_MODEL)
    return (y_prompt, y_sample, p_meta_k, p_meta_v, p_win_k, p_win_v, p_conv,
            s_win_k, s_win_v, s_conv)
```

```python
import functools

import jax
import jax.numpy as jnp
from jax import lax
from jax.experimental import pallas as pl
from jax.experimental.pallas import tpu as pltpu

F32 = jnp.float32
BF16 = jnp.bfloat16

D_MODEL = 1024
HEAD_DIM = 64
N_HEADS = 16
N_KV_HEADS = 4
GROUP = N_HEADS // N_KV_HEADS
KV_DIM = N_KV_HEADS * HEAD_DIM
QKV_DIM = D_MODEL + 2 * KV_DIM
ROT_DIM = 16
ROPE_THETA = 500000.0
WINDOW = 128
ATTN_BLOCK = 128
ATTN_SCALE = HEAD_DIM ** -0.5
META_LEN = 16
CONV_WIDTH = 31
CONV_CTX = CONV_WIDTH - 1
N_EXPERTS = 32
TOP_K = 4
MOE_FF = 1024
SWIGLU_LIMIT = 7.0
SWIGLU_ALPHA = 1.702
LN_EPS = 1e-5
DEPTH = 2
DEEPNORM_ALPHA = (2 * DEPTH) ** 0.25
PAST_LEN = 16384
NEG_INF = -1e30

LANES = 128
SUBLANES = 8
GMM_TILE = 256
CONV_HALO = 32
CONV_BLOCK = 32
N_SLAB = D_MODEL // LANES
COMBINE_ROWS = 32
VMEM_LIMIT = 56 * 1024 * 1024


def _cparams(n_axes):
    return pltpu.CompilerParams(dimension_semantics=("arbitrary",) * n_axes,
                                vmem_limit_bytes=VMEM_LIMIT)


def _pick_tile(n, cap):
    best = None
    t = LANES
    while t <= cap:
        if n % t == 0:
            best = t
        t += LANES
    assert best is not None, n
    return best


def _layer_norm(h, g, b):
    mu = jnp.mean(h, axis=-1, keepdims=True)
    hc = h - mu
    var = jnp.mean(hc * hc, axis=-1, keepdims=True)
    return hc * lax.rsqrt(var + LN_EPS) * g + b


def _qkv_kernel(x_ref, w_ref, b_ref, c_ref, a_ref, s_ref, q_ref, k_ref, v_ref):
    x = x_ref[...].astype(BF16)
    acc = jnp.dot(x, w_ref[...], preferred_element_type=F32) + b_ref[...]
    c = c_ref[...]
    a = a_ref[...]
    s = s_ref[...]

    def rope(t):
        return t * c + pltpu.roll(t, LANES - ROT_DIM // 2, 1) * a + pltpu.roll(t, ROT_DIM // 2, 1) * s

    for j in range(D_MODEL // LANES):
        sl = slice(j * LANES, (j + 1) * LANES)
        q_ref[:, sl] = (rope(acc[:, sl]) * ATTN_SCALE).astype(BF16)
    for j in range(KV_DIM // LANES):
        sl = slice(D_MODEL + j * LANES, D_MODEL + (j + 1) * LANES)
        k_ref[:, j * LANES:(j + 1) * LANES] = rope(acc[:, sl])
    v_ref[...] = acc[:, D_MODEL + KV_DIM:]


def _qkv_rope(x, w_bf, b, cos_t, sa_t, sb_t, tm):
    t = x.shape[0]
    period = cos_t.shape[0] // tm
    row = lambda i: (i, 0)
    tab = lambda i: (i % period, 0)
    fixed = lambda i: (0, 0)
    return pl.pallas_call(
        _qkv_kernel,
        grid=(t // tm,),
        in_specs=[pl.BlockSpec((tm, D_MODEL), row),
                  pl.BlockSpec((D_MODEL, QKV_DIM), fixed),
                  pl.BlockSpec((1, QKV_DIM), fixed),
                  pl.BlockSpec((tm, LANES), tab),
                  pl.BlockSpec((tm, LANES), tab),
                  pl.BlockSpec((tm, LANES), tab)],
        out_specs=[pl.BlockSpec((tm, D_MODEL), row),
                   pl.BlockSpec((tm, KV_DIM), row),
                   pl.BlockSpec((tm, KV_DIM), row)],
        out_shape=[jax.ShapeDtypeStruct((t, D_MODEL), BF16),
                   jax.ShapeDtypeStruct((t, KV_DIM), F32),
                   jax.ShapeDtypeStruct((t, KV_DIM), F32)],
        compiler_params=_cparams(1),
        name="qkv_rope",
    )(x, w_bf, b, cos_t, sa_t, sb_t)


def _attend(q, kcat, vcat, mask, sink_ref, write):
    for h in range(N_HEADS):
        g = h // GROUP
        qh = q[:, h * HEAD_DIM:(h + 1) * HEAD_DIM]
        kg = kcat[:, g * HEAD_DIM:(g + 1) * HEAD_DIM]
        vg = vcat[:, g * HEAD_DIM:(g + 1) * HEAD_DIM]
        s = lax.dot_general(qh, kg, (((1,), (1,)), ((), ())), preferred_element_type=F32)
        s = jnp.where(mask, s, NEG_INF)
        sink = sink_ref[h]
        m = jnp.maximum(jnp.max(s, axis=-1, keepdims=True), sink)
        p = jnp.exp(s - m)
        denom = jnp.sum(p, axis=-1, keepdims=True) + jnp.exp(sink - m)
        o = jnp.dot(p.astype(BF16), vg, preferred_element_type=F32)
        write(h, o / denom)


def _attn_prompt_kernel(sink_ref, q_ref, km_ref, kp_ref, ko_ref, vm_ref, vp_ref, vo_ref, o_ref):
    i = pl.program_id(1)
    kcat = jnp.concatenate([km_ref[...], kp_ref[...], ko_ref[...]], axis=0).astype(BF16)
    vcat = jnp.concatenate([vm_ref[...], vp_ref[...], vo_ref[...]], axis=0).astype(BF16)
    nk = META_LEN + 2 * ATTN_BLOCK
    r = lax.broadcasted_iota(jnp.int32, (ATTN_BLOCK, nk), 0)
    c = lax.broadcasted_iota(jnp.int32, (ATTN_BLOCK, nk), 1)
    first = jnp.where(i > 0, 0, 2 * ATTN_BLOCK)
    prev_ok = (c >= META_LEN + first) & (c < META_LEN + ATTN_BLOCK) & (c - META_LEN >= r)
    own_ok = (c >= META_LEN + ATTN_BLOCK) & (c - (META_LEN + ATTN_BLOCK) <= r)
    mask = (c < META_LEN) | prev_ok | own_ok

    def write(h, o):
        o_ref[:, h * HEAD_DIM:(h + 1) * HEAD_DIM] = o.astype(o_ref.dtype)

    _attend(q_ref[...], kcat, vcat, mask, sink_ref, write)


def _attn_prompt(q, k, v, k_meta, v_meta, sinks, n_batch, seq):
    nblk = seq // ATTN_BLOCK
    qmap = lambda b, i: (b * nblk + i, 0)
    pmap = lambda b, i: (b * nblk + jnp.maximum(i - 1, 0), 0)
    mmap = lambda b, i: (b, 0)
    kspec = lambda m: pl.BlockSpec((ATTN_BLOCK, KV_DIM), m)
    mspec = pl.BlockSpec((META_LEN, KV_DIM), mmap)
    return pl.pallas_call(
        _attn_prompt_kernel,
        grid=(n_batch, nblk),
        in_specs=[pl.BlockSpec(memory_space=pltpu.SMEM),
                  pl.BlockSpec((ATTN_BLOCK, D_MODEL), qmap),
                  mspec, kspec(pmap), kspec(qmap),
                  mspec, kspec(pmap), kspec(qmap)],
        out_specs=pl.BlockSpec((ATTN_BLOCK, D_MODEL), qmap),
        out_shape=jax.ShapeDtypeStruct((n_batch * seq, D_MODEL), BF16),
        compiler_params=_cparams(2),
        name="attn_prompt",
    )(sinks, q, k_meta, k, k, v_meta, v, v)


def _attn_meta_kernel(sink_ref, q_ref, k_ref, v_ref, o_ref):
    r = lax.broadcasted_iota(jnp.int32, (META_LEN, META_LEN), 0)
    c = lax.broadcasted_iota(jnp.int32, (META_LEN, META_LEN), 1)

    def write(h, o):
        o_ref[:, h * HEAD_DIM:(h + 1) * HEAD_DIM] = o.astype(o_ref.dtype)

    _attend(q_ref[...], k_ref[...].astype(BF16), v_ref[...].astype(BF16), c <= r, sink_ref, write)


def _attn_meta(q, k, v, sinks, n_batch):
    imap = lambda b: (b, 0)
    return pl.pallas_call(
        _attn_meta_kernel,
        grid=(n_batch,),
        in_specs=[pl.BlockSpec(memory_space=pltpu.SMEM),
                  pl.BlockSpec((META_LEN, D_MODEL), imap),
                  pl.BlockSpec((META_LEN, KV_DIM), imap),
                  pl.BlockSpec((META_LEN, KV_DIM), imap)],
        out_specs=pl.BlockSpec((META_LEN, D_MODEL), lambda b: (b, 0)),
        out_shape=jax.ShapeDtypeStruct((n_batch * META_LEN, D_MODEL), BF16),
        compiler_params=_cparams(1),
        name="attn_meta",
    )(sinks, q, k, v)


SAMPLE_GROUP = 2


def _attn_sample_kernel(sink_ref, q_ref, kn_ref, vn_ref, cmk_ref, cmv_ref, cwk_ref, cwv_ref,
                        o_ref, acc_ref, *, dec_seq):
    nq = SAMPLE_GROUP * dec_seq
    nk = META_LEN + WINDOW + nq
    q = q_ref[...]
    kn = kn_ref[...]
    vn = vn_ref[...]
    r = lax.broadcasted_iota(jnp.int32, (nq, nk), 0)
    c = lax.broadcasted_iota(jnp.int32, (nq, nk), 1)
    acc_ref[...] = jnp.zeros_like(acc_ref)

    def write(h, o):
        acc_ref[:, h * HEAD_DIM:(h + 1) * HEAD_DIM] += o

    for j in range(SAMPLE_GROUP):
        kcat = jnp.concatenate([cmk_ref[j], cwk_ref[j], kn], axis=0).astype(BF16)
        vcat = jnp.concatenate([cmv_ref[j], cwv_ref[j], vn], axis=0).astype(BF16)
        tq = r - j * dec_seq
        mine = (tq >= 0) & (tq < dec_seq)
        cn = c - (META_LEN + WINDOW) - j * dec_seq
        win_ok = (c >= META_LEN) & (c < META_LEN + WINDOW) & (c - META_LEN >= tq)
        new_ok = (cn >= 0) & (cn <= tq)
        vis = (c < META_LEN) | win_ok | new_ok
        _attend(q, kcat, vcat, mine & vis, sink_ref, write)
    o_ref[...] = acc_ref[...].astype(o_ref.dtype)


def _attn_sample(q, k, v, cmk, cmv, cwk, cwv, sinks, row0, dec_batch, dec_seq):
    nq = SAMPLE_GROUP * dec_seq
    blk0 = row0 // nq
    qmap = lambda n: (blk0 + n, 0)
    cmap = lambda n: (n, 0, 0)
    return pl.pallas_call(
        functools.partial(_attn_sample_kernel, dec_seq=dec_seq),
        grid=(dec_batch // SAMPLE_GROUP,),
        in_specs=[pl.BlockSpec(memory_space=pltpu.SMEM),
                  pl.BlockSpec((nq, D_MODEL), qmap),
                  pl.BlockSpec((nq, KV_DIM), qmap),
                  pl.BlockSpec((nq, KV_DIM), qmap),
                  pl.BlockSpec((SAMPLE_GROUP, META_LEN, KV_DIM), cmap),
                  pl.BlockSpec((SAMPLE_GROUP, META_LEN, KV_DIM), cmap),
                  pl.BlockSpec((SAMPLE_GROUP, WINDOW, KV_DIM), cmap),
                  pl.BlockSpec((SAMPLE_GROUP, WINDOW, KV_DIM), cmap)],
        out_specs=pl.BlockSpec((nq, D_MODEL), lambda n: (n, 0)),
        out_shape=jax.ShapeDtypeStruct((dec_batch * dec_seq, D_MODEL), BF16),
        scratch_shapes=[pltpu.VMEM((nq, D_MODEL), F32)],
        compiler_params=_cparams(1),
        name="attn_sample",
    )(sinks, q, k, v, cmk, cmv, cwk, cwv)


def _proj_ln_route_kernel(a_ref, w_ref, b_ref, x_ref, g_ref, bb_ref, wr_ref, br_ref, tri_ref, cin_ref,
                          x1_ref, ids_ref, gates_ref, rank_ref, cnt_ref, carry_ref):
    @pl.when(pl.program_id(0) == 0)
    def _():
        carry_ref[...] = cin_ref[...].astype(F32)

    y = jnp.dot(a_ref[...], w_ref[...], preferred_element_type=F32) + b_ref[...]
    x1 = _layer_norm(DEEPNORM_ALPHA * x_ref[...] + y, g_ref[...], bb_ref[...])
    for j in range(N_SLAB):
        x1_ref[pl.ds(j, x1.shape[0], stride=N_SLAB), :] = x1[:, j * LANES:(j + 1) * LANES]

    logits = lax.dot_general(wr_ref[...], x1.astype(BF16), (((1,), (1,)), ((), ())),
                             preferred_element_type=F32) + br_ref[...]
    tm = logits.shape[1]
    eidx = lax.broadcasted_iota(jnp.int32, (N_EXPERTS, tm), 0).astype(F32)
    cur = logits
    vals, idxs, sels = [], [], []
    for _ in range(TOP_K):
        m = jnp.max(cur, axis=0, keepdims=True)
        idx = jnp.min(jnp.where(cur == m, eidx, float(N_EXPERTS)), axis=0, keepdims=True)
        sel = eidx == idx
        vals.append(m)
        idxs.append(idx)
        sels.append(sel)
        cur = jnp.where(sel, -jnp.inf, cur)
    exps = [jnp.exp(v - vals[0]) for v in vals]
    tot = exps[0] + exps[1] + exps[2] + exps[3]
    gates_ref[...] = jnp.concatenate([e / tot for e in exps], axis=0)
    ids_ref[...] = jnp.concatenate(idxs, axis=0).astype(jnp.int32)

    chosen = jnp.where(sels[0] | sels[1] | sels[2] | sels[3], 1.0, 0.0)
    before = jnp.dot(chosen.astype(BF16), tri_ref[...], preferred_element_type=F32)
    before = before + carry_ref[:, 0:1]
    ranks = [jnp.sum(jnp.where(s, before, 0.0), axis=0, keepdims=True) for s in sels]
    rank_ref[...] = jnp.concatenate(ranks, axis=0).astype(jnp.int32)
    carry_ref[...] = carry_ref[...] + jnp.sum(chosen, axis=1, keepdims=True)
    cnt_ref[...] = carry_ref[...].astype(jnp.int32)


def _proj_ln_route(a_bf, w_bf, b, x, g, bb, wr_t_bf, br_col, counts_in, tm):
    t = x.shape[0]
    row = lambda i: (i, 0)
    col = lambda i: (0, i)
    fixed = lambda i: (0, 0)
    tri = (jnp.arange(tm)[:, None] < jnp.arange(tm)[None, :]).astype(BF16)
    return pl.pallas_call(
        _proj_ln_route_kernel,
        grid=(t // tm,),
        in_specs=[pl.BlockSpec((tm, D_MODEL), row),
                  pl.BlockSpec((D_MODEL, D_MODEL), fixed),
                  pl.BlockSpec((1, D_MODEL), fixed),
                  pl.BlockSpec((tm, D_MODEL), row),
                  pl.BlockSpec((1, D_MODEL), fixed),
                  pl.BlockSpec((1, D_MODEL), fixed),
                  pl.BlockSpec((N_EXPERTS, D_MODEL), fixed),
                  pl.BlockSpec((N_EXPERTS, 1), fixed),
                  pl.BlockSpec((tm, tm), fixed),
                  pl.BlockSpec((N_EXPERTS, LANES), fixed)],
        out_specs=[pl.BlockSpec((tm * N_SLAB, LANES), row),
                   pl.BlockSpec((TOP_K, tm), col),
                   pl.BlockSpec((TOP_K, tm), col),
                   pl.BlockSpec((TOP_K, tm), col),
                   pl.BlockSpec((N_EXPERTS, LANES), fixed)],
        out_shape=[jax.ShapeDtypeStruct((t * N_SLAB, LANES), F32),
                   jax.ShapeDtypeStruct((TOP_K, t), jnp.int32),
                   jax.ShapeDtypeStruct((TOP_K, t), F32),
                   jax.ShapeDtypeStruct((TOP_K, t), jnp.int32),
                   jax.ShapeDtypeStruct((N_EXPERTS, LANES), jnp.int32)],
        scratch_shapes=[pltpu.VMEM((N_EXPERTS, LANES), F32)],
        compiler_params=_cparams(1),
        name="proj_ln_route",
    )(a_bf, w_bf, b, x, g, bb, wr_t_bf, br_col, tri, counts_in)


def _idx_copy(dest_hbm, dsm, sem, step, stride):
    slot = step % 2
    return pltpu.make_async_copy(dest_hbm.at[pl.ds(pl.multiple_of(step * stride, stride), stride)],
                                 dsm.at[pl.ds(pl.multiple_of(slot * stride, stride), stride)],
                                 sem)


def _dispatch_kernel(pstart_ref, padded_ref, nact_ref, dest_hbm, *rest, tiles, steps, stride, n_tiles):
    x_refs = rest[:len(tiles)]
    xs_hbm, dsm, zbuf, sems = rest[len(tiles):]
    i = pl.program_id(0)
    nsteps = pl.num_programs(0)

    def zero_tile(row0):
        n = GMM_TILE * N_SLAB
        return pltpu.make_async_copy(zbuf, xs_hbm.at[pl.ds(pl.multiple_of(row0 * N_SLAB, n), n)], sems.at[2])

    def token(ref, r):
        return ref.at[pl.ds(pl.multiple_of(r * N_SLAB, N_SLAB), N_SLAB)]

    @pl.when(i == 0)
    def _():
        zbuf[...] = jnp.zeros_like(zbuf)

        def each_expert(fn):
            def body(e, _):
                @pl.when(padded_ref[e] > 0)
                def _():
                    fn(zero_tile(pstart_ref[e] + padded_ref[e] - GMM_TILE))
                return 0
            lax.fori_loop(0, N_EXPERTS, body, 0)

        def each_tail(fn):
            def body(t, _):
                fn(zero_tile(t * GMM_TILE))
                return 0
            lax.fori_loop(nact_ref[0], n_tiles, body, 0)

        each_expert(lambda cp: cp.start())
        each_tail(lambda cp: cp.start())
        each_expert(lambda cp: cp.wait())
        each_tail(lambda cp: cp.wait())
        _idx_copy(dest_hbm, dsm, sems.at[0], i, stride).start()

    _idx_copy(dest_hbm, dsm, sems.at[0], i, stride).wait()

    @pl.when(i + 1 < nsteps)
    def _():
        _idx_copy(dest_hbm, dsm, sems.at[0], i + 1, stride).start()

    base = (i % 2) * stride
    step0 = 0
    for x_ref, tm, n in zip(x_refs, tiles, steps):
        @pl.when((i >= step0) & (i < step0 + n))
        def _(x_ref=x_ref, tm=tm):
            def row_body(r, _):
                for k in range(TOP_K):
                    d = dsm[base + k * tm + r]
                    pltpu.make_async_copy(token(x_ref, r), token(xs_hbm, d), sems.at[1]).start(priority=k % 2)
                return 0

            lax.fori_loop(0, tm, row_body, 0, unroll=2)
            for k in range(TOP_K):
                pltpu.make_async_copy(x_ref, xs_hbm.at[pl.ds(0, tm * N_SLAB)], sems.at[1]).wait()
        step0 += n


def _seg_map(step0, n):
    return lambda i, *_: (jnp.clip(i - step0, 0, n - 1), 0)


def _dispatch(xs_list, tiles, dest_steps, pstart, padded, nact, stride, n_tiles):
    steps = [x.shape[0] // (tm * N_SLAB) for x, tm in zip(xs_list, tiles)]
    in_specs = [pl.BlockSpec(memory_space=pl.ANY)]
    step0 = 0
    for tm, n in zip(tiles, steps):
        in_specs.append(pl.BlockSpec((tm * N_SLAB, LANES), _seg_map(step0, n)))
        step0 += n
    grid_spec = pltpu.PrefetchScalarGridSpec(
        num_scalar_prefetch=3,
        grid=(sum(steps),),
        in_specs=in_specs,
        out_specs=pl.BlockSpec(memory_space=pl.ANY),
        scratch_shapes=[pltpu.SMEM((2 * stride,), jnp.int32),
                        pltpu.VMEM((GMM_TILE * N_SLAB, LANES), F32),
                        pltpu.SemaphoreType.DMA((3,))],
    )
    return pl.pallas_call(
        functools.partial(_dispatch_kernel, tiles=tuple(tiles), steps=tuple(steps), stride=stride,
                          n_tiles=n_tiles),
        grid_spec=grid_spec,
        out_shape=jax.ShapeDtypeStruct((n_tiles * GMM_TILE * N_SLAB, LANES), F32),
        compiler_params=_cparams(1),
        name="moe_dispatch",
    )(pstart, padded, nact, dest_steps, *xs_list)


def _gmm_kernel(te_ref, tsrc_ref, tfirst_ref, nact_ref, x_ref, wgu_ref, bgu_ref, wdn_ref, bdn_ref, y_ref,
                wgu_bf, wdn_bf):
    i = pl.program_id(0)

    @pl.when(tfirst_ref[i] == 1)
    def _():
        wgu_bf[...] = wgu_ref[...].astype(BF16)
        wdn_bf[...] = wdn_ref[...].astype(BF16)

    @pl.when(i < nact_ref[0])
    def _():
        x = jnp.concatenate([x_ref[pl.ds(j, GMM_TILE, stride=N_SLAB), :] for j in range(N_SLAB)], axis=1)
        gu = jnp.dot(x.astype(BF16), wgu_bf[...], preferred_element_type=F32) + bgu_ref[...]
        gate = jnp.minimum(gu[:, :MOE_FF], SWIGLU_LIMIT)
        up = jnp.clip(gu[:, MOE_FF:], -SWIGLU_LIMIT, SWIGLU_LIMIT)
        glu = gate * jax.nn.sigmoid(SWIGLU_ALPHA * gate)
        h = ((up + 1.0) * glu).astype(BF16)
        y = jnp.dot(h, wdn_bf[...], preferred_element_type=F32) + bdn_ref[...]
        for j in range(N_SLAB):
            y_ref[pl.ds(j, GMM_TILE, stride=N_SLAB), :] = y[:, j * LANES:(j + 1) * LANES]

    @pl.when(i >= nact_ref[0])
    def _():
        y_ref[...] = jnp.zeros_like(y_ref)


def _gmm(xs, te, tsrc, tfirst, nact, wgu, bgu, wdn, bdn, n_tiles):
    emap = lambda i, te, *_: (te[i], 0, 0)
    grid_spec = pltpu.PrefetchScalarGridSpec(
        num_scalar_prefetch=4,
        grid=(n_tiles,),
        in_specs=[pl.BlockSpec((GMM_TILE * N_SLAB, LANES), lambda i, te, ts, *_: (ts[i], 0)),
                  pl.BlockSpec((None, D_MODEL, 2 * MOE_FF), emap),
                  pl.BlockSpec((None, 1, 2 * MOE_FF), emap),
                  pl.BlockSpec((None, MOE_FF, D_MODEL), emap),
                  pl.BlockSpec((None, 1, D_MODEL), emap)],
        out_specs=pl.BlockSpec((GMM_TILE * N_SLAB, LANES), lambda i, *_: (i, 0)),
        scratch_shapes=[pltpu.VMEM((D_MODEL, 2 * MOE_FF), BF16),
                        pltpu.VMEM((MOE_FF, D_MODEL), BF16)],
    )
    return pl.pallas_call(
        _gmm_kernel,
        grid_spec=grid_spec,
        out_shape=jax.ShapeDtypeStruct((n_tiles * GMM_TILE * N_SLAB, LANES), F32),
        compiler_params=_cparams(1),
        name="moe_gmm",
    )(te, tsrc, tfirst, nact, xs, wgu, bgu, wdn, bdn)


def _combine_kernel(dest_hbm, ys_hbm, x_ref, gt_ref, g_ref, bb_ref, o_ref, dsm, buf, sems, *, tm, stride):
    i = pl.program_id(0)
    nsteps = pl.num_programs(0)

    @pl.when(i == 0)
    def _():
        _idx_copy(dest_hbm, dsm, sems.at[0], i, stride).start()

    _idx_copy(dest_hbm, dsm, sems.at[0], i, stride).wait()

    @pl.when(i + 1 < nsteps)
    def _():
        _idx_copy(dest_hbm, dsm, sems.at[0], i + 1, stride).start()

    base = (i % 2) * stride

    def token(ref, r):
        return ref.at[pl.ds(pl.multiple_of(r * N_SLAB, N_SLAB), N_SLAB)]

    def row_body(r, _):
        for k in range(TOP_K):
            d = dsm[base + k * tm + r]
            pltpu.make_async_copy(token(ys_hbm, d), token(buf.at[k], r), sems.at[1]).start(priority=k % 2)
        return 0

    lax.fori_loop(0, tm, row_body, 0, unroll=2)
    for k in range(TOP_K):
        pltpu.make_async_copy(ys_hbm.at[pl.ds(0, tm * N_SLAB)], buf.at[k], sems.at[1]).wait()

    def block(b, _):
        r0 = pl.multiple_of(b * COMBINE_ROWS, COMBINE_ROWS)
        gt = gt_ref[pl.ds(r0, COMBINE_ROWS), :]
        gk = [jnp.broadcast_to(gt[:, k:k + 1], (COMBINE_ROWS, LANES)) for k in range(TOP_K)]
        hs = []
        for j in range(N_SLAB):
            rows = pl.ds(r0 * N_SLAB + j, COMBINE_ROWS, stride=N_SLAB)
            h = DEEPNORM_ALPHA * x_ref[rows, :]
            for k in range(TOP_K):
                h = h + gk[k] * buf[k, rows, :]
            hs.append(h)
        tot = hs[0]
        for j in range(1, N_SLAB):
            tot = tot + hs[j]
        mu = jnp.sum(tot, axis=1, keepdims=True) * (1.0 / D_MODEL)
        cen = [h - mu for h in hs]
        sq = cen[0] * cen[0]
        for j in range(1, N_SLAB):
            sq = sq + cen[j] * cen[j]
        inv = lax.rsqrt(jnp.sum(sq, axis=1, keepdims=True) * (1.0 / D_MODEL) + LN_EPS)
        for j in range(N_SLAB):
            o_ref[pl.ds(r0, COMBINE_ROWS), j * LANES:(j + 1) * LANES] = (
                cen[j] * inv * g_ref[j:j + 1, :] + bb_ref[j:j + 1, :])
        return 0

    lax.fori_loop(0, tm // COMBINE_ROWS, block, 0)


def _combine(ys, dest_steps, x_tiles, gates_t, g3, bb3, tm, stride):
    t = x_tiles.shape[0] // N_SLAB
    row = lambda i: (i, 0)
    fixed = lambda i: (0, 0)
    return pl.pallas_call(
        functools.partial(_combine_kernel, tm=tm, stride=stride),
        grid=(t // tm,),
        in_specs=[pl.BlockSpec(memory_space=pl.ANY),
                  pl.BlockSpec(memory_space=pl.ANY),
                  pl.BlockSpec((tm * N_SLAB, LANES), row),
                  pl.BlockSpec((tm, TOP_K), row),
                  pl.BlockSpec((N_SLAB, LANES), fixed),
                  pl.BlockSpec((N_SLAB, LANES), fixed)],
        out_specs=pl.BlockSpec((tm, D_MODEL), row),
        out_shape=jax.ShapeDtypeStruct((t, D_MODEL), F32),
        scratch_shapes=[pltpu.SMEM((2 * stride,), jnp.int32),
                        pltpu.VMEM((TOP_K, tm * N_SLAB, LANES), F32),
                        pltpu.SemaphoreType.DMA((2,))],
        compiler_params=_cparams(1),
        name="moe_combine",
    )(dest_steps, ys, x_tiles, gates_t, g3, bb3)


def _moe(segs, counts, expert0, wgu, bgu, wdn, bdn, g, bb):
    t = sum(seg[1].shape[1] for seg in segs)
    n_tiles = -(-(t * TOP_K) // GMM_TILE) + N_EXPERTS
    counts = counts[:, 0]
    padded = ((counts + GMM_TILE - 1) // GMM_TILE) * GMM_TILE
    pend = jnp.cumsum(padded)
    pstart = (pend - padded).astype(jnp.int32)
    padded = padded.astype(jnp.int32)
    nact = (pend[-1] // GMM_TILE).astype(jnp.int32).reshape(1)
    tile = jnp.arange(n_tiles, dtype=jnp.int32)
    tsrc = jnp.minimum(tile, jnp.maximum(nact[0] - 1, 0))
    te = jnp.sum((pend[None, :] <= (tsrc * GMM_TILE)[:, None]).astype(jnp.int32), axis=1)
    te = jnp.minimum(te, N_EXPERTS - 1).astype(jnp.int32)
    tfirst = jnp.concatenate([jnp.ones((1,), jnp.int32), (te[1:] != te[:-1]).astype(jnp.int32)])
    eids = jnp.arange(N_EXPERTS, dtype=jnp.int32)[:, None, None]

    stride = -(-(TOP_K * max(seg[4] for seg in segs)) // 1024) * 1024
    plans = []
    for x1, ids, gates, rank, tm in segs:
        n = ids.shape[1]
        dest =jnp.sum(jnp.where(ids[None] == eids, pstart[:, None, None], 0), axis=0) + rank
        nsteps = n // tm
        dest_steps = dest.reshape(TOP_K, nsteps, tm).transpose(1, 0, 2).reshape(nsteps, TOP_K * tm)
        plans.append(jnp.pad(dest_steps, ((0, 0), (0, stride - TOP_K * tm))).reshape(-1))
    xs = _dispatch([seg[0] for seg in segs], [seg[4] for seg in segs], jnp.concatenate(plans),
                   pstart, padded, nact, stride, n_tiles)
    ys = _gmm(xs, te + expert0, tsrc, tfirst, nact, wgu, bgu, wdn, bdn, n_tiles)
    return [_combine(ys, dest_steps, x1, gates.T, g, bb, tm, stride)
            for (x1, ids, gates, rank, tm), dest_steps in zip(segs, plans)]


def _pw1_glu_kernel(x_ref, w_ref, b_ref, u_ref):
    a = jnp.dot(x_ref[...].astype(BF16), w_ref[...], preferred_element_type=F32) + b_ref[...]
    u_ref[...] = a[:, :D_MODEL] * jax.nn.sigmoid(a[:, D_MODEL:])


def _pw1_glu(x, w_bf, b, tm):
    t = x.shape[0]
    row = lambda i: (i, 0)
    fixed = lambda i: (0, 0)
    return pl.pallas_call(
        _pw1_glu_kernel,
        grid=(t // tm,),
        in_specs=[pl.BlockSpec((tm, D_MODEL), row),
                  pl.BlockSpec((D_MODEL, 2 * D_MODEL), fixed),
                  pl.BlockSpec((1, 2 * D_MODEL), fixed)],
        out_specs=pl.BlockSpec((tm, D_MODEL), row),
        out_shape=jax.ShapeDtypeStruct((t, D_MODEL), F32),
        compiler_params=_cparams(1),
        name="pw1_glu",
    )(x, w_bf, b)


def _conv_rows(win_ref, zs_ref, base, offs, w_ref, b_ref, g_ref, bb_ref):
    shift = CONV_HALO - CONV_CTX
    accs = [[None] * N_SLAB for _ in offs]
    for c in range(N_SLAB):
        bias = b_ref[c:c + 1, :]
        for j in range(CONV_WIDTH):
            w = w_ref[c, j:j + 1, :]
            for q, o in enumerate(offs):
                x = win_ref[c, pl.ds(base + (shift + j + o), SUBLANES, stride=2), :]
                accs[q][c] = (bias if j == 0 else accs[q][c]) + w * x
    for q, o in enumerate(offs):
        tot = accs[q][0]
        for c in range(1, N_SLAB):
            tot = tot + accs[q][c]
        mu = jnp.sum(tot, axis=1, keepdims=True) * (1.0 / D_MODEL)
        cen = [a - mu for a in accs[q]]
        sq = cen[0] * cen[0]
        for c in range(1, N_SLAB):
            sq = sq + cen[c] * cen[c]
        inv = lax.rsqrt(jnp.sum(sq, axis=1, keepdims=True) * (1.0 / D_MODEL) + LN_EPS)
        for c in range(N_SLAB):
            z = cen[c] * inv * g_ref[c:c + 1, :] + bb_ref[c:c + 1, :]
            zs_ref[c, pl.ds(base + o, SUBLANES, stride=2), :] = z * jax.nn.sigmoid(z)


def _conv_prompt_kernel(um_ref, up_ref, uc_ref, w_ref, b_ref, g_ref, bb_ref, z_ref, win_ref, zs_ref, *, tr):
    i = pl.program_id(1)
    gap = CONV_HALO - META_LEN

    @pl.when(i == 0)
    def _():
        for c in range(N_SLAB):
            win_ref[c, 0:gap, :] = jnp.zeros((gap, LANES), F32)
            win_ref[c, gap:CONV_HALO, :] = um_ref[:, c * LANES:(c + 1) * LANES]

    @pl.when(i > 0)
    def _():
        for c in range(N_SLAB):
            win_ref[c, 0:CONV_HALO, :] = up_ref[:, c * LANES:(c + 1) * LANES]

    for c in range(N_SLAB):
        win_ref[c, CONV_HALO:, :] = uc_ref[:, c * LANES:(c + 1) * LANES]

    def block(bi, _):
        base = pl.multiple_of(bi * CONV_BLOCK, CONV_BLOCK)
        _conv_rows(win_ref, zs_ref, base, (0, 1, 16, 17), w_ref, b_ref, g_ref, bb_ref)
        return 0

    lax.fori_loop(0, tr // CONV_BLOCK, block, 0)
    for c in range(N_SLAB):
        z_ref[:, c * LANES:(c + 1) * LANES] = zs_ref[c].astype(z_ref.dtype)


def _conv_prompt(u, u_meta, w3, b3, g3, bb3, n_batch, seq, tr):
    nt = seq // tr
    cur = lambda b, i: (b * nt + i, 0)
    prev = lambda b, i: (jnp.maximum((b * seq + i * tr) // CONV_HALO - 1, 0), 0)
    fixed2 = lambda b, i: (0, 0)
    fixed3 = lambda b, i: (0, 0, 0)
    return pl.pallas_call(
        functools.partial(_conv_prompt_kernel, tr=tr),
        grid=(n_batch, nt),
        in_specs=[pl.BlockSpec((META_LEN, D_MODEL), lambda b, i: (b, 0)),
                  pl.BlockSpec((CONV_HALO, D_MODEL), prev),
                  pl.BlockSpec((tr, D_MODEL), cur),
                  pl.BlockSpec((N_SLAB, CONV_HALO, LANES), fixed3),
                  pl.BlockSpec((N_SLAB, LANES), fixed2),
                  pl.BlockSpec((N_SLAB, LANES), fixed2),
                  pl.BlockSpec((N_SLAB, LANES), fixed2)],
        out_specs=pl.BlockSpec((tr, D_MODEL), cur),
        out_shape=jax.ShapeDtypeStruct((n_batch * seq, D_MODEL), BF16),
        scratch_shapes=[pltpu.VMEM((N_SLAB, CONV_HALO + tr, LANES), F32),
                        pltpu.VMEM((N_SLAB, tr, LANES), F32)],
        compiler_params=_cparams(2),
        name="conv_prompt",
    )(u_meta, u, u, w3, b3, g3, bb3)


SHORT_ROWS = 16


def _conv_short_kernel(ctx_ref, u_ref, w_ref, b_ref, g_ref, bb_ref, z_ref, win_ref, zs_ref, *, n_seq, t_len):
    for n in range(n_seq):
        for c in range(N_SLAB):
            sl = slice(c * LANES, (c + 1) * LANES)
            win_ref[c, 0:CONV_HALO, :] = ctx_ref[n, :, sl]
            win_ref[c, CONV_HALO:CONV_HALO + t_len, :] = u_ref[n * t_len:(n + 1) * t_len, sl]
            if t_len < SHORT_ROWS:
                win_ref[c, CONV_HALO + t_len:, :] = jnp.zeros((SHORT_ROWS - t_len, LANES), F32)
        _conv_rows(win_ref, zs_ref, 0, (0, 1), w_ref, b_ref, g_ref, bb_ref)
        for c in range(N_SLAB):
            z_ref[n * t_len:(n + 1) * t_len, c * LANES:(c + 1) * LANES] = zs_ref[c, 0:t_len, :]


def _conv_short(ctx_pad, u, row0, w3, b3, g3, bb3, t_len, n_seq):
    n_total = ctx_pad.shape[0]
    rows = n_seq * t_len
    blk0 = row0 // rows
    fixed2 = lambda n: (0, 0)
    fixed3 = lambda n: (0, 0, 0)
    return pl.pallas_call(
        functools.partial(_conv_short_kernel, n_seq=n_seq, t_len=t_len),
        grid=(n_total // n_seq,),
        in_specs=[pl.BlockSpec((n_seq, CONV_HALO, D_MODEL), lambda n: (n, 0, 0)),
                  pl.BlockSpec((rows, D_MODEL), lambda n: (blk0 + n, 0)),
                  pl.BlockSpec((N_SLAB, CONV_HALO, LANES), fixed3),
                  pl.BlockSpec((N_SLAB, LANES), fixed2),
                  pl.BlockSpec((N_SLAB, LANES), fixed2),
                  pl.BlockSpec((N_SLAB, LANES), fixed2)],
        out_specs=pl.BlockSpec((rows, D_MODEL), lambda n: (n, 0)),
        out_shape=jax.ShapeDtypeStruct((n_total * t_len, D_MODEL), F32),
        scratch_shapes=[pltpu.VMEM((N_SLAB, CONV_HALO + SHORT_ROWS, LANES), F32),
                        pltpu.VMEM((N_SLAB, SHORT_ROWS, LANES), F32)],
        compiler_params=_cparams(1),
        name="conv_short",
    )(ctx_pad, u, w3, b3, g3, bb3)


def _rope_tables(pos):
    half = ROT_DIM // 2
    inv = 1.0 / (ROPE_THETA ** (jnp.arange(0, ROT_DIM, 2, dtype=F32) / ROT_DIM))
    ang = pos.astype(F32)[:, None] * inv[None, :]
    cos, sin = jnp.cos(ang), jnp.sin(ang)
    n = pos.shape[0]
    ones = jnp.ones((n, HEAD_DIM - ROT_DIM), F32)
    zeros = jnp.zeros((n, HEAD_DIM - ROT_DIM), F32)
    zh = jnp.zeros((n, half), F32)
    c = jnp.concatenate([cos, cos, ones], axis=1)
    a = jnp.concatenate([-sin, zh, zeros], axis=1)
    s = jnp.concatenate([zh, sin, zeros], axis=1)
    rep = LANES // HEAD_DIM
    return jnp.tile(c, (1, rep)), jnp.tile(a, (1, rep)), jnp.tile(s, (1, rep))


def kernel(x_prompt, x_sample, cache_attn_meta_k, cache_attn_meta_v, cache_attn_win_k, cache_attn_win_v, state_conv, meta_tokens, attn_w_qkv, attn_b_qkv, attn_sinks, attn_w_o, attn_b_o, conv_w_pw1, conv_b_pw1, conv_w_dw, conv_b_dw, conv_ln_g, conv_ln_b, conv_w_pw2, conv_b_pw2, ln_mix_g, ln_mix_b, ln_ffn_g, ln_ffn_b, moe_w_router, moe_b_router, moe_w_gate_up, moe_b_gate_up, moe_w_down, moe_b_down):
    n_batch, seq, _ = x_prompt.shape
    dec_batch, dec_seq, _ = x_sample.shape
    n_real = n_batch * seq
    n_meta = n_batch * META_LEN
    n_samp = dec_batch * dec_seq
    n_small = n_meta + n_samp
    tb = _pick_tile(seq, 512)
    ts = _pick_tile(n_small, 512)
    row2 = lambda v: v.reshape(1, -1)

    meta_rows = jnp.broadcast_to(meta_tokens[None], (n_batch, META_LEN, D_MODEL)).reshape(n_meta, D_MODEL)
    xb = x_prompt.reshape(n_real, D_MODEL)
    xs = jnp.concatenate([meta_rows.astype(F32), x_sample.reshape(n_samp, D_MODEL)], axis=0)

    def moe_layer(i, a_big, a_small, w, b, xb, xs):
        w_bf = w.astype(BF16)
        wr = moe_w_router[i].T.astype(BF16)
        br = moe_b_router[i].reshape(N_EXPERTS, 1)
        lng, lnb = row2(ln_mix_g[i]), row2(ln_mix_b[i])
        zero_counts = jnp.zeros((N_EXPERTS, LANES), jnp.int32)
        x1b, idb, gab, rab, cnt = _proj_ln_route(a_big, w_bf, row2(b), xb, lng, lnb, wr, br, zero_counts, tb)
        x1s, ids_, gas, ras, cnt = _proj_ln_route(a_small, w_bf, row2(b), xs, lng, lnb, wr, br, cnt, ts)
        n_all = moe_w_gate_up.shape[0] * N_EXPERTS
        return _moe([(x1b, idb, gab, rab, tb), (x1s, ids_, gas, ras, ts)], cnt, i * N_EXPERTS,
                    moe_w_gate_up.reshape(n_all, D_MODEL, 2 * MOE_FF),
                    moe_b_gate_up.reshape(n_all, 1, 2 * MOE_FF),
                    moe_w_down.reshape(n_all, MOE_FF, D_MODEL),
                    moe_b_down.reshape(n_all, 1, D_MODEL),
                    ln_ffn_g[i].reshape(N_SLAB, LANES), ln_ffn_b[i].reshape(N_SLAB, LANES))

    w_qkv = attn_w_qkv[0].astype(BF16)
    b_qkv = row2(attn_b_qkv[0])
    pos_small = jnp.concatenate([jnp.tile(jnp.arange(META_LEN), n_batch),
                                 jnp.tile(PAST_LEN + jnp.arange(dec_seq), dec_batch)])
    qb, kb, vb = _qkv_rope(xb, w_qkv, b_qkv, *_rope_tables(META_LEN + jnp.arange(seq)), tb)
    qs, ks, vs = _qkv_rope(xs, w_qkv, b_qkv, *_rope_tables(pos_small), ts)
    sinks = attn_sinks[0]
    cmk = cache_attn_meta_k[0].reshape(dec_batch, META_LEN, KV_DIM)
    cmv = cache_attn_meta_v[0].reshape(dec_batch, META_LEN, KV_DIM)
    cwk = cache_attn_win_k[0].reshape(dec_batch, WINDOW, KV_DIM)
    cwv = cache_attn_win_v[0].reshape(dec_batch, WINDOW, KV_DIM)
    o_big = _attn_prompt(qb, kb, vb, ks, vs, sinks, n_batch, seq)
    o_small = jnp.concatenate([
        _attn_meta(qs, ks, vs, sinks, n_batch),
        _attn_sample(qs, ks, vs, cmk, cmv, cwk, cwv, sinks, n_meta, dec_batch, dec_seq)], axis=0)
    xb, xs = moe_layer(0, o_big, o_small, attn_w_o[0], attn_b_o[0], xb, xs)

    kv4 = lambda a, n, t_len: a.reshape(n, t_len, N_KV_HEADS, HEAD_DIM)
    p_meta_k = kv4(ks[:n_meta], n_batch, META_LEN)[None]
    p_meta_v = kv4(vs[:n_meta], n_batch, META_LEN)[None]
    p_win_k = kv4(kb, n_batch, seq)[:, seq - WINDOW:][None]
    p_win_v = kv4(vb, n_batch, seq)[:, seq - WINDOW:][None]
    k_new = kv4(ks[n_meta:], dec_batch, dec_seq)
    v_new = kv4(vs[n_meta:], dec_batch, dec_seq)
    s_win_k = jnp.concatenate([cache_attn_win_k[0], k_new], axis=1)[:, -WINDOW:][None]
    s_win_v = jnp.concatenate([cache_attn_win_v[0], v_new], axis=1)[:, -WINDOW:][None]

    w_pw1 = conv_w_pw1[0].astype(BF16)
    ub = _pw1_glu(xb, w_pw1, row2(conv_b_pw1[0]), tb)
    us = _pw1_glu(xs, w_pw1, row2(conv_b_pw1[0]), ts)
    slab = lambda v: v.reshape(N_SLAB, LANES)
    w_dw = jnp.pad(conv_w_dw[0], ((0, CONV_HALO - CONV_WIDTH), (0, 0)))
    w3 = w_dw.reshape(CONV_HALO, N_SLAB, LANES).transpose(1, 0, 2)
    conv_args = (w3, slab(conv_b_dw[0]), slab(conv_ln_g[0]), slab(conv_ln_b[0]))
    z_big = _conv_prompt(ub, us, *conv_args, n_batch, seq, tb)
    z_meta = _conv_short(jnp.zeros((n_batch, CONV_HALO, D_MODEL), F32), us, 0, *conv_args, META_LEN, 1)
    ctx_pad = jnp.pad(state_conv[0], ((0, 0), (CONV_HALO - CONV_CTX, 0), (0, 0)))
    z_samp = _conv_short(ctx_pad, us, n_meta, *conv_args, dec_seq, 2)
    z_small = jnp.concatenate([z_meta, z_samp], axis=0).astype(BF16)
    xb, xs = moe_layer(1, z_big, z_small, conv_w_pw2[0], conv_b_pw2[0], xb, xs)

    p_conv = ub.reshape(n_batch, seq, D_MODEL)[:, seq - CONV_CTX:][None]
    u_samp = us[n_meta:].reshape(dec_batch, dec_seq, D_MODEL)
    s_conv = jnp.concatenate([state_conv[0], u_samp], axis=1)[:, -CONV_CTX:][None]

    y_prompt = xb.reshape(n_batch, seq, D_MODEL)
    y_sample = xs[n_meta:].reshape(dec_batch, dec_seq, D_MODEL)
    return (y_prompt, y_sample, p_meta_k, p_meta_v, p_win_k, p_win_v, p_conv,
            s_win_k, s_win_v, s_conv)
```

```python
import functools

import numpy as np
import jax
import jax.numpy as jnp
from jax import lax
from jax.experimental import pallas as pl
from jax.experimental.pallas import tpu as pltpu

F32 = jnp.float32
BF16 = jnp.bfloat16

D_MODEL = 1024
HEAD_DIM = 64
N_HEADS = 16
N_KV_HEADS = 4
GROUP = N_HEADS // N_KV_HEADS
KV_DIM = N_KV_HEADS * HEAD_DIM
QKV_DIM = D_MODEL + 2 * KV_DIM
ROT_DIM = 16
ROPE_THETA = 500000.0
WINDOW = 128
ATTN_BLOCK = 128
ATTN_SCALE = HEAD_DIM ** -0.5
LOG2_E = 1.4426950408889634
Q_SCALE = ATTN_SCALE * LOG2_E
META_LEN = 16
CONV_WIDTH = 31
CONV_CTX = CONV_WIDTH - 1
N_EXPERTS = 32
TOP_K = 4
MOE_FF = 1024
SWIGLU_LIMIT = 7.0
SWIGLU_ALPHA = 1.702
LN_EPS = 1e-5
DEPTH = 2
DEEPNORM_ALPHA = (2 * DEPTH) ** 0.25
PAST_LEN = 16384
NEG_INF = -1e30

LANES = 128
SUBLANES = 8
GMM_TILE = 256
CONV_HALO = 32
CONV_BLOCK = 32
N_SLAB = D_MODEL // LANES
COMBINE_ROWS = 32
HEADS_PER_DOT = 2
VMEM_LIMIT = 56 * 1024 * 1024


def _cparams(n_axes):
    return pltpu.CompilerParams(dimension_semantics=("arbitrary",) * n_axes,
                                vmem_limit_bytes=VMEM_LIMIT)


def _pick_tile(n, cap):
    best = None
    t = LANES
    while t <= cap:
        if n % t == 0:
            best = t
        t += LANES
    assert best is not None, n
    return best


def _layer_norm(h, g, b):
    mu = jnp.mean(h, axis=-1, keepdims=True)
    hc = h - mu
    var = jnp.mean(hc * hc, axis=-1, keepdims=True)
    return hc * lax.rsqrt(var + LN_EPS) * g + b


def _qkv_kernel(x_ref, w_ref, b_ref, c_ref, a_ref, s_ref, q_ref, k_ref, v_ref):
    x = x_ref[...].astype(BF16)
    acc = jnp.dot(x, w_ref[...], preferred_element_type=F32) + b_ref[...]
    c = c_ref[...]
    a = a_ref[...]
    s = s_ref[...]

    def rope(t):
        return t * c + pltpu.roll(t, LANES - ROT_DIM // 2, 1) * a + pltpu.roll(t, ROT_DIM // 2, 1) * s

    for j in range(D_MODEL // LANES):
        sl = slice(j * LANES, (j + 1) * LANES)
        q_ref[:, sl] = (rope(acc[:, sl]) * Q_SCALE).astype(BF16)
    for j in range(KV_DIM // LANES):
        sl = slice(D_MODEL + j * LANES, D_MODEL + (j + 1) * LANES)
        k_ref[:, j * LANES:(j + 1) * LANES] = rope(acc[:, sl])
    v_ref[...] = acc[:, D_MODEL + KV_DIM:]


def _qkv_rope(x, w_bf, b, cos_t, sa_t, sb_t, tm):
    t = x.shape[0]
    period = cos_t.shape[0] // tm
    row = lambda i: (i, 0)
    tab = lambda i: (i % period, 0)
    fixed = lambda i: (0, 0)
    return pl.pallas_call(
        _qkv_kernel,
        grid=(t // tm,),
        in_specs=[pl.BlockSpec((tm, D_MODEL), row),
                  pl.BlockSpec((D_MODEL, QKV_DIM), fixed),
                  pl.BlockSpec((1, QKV_DIM), fixed),
                  pl.BlockSpec((tm, LANES), tab),
                  pl.BlockSpec((tm, LANES), tab),
                  pl.BlockSpec((tm, LANES), tab)],
        out_specs=[pl.BlockSpec((tm, D_MODEL), row),
                   pl.BlockSpec((tm, KV_DIM), row),
                   pl.BlockSpec((tm, KV_DIM), row)],
        out_shape=[jax.ShapeDtypeStruct((t, D_MODEL), BF16),
                   jax.ShapeDtypeStruct((t, KV_DIM), F32),
                   jax.ShapeDtypeStruct((t, KV_DIM), F32)],
        compiler_params=_cparams(1),
        name="qkv_rope",
    )(x, w_bf, b, cos_t, sa_t, sb_t)


def _qkv_t_kernel(x_ref, wqt_ref, bq_ref, wkv_ref, bkv_ref, wvt_ref, bv_ref, c_ref, a_ref, s_ref,
                  ct_ref, st_ref, qt_ref, k_ref, v_ref, vt_ref):
    x = x_ref[...].astype(BF16)
    nt = (((1,), (1,)), ((), ()))
    qt = lax.dot_general(wqt_ref[...], x, nt, preferred_element_type=F32) + bq_ref[...]
    ct = ct_ref[...]
    st = st_ref[...]
    half = ROT_DIM // 2
    for h in range(N_HEADS):
        r0 = h * HEAD_DIM
        x1 = qt[r0:r0 + half]
        x2 = qt[r0 + half:r0 + ROT_DIM]
        rot = jnp.concatenate([x1 * ct - x2 * st, x2 * ct + x1 * st], axis=0)
        qt_ref[r0:r0 + ROT_DIM, :] = (rot * Q_SCALE).astype(BF16)
        qt_ref[r0 + ROT_DIM:r0 + HEAD_DIM, :] = (qt[r0 + ROT_DIM:r0 + HEAD_DIM] * Q_SCALE).astype(BF16)
    vt = lax.dot_general(wvt_ref[...], x, nt, preferred_element_type=F32) + bv_ref[...]
    vt_ref[...] = vt.astype(BF16)

    kv = jnp.dot(x, wkv_ref[...], preferred_element_type=F32) + bkv_ref[...]
    c = c_ref[...]
    a = a_ref[...]
    s = s_ref[...]
    for j in range(KV_DIM // LANES):
        t = kv[:, j * LANES:(j + 1) * LANES]
        k_ref[:, j * LANES:(j + 1) * LANES] = (
            t * c + pltpu.roll(t, LANES - half, 1) * a + pltpu.roll(t, half, 1) * s)
    v_ref[...] = kv[:, KV_DIM:]


def _qkv_rope_t(x, w_bf, b, cos_t, sa_t, sb_t, cos_tt, sin_tt, tm):
    t = x.shape[0]
    period = cos_t.shape[0] // tm
    row = lambda i: (i, 0)
    col = lambda i: (0, i)
    tab = lambda i: (i % period, 0)
    tabt = lambda i: (0, i % period)
    fixed = lambda i: (0, 0)
    wqt = w_bf[:, :D_MODEL].T
    wvt = w_bf[:, D_MODEL + KV_DIM:].T
    half = ROT_DIM // 2
    return pl.pallas_call(
        _qkv_t_kernel,
        grid=(t // tm,),
        in_specs=[pl.BlockSpec((tm, D_MODEL), row),
                  pl.BlockSpec((D_MODEL, D_MODEL), fixed),
                  pl.BlockSpec((D_MODEL, 1), fixed),
                  pl.BlockSpec((D_MODEL, 2 * KV_DIM), fixed),
                  pl.BlockSpec((1, 2 * KV_DIM), fixed),
                  pl.BlockSpec((KV_DIM, D_MODEL), fixed),
                  pl.BlockSpec((KV_DIM, 1), fixed),
                  pl.BlockSpec((tm, LANES), tab),
                  pl.BlockSpec((tm, LANES), tab),
                  pl.BlockSpec((tm, LANES), tab),
                  pl.BlockSpec((half, tm), tabt),
                  pl.BlockSpec((half, tm), tabt)],
        out_specs=[pl.BlockSpec((D_MODEL, tm), col),
                   pl.BlockSpec((tm, KV_DIM), row),
                   pl.BlockSpec((tm, KV_DIM), row),
                   pl.BlockSpec((KV_DIM, tm), col)],
        out_shape=[jax.ShapeDtypeStruct((D_MODEL, t), BF16),
                   jax.ShapeDtypeStruct((t, KV_DIM), F32),
                   jax.ShapeDtypeStruct((t, KV_DIM), F32),
                   jax.ShapeDtypeStruct((KV_DIM, t), BF16)],
        compiler_params=_cparams(1),
        name="qkv_rope_t",
    )(x, wqt, b[:, :D_MODEL].reshape(D_MODEL, 1), w_bf[:, D_MODEL:], b[:, D_MODEL:],
      wvt, b[:, D_MODEL + KV_DIM:].reshape(KV_DIM, 1), cos_t, sa_t, sb_t, cos_tt, sin_tt)


def _attend(q, kcat, vcat, bias, sink_ref, write):
    kgs = [kcat[:, g * HEAD_DIM:(g + 1) * HEAD_DIM] for g in range(N_KV_HEADS)]

    def scores(h):
        qh = q[:, h * HEAD_DIM:(h + 1) * HEAD_DIM]
        return lax.dot_general(qh, kgs[h // GROUP], (((1,), (1,)), ((), ())), preferred_element_type=F32) + bias

    s_next = scores(0)
    for h in range(N_HEADS):
        s = s_next
        if h + 1 < N_HEADS:
            s_next = scores(h + 1)
        vg = vcat[:, (h // GROUP) * HEAD_DIM:(h // GROUP + 1) * HEAD_DIM]
        sink = sink_ref[h] * LOG2_E
        m = jnp.maximum(jnp.max(s, axis=-1, keepdims=True), sink)
        p = jnp.exp2(s - m)
        denom = jnp.sum(p, axis=-1, keepdims=True) + jnp.exp2(sink - m)
        o = jnp.dot(p.astype(BF16), vg, preferred_element_type=F32)
        write(h, o / denom)


def _attn_prompt_kernel(sink_ref, qt_ref, kp_ref, ko_ref, km_ref, vtp_ref, vto_ref, vtm_ref, o_ref, acc_ref):
    i = pl.program_id(1)
    kcat = jnp.concatenate([kp_ref[...], ko_ref[...], km_ref[...]], axis=0).astype(BF16)
    vt = jnp.concatenate([vtp_ref[...], vto_ref[...], vtm_ref[...]], axis=1)
    nk = 2 * ATTN_BLOCK + META_LEN
    key = lax.broadcasted_iota(jnp.int32, (nk, ATTN_BLOCK), 0)
    qry = lax.broadcasted_iota(jnp.int32, (nk, ATTN_BLOCK), 1)
    first = jnp.where(i > 0, 0, 2 * ATTN_BLOCK)
    prev_ok = (key < ATTN_BLOCK) & (key >= qry + first)
    own_ok = (key >= ATTN_BLOCK) & (key - ATTN_BLOCK <= qry)
    mask = prev_ok | own_ok | (key >= 2 * ATTN_BLOCK)
    bias = jnp.where(mask, 0.0, NEG_INF)
    bias = jnp.concatenate([bias] * HEADS_PER_DOT, axis=1)
    lane = lax.broadcasted_iota(jnp.int32, (1, HEADS_PER_DOT * ATTN_BLOCK), 1)
    kgs = [kcat[:, g * HEAD_DIM:(g + 1) * HEAD_DIM] for g in range(N_KV_HEADS)]

    def scores(u):
        h0 = u * HEADS_PER_DOT
        rhs = jnp.concatenate([qt_ref[(h0 + j) * HEAD_DIM:(h0 + j + 1) * HEAD_DIM, :]
                               for j in range(HEADS_PER_DOT)], axis=1)
        return jnp.dot(kgs[h0 // GROUP], rhs, preferred_element_type=F32) + bias

    n_units = N_HEADS // HEADS_PER_DOT
    s_next = scores(0)
    for u in range(n_units):
        s = s_next
        if u + 1 < n_units:
            s_next = scores(u + 1)
        h0 = u * HEADS_PER_DOT
        g = h0 // GROUP
        sink = sink_ref[h0] * LOG2_E
        for j in range(1, HEADS_PER_DOT):
            sink = jnp.where(lane >= j * ATTN_BLOCK, sink_ref[h0 + j] * LOG2_E, sink)
        m = jnp.maximum(jnp.max(s, axis=0, keepdims=True), sink)
        p = jnp.exp2(s - m)
        denom = jnp.sum(p, axis=0, keepdims=True) + jnp.exp2(sink - m)
        o = jnp.dot(vt[g * HEAD_DIM:(g + 1) * HEAD_DIM, :], p.astype(BF16), preferred_element_type=F32)
        o = o * (1.0 / denom)
        for j in range(HEADS_PER_DOT):
            acc_ref[(h0 + j) * HEAD_DIM:(h0 + j + 1) * HEAD_DIM, :] = o[:, j * ATTN_BLOCK:(j + 1) * ATTN_BLOCK]
    o_ref[...] = acc_ref[...].T.astype(o_ref.dtype)


def _attn_prompt(qt, k, vt, k_meta, vt_meta, sinks, n_batch, seq):
    nblk = seq // ATTN_BLOCK
    own = lambda b, i: b * nblk + i
    prev = lambda b, i: b * nblk + jnp.maximum(i - 1, 0)
    kspec = lambda m: pl.BlockSpec((ATTN_BLOCK, KV_DIM), lambda b, i: (m(b, i), 0))
    vspec = lambda m: pl.BlockSpec((KV_DIM, ATTN_BLOCK), lambda b, i: (0, m(b, i)))
    return pl.pallas_call(
        _attn_prompt_kernel,
        grid=(n_batch, nblk),
        in_specs=[pl.BlockSpec(memory_space=pltpu.SMEM),
                  pl.BlockSpec((D_MODEL, ATTN_BLOCK), lambda b, i: (0, own(b, i))),
                  kspec(prev), kspec(own),
                  pl.BlockSpec((META_LEN, KV_DIM), lambda b, i: (b, 0)),
                  vspec(prev), vspec(own),
                  pl.BlockSpec((None, KV_DIM, META_LEN), lambda b, i: (b, 0, 0))],
        out_specs=pl.BlockSpec((ATTN_BLOCK, D_MODEL), lambda b, i: (own(b, i), 0)),
        out_shape=jax.ShapeDtypeStruct((n_batch * seq, D_MODEL), BF16),
        scratch_shapes=[pltpu.VMEM((D_MODEL, ATTN_BLOCK), F32)],
        compiler_params=_cparams(2),
        name="attn_prompt",
    )(sinks, qt, k, k, k_meta, vt, vt, vt_meta)


def _attn_meta_kernel(sink_ref, q_ref, k_ref, v_ref, o_ref):
    r = lax.broadcasted_iota(jnp.int32, (META_LEN, META_LEN), 0)
    c = lax.broadcasted_iota(jnp.int32, (META_LEN, META_LEN), 1)

    def write(h, o):
        o_ref[:, h * HEAD_DIM:(h + 1) * HEAD_DIM] = o.astype(o_ref.dtype)

    bias = jnp.where(c <= r, 0.0, NEG_INF)
    _attend(q_ref[...], k_ref[...].astype(BF16), v_ref[...].astype(BF16), bias, sink_ref, write)


def _attn_meta(q, k, v, sinks, n_batch):
    imap = lambda b: (b, 0)
    return pl.pallas_call(
        _attn_meta_kernel,
        grid=(n_batch,),
        in_specs=[pl.BlockSpec(memory_space=pltpu.SMEM),
                  pl.BlockSpec((META_LEN, D_MODEL), imap),
                  pl.BlockSpec((META_LEN, KV_DIM), imap),
                  pl.BlockSpec((META_LEN, KV_DIM), imap)],
        out_specs=pl.BlockSpec((META_LEN, D_MODEL), lambda b: (b, 0)),
        out_shape=jax.ShapeDtypeStruct((n_batch * META_LEN, D_MODEL), BF16),
        compiler_params=_cparams(1),
        name="attn_meta",
    )(sinks, q, k, v)


SAMPLE_GROUP = 8


def _attn_sample_kernel(sink_ref, bias_ref, q_ref, kn_ref, vn_ref, cmk_ref, cmv_ref, cwk_ref, cwv_ref,
                        o_ref, *, dec_seq, group):
    def keys(cm_ref, cw_ref, new):
        parts = []
        for j in range(group):
            parts += [cm_ref[j], cw_ref[j], new[j * dec_seq:(j + 1) * dec_seq]]
        return jnp.concatenate(parts, axis=0).astype(BF16)

    kcat = keys(cmk_ref, cwk_ref, kn_ref[...])
    vcat = keys(cmv_ref, cwv_ref, vn_ref[...])

    def write(h, o):
        o_ref[:, h * HEAD_DIM:(h + 1) * HEAD_DIM] = o.astype(o_ref.dtype)

    _attend(q_ref[...], kcat, vcat, bias_ref[...], sink_ref, write)


def _sample_bias(group, dec_seq):
    per = META_LEN + WINDOW + dec_seq
    r = np.arange(group * dec_seq)[:, None]
    c = np.arange(group * per)[None, :]
    tq, ck = r % dec_seq, c % per
    win_ok = (ck >= META_LEN) & (ck < META_LEN + WINDOW) & (ck - META_LEN >= tq)
    new_ok = (ck >= META_LEN + WINDOW) & (ck - (META_LEN + WINDOW) <= tq)
    vis = (r // dec_seq == c // per) & ((ck < META_LEN) | win_ok | new_ok)
    return np.where(vis, 0.0, NEG_INF).astype(np.float32)


def _attn_sample(q, k, v, cmk, cmv, cwk, cwv, sinks, row0, dec_batch, dec_seq):
    group = SAMPLE_GROUP
    nq = group * dec_seq
    blk0 = row0 // nq
    bias = jnp.asarray(_sample_bias(group, dec_seq))
    qmap = lambda n: (blk0 + n, 0)
    cmap = lambda n: (n, 0, 0)
    return pl.pallas_call(
        functools.partial(_attn_sample_kernel, dec_seq=dec_seq, group=group),
        grid=(dec_batch // group,),
        in_specs=[pl.BlockSpec(memory_space=pltpu.SMEM),
                  pl.BlockSpec(bias.shape, lambda n: (0, 0)),
                  pl.BlockSpec((nq, D_MODEL), qmap),
                  pl.BlockSpec((nq, KV_DIM), qmap),
                  pl.BlockSpec((nq, KV_DIM), qmap),
                  pl.BlockSpec((group, META_LEN, KV_DIM), cmap),
                  pl.BlockSpec((group, META_LEN, KV_DIM), cmap),
                  pl.BlockSpec((group, WINDOW, KV_DIM), cmap),
                  pl.BlockSpec((group, WINDOW, KV_DIM), cmap)],
        out_specs=pl.BlockSpec((nq, D_MODEL), lambda n: (n, 0)),
        out_shape=jax.ShapeDtypeStruct((dec_batch * dec_seq, D_MODEL), BF16),
        compiler_params=_cparams(1),
        name="attn_sample",
    )(sinks, bias, q, k, v, cmk, cmv, cwk, cwv)


def _proj_ln_route_kernel(a_ref, w_ref, b_ref, x_ref, g_ref, bb_ref, wr_ref, br_ref, tri_ref, cin_ref,
                          x1_ref, ids_ref, gates_ref, rank_ref, cnt_ref, carry_ref):
    @pl.when(pl.program_id(0) == 0)
    def _():
        carry_ref[...] = cin_ref[...].astype(F32)

    y = jnp.dot(a_ref[...], w_ref[...], preferred_element_type=F32) + b_ref[...]
    x1 = _layer_norm(DEEPNORM_ALPHA * x_ref[...] + y, g_ref[...], bb_ref[...])
    for j in range(N_SLAB):
        x1_ref[pl.ds(j, x1.shape[0], stride=N_SLAB), :] = x1[:, j * LANES:(j + 1) * LANES]

    logits = lax.dot_general(wr_ref[...], x1.astype(BF16), (((1,), (1,)), ((), ())),
                             preferred_element_type=F32) + br_ref[...]
    tm = logits.shape[1]
    eidx = lax.broadcasted_iota(jnp.int32, (N_EXPERTS, tm), 0).astype(F32)
    cur = logits
    vals, idxs, sels = [], [], []
    for _ in range(TOP_K):
        m = jnp.max(cur, axis=0, keepdims=True)
        idx = jnp.min(jnp.where(cur == m, eidx, float(N_EXPERTS)), axis=0, keepdims=True)
        sel = eidx == idx
        vals.append(m)
        idxs.append(idx)
        sels.append(sel)
        cur = jnp.where(sel, -jnp.inf, cur)
    exps = [jnp.exp(v - vals[0]) for v in vals]
    tot = exps[0] + exps[1] + exps[2] + exps[3]
    gates_ref[...] = jnp.concatenate([e / tot for e in exps], axis=0)
    ids_ref[...] = jnp.concatenate(idxs, axis=0).astype(jnp.int32)

    chosen = jnp.where(sels[0] | sels[1] | sels[2] | sels[3], 1.0, 0.0)
    before = jnp.dot(chosen.astype(BF16), tri_ref[...], preferred_element_type=F32)
    before = before + carry_ref[:, 0:1]
    ranks = [jnp.sum(jnp.where(s, before, 0.0), axis=0, keepdims=True) for s in sels]
    rank_ref[...] = jnp.concatenate(ranks, axis=0).astype(jnp.int32)
    carry_ref[...] = carry_ref[...] + jnp.sum(chosen, axis=1, keepdims=True)
    cnt_ref[...] = carry_ref[...].astype(jnp.int32)


def _proj_ln_route(a_bf, w_bf, b, x, g, bb, wr_t_bf, br_col, counts_in, tm):
    t = x.shape[0]
    row = lambda i: (i, 0)
    col = lambda i: (0, i)
    fixed = lambda i: (0, 0)
    tri = (jnp.arange(tm)[:, None] < jnp.arange(tm)[None, :]).astype(BF16)
    return pl.pallas_call(
        _proj_ln_route_kernel,
        grid=(t // tm,),
        in_specs=[pl.BlockSpec((tm, D_MODEL), row),
                  pl.BlockSpec((D_MODEL, D_MODEL), fixed),
                  pl.BlockSpec((1, D_MODEL), fixed),
                  pl.BlockSpec((tm, D_MODEL), row),
                  pl.BlockSpec((1, D_MODEL), fixed),
                  pl.BlockSpec((1, D_MODEL), fixed),
                  pl.BlockSpec((N_EXPERTS, D_MODEL), fixed),
                  pl.BlockSpec((N_EXPERTS, 1), fixed),
                  pl.BlockSpec((tm, tm), fixed),
                  pl.BlockSpec((N_EXPERTS, LANES), fixed)],
        out_specs=[pl.BlockSpec((tm * N_SLAB, LANES), row),
                   pl.BlockSpec((TOP_K, tm), col),
                   pl.BlockSpec((TOP_K, tm), col),
                   pl.BlockSpec((TOP_K, tm), col),
                   pl.BlockSpec((N_EXPERTS, LANES), fixed)],
        out_shape=[jax.ShapeDtypeStruct((t * N_SLAB, LANES), F32),
                   jax.ShapeDtypeStruct((TOP_K, t), jnp.int32),
                   jax.ShapeDtypeStruct((TOP_K, t), F32),
                   jax.ShapeDtypeStruct((TOP_K, t), jnp.int32),
                   jax.ShapeDtypeStruct((N_EXPERTS, LANES), jnp.int32)],
        scratch_shapes=[pltpu.VMEM((N_EXPERTS, LANES), F32)],
        compiler_params=_cparams(1),
        name="proj_ln_route",
    )(a_bf, w_bf, b, x, g, bb, wr_t_bf, br_col, tri, counts_in)


def _idx_copy(dest_hbm, dsm, sem, step, stride):
    slot = step % 2
    return pltpu.make_async_copy(dest_hbm.at[pl.ds(pl.multiple_of(step * stride, stride), stride)],
                                 dsm.at[pl.ds(pl.multiple_of(slot * stride, stride), stride)],
                                 sem)


def _dispatch_kernel(pstart_ref, padded_ref, nact_ref, dest_hbm, *rest, tiles, steps, stride, n_tiles):
    x_refs = rest[:len(tiles)]
    xs_hbm, dsm, zbuf, sems = rest[len(tiles):]
    i = pl.program_id(0)
    nsteps = pl.num_programs(0)

    def zero_tile(row0):
        n = GMM_TILE * N_SLAB
        return pltpu.make_async_copy(zbuf, xs_hbm.at[pl.ds(pl.multiple_of(row0 * N_SLAB, n), n)], sems.at[2])

    def token(ref, r):
        return ref.at[pl.ds(pl.multiple_of(r * N_SLAB, N_SLAB), N_SLAB)]

    @pl.when(i == 0)
    def _():
        zbuf[...] = jnp.zeros_like(zbuf)

        def each_expert(fn):
            def body(e, _):
                @pl.when(padded_ref[e] > 0)
                def _():
                    fn(zero_tile(pstart_ref[e] + padded_ref[e] - GMM_TILE))
                return 0
            lax.fori_loop(0, N_EXPERTS, body, 0)

        def each_tail(fn):
            def body(t, _):
                fn(zero_tile(t * GMM_TILE))
                return 0
            lax.fori_loop(nact_ref[0], n_tiles, body, 0)

        each_expert(lambda cp: cp.start())
        each_tail(lambda cp: cp.start())
        each_expert(lambda cp: cp.wait())
        each_tail(lambda cp: cp.wait())
        _idx_copy(dest_hbm, dsm, sems.at[0], i, stride).start()

    _idx_copy(dest_hbm, dsm, sems.at[0], i, stride).wait()

    @pl.when(i + 1 < nsteps)
    def _():
        _idx_copy(dest_hbm, dsm, sems.at[0], i + 1, stride).start()

    base = (i % 2) * stride
    step0 = 0
    for x_ref, tm, n in zip(x_refs, tiles, steps):
        @pl.when((i >= step0) & (i < step0 + n))
        def _(x_ref=x_ref, tm=tm):
            def row_body(r, _):
                for k in range(TOP_K):
                    d = dsm[base + k * tm + r]
                    pltpu.make_async_copy(token(x_ref, r), token(xs_hbm, d), sems.at[1]).start(priority=k % 2)
                return 0

            lax.fori_loop(0, tm, row_body, 0, unroll=2)
            for k in range(TOP_K):
                pltpu.make_async_copy(x_ref, xs_hbm.at[pl.ds(0, tm * N_SLAB)], sems.at[1]).wait()
        step0 += n


def _seg_map(step0, n):
    return lambda i, *_: (jnp.clip(i - step0, 0, n - 1), 0)


def _dispatch(xs_list, tiles, dest_steps, pstart, padded, nact, stride, n_tiles):
    steps = [x.shape[0] // (tm * N_SLAB) for x, tm in zip(xs_list, tiles)]
    in_specs = [pl.BlockSpec(memory_space=pl.ANY)]
    step0 = 0
    for tm, n in zip(tiles, steps):
        in_specs.append(pl.BlockSpec((tm * N_SLAB, LANES), _seg_map(step0, n)))
        step0 += n
    grid_spec = pltpu.PrefetchScalarGridSpec(
        num_scalar_prefetch=3,
        grid=(sum(steps),),
        in_specs=in_specs,
        out_specs=pl.BlockSpec(memory_space=pl.ANY),
        scratch_shapes=[pltpu.SMEM((2 * stride,), jnp.int32),
                        pltpu.VMEM((GMM_TILE * N_SLAB, LANES), F32),
                        pltpu.SemaphoreType.DMA((3,))],
    )
    return pl.pallas_call(
        functools.partial(_dispatch_kernel, tiles=tuple(tiles), steps=tuple(steps), stride=stride,
                          n_tiles=n_tiles),
        grid_spec=grid_spec,
        out_shape=jax.ShapeDtypeStruct((n_tiles * GMM_TILE * N_SLAB, LANES), F32),
        compiler_params=_cparams(1),
        name="moe_dispatch",
    )(pstart, padded, nact, dest_steps, *xs_list)


def _gmm_kernel(te_ref, tsrc_ref, tfirst_ref, nact_ref, x_ref, wgu_ref, bgu_ref, wdn_ref, bdn_ref, y_ref,
                wgu_bf, wdn_bf):
    i = pl.program_id(0)

    @pl.when(tfirst_ref[i] == 1)
    def _():
        wgu_bf[...] = wgu_ref[...].astype(BF16)
        wdn_bf[...] = wdn_ref[...].astype(BF16)

    @pl.when(i < nact_ref[0])
    def _():
        x = jnp.concatenate([x_ref[pl.ds(j, GMM_TILE, stride=N_SLAB), :] for j in range(N_SLAB)], axis=1)
        gu = jnp.dot(x.astype(BF16), wgu_bf[...], preferred_element_type=F32) + bgu_ref[...]
        gate = jnp.minimum(gu[:, :MOE_FF], SWIGLU_LIMIT)
        up = jnp.clip(gu[:, MOE_FF:], -SWIGLU_LIMIT, SWIGLU_LIMIT)
        glu = gate * jax.nn.sigmoid(SWIGLU_ALPHA * gate)
        h = ((up + 1.0) * glu).astype(BF16)
        y = jnp.dot(h, wdn_bf[...], preferred_element_type=F32) + bdn_ref[...]
        for j in range(N_SLAB):
            y_ref[pl.ds(j, GMM_TILE, stride=N_SLAB), :] = y[:, j * LANES:(j + 1) * LANES]

    @pl.when(i >= nact_ref[0])
    def _():
        y_ref[...] = jnp.zeros_like(y_ref)


def _gmm(xs, te, tsrc, tfirst, nact, wgu, bgu, wdn, bdn, n_tiles):
    emap = lambda i, te, *_: (te[i], 0, 0)
    grid_spec = pltpu.PrefetchScalarGridSpec(
        num_scalar_prefetch=4,
        grid=(n_tiles,),
        in_specs=[pl.BlockSpec((GMM_TILE * N_SLAB, LANES), lambda i, te, ts, *_: (ts[i], 0)),
                  pl.BlockSpec((None, D_MODEL, 2 * MOE_FF), emap),
                  pl.BlockSpec((None, 1, 2 * MOE_FF), emap),
                  pl.BlockSpec((None, MOE_FF, D_MODEL), emap),
                  pl.BlockSpec((None, 1, D_MODEL), emap)],
        out_specs=pl.BlockSpec((GMM_TILE * N_SLAB, LANES), lambda i, *_: (i, 0)),
        scratch_shapes=[pltpu.VMEM((D_MODEL, 2 * MOE_FF), BF16),
                        pltpu.VMEM((MOE_FF, D_MODEL), BF16)],
    )
    return pl.pallas_call(
        _gmm_kernel,
        grid_spec=grid_spec,
        out_shape=jax.ShapeDtypeStruct((n_tiles * GMM_TILE * N_SLAB, LANES), F32),
        compiler_params=_cparams(1),
        name="moe_gmm",
    )(te, tsrc, tfirst, nact, xs, wgu, bgu, wdn, bdn)


def _combine_kernel(dest_hbm, ys_hbm, x_ref, gt_ref, g_ref, bb_ref, o_ref, dsm, buf, sems, *, tm, stride):
    i = pl.program_id(0)
    nsteps = pl.num_programs(0)

    @pl.when(i == 0)
    def _():
        _idx_copy(dest_hbm, dsm, sems.at[0], i, stride).start()

    _idx_copy(dest_hbm, dsm, sems.at[0], i, stride).wait()

    @pl.when(i + 1 < nsteps)
    def _():
        _idx_copy(dest_hbm, dsm, sems.at[0], i + 1, stride).start()

    base = (i % 2) * stride

    def token(ref, r):
        return ref.at[pl.ds(pl.multiple_of(r * N_SLAB, N_SLAB), N_SLAB)]

    def row_body(r, _):
        for k in range(TOP_K):
            d = dsm[base + k * tm + r]
            pltpu.make_async_copy(token(ys_hbm, d), token(buf.at[k], r), sems.at[1]).start(priority=k % 2)
        return 0

    lax.fori_loop(0, tm, row_body, 0, unroll=2)
    for k in range(TOP_K):
        pltpu.make_async_copy(ys_hbm.at[pl.ds(0, tm * N_SLAB)], buf.at[k], sems.at[1]).wait()

    def block(b, _):
        r0 = pl.multiple_of(b * COMBINE_ROWS, COMBINE_ROWS)
        gt = gt_ref[pl.ds(r0, COMBINE_ROWS), :]
        gk = [jnp.broadcast_to(gt[:, k:k + 1], (COMBINE_ROWS, LANES)) for k in range(TOP_K)]
        hs = []
        for j in range(N_SLAB):
            rows = pl.ds(r0 * N_SLAB + j, COMBINE_ROWS, stride=N_SLAB)
            h = DEEPNORM_ALPHA * x_ref[rows, :]
            for k in range(TOP_K):
                h = h + gk[k] * buf[k, rows, :]
            hs.append(h)
        tot = hs[0]
        for j in range(1, N_SLAB):
            tot = tot + hs[j]
        mu = jnp.sum(tot, axis=1, keepdims=True) * (1.0 / D_MODEL)
        cen = [h - mu for h in hs]
        sq = cen[0] * cen[0]
        for j in range(1, N_SLAB):
            sq = sq + cen[j] * cen[j]
        inv = lax.rsqrt(jnp.sum(sq, axis=1, keepdims=True) * (1.0 / D_MODEL) + LN_EPS)
        for j in range(N_SLAB):
            o_ref[pl.ds(r0, COMBINE_ROWS), j * LANES:(j + 1) * LANES] = (
                cen[j] * inv * g_ref[j:j + 1, :] + bb_ref[j:j + 1, :])
        return 0

    lax.fori_loop(0, tm // COMBINE_ROWS, block, 0)


def _combine(ys, dest_steps, x_tiles, gates_t, g3, bb3, tm, stride):
    t = x_tiles.shape[0] // N_SLAB
    row = lambda i: (i, 0)
    fixed = lambda i: (0, 0)
    return pl.pallas_call(
        functools.partial(_combine_kernel, tm=tm, stride=stride),
        grid=(t // tm,),
        in_specs=[pl.BlockSpec(memory_space=pl.ANY),
                  pl.BlockSpec(memory_space=pl.ANY),
                  pl.BlockSpec((tm * N_SLAB, LANES), row),
                  pl.BlockSpec((tm, TOP_K), row),
                  pl.BlockSpec((N_SLAB, LANES), fixed),
                  pl.BlockSpec((N_SLAB, LANES), fixed)],
        out_specs=pl.BlockSpec((tm, D_MODEL), row),
        out_shape=jax.ShapeDtypeStruct((t, D_MODEL), F32),
        scratch_shapes=[pltpu.SMEM((2 * stride,), jnp.int32),
                        pltpu.VMEM((TOP_K, tm * N_SLAB, LANES), F32),
                        pltpu.SemaphoreType.DMA((2,))],
        compiler_params=_cparams(1),
        name="moe_combine",
    )(dest_steps, ys, x_tiles, gates_t, g3, bb3)


def _moe(segs, counts, expert0, wgu, bgu, wdn, bdn, g, bb):
    t = sum(seg[1].shape[1] for seg in segs)
    n_tiles = -(-(t * TOP_K) // GMM_TILE) + N_EXPERTS
    counts = counts[:, 0]
    padded = ((counts + GMM_TILE - 1) // GMM_TILE) * GMM_TILE
    pend = jnp.cumsum(padded)
    pstart = (pend - padded).astype(jnp.int32)
    padded = padded.astype(jnp.int32)
    nact = (pend[-1] // GMM_TILE).astype(jnp.int32).reshape(1)
    tile = jnp.arange(n_tiles, dtype=jnp.int32)
    tsrc = jnp.minimum(tile, jnp.maximum(nact[0] - 1, 0))
    te = jnp.sum((pend[None, :] <= (tsrc * GMM_TILE)[:, None]).astype(jnp.int32), axis=1)
    te = jnp.minimum(te, N_EXPERTS - 1).astype(jnp.int32)
    tfirst = jnp.concatenate([jnp.ones((1,), jnp.int32), (te[1:] != te[:-1]).astype(jnp.int32)])
    eids = jnp.arange(N_EXPERTS, dtype=jnp.int32)[:, None, None]

    stride = -(-(TOP_K * max(seg[4] for seg in segs)) // 1024) * 1024
    plans = []
    for x1, ids, gates, rank, tm in segs:
        n = ids.shape[1]
        dest =jnp.sum(jnp.where(ids[None] == eids, pstart[:, None, None], 0), axis=0) + rank
        nsteps = n // tm
        dest_steps = dest.reshape(TOP_K, nsteps, tm).transpose(1, 0, 2).reshape(nsteps, TOP_K * tm)
        plans.append(jnp.pad(dest_steps, ((0, 0), (0, stride - TOP_K * tm))).reshape(-1))
    xs = _dispatch([seg[0] for seg in segs], [seg[4] for seg in segs], jnp.concatenate(plans),
                   pstart, padded, nact, stride, n_tiles)
    ys = _gmm(xs, te + expert0, tsrc, tfirst, nact, wgu, bgu, wdn, bdn, n_tiles)
    return [_combine(ys, dest_steps, x1, gates.T, g, bb, tm, stride)
            for (x1, ids, gates, rank, tm), dest_steps in zip(segs, plans)]


def _pw1_glu_kernel(x_ref, w_ref, b_ref, u_ref):
    a = jnp.dot(x_ref[...].astype(BF16), w_ref[...], preferred_element_type=F32) + b_ref[...]
    u_ref[...] = a[:, :D_MODEL] * jax.nn.sigmoid(a[:, D_MODEL:])


def _pw1_glu(x, w_bf, b, tm):
    t = x.shape[0]
    row = lambda i: (i, 0)
    fixed = lambda i: (0, 0)
    return pl.pallas_call(
        _pw1_glu_kernel,
        grid=(t // tm,),
        in_specs=[pl.BlockSpec((tm, D_MODEL), row),
                  pl.BlockSpec((D_MODEL, 2 * D_MODEL), fixed),
                  pl.BlockSpec((1, 2 * D_MODEL), fixed)],
        out_specs=pl.BlockSpec((tm, D_MODEL), row),
        out_shape=jax.ShapeDtypeStruct((t, D_MODEL), F32),
        compiler_params=_cparams(1),
        name="pw1_glu",
    )(x, w_bf, b)


def _conv_rows(win_ref, zs_ref, base, offs, w_ref, b_ref, g_ref, bb_ref):
    shift = CONV_HALO - CONV_CTX
    accs = [[None] * N_SLAB for _ in offs]
    for c in range(N_SLAB):
        bias = b_ref[c:c + 1, :]
        for j in range(CONV_WIDTH):
            w = w_ref[c, j:j + 1, :]
            for q, o in enumerate(offs):
                x = win_ref[c, pl.ds(base + (shift + j + o), SUBLANES, stride=2), :]
                accs[q][c] = (bias if j == 0 else accs[q][c]) + w * x
    for q, o in enumerate(offs):
        tot = accs[q][0]
        for c in range(1, N_SLAB):
            tot = tot + accs[q][c]
        mu = jnp.sum(tot, axis=1, keepdims=True) * (1.0 / D_MODEL)
        cen = [a - mu for a in accs[q]]
        sq = cen[0] * cen[0]
        for c in range(1, N_SLAB):
            sq = sq + cen[c] * cen[c]
        inv = lax.rsqrt(jnp.sum(sq, axis=1, keepdims=True) * (1.0 / D_MODEL) + LN_EPS)
        for c in range(N_SLAB):
            z = cen[c] * inv * g_ref[c:c + 1, :] + bb_ref[c:c + 1, :]
            zs_ref[c, pl.ds(base + o, SUBLANES, stride=2), :] = z * jax.nn.sigmoid(z)


def _conv_prompt_kernel(um_ref, up_ref, uc_ref, w_ref, b_ref, g_ref, bb_ref, z_ref, win_ref, zs_ref, *, tr):
    i = pl.program_id(1)
    gap = CONV_HALO - META_LEN

    @pl.when(i == 0)
    def _():
        for c in range(N_SLAB):
            win_ref[c, 0:gap, :] = jnp.zeros((gap, LANES), F32)
            win_ref[c, gap:CONV_HALO, :] = um_ref[:, c * LANES:(c + 1) * LANES]

    @pl.when(i > 0)
    def _():
        for c in range(N_SLAB):
            win_ref[c, 0:CONV_HALO, :] = up_ref[:, c * LANES:(c + 1) * LANES]

    for c in range(N_SLAB):
        win_ref[c, CONV_HALO:, :] = uc_ref[:, c * LANES:(c + 1) * LANES]

    def block(bi, _):
        base = pl.multiple_of(bi * CONV_BLOCK, CONV_BLOCK)
        _conv_rows(win_ref, zs_ref, base, (0, 1, 16, 17), w_ref, b_ref, g_ref, bb_ref)
        return 0

    lax.fori_loop(0, tr // CONV_BLOCK, block, 0)
    for c in range(N_SLAB):
        z_ref[:, c * LANES:(c + 1) * LANES] = zs_ref[c].astype(z_ref.dtype)


def _conv_prompt(u, u_meta, w3, b3, g3, bb3, n_batch, seq, tr):
    nt = seq // tr
    cur = lambda b, i: (b * nt + i, 0)
    prev = lambda b, i: (jnp.maximum((b * seq + i * tr) // CONV_HALO - 1, 0), 0)
    fixed2 = lambda b, i: (0, 0)
    fixed3 = lambda b, i: (0, 0, 0)
    return pl.pallas_call(
        functools.partial(_conv_prompt_kernel, tr=tr),
        grid=(n_batch, nt),
        in_specs=[pl.BlockSpec((META_LEN, D_MODEL), lambda b, i: (b, 0)),
                  pl.BlockSpec((CONV_HALO, D_MODEL), prev),
                  pl.BlockSpec((tr, D_MODEL), cur),
                  pl.BlockSpec((N_SLAB, CONV_HALO, LANES), fixed3),
                  pl.BlockSpec((N_SLAB, LANES), fixed2),
                  pl.BlockSpec((N_SLAB, LANES), fixed2),
                  pl.BlockSpec((N_SLAB, LANES), fixed2)],
        out_specs=pl.BlockSpec((tr, D_MODEL), cur),
        out_shape=jax.ShapeDtypeStruct((n_batch * seq, D_MODEL), BF16),
        scratch_shapes=[pltpu.VMEM((N_SLAB, CONV_HALO + tr, LANES), F32),
                        pltpu.VMEM((N_SLAB, tr, LANES), F32)],
        compiler_params=_cparams(2),
        name="conv_prompt",
    )(u_meta, u, u, w3, b3, g3, bb3)


SHORT_ROWS = 16


def _conv_short_kernel(ctx_ref, u_ref, w_ref, b_ref, g_ref, bb_ref, z_ref, win_ref, zs_ref, *, n_seq, t_len):
    for n in range(n_seq):
        for c in range(N_SLAB):
            sl = slice(c * LANES, (c + 1) * LANES)
            win_ref[c, 0:CONV_HALO, :] = ctx_ref[n, :, sl]
            win_ref[c, CONV_HALO:CONV_HALO + t_len, :] = u_ref[n * t_len:(n + 1) * t_len, sl]
            if t_len < SHORT_ROWS:
                win_ref[c, CONV_HALO + t_len:, :] = jnp.zeros((SHORT_ROWS - t_len, LANES), F32)
        _conv_rows(win_ref, zs_ref, 0, (0, 1), w_ref, b_ref, g_ref, bb_ref)
        for c in range(N_SLAB):
            z_ref[n * t_len:(n + 1) * t_len, c * LANES:(c + 1) * LANES] = zs_ref[c, 0:t_len, :]


def _conv_short(ctx_pad, u, row0, w3, b3, g3, bb3, t_len, n_seq):
    n_total = ctx_pad.shape[0]
    rows = n_seq * t_len
    blk0 = row0 // rows
    fixed2 = lambda n: (0, 0)
    fixed3 = lambda n: (0, 0, 0)
    return pl.pallas_call(
        functools.partial(_conv_short_kernel, n_seq=n_seq, t_len=t_len),
        grid=(n_total // n_seq,),
        in_specs=[pl.BlockSpec((n_seq, CONV_HALO, D_MODEL), lambda n: (n, 0, 0)),
                  pl.BlockSpec((rows, D_MODEL), lambda n: (blk0 + n, 0)),
                  pl.BlockSpec((N_SLAB, CONV_HALO, LANES), fixed3),
                  pl.BlockSpec((N_SLAB, LANES), fixed2),
                  pl.BlockSpec((N_SLAB, LANES), fixed2),
                  pl.BlockSpec((N_SLAB, LANES), fixed2)],
        out_specs=pl.BlockSpec((rows, D_MODEL), lambda n: (n, 0)),
        out_shape=jax.ShapeDtypeStruct((n_total * t_len, D_MODEL), F32),
        scratch_shapes=[pltpu.VMEM((N_SLAB, CONV_HALO + SHORT_ROWS, LANES), F32),
                        pltpu.VMEM((N_SLAB, SHORT_ROWS, LANES), F32)],
        compiler_params=_cparams(1),
        name="conv_short",
    )(ctx_pad, u, w3, b3, g3, bb3)


def _rope_angles(pos):
    inv = 1.0 / (ROPE_THETA ** (jnp.arange(0, ROT_DIM, 2, dtype=F32) / ROT_DIM))
    ang = pos.astype(F32)[:, None] * inv[None, :]
    return jnp.cos(ang), jnp.sin(ang)


def _rope_tables_t(pos):
    cos, sin = _rope_angles(pos)
    return cos.T, sin.T


def _rope_tables(pos):
    half = ROT_DIM // 2
    cos, sin = _rope_angles(pos)
    n = pos.shape[0]
    ones = jnp.ones((n, HEAD_DIM - ROT_DIM), F32)
    zeros = jnp.zeros((n, HEAD_DIM - ROT_DIM), F32)
    zh = jnp.zeros((n, half), F32)
    c = jnp.concatenate([cos, cos, ones], axis=1)
    a = jnp.concatenate([-sin, zh, zeros], axis=1)
    s = jnp.concatenate([zh, sin, zeros], axis=1)
    rep = LANES // HEAD_DIM
    return jnp.tile(c, (1, rep)), jnp.tile(a, (1, rep)), jnp.tile(s, (1, rep))


def kernel(x_prompt, x_sample, cache_attn_meta_k, cache_attn_meta_v, cache_attn_win_k, cache_attn_win_v, state_conv, meta_tokens, attn_w_qkv, attn_b_qkv, attn_sinks, attn_w_o, attn_b_o, conv_w_pw1, conv_b_pw1, conv_w_dw, conv_b_dw, conv_ln_g, conv_ln_b, conv_w_pw2, conv_b_pw2, ln_mix_g, ln_mix_b, ln_ffn_g, ln_ffn_b, moe_w_router, moe_b_router, moe_w_gate_up, moe_b_gate_up, moe_w_down, moe_b_down):
    n_batch, seq, _ = x_prompt.shape
    dec_batch, dec_seq, _ = x_sample.shape
    n_real = n_batch * seq
    n_meta = n_batch * META_LEN
    n_samp = dec_batch * dec_seq
    n_small = n_meta + n_samp
    tb = _pick_tile(seq, 512)
    ts = _pick_tile(n_small, 512)
    row2 = lambda v: v.reshape(1, -1)

    meta_rows = jnp.broadcast_to(meta_tokens[None], (n_batch, META_LEN, D_MODEL)).reshape(n_meta, D_MODEL)
    xb = x_prompt.reshape(n_real, D_MODEL)
    xs = jnp.concatenate([meta_rows.astype(F32), x_sample.reshape(n_samp, D_MODEL)], axis=0)

    def moe_layer(i, a_big, a_small, w, b, xb, xs):
        w_bf = w.astype(BF16)
        wr = moe_w_router[i].T.astype(BF16)
        br = moe_b_router[i].reshape(N_EXPERTS, 1)
        lng, lnb = row2(ln_mix_g[i]), row2(ln_mix_b[i])
        zero_counts = jnp.zeros((N_EXPERTS, LANES), jnp.int32)
        x1b, idb, gab, rab, cnt = _proj_ln_route(a_big, w_bf, row2(b), xb, lng, lnb, wr, br, zero_counts, tb)
        x1s, ids_, gas, ras, cnt = _proj_ln_route(a_small, w_bf, row2(b), xs, lng, lnb, wr, br, cnt, ts)
        n_all = moe_w_gate_up.shape[0] * N_EXPERTS
        return _moe([(x1b, idb, gab, rab, tb), (x1s, ids_, gas, ras, ts)], cnt, i * N_EXPERTS,
                    moe_w_gate_up.reshape(n_all, D_MODEL, 2 * MOE_FF),
                    moe_b_gate_up.reshape(n_all, 1, 2 * MOE_FF),
                    moe_w_down.reshape(n_all, MOE_FF, D_MODEL),
                    moe_b_down.reshape(n_all, 1, D_MODEL),
                    ln_ffn_g[i].reshape(N_SLAB, LANES), ln_ffn_b[i].reshape(N_SLAB, LANES))

    w_qkv = attn_w_qkv[0].astype(BF16)
    b_qkv = row2(attn_b_qkv[0])
    pos_small = jnp.concatenate([jnp.tile(jnp.arange(META_LEN), n_batch),
                                 jnp.tile(PAST_LEN + jnp.arange(dec_seq), dec_batch)])
    pos_big = META_LEN + jnp.arange(seq)
    qtb, kb, vb, vtb = _qkv_rope_t(xb, w_qkv, b_qkv, *_rope_tables(pos_big), *_rope_tables_t(pos_big), tb)
    qs, ks, vs = _qkv_rope(xs, w_qkv, b_qkv, *_rope_tables(pos_small), ts)
    sinks = attn_sinks[0]
    cmk = cache_attn_meta_k[0].reshape(dec_batch, META_LEN, KV_DIM)
    cmv = cache_attn_meta_v[0].reshape(dec_batch, META_LEN, KV_DIM)
    cwk = cache_attn_win_k[0].reshape(dec_batch, WINDOW, KV_DIM)
    cwv = cache_attn_win_v[0].reshape(dec_batch, WINDOW, KV_DIM)
    vt_meta = vs[:n_meta].reshape(n_batch, META_LEN, KV_DIM).transpose(0, 2, 1).astype(BF16)
    o_big = _attn_prompt(qtb, kb, vtb, ks, vt_meta, sinks, n_batch, seq)
    o_small = jnp.concatenate([
        _attn_meta(qs, ks, vs, sinks, n_batch),
        _attn_sample(qs, ks, vs, cmk, cmv, cwk, cwv, sinks, n_meta, dec_batch, dec_seq)], axis=0)
    xb, xs = moe_layer(0, o_big, o_small, attn_w_o[0], attn_b_o[0], xb, xs)

    kv4 = lambda a, n, t_len: a.reshape(n, t_len, N_KV_HEADS, HEAD_DIM)
    p_meta_k = kv4(ks[:n_meta], n_batch, META_LEN)[None]
    p_meta_v = kv4(vs[:n_meta], n_batch, META_LEN)[None]
    p_win_k = kv4(kb, n_batch, seq)[:, seq - WINDOW:][None]
    p_win_v = kv4(vb, n_batch, seq)[:, seq - WINDOW:][None]
    k_new = kv4(ks[n_meta:], dec_batch, dec_seq)
    v_new = kv4(vs[n_meta:], dec_batch, dec_seq)
    s_win_k = jnp.concatenate([cache_attn_win_k[0], k_new], axis=1)[:, -WINDOW:][None]
    s_win_v = jnp.concatenate([cache_attn_win_v[0], v_new], axis=1)[:, -WINDOW:][None]

    w_pw1 = conv_w_pw1[0].astype(BF16)
    ub = _pw1_glu(xb, w_pw1, row2(conv_b_pw1[0]), tb)
    us = _pw1_glu(xs, w_pw1, row2(conv_b_pw1[0]), ts)
    slab = lambda v: v.reshape(N_SLAB, LANES)
    w_dw = jnp.pad(conv_w_dw[0], ((0, CONV_HALO - CONV_WIDTH), (0, 0)))
    w3 = w_dw.reshape(CONV_HALO, N_SLAB, LANES).transpose(1, 0, 2)
    conv_args = (w3, slab(conv_b_dw[0]), slab(conv_ln_g[0]), slab(conv_ln_b[0]))
    z_big = _conv_prompt(ub, us, *conv_args, n_batch, seq, tb)
    z_meta = _conv_short(jnp.zeros((n_batch, CONV_HALO, D_MODEL), F32), us, 0, *conv_args, META_LEN, 1)
    ctx_pad = jnp.pad(state_conv[0], ((0, 0), (CONV_HALO - CONV_CTX, 0), (0, 0)))
    z_samp = _conv_short(ctx_pad, us, n_meta, *conv_args, dec_seq, 2)
    z_small = jnp.concatenate([z_meta, z_samp], axis=0).astype(BF16)
    xb, xs = moe_layer(1, z_big, z_small, conv_w_pw2[0], conv_b_pw2[0], xb, xs)

    p_conv = ub.reshape(n_batch, seq, D_MODEL)[:, seq - CONV_CTX:][None]
    u_samp = us[n_meta:].reshape(dec_batch, dec_seq, D_MODEL)
    s_conv = jnp.concatenate([state_conv[0], u_samp], axis=1)[:, -CONV_CTX:][None]

    y_prompt = xb.reshape(n_batch, seq, D_MODEL)
    y_sample = xs[n_meta:].reshape(dec_batch, dec_seq, D_MODEL)
    return (y_prompt, y_sample, p_meta_k, p_meta_v, p_win_k, p_win_v, p_conv,
            s_win_k, s_win_v, s_conv)
```

```python
import functools

import numpy as np
import jax
import jax.numpy as jnp
from jax import lax
from jax.experimental import pallas as pl
from jax.experimental.pallas import tpu as pltpu

F32 = jnp.float32
BF16 = jnp.bfloat16

D_MODEL = 1024
HEAD_DIM = 64
N_HEADS = 16
N_KV_HEADS = 4
GROUP = N_HEADS // N_KV_HEADS
KV_DIM = N_KV_HEADS * HEAD_DIM
QKV_DIM = D_MODEL + 2 * KV_DIM
ROT_DIM = 16
ROPE_THETA = 500000.0
WINDOW = 128
ATTN_BLOCK = 128
ATTN_SCALE = HEAD_DIM ** -0.5
LOG2_E = 1.4426950408889634
Q_SCALE = ATTN_SCALE * LOG2_E
META_LEN = 16
CONV_WIDTH = 31
CONV_CTX = CONV_WIDTH - 1
N_EXPERTS = 32
TOP_K = 4
MOE_FF = 1024
SWIGLU_LIMIT = 7.0
SWIGLU_ALPHA = 1.702
LN_EPS = 1e-5
DEPTH = 2
DEEPNORM_ALPHA = (2 * DEPTH) ** 0.25
PAST_LEN = 16384
NEG_INF = -1e30

LANES = 128
SUBLANES = 8
GMM_TILE = 512
CONV_HALO = 32
CONV_BLOCK = 32
LN_ROWS = 16
N_SLAB = D_MODEL // LANES
COMBINE_ROWS = 32
HEADS_PER_DOT = 2
VMEM_LIMIT = 56 * 1024 * 1024


def _cparams(n_axes):
    return pltpu.CompilerParams(dimension_semantics=("arbitrary",) * n_axes,
                                vmem_limit_bytes=VMEM_LIMIT)


def _pick_tile(n, cap):
    best = None
    t = LANES
    while t <= cap:
        if n % t == 0:
            best = t
        t += LANES
    assert best is not None, n
    return best


def _layer_norm(h, g, b):
    mu = jnp.mean(h, axis=-1, keepdims=True)
    hc = h - mu
    var = jnp.mean(hc * hc, axis=-1, keepdims=True)
    return hc * lax.rsqrt(var + LN_EPS) * g + b


def _qkv_kernel(x_ref, w_ref, b_ref, c_ref, a_ref, s_ref, q_ref, k_ref, v_ref):
    x = x_ref[...].astype(BF16)
    acc = jnp.dot(x, w_ref[...], preferred_element_type=F32) + b_ref[...]
    c = c_ref[...]
    a = a_ref[...]
    s = s_ref[...]

    def rope(t):
        return t * c + pltpu.roll(t, LANES - ROT_DIM // 2, 1) * a + pltpu.roll(t, ROT_DIM // 2, 1) * s

    for j in range(D_MODEL // LANES):
        sl = slice(j * LANES, (j + 1) * LANES)
        q_ref[:, sl] = (rope(acc[:, sl]) * Q_SCALE).astype(BF16)
    for j in range(KV_DIM // LANES):
        sl = slice(D_MODEL + j * LANES, D_MODEL + (j + 1) * LANES)
        k_ref[:, j * LANES:(j + 1) * LANES] = rope(acc[:, sl])
    v_ref[...] = acc[:, D_MODEL + KV_DIM:]


def _qkv_rope(x, w_bf, b, cos_t, sa_t, sb_t, tm):
    t = x.shape[0]
    period = cos_t.shape[0] // tm
    row = lambda i: (i, 0)
    tab = lambda i: (i % period, 0)
    fixed = lambda i: (0, 0)
    return pl.pallas_call(
        _qkv_kernel,
        grid=(t // tm,),
        in_specs=[pl.BlockSpec((tm, D_MODEL), row),
                  pl.BlockSpec((D_MODEL, QKV_DIM), fixed),
                  pl.BlockSpec((1, QKV_DIM), fixed),
                  pl.BlockSpec((tm, LANES), tab),
                  pl.BlockSpec((tm, LANES), tab),
                  pl.BlockSpec((tm, LANES), tab)],
        out_specs=[pl.BlockSpec((tm, D_MODEL), row),
                   pl.BlockSpec((tm, KV_DIM), row),
                   pl.BlockSpec((tm, KV_DIM), row)],
        out_shape=[jax.ShapeDtypeStruct((t, D_MODEL), BF16),
                   jax.ShapeDtypeStruct((t, KV_DIM), F32),
                   jax.ShapeDtypeStruct((t, KV_DIM), F32)],
        compiler_params=_cparams(1),
        name="qkv_rope",
    )(x, w_bf, b, cos_t, sa_t, sb_t)


def _qkv_t_kernel(x_ref, wqt_ref, bq_ref, wkv_ref, bkv_ref, wvt_ref, bv_ref, c_ref, a_ref, s_ref,
                  ct_ref, st_ref, qt_ref, k_ref, v_ref, vt_ref):
    x = x_ref[...].astype(BF16)
    nt = (((1,), (1,)), ((), ()))
    qt = lax.dot_general(wqt_ref[...], x, nt, preferred_element_type=F32) + bq_ref[...]
    ct = ct_ref[...]
    st = st_ref[...]
    half = ROT_DIM // 2
    for h in range(N_HEADS):
        r0 = h * HEAD_DIM
        x1 = qt[r0:r0 + half]
        x2 = qt[r0 + half:r0 + ROT_DIM]
        rot = jnp.concatenate([x1 * ct - x2 * st, x2 * ct + x1 * st], axis=0)
        qt_ref[r0:r0 + ROT_DIM, :] = (rot * Q_SCALE).astype(BF16)
        qt_ref[r0 + ROT_DIM:r0 + HEAD_DIM, :] = (qt[r0 + ROT_DIM:r0 + HEAD_DIM] * Q_SCALE).astype(BF16)
    vt = lax.dot_general(wvt_ref[...], x, nt, preferred_element_type=F32) + bv_ref[...]
    vt_ref[...] = vt.astype(BF16)

    kv = jnp.dot(x, wkv_ref[...], preferred_element_type=F32) + bkv_ref[...]
    c = c_ref[...]
    a = a_ref[...]
    s = s_ref[...]
    for j in range(KV_DIM // LANES):
        t = kv[:, j * LANES:(j + 1) * LANES]
        k_ref[:, j * LANES:(j + 1) * LANES] = (
            t * c + pltpu.roll(t, LANES - half, 1) * a + pltpu.roll(t, half, 1) * s)
    v_ref[...] = kv[:, KV_DIM:]


def _qkv_rope_t(x, w_bf, b, cos_t, sa_t, sb_t, cos_tt, sin_tt, tm):
    t = x.shape[0]
    period = cos_t.shape[0] // tm
    row = lambda i: (i, 0)
    col = lambda i: (0, i)
    tab = lambda i: (i % period, 0)
    tabt = lambda i: (0, i % period)
    fixed = lambda i: (0, 0)
    wqt = w_bf[:, :D_MODEL].T
    wvt = w_bf[:, D_MODEL + KV_DIM:].T
    half = ROT_DIM // 2
    return pl.pallas_call(
        _qkv_t_kernel,
        grid=(t // tm,),
        in_specs=[pl.BlockSpec((tm, D_MODEL), row),
                  pl.BlockSpec((D_MODEL, D_MODEL), fixed),
                  pl.BlockSpec((D_MODEL, 1), fixed),
                  pl.BlockSpec((D_MODEL, 2 * KV_DIM), fixed),
                  pl.BlockSpec((1, 2 * KV_DIM), fixed),
                  pl.BlockSpec((KV_DIM, D_MODEL), fixed),
                  pl.BlockSpec((KV_DIM, 1), fixed),
                  pl.BlockSpec((tm, LANES), tab),
                  pl.BlockSpec((tm, LANES), tab),
                  pl.BlockSpec((tm, LANES), tab),
                  pl.BlockSpec((half, tm), tabt),
                  pl.BlockSpec((half, tm), tabt)],
        out_specs=[pl.BlockSpec((D_MODEL, tm), col),
                   pl.BlockSpec((tm, KV_DIM), row),
                   pl.BlockSpec((tm, KV_DIM), row),
                   pl.BlockSpec((KV_DIM, tm), col)],
        out_shape=[jax.ShapeDtypeStruct((D_MODEL, t), BF16),
                   jax.ShapeDtypeStruct((t, KV_DIM), F32),
                   jax.ShapeDtypeStruct((t, KV_DIM), F32),
                   jax.ShapeDtypeStruct((KV_DIM, t), BF16)],
        compiler_params=_cparams(1),
        name="qkv_rope_t",
    )(x, wqt, b[:, :D_MODEL].reshape(D_MODEL, 1), w_bf[:, D_MODEL:], b[:, D_MODEL:],
      wvt, b[:, D_MODEL + KV_DIM:].reshape(KV_DIM, 1), cos_t, sa_t, sb_t, cos_tt, sin_tt)


def _attend(q, kcat, vcat, bias, sink_ref, write):
    kgs = [kcat[:, g * HEAD_DIM:(g + 1) * HEAD_DIM] for g in range(N_KV_HEADS)]

    def scores(h):
        qh = q[:, h * HEAD_DIM:(h + 1) * HEAD_DIM]
        return lax.dot_general(qh, kgs[h // GROUP], (((1,), (1,)), ((), ())), preferred_element_type=F32) + bias

    s_next = scores(0)
    for h in range(N_HEADS):
        s = s_next
        if h + 1 < N_HEADS:
            s_next = scores(h + 1)
        vg = vcat[:, (h // GROUP) * HEAD_DIM:(h // GROUP + 1) * HEAD_DIM]
        sink = sink_ref[h] * LOG2_E
        m = jnp.maximum(jnp.max(s, axis=-1, keepdims=True), sink)
        p = jnp.exp2(s - m)
        denom = jnp.sum(p, axis=-1, keepdims=True) + jnp.exp2(sink - m)
        o = jnp.dot(p.astype(BF16), vg, preferred_element_type=F32)
        write(h, o / denom)


def _attn_prompt_kernel(sink_ref, qt_ref, kp_ref, ko_ref, km_ref, vtp_ref, vto_ref, vtm_ref, o_ref, acc_ref):
    i = pl.program_id(1)
    kcat = jnp.concatenate([kp_ref[...], ko_ref[...], km_ref[...]], axis=0).astype(BF16)
    vt = jnp.concatenate([vtp_ref[...], vto_ref[...], vtm_ref[...]], axis=1)
    nk = 2 * ATTN_BLOCK + META_LEN
    key = lax.broadcasted_iota(jnp.int32, (nk, ATTN_BLOCK), 0)
    qry = lax.broadcasted_iota(jnp.int32, (nk, ATTN_BLOCK), 1)
    first = jnp.where(i > 0, 0, 2 * ATTN_BLOCK)
    prev_ok = (key < ATTN_BLOCK) & (key >= qry + first)
    own_ok = (key >= ATTN_BLOCK) & (key - ATTN_BLOCK <= qry)
    mask = prev_ok | own_ok | (key >= 2 * ATTN_BLOCK)
    bias = jnp.where(mask, 0.0, NEG_INF)
    bias = jnp.concatenate([bias] * HEADS_PER_DOT, axis=1)
    lane = lax.broadcasted_iota(jnp.int32, (1, HEADS_PER_DOT * ATTN_BLOCK), 1)
    kgs = [kcat[:, g * HEAD_DIM:(g + 1) * HEAD_DIM] for g in range(N_KV_HEADS)]

    def scores(u):
        h0 = u * HEADS_PER_DOT
        rhs = jnp.concatenate([qt_ref[(h0 + j) * HEAD_DIM:(h0 + j + 1) * HEAD_DIM, :]
                               for j in range(HEADS_PER_DOT)], axis=1)
        return jnp.dot(kgs[h0 // GROUP], rhs, preferred_element_type=F32) + bias

    n_units = N_HEADS // HEADS_PER_DOT
    s_next = scores(0)
    for u in range(n_units):
        s = s_next
        if u + 1 < n_units:
            s_next = scores(u + 1)
        h0 = u * HEADS_PER_DOT
        g = h0 // GROUP
        sink = sink_ref[h0] * LOG2_E
        for j in range(1, HEADS_PER_DOT):
            sink = jnp.where(lane >= j * ATTN_BLOCK, sink_ref[h0 + j] * LOG2_E, sink)
        m = jnp.maximum(jnp.max(s, axis=0, keepdims=True), sink)
        p = jnp.exp2(s - m)
        denom = jnp.sum(p, axis=0, keepdims=True) + jnp.exp2(sink - m)
        o = jnp.dot(vt[g * HEAD_DIM:(g + 1) * HEAD_DIM, :], p.astype(BF16), preferred_element_type=F32)
        o = o * (1.0 / denom)
        for j in range(HEADS_PER_DOT):
            acc_ref[(h0 + j) * HEAD_DIM:(h0 + j + 1) * HEAD_DIM, :] = o[:, j * ATTN_BLOCK:(j + 1) * ATTN_BLOCK]
    o_ref[...] = acc_ref[...].T.astype(o_ref.dtype)


def _attn_prompt(qt, k, vt, k_meta, vt_meta, sinks, n_batch, seq):
    nblk = seq // ATTN_BLOCK
    own = lambda b, i: b * nblk + i
    prev = lambda b, i: b * nblk + jnp.maximum(i - 1, 0)
    kspec = lambda m: pl.BlockSpec((ATTN_BLOCK, KV_DIM), lambda b, i: (m(b, i), 0))
    vspec = lambda m: pl.BlockSpec((KV_DIM, ATTN_BLOCK), lambda b, i: (0, m(b, i)))
    return pl.pallas_call(
        _attn_prompt_kernel,
        grid=(n_batch, nblk),
        in_specs=[pl.BlockSpec(memory_space=pltpu.SMEM),
                  pl.BlockSpec((D_MODEL, ATTN_BLOCK), lambda b, i: (0, own(b, i))),
                  kspec(prev), kspec(own),
                  pl.BlockSpec((META_LEN, KV_DIM), lambda b, i: (b, 0)),
                  vspec(prev), vspec(own),
                  pl.BlockSpec((None, KV_DIM, META_LEN), lambda b, i: (b, 0, 0))],
        out_specs=pl.BlockSpec((ATTN_BLOCK, D_MODEL), lambda b, i: (own(b, i), 0)),
        out_shape=jax.ShapeDtypeStruct((n_batch * seq, D_MODEL), BF16),
        scratch_shapes=[pltpu.VMEM((D_MODEL, ATTN_BLOCK), F32)],
        compiler_params=_cparams(2),
        name="attn_prompt",
    )(sinks, qt, k, k, k_meta, vt, vt, vt_meta)


def _attn_meta_kernel(sink_ref, q_ref, k_ref, v_ref, o_ref):
    r = lax.broadcasted_iota(jnp.int32, (META_LEN, META_LEN), 0)
    c = lax.broadcasted_iota(jnp.int32, (META_LEN, META_LEN), 1)

    def write(h, o):
        o_ref[:, h * HEAD_DIM:(h + 1) * HEAD_DIM] = o.astype(o_ref.dtype)

    bias = jnp.where(c <= r, 0.0, NEG_INF)
    _attend(q_ref[...], k_ref[...].astype(BF16), v_ref[...].astype(BF16), bias, sink_ref, write)


def _attn_meta(q, k, v, sinks, n_batch):
    imap = lambda b: (b, 0)
    return pl.pallas_call(
        _attn_meta_kernel,
        grid=(n_batch,),
        in_specs=[pl.BlockSpec(memory_space=pltpu.SMEM),
                  pl.BlockSpec((META_LEN, D_MODEL), imap),
                  pl.BlockSpec((META_LEN, KV_DIM), imap),
                  pl.BlockSpec((META_LEN, KV_DIM), imap)],
        out_specs=pl.BlockSpec((META_LEN, D_MODEL), lambda b: (b, 0)),
        out_shape=jax.ShapeDtypeStruct((n_batch * META_LEN, D_MODEL), BF16),
        compiler_params=_cparams(1),
        name="attn_meta",
    )(sinks, q, k, v)


SAMPLE_GROUP = 8


def _attn_sample_kernel(sink_ref, bias_ref, q_ref, kn_ref, vn_ref, cmk_ref, cmv_ref, cwk_ref, cwv_ref,
                        o_ref, *, dec_seq, group):
    def keys(cm_ref, cw_ref, new):
        parts = []
        for j in range(group):
            parts += [cm_ref[j], cw_ref[j], new[j * dec_seq:(j + 1) * dec_seq]]
        return jnp.concatenate(parts, axis=0).astype(BF16)

    kcat = keys(cmk_ref, cwk_ref, kn_ref[...])
    vcat = keys(cmv_ref, cwv_ref, vn_ref[...])

    def write(h, o):
        o_ref[:, h * HEAD_DIM:(h + 1) * HEAD_DIM] = o.astype(o_ref.dtype)

    _attend(q_ref[...], kcat, vcat, bias_ref[...], sink_ref, write)


def _sample_bias(group, dec_seq):
    per = META_LEN + WINDOW + dec_seq
    r = np.arange(group * dec_seq)[:, None]
    c = np.arange(group * per)[None, :]
    tq, ck = r % dec_seq, c % per
    win_ok = (ck >= META_LEN) & (ck < META_LEN + WINDOW) & (ck - META_LEN >= tq)
    new_ok = (ck >= META_LEN + WINDOW) & (ck - (META_LEN + WINDOW) <= tq)
    vis = (r // dec_seq == c // per) & ((ck < META_LEN) | win_ok | new_ok)
    return np.where(vis, 0.0, NEG_INF).astype(np.float32)


def _attn_sample(q, k, v, cmk, cmv, cwk, cwv, sinks, row0, dec_batch, dec_seq):
    group = SAMPLE_GROUP
    nq = group * dec_seq
    blk0 = row0 // nq
    bias = jnp.asarray(_sample_bias(group, dec_seq))
    qmap = lambda n: (blk0 + n, 0)
    cmap = lambda n: (n, 0, 0)
    return pl.pallas_call(
        functools.partial(_attn_sample_kernel, dec_seq=dec_seq, group=group),
        grid=(dec_batch // group,),
        in_specs=[pl.BlockSpec(memory_space=pltpu.SMEM),
                  pl.BlockSpec(bias.shape, lambda n: (0, 0)),
                  pl.BlockSpec((nq, D_MODEL), qmap),
                  pl.BlockSpec((nq, KV_DIM), qmap),
                  pl.BlockSpec((nq, KV_DIM), qmap),
                  pl.BlockSpec((group, META_LEN, KV_DIM), cmap),
                  pl.BlockSpec((group, META_LEN, KV_DIM), cmap),
                  pl.BlockSpec((group, WINDOW, KV_DIM), cmap),
                  pl.BlockSpec((group, WINDOW, KV_DIM), cmap)],
        out_specs=pl.BlockSpec((nq, D_MODEL), lambda n: (n, 0)),
        out_shape=jax.ShapeDtypeStruct((dec_batch * dec_seq, D_MODEL), BF16),
        compiler_params=_cparams(1),
        name="attn_sample",
    )(sinks, bias, q, k, v, cmk, cmv, cwk, cwv)


def _proj_ln_route_kernel(a_ref, w_ref, b_ref, x_ref, g_ref, bb_ref, wr_ref, br_ref, tri_ref, cin_ref,
                          x1_ref, ids_ref, gates_ref, rank_ref, cnt_ref, carry_ref):
    @pl.when(pl.program_id(0) == 0)
    def _():
        carry_ref[...] = cin_ref[...].astype(F32)

    y = jnp.dot(a_ref[...], w_ref[...], preferred_element_type=F32) + b_ref[...]
    x1 = _layer_norm(DEEPNORM_ALPHA * x_ref[...] + y, g_ref[...], bb_ref[...])
    for j in range(N_SLAB):
        x1_ref[pl.ds(j, x1.shape[0], stride=N_SLAB), :] = x1[:, j * LANES:(j + 1) * LANES]

    logits = lax.dot_general(wr_ref[...], x1.astype(BF16), (((1,), (1,)), ((), ())),
                             preferred_element_type=F32) + br_ref[...]
    tm = logits.shape[1]
    eidx = lax.broadcasted_iota(jnp.int32, (N_EXPERTS, tm), 0).astype(F32)
    cur = logits
    vals, idxs, sels = [], [], []
    for _ in range(TOP_K):
        m = jnp.max(cur, axis=0, keepdims=True)
        idx = jnp.min(jnp.where(cur == m, eidx, float(N_EXPERTS)), axis=0, keepdims=True)
        sel = eidx == idx
        vals.append(m)
        idxs.append(idx)
        sels.append(sel)
        cur = jnp.where(sel, -jnp.inf, cur)
    exps = [jnp.exp(v - vals[0]) for v in vals]
    tot = exps[0] + exps[1] + exps[2] + exps[3]
    gates_ref[...] = jnp.concatenate([e / tot for e in exps], axis=0)
    ids_ref[...] = jnp.concatenate(idxs, axis=0).astype(jnp.int32)

    chosen = jnp.where(sels[0] | sels[1] | sels[2] | sels[3], 1.0, 0.0)
    before = jnp.dot(chosen.astype(BF16), tri_ref[...], preferred_element_type=F32)
    before = before + carry_ref[:, 0:1]
    ranks = [jnp.sum(jnp.where(s, before, 0.0), axis=0, keepdims=True) for s in sels]
    rank_ref[...] = jnp.concatenate(ranks, axis=0).astype(jnp.int32)
    carry_ref[...] = carry_ref[...] + jnp.sum(chosen, axis=1, keepdims=True)
    cnt_ref[...] = carry_ref[...].astype(jnp.int32)


def _proj_ln_route(a_bf, w_bf, b, x, g, bb, wr_t_bf, br_col, counts_in, tm):
    t = x.shape[0]
    row = lambda i: (i, 0)
    col = lambda i: (0, i)
    fixed = lambda i: (0, 0)
    tri = (jnp.arange(tm)[:, None] < jnp.arange(tm)[None, :]).astype(BF16)
    return pl.pallas_call(
        _proj_ln_route_kernel,
        grid=(t // tm,),
        in_specs=[pl.BlockSpec((tm, D_MODEL), row),
                  pl.BlockSpec((D_MODEL, D_MODEL), fixed),
                  pl.BlockSpec((1, D_MODEL), fixed),
                  pl.BlockSpec((tm, D_MODEL), row),
                  pl.BlockSpec((1, D_MODEL), fixed),
                  pl.BlockSpec((1, D_MODEL), fixed),
                  pl.BlockSpec((N_EXPERTS, D_MODEL), fixed),
                  pl.BlockSpec((N_EXPERTS, 1), fixed),
                  pl.BlockSpec((tm, tm), fixed),
                  pl.BlockSpec((N_EXPERTS, LANES), fixed)],
        out_specs=[pl.BlockSpec((tm * N_SLAB, LANES), row),
                   pl.BlockSpec((TOP_K, tm), col),
                   pl.BlockSpec((TOP_K, tm), col),
                   pl.BlockSpec((TOP_K, tm), col),
                   pl.BlockSpec((N_EXPERTS, LANES), fixed)],
        out_shape=[jax.ShapeDtypeStruct((t * N_SLAB, LANES), F32),
                   jax.ShapeDtypeStruct((TOP_K, t), jnp.int32),
                   jax.ShapeDtypeStruct((TOP_K, t), F32),
                   jax.ShapeDtypeStruct((TOP_K, t), jnp.int32),
                   jax.ShapeDtypeStruct((N_EXPERTS, LANES), jnp.int32)],
        scratch_shapes=[pltpu.VMEM((N_EXPERTS, LANES), F32)],
        compiler_params=_cparams(1),
        name="proj_ln_route",
    )(a_bf, w_bf, b, x, g, bb, wr_t_bf, br_col, tri, counts_in)


def _idx_copy(dest_hbm, dsm, sem, step, stride):
    slot = step % 2
    return pltpu.make_async_copy(dest_hbm.at[pl.ds(pl.multiple_of(step * stride, stride), stride)],
                                 dsm.at[pl.ds(pl.multiple_of(slot * stride, stride), stride)],
                                 sem)


def _dispatch_kernel(pstart_ref, padded_ref, nact_ref, dest_hbm, *rest, tiles, steps, stride, n_tiles):
    x_refs = rest[:len(tiles)]
    xs_hbm, dsm, zbuf, sems = rest[len(tiles):]
    i = pl.program_id(0)
    nsteps = pl.num_programs(0)

    def zero_tile(row0):
        n = GMM_TILE * N_SLAB
        return pltpu.make_async_copy(zbuf, xs_hbm.at[pl.ds(pl.multiple_of(row0 * N_SLAB, n), n)], sems.at[2])

    def token(ref, r):
        return ref.at[pl.ds(pl.multiple_of(r * N_SLAB, N_SLAB), N_SLAB)]

    @pl.when(i == 0)
    def _():
        zbuf[...] = jnp.zeros_like(zbuf)

        def each_expert(fn):
            def body(e, _):
                @pl.when(padded_ref[e] > 0)
                def _():
                    fn(zero_tile(pstart_ref[e] + padded_ref[e] - GMM_TILE))
                return 0
            lax.fori_loop(0, N_EXPERTS, body, 0)

        def each_tail(fn):
            def body(t, _):
                fn(zero_tile(t * GMM_TILE))
                return 0
            lax.fori_loop(nact_ref[0], n_tiles, body, 0)

        each_expert(lambda cp: cp.start())
        each_tail(lambda cp: cp.start())
        each_expert(lambda cp: cp.wait())
        each_tail(lambda cp: cp.wait())
        _idx_copy(dest_hbm, dsm, sems.at[0], i, stride).start()

    _idx_copy(dest_hbm, dsm, sems.at[0], i, stride).wait()

    @pl.when(i + 1 < nsteps)
    def _():
        _idx_copy(dest_hbm, dsm, sems.at[0], i + 1, stride).start()

    base = (i % 2) * stride
    step0 = 0
    for x_ref, tm, n in zip(x_refs, tiles, steps):
        @pl.when((i >= step0) & (i < step0 + n))
        def _(x_ref=x_ref, tm=tm):
            def row_body(r, _):
                for k in range(TOP_K):
                    d = dsm[base + k * tm + r]
                    pltpu.make_async_copy(token(x_ref, r), token(xs_hbm, d), sems.at[1]).start(priority=k % 2)
                return 0

            lax.fori_loop(0, tm, row_body, 0, unroll=2)
            for k in range(TOP_K):
                pltpu.make_async_copy(x_ref, xs_hbm.at[pl.ds(0, tm * N_SLAB)], sems.at[1]).wait()
        step0 += n


def _seg_map(step0, n):
    return lambda i, *_: (jnp.clip(i - step0, 0, n - 1), 0)


def _dispatch(xs_list, tiles, dest_steps, pstart, padded, nact, stride, n_tiles):
    steps = [x.shape[0] // (tm * N_SLAB) for x, tm in zip(xs_list, tiles)]
    in_specs = [pl.BlockSpec(memory_space=pl.ANY)]
    step0 = 0
    for tm, n in zip(tiles, steps):
        in_specs.append(pl.BlockSpec((tm * N_SLAB, LANES), _seg_map(step0, n)))
        step0 += n
    grid_spec = pltpu.PrefetchScalarGridSpec(
        num_scalar_prefetch=3,
        grid=(sum(steps),),
        in_specs=in_specs,
        out_specs=pl.BlockSpec(memory_space=pl.ANY),
        scratch_shapes=[pltpu.SMEM((2 * stride,), jnp.int32),
                        pltpu.VMEM((GMM_TILE * N_SLAB, LANES), F32),
                        pltpu.SemaphoreType.DMA((3,))],
    )
    return pl.pallas_call(
        functools.partial(_dispatch_kernel, tiles=tuple(tiles), steps=tuple(steps), stride=stride,
                          n_tiles=n_tiles),
        grid_spec=grid_spec,
        out_shape=jax.ShapeDtypeStruct((n_tiles * GMM_TILE * N_SLAB, LANES), F32),
        compiler_params=_cparams(1),
        name="moe_dispatch",
    )(pstart, padded, nact, dest_steps, *xs_list)


def _gmm_kernel(te_ref, tsrc_ref, tfirst_ref, nact_ref, x_ref, wgu_ref, bgu_ref, wdn_ref, bdn_ref, y_ref,
                wgu_bf, wdn_bf):
    i = pl.program_id(0)

    @pl.when(tfirst_ref[i] == 1)
    def _():
        wgu_bf[...] = wgu_ref[...].astype(BF16)
        wdn_bf[...] = wdn_ref[...].astype(BF16)

    @pl.when(i < nact_ref[0])
    def _():
        x = jnp.concatenate([x_ref[pl.ds(j, GMM_TILE, stride=N_SLAB), :] for j in range(N_SLAB)], axis=1)
        gu = jnp.dot(x.astype(BF16), wgu_bf[...], preferred_element_type=F32) + bgu_ref[...]
        gate = jnp.minimum(gu[:, :MOE_FF], SWIGLU_LIMIT)
        up = jnp.clip(gu[:, MOE_FF:], -SWIGLU_LIMIT, SWIGLU_LIMIT)
        glu = gate * jax.nn.sigmoid(SWIGLU_ALPHA * gate)
        h = ((up + 1.0) * glu).astype(BF16)
        y = jnp.dot(h, wdn_bf[...], preferred_element_type=F32) + bdn_ref[...]
        for j in range(N_SLAB):
            y_ref[pl.ds(j, GMM_TILE, stride=N_SLAB), :] = y[:, j * LANES:(j + 1) * LANES]

    @pl.when(i >= nact_ref[0])
    def _():
        y_ref[...] = jnp.zeros_like(y_ref)


def _gmm(xs, te, tsrc, tfirst, nact, wgu, bgu, wdn, bdn, n_tiles):
    emap = lambda i, te, *_: (te[i], 0, 0)
    grid_spec = pltpu.PrefetchScalarGridSpec(
        num_scalar_prefetch=4,
        grid=(n_tiles,),
        in_specs=[pl.BlockSpec((GMM_TILE * N_SLAB, LANES), lambda i, te, ts, *_: (ts[i], 0)),
                  pl.BlockSpec((None, D_MODEL, 2 * MOE_FF), emap),
                  pl.BlockSpec((None, 1, 2 * MOE_FF), emap),
                  pl.BlockSpec((None, MOE_FF, D_MODEL), emap),
                  pl.BlockSpec((None, 1, D_MODEL), emap)],
        out_specs=pl.BlockSpec((GMM_TILE * N_SLAB, LANES), lambda i, *_: (i, 0)),
        scratch_shapes=[pltpu.VMEM((D_MODEL, 2 * MOE_FF), BF16),
                        pltpu.VMEM((MOE_FF, D_MODEL), BF16)],
    )
    return pl.pallas_call(
        _gmm_kernel,
        grid_spec=grid_spec,
        out_shape=jax.ShapeDtypeStruct((n_tiles * GMM_TILE * N_SLAB, LANES), F32),
        compiler_params=_cparams(1),
        name="moe_gmm",
    )(te, tsrc, tfirst, nact, xs, wgu, bgu, wdn, bdn)


def _combine_kernel(dest_hbm, gate_hbm, ys_hbm, x_ref, g_ref, bb_ref, o_ref, dsm, gsm, buf, sems,
                    *, tm, stride):
    i = pl.program_id(0)
    nsteps = pl.num_programs(0)

    def idx_copies(step):
        return (_idx_copy(dest_hbm, dsm, sems.at[0], step, stride),
                _idx_copy(gate_hbm, gsm, sems.at[2], step, stride))

    @pl.when(i == 0)
    def _():
        for cp in idx_copies(i):
            cp.start()

    for cp in idx_copies(i):
        cp.wait()

    @pl.when(i + 1 < nsteps)
    def _():
        for cp in idx_copies(i + 1):
            cp.start()

    base = (i % 2) * stride

    def token(ref, r):
        return ref.at[pl.ds(pl.multiple_of(r * N_SLAB, N_SLAB), N_SLAB)]

    def row_body(r, _):
        for k in range(TOP_K):
            d = dsm[base + k * tm + r]
            pltpu.make_async_copy(token(ys_hbm, d), token(buf.at[k], r), sems.at[1]).start(priority=k % 2)
        return 0

    lax.fori_loop(0, tm, row_body, 0, unroll=2)
    for k in range(TOP_K):
        pltpu.make_async_copy(ys_hbm.at[pl.ds(0, tm * N_SLAB)], buf.at[k], sems.at[1]).wait()

    def token_sum(r, _):
        rows = pl.ds(pl.multiple_of(r * N_SLAB, N_SLAB), N_SLAB)
        h = DEEPNORM_ALPHA * x_ref[rows, :]
        for k in range(TOP_K):
            h = h + gsm[base + k * tm + r] * buf[k, rows, :]
        buf[0, rows, :] = h
        return 0

    lax.fori_loop(0, tm, token_sum, 0, unroll=8)

    def block(b, _):
        r0 = pl.multiple_of(b * COMBINE_ROWS, COMBINE_ROWS)
        hs = [buf[0, pl.ds(r0 * N_SLAB + j, COMBINE_ROWS, stride=N_SLAB), :] for j in range(N_SLAB)]
        tot = hs[0]
        for j in range(1, N_SLAB):
            tot = tot + hs[j]
        mu = jnp.sum(tot, axis=1, keepdims=True) * (1.0 / D_MODEL)
        cen = [h - mu for h in hs]
        sq = cen[0] * cen[0]
        for j in range(1, N_SLAB):
            sq = sq + cen[j] * cen[j]
        inv = lax.rsqrt(jnp.sum(sq, axis=1, keepdims=True) * (1.0 / D_MODEL) + LN_EPS)
        for j in range(N_SLAB):
            o_ref[pl.ds(r0, COMBINE_ROWS), j * LANES:(j + 1) * LANES] = (
                cen[j] * inv * g_ref[j:j + 1, :] + bb_ref[j:j + 1, :])
        return 0

    lax.fori_loop(0, tm // COMBINE_ROWS, block, 0, unroll=2)


def _combine(ys, dest_steps, gate_steps, x_tiles, g3, bb3, tm, stride):
    t = x_tiles.shape[0] // N_SLAB
    row = lambda i: (i, 0)
    fixed = lambda i: (0, 0)
    return pl.pallas_call(
        functools.partial(_combine_kernel, tm=tm, stride=stride),
        grid=(t // tm,),
        in_specs=[pl.BlockSpec(memory_space=pl.ANY),
                  pl.BlockSpec(memory_space=pl.ANY),
                  pl.BlockSpec(memory_space=pl.ANY),
                  pl.BlockSpec((tm * N_SLAB, LANES), row),
                  pl.BlockSpec((N_SLAB, LANES), fixed),
                  pl.BlockSpec((N_SLAB, LANES), fixed)],
        out_specs=pl.BlockSpec((tm, D_MODEL), row),
        out_shape=jax.ShapeDtypeStruct((t, D_MODEL), F32),
        scratch_shapes=[pltpu.SMEM((2 * stride,), jnp.int32),
                        pltpu.SMEM((2 * stride,), F32),
                        pltpu.VMEM((TOP_K, tm * N_SLAB, LANES), F32),
                        pltpu.SemaphoreType.DMA((3,))],
        compiler_params=_cparams(1),
        name="moe_combine",
    )(dest_steps, gate_steps, ys, x_tiles, g3, bb3)


def _moe(segs, counts, expert0, wgu, bgu, wdn, bdn, g, bb):
    t = sum(seg[1].shape[1] for seg in segs)
    n_tiles = -(-(t * TOP_K) // GMM_TILE) + N_EXPERTS
    counts = counts[:, 0]
    padded = ((counts + GMM_TILE - 1) // GMM_TILE) * GMM_TILE
    pend = jnp.cumsum(padded)
    pstart = (pend - padded).astype(jnp.int32)
    padded = padded.astype(jnp.int32)
    nact = (pend[-1] // GMM_TILE).astype(jnp.int32).reshape(1)
    tile = jnp.arange(n_tiles, dtype=jnp.int32)
    tsrc = jnp.minimum(tile, jnp.maximum(nact[0] - 1, 0))
    te = jnp.sum((pend[None, :] <= (tsrc * GMM_TILE)[:, None]).astype(jnp.int32), axis=1)
    te = jnp.minimum(te, N_EXPERTS - 1).astype(jnp.int32)
    tfirst = jnp.concatenate([jnp.ones((1,), jnp.int32), (te[1:] != te[:-1]).astype(jnp.int32)])
    eids = jnp.arange(N_EXPERTS, dtype=jnp.int32)[:, None, None]

    stride = -(-(TOP_K * max(seg[4] for seg in segs)) // 1024) * 1024
    def per_step(a, tm):
        n = a.shape[1]
        steps = a.reshape(TOP_K, n // tm, tm).transpose(1, 0, 2).reshape(n // tm, TOP_K * tm)
        return jnp.pad(steps, ((0, 0), (0, stride - TOP_K * tm))).reshape(-1)

    plans = []
    for x1, ids, gates, rank, tm in segs:
        dest = jnp.sum(jnp.where(ids[None] == eids, pstart[:, None, None], 0), axis=0) + rank
        plans.append((per_step(dest, tm), per_step(gates, tm)))
    xs = _dispatch([seg[0] for seg in segs], [seg[4] for seg in segs],
                   jnp.concatenate([p[0] for p in plans]), pstart, padded, nact, stride, n_tiles)
    ys = _gmm(xs, te + expert0, tsrc, tfirst, nact, wgu, bgu, wdn, bdn, n_tiles)
    return [_combine(ys, dest_steps, gate_steps, seg[0], g, bb, seg[4], stride)
            for seg, (dest_steps, gate_steps) in zip(segs, plans)]


def _pw1_glu_kernel(x_ref, w_ref, b_ref, u_ref):
    a = jnp.dot(x_ref[...].astype(BF16), w_ref[...], preferred_element_type=F32) + b_ref[...]
    u_ref[...] = a[:, :D_MODEL] * jax.nn.sigmoid(a[:, D_MODEL:])


def _pw1_glu(x, w_bf, b, tm):
    t = x.shape[0]
    row = lambda i: (i, 0)
    fixed = lambda i: (0, 0)
    return pl.pallas_call(
        _pw1_glu_kernel,
        grid=(t // tm,),
        in_specs=[pl.BlockSpec((tm, D_MODEL), row),
                  pl.BlockSpec((D_MODEL, 2 * D_MODEL), fixed),
                  pl.BlockSpec((1, 2 * D_MODEL), fixed)],
        out_specs=pl.BlockSpec((tm, D_MODEL), row),
        out_shape=jax.ShapeDtypeStruct((t, D_MODEL), F32),
        compiler_params=_cparams(1),
        name="pw1_glu",
    )(x, w_bf, b)


def _conv_rows(win_ref, zs_ref, base, n_sets, w_ref, b_ref):
    shift = CONV_HALO - CONV_CTX
    offs = tuple(range(n_sets))

    def out_rows(o):
        return pl.ds(base + o, SUBLANES, stride=n_sets)

    def slab(c, _):
        def window(t):
            return win_ref[c, pl.ds(base + (shift + t), SUBLANES, stride=n_sets), :]

        bias = b_ref[pl.ds(c, 1), :]
        acc = [bias] * n_sets
        wins = [window(t) for t in range(n_sets - 1)]
        for j in range(CONV_WIDTH):
            w = w_ref[c, j:j + 1, :]
            wins.append(window(j + n_sets - 1))
            acc = [acc[o] + w * wins[o] for o in offs]
            wins.pop(0)
        for o in offs:
            zs_ref[c, out_rows(o), :] = acc[o]
        return 0

    lax.fori_loop(0, N_SLAB, slab, 0)


def _ln_silu_rows(zs_ref, rows, g_ref, bb_ref):
    acc = [zs_ref[c, rows, :] for c in range(N_SLAB)]
    tot = acc[0]
    for c in range(1, N_SLAB):
        tot = tot + acc[c]
    mu = jnp.sum(tot, axis=1, keepdims=True) * (1.0 / D_MODEL)
    cen = [a - mu for a in acc]
    sq = cen[0] * cen[0]
    for c in range(1, N_SLAB):
        sq = sq + cen[c] * cen[c]
    inv = lax.rsqrt(jnp.sum(sq, axis=1, keepdims=True) * (1.0 / D_MODEL) + LN_EPS)
    out = []
    for c in range(N_SLAB):
        z = cen[c] * inv * g_ref[c:c + 1, :] + bb_ref[c:c + 1, :]
        out.append(z * jax.nn.sigmoid(z))
    return out


def _conv_prompt_kernel(um_ref, up_ref, uc_ref, w_ref, b_ref, g_ref, bb_ref, z_ref, win_ref, zs_ref, *, tr):
    i = pl.program_id(1)
    gap = CONV_HALO - META_LEN

    @pl.when(i == 0)
    def _():
        for c in range(N_SLAB):
            win_ref[c, 0:gap, :] = jnp.zeros((gap, LANES), F32)
            win_ref[c, gap:CONV_HALO, :] = um_ref[:, c * LANES:(c + 1) * LANES]

    @pl.when(i > 0)
    def _():
        for c in range(N_SLAB):
            win_ref[c, 0:CONV_HALO, :] = up_ref[:, c * LANES:(c + 1) * LANES]

    for c in range(N_SLAB):
        win_ref[c, CONV_HALO:, :] = uc_ref[:, c * LANES:(c + 1) * LANES]

    def block(bi, _):
        base = pl.multiple_of(bi * CONV_BLOCK, CONV_BLOCK)
        _conv_rows(win_ref, zs_ref, base, CONV_BLOCK // SUBLANES, w_ref, b_ref)
        return 0

    lax.fori_loop(0, tr // CONV_BLOCK, block, 0)

    def norm(bi, _):
        rows = pl.ds(pl.multiple_of(bi * LN_ROWS, LN_ROWS), LN_ROWS)
        for c, z in enumerate(_ln_silu_rows(zs_ref, rows, g_ref, bb_ref)):
            z_ref[rows, c * LANES:(c + 1) * LANES] = z.astype(z_ref.dtype)
        return 0

    lax.fori_loop(0, tr // LN_ROWS, norm, 0, unroll=4)


def _conv_prompt(u, u_meta, w3, b3, g3, bb3, n_batch, seq, tr):
    nt = seq // tr
    cur = lambda b, i: (b * nt + i, 0)
    prev = lambda b, i: (jnp.maximum((b * seq + i * tr) // CONV_HALO - 1, 0), 0)
    fixed2 = lambda b, i: (0, 0)
    fixed3 = lambda b, i: (0, 0, 0)
    return pl.pallas_call(
        functools.partial(_conv_prompt_kernel, tr=tr),
        grid=(n_batch, nt),
        in_specs=[pl.BlockSpec((META_LEN, D_MODEL), lambda b, i: (b, 0)),
                  pl.BlockSpec((CONV_HALO, D_MODEL), prev),
                  pl.BlockSpec((tr, D_MODEL), cur),
                  pl.BlockSpec((N_SLAB, CONV_HALO, LANES), fixed3),
                  pl.BlockSpec((N_SLAB, LANES), fixed2),
                  pl.BlockSpec((N_SLAB, LANES), fixed2),
                  pl.BlockSpec((N_SLAB, LANES), fixed2)],
        out_specs=pl.BlockSpec((tr, D_MODEL), cur),
        out_shape=jax.ShapeDtypeStruct((n_batch * seq, D_MODEL), BF16),
        scratch_shapes=[pltpu.VMEM((N_SLAB, CONV_HALO + tr, LANES), F32),
                        pltpu.VMEM((N_SLAB, tr, LANES), F32)],
        compiler_params=_cparams(2),
        name="conv_prompt",
    )(u_meta, u, u, w3, b3, g3, bb3)


SHORT_ROWS = 16


def _conv_short_kernel(ctx_ref, u_ref, w_ref, b_ref, g_ref, bb_ref, z_ref, win_ref, zs_ref, *, n_seq, t_len):
    for n in range(n_seq):
        for c in range(N_SLAB):
            sl = slice(c * LANES, (c + 1) * LANES)
            win_ref[c, 0:CONV_HALO, :] = ctx_ref[n, :, sl]
            win_ref[c, CONV_HALO:CONV_HALO + t_len, :] = u_ref[n * t_len:(n + 1) * t_len, sl]
            if t_len < SHORT_ROWS:
                win_ref[c, CONV_HALO + t_len:, :] = jnp.zeros((SHORT_ROWS - t_len, LANES), F32)
        _conv_rows(win_ref, zs_ref, 0, SHORT_ROWS // SUBLANES, w_ref, b_ref)
        for c, z in enumerate(_ln_silu_rows(zs_ref, slice(0, t_len), g_ref, bb_ref)):
            z_ref[n * t_len:(n + 1) * t_len, c * LANES:(c + 1) * LANES] = z


def _conv_short(ctx_pad, u, row0, w3, b3, g3, bb3, t_len, n_seq):
    n_total = ctx_pad.shape[0]
    rows = n_seq * t_len
    blk0 = row0 // rows
    fixed2 = lambda n: (0, 0)
    fixed3 = lambda n: (0, 0, 0)
    return pl.pallas_call(
        functools.partial(_conv_short_kernel, n_seq=n_seq, t_len=t_len),
        grid=(n_total // n_seq,),
        in_specs=[pl.BlockSpec((n_seq, CONV_HALO, D_MODEL), lambda n: (n, 0, 0)),
                  pl.BlockSpec((rows, D_MODEL), lambda n: (blk0 + n, 0)),
                  pl.BlockSpec((N_SLAB, CONV_HALO, LANES), fixed3),
                  pl.BlockSpec((N_SLAB, LANES), fixed2),
                  pl.BlockSpec((N_SLAB, LANES), fixed2),
                  pl.BlockSpec((N_SLAB, LANES), fixed2)],
        out_specs=pl.BlockSpec((rows, D_MODEL), lambda n: (n, 0)),
        out_shape=jax.ShapeDtypeStruct((n_total * t_len, D_MODEL), F32),
        scratch_shapes=[pltpu.VMEM((N_SLAB, CONV_HALO + SHORT_ROWS, LANES), F32),
                        pltpu.VMEM((N_SLAB, SHORT_ROWS, LANES), F32)],
        compiler_params=_cparams(1),
        name="conv_short",
    )(ctx_pad, u, w3, b3, g3, bb3)


def _rope_angles(pos):
    inv = 1.0 / (ROPE_THETA ** (jnp.arange(0, ROT_DIM, 2, dtype=F32) / ROT_DIM))
    ang = pos.astype(F32)[:, None] * inv[None, :]
    return jnp.cos(ang), jnp.sin(ang)


def _rope_tables_t(pos):
    cos, sin = _rope_angles(pos)
    return cos.T, sin.T


def _rope_tables(pos):
    half = ROT_DIM // 2
    cos, sin = _rope_angles(pos)
    n = pos.shape[0]
    ones = jnp.ones((n, HEAD_DIM - ROT_DIM), F32)
    zeros = jnp.zeros((n, HEAD_DIM - ROT_DIM), F32)
    zh = jnp.zeros((n, half), F32)
    c = jnp.concatenate([cos, cos, ones], axis=1)
    a = jnp.concatenate([-sin, zh, zeros], axis=1)
    s = jnp.concatenate([zh, sin, zeros], axis=1)
    rep = LANES // HEAD_DIM
    return jnp.tile(c, (1, rep)), jnp.tile(a, (1, rep)), jnp.tile(s, (1, rep))


def kernel(x_prompt, x_sample, cache_attn_meta_k, cache_attn_meta_v, cache_attn_win_k, cache_attn_win_v, state_conv, meta_tokens, attn_w_qkv, attn_b_qkv, attn_sinks, attn_w_o, attn_b_o, conv_w_pw1, conv_b_pw1, conv_w_dw, conv_b_dw, conv_ln_g, conv_ln_b, conv_w_pw2, conv_b_pw2, ln_mix_g, ln_mix_b, ln_ffn_g, ln_ffn_b, moe_w_router, moe_b_router, moe_w_gate_up, moe_b_gate_up, moe_w_down, moe_b_down):
    n_batch, seq, _ = x_prompt.shape
    dec_batch, dec_seq, _ = x_sample.shape
    n_real = n_batch * seq
    n_meta = n_batch * META_LEN
    n_samp = dec_batch * dec_seq
    n_small = n_meta + n_samp
    tb = _pick_tile(seq, 512)
    ts = _pick_tile(n_small, 512)
    row2 = lambda v: v.reshape(1, -1)

    meta_rows = jnp.broadcast_to(meta_tokens[None], (n_batch, META_LEN, D_MODEL)).reshape(n_meta, D_MODEL)
    xb = x_prompt.reshape(n_real, D_MODEL)
    xs = jnp.concatenate([meta_rows.astype(F32), x_sample.reshape(n_samp, D_MODEL)], axis=0)

    def moe_layer(i, a_big, a_small, w, b, xb, xs):
        w_bf = w.astype(BF16)
        wr = moe_w_router[i].T.astype(BF16)
        br = moe_b_router[i].reshape(N_EXPERTS, 1)
        lng, lnb = row2(ln_mix_g[i]), row2(ln_mix_b[i])
        zero_counts = jnp.zeros((N_EXPERTS, LANES), jnp.int32)
        x1b, idb, gab, rab, cnt = _proj_ln_route(a_big, w_bf, row2(b), xb, lng, lnb, wr, br, zero_counts, tb)
        x1s, ids_, gas, ras, cnt = _proj_ln_route(a_small, w_bf, row2(b), xs, lng, lnb, wr, br, cnt, ts)
        n_all = moe_w_gate_up.shape[0] * N_EXPERTS
        return _moe([(x1b, idb, gab, rab, tb), (x1s, ids_, gas, ras, ts)], cnt, i * N_EXPERTS,
                    moe_w_gate_up.reshape(n_all, D_MODEL, 2 * MOE_FF),
                    moe_b_gate_up.reshape(n_all, 1, 2 * MOE_FF),
                    moe_w_down.reshape(n_all, MOE_FF, D_MODEL),
                    moe_b_down.reshape(n_all, 1, D_MODEL),
                    ln_ffn_g[i].reshape(N_SLAB, LANES), ln_ffn_b[i].reshape(N_SLAB, LANES))

    w_qkv = attn_w_qkv[0].astype(BF16)
    b_qkv = row2(attn_b_qkv[0])
    pos_small = jnp.concatenate([jnp.tile(jnp.arange(META_LEN), n_batch),
                                 jnp.tile(PAST_LEN + jnp.arange(dec_seq), dec_batch)])
    pos_big = META_LEN + jnp.arange(seq)
    qtb, kb, vb, vtb = _qkv_rope_t(xb, w_qkv, b_qkv, *_rope_tables(pos_big), *_rope_tables_t(pos_big), tb)
    qs, ks, vs = _qkv_rope(xs, w_qkv, b_qkv, *_rope_tables(pos_small), ts)
    sinks = attn_sinks[0]
    cmk = cache_attn_meta_k[0].reshape(dec_batch, META_LEN, KV_DIM)
    cmv = cache_attn_meta_v[0].reshape(dec_batch, META_LEN, KV_DIM)
    cwk = cache_attn_win_k[0].reshape(dec_batch, WINDOW, KV_DIM)
    cwv = cache_attn_win_v[0].reshape(dec_batch, WINDOW, KV_DIM)
    vt_meta = vs[:n_meta].reshape(n_batch, META_LEN, KV_DIM).transpose(0, 2, 1).astype(BF16)
    o_big = _attn_prompt(qtb, kb, vtb, ks, vt_meta, sinks, n_batch, seq)
    o_small = jnp.concatenate([
        _attn_meta(qs, ks, vs, sinks, n_batch),
        _attn_sample(qs, ks, vs, cmk, cmv, cwk, cwv, sinks, n_meta, dec_batch, dec_seq)], axis=0)
    xb, xs = moe_layer(0, o_big, o_small, attn_w_o[0], attn_b_o[0], xb, xs)

    kv4 = lambda a, n, t_len: a.reshape(n, t_len, N_KV_HEADS, HEAD_DIM)
    p_meta_k = kv4(ks[:n_meta], n_batch, META_LEN)[None]
    p_meta_v = kv4(vs[:n_meta], n_batch, META_LEN)[None]
    p_win_k = kv4(kb, n_batch, seq)[:, seq - WINDOW:][None]
    p_win_v = kv4(vb, n_batch, seq)[:, seq - WINDOW:][None]
    k_new = kv4(ks[n_meta:], dec_batch, dec_seq)
    v_new = kv4(vs[n_meta:], dec_batch, dec_seq)
    s_win_k = jnp.concatenate([cache_attn_win_k[0], k_new], axis=1)[:, -WINDOW:][None]
    s_win_v = jnp.concatenate([cache_attn_win_v[0], v_new], axis=1)[:, -WINDOW:][None]

    w_pw1 = conv_w_pw1[0].astype(BF16)
    ub = _pw1_glu(xb, w_pw1, row2(conv_b_pw1[0]), tb)
    us = _pw1_glu(xs, w_pw1, row2(conv_b_pw1[0]), ts)
    slab = lambda v: v.reshape(N_SLAB, LANES)
    w_dw = jnp.pad(conv_w_dw[0], ((0, CONV_HALO - CONV_WIDTH), (0, 0)))
    w3 = w_dw.reshape(CONV_HALO, N_SLAB, LANES).transpose(1, 0, 2)
    conv_args = (w3, slab(conv_b_dw[0]), slab(conv_ln_g[0]), slab(conv_ln_b[0]))
    z_big = _conv_prompt(ub, us, *conv_args, n_batch, seq, tb)
    z_meta = _conv_short(jnp.zeros((n_batch, CONV_HALO, D_MODEL), F32), us, 0, *conv_args, META_LEN, 1)
    ctx_pad = jnp.pad(state_conv[0], ((0, 0), (CONV_HALO - CONV_CTX, 0), (0, 0)))
    z_samp = _conv_short(ctx_pad, us, n_meta, *conv_args, dec_seq, 2)
    z_small = jnp.concatenate([z_meta, z_samp], axis=0).astype(BF16)
    xb, xs = moe_layer(1, z_big, z_small, conv_w_pw2[0], conv_b_pw2[0], xb, xs)

    p_conv = ub.reshape(n_batch, seq, D_MODEL)[:, seq - CONV_CTX:][None]
    u_samp = us[n_meta:].reshape(dec_batch, dec_seq, D_MODEL)
    s_conv = jnp.concatenate([state_conv[0], u_samp], axis=1)[:, -CONV_CTX:][None]

    y_prompt = xb.reshape(n_batch, seq, D_MODEL)
    y_sample = xs[n_meta:].reshape(dec_batch, dec_seq, D_MODEL)
    return (y_prompt, y_sample, p_meta_k, p_meta_v, p_win_k, p_win_v, p_conv,
            s_win_k, s_win_v, s_conv)
```

```python
import functools

import numpy as np
import jax
import jax.numpy as jnp
from jax import lax
from jax.experimental import pallas as pl
from jax.experimental.pallas import tpu as pltpu

F32 = jnp.float32
BF16 = jnp.bfloat16

D_MODEL = 1024
HEAD_DIM = 64
N_HEADS = 16
N_KV_HEADS = 4
GROUP = N_HEADS // N_KV_HEADS
KV_DIM = N_KV_HEADS * HEAD_DIM
QKV_DIM = D_MODEL + 2 * KV_DIM
ROT_DIM = 16
ROPE_THETA = 500000.0
WINDOW = 128
ATTN_BLOCK = 128
ATTN_SCALE = HEAD_DIM ** -0.5
LOG2_E = 1.4426950408889634
Q_SCALE = ATTN_SCALE * LOG2_E
META_LEN = 16
CONV_WIDTH = 31
CONV_CTX = CONV_WIDTH - 1
N_EXPERTS = 32
TOP_K = 4
MOE_FF = 1024
SWIGLU_LIMIT = 7.0
SWIGLU_ALPHA = 1.702
LN_EPS = 1e-5
DEPTH = 2
DEEPNORM_ALPHA = (2 * DEPTH) ** 0.25
PAST_LEN = 16384
NEG_INF = -1e30

LANES = 128
SUBLANES = 8
GMM_TILE = 512
CONV_HALO = 32
CONV_BLOCK = 32
LN_ROWS = 16
N_SLAB = D_MODEL // LANES
COMBINE_ROWS = 32
HEADS_PER_DOT = 4
VMEM_LIMIT = 56 * 1024 * 1024


def _cparams(n_axes):
    return pltpu.CompilerParams(dimension_semantics=("arbitrary",) * n_axes,
                                vmem_limit_bytes=VMEM_LIMIT)


def _pick_tile(n, cap):
    best = None
    t = LANES
    while t <= cap:
        if n % t == 0:
            best = t
        t += LANES
    assert best is not None, n
    return best


def _layer_norm(h, g, b):
    mu = jnp.mean(h, axis=-1, keepdims=True)
    hc = h - mu
    var = jnp.mean(hc * hc, axis=-1, keepdims=True)
    return hc * lax.rsqrt(var + LN_EPS) * g + b


def _qkv_kernel(x_ref, w_ref, b_ref, c_ref, a_ref, s_ref, q_ref, k_ref, v_ref):
    x = x_ref[...].astype(BF16)
    acc = jnp.dot(x, w_ref[...], preferred_element_type=F32) + b_ref[...]
    c = c_ref[...]
    a = a_ref[...]
    s = s_ref[...]

    def rope(t):
        return t * c + pltpu.roll(t, LANES - ROT_DIM // 2, 1) * a + pltpu.roll(t, ROT_DIM // 2, 1) * s

    for j in range(D_MODEL // LANES):
        sl = slice(j * LANES, (j + 1) * LANES)
        q_ref[:, sl] = (rope(acc[:, sl]) * Q_SCALE).astype(BF16)
    for j in range(KV_DIM // LANES):
        sl = slice(D_MODEL + j * LANES, D_MODEL + (j + 1) * LANES)
        k_ref[:, j * LANES:(j + 1) * LANES] = rope(acc[:, sl])
    v_ref[...] = acc[:, D_MODEL + KV_DIM:]


def _qkv_rope(x, w_bf, b, cos_t, sa_t, sb_t, tm):
    t = x.shape[0]
    period = cos_t.shape[0] // tm
    row = lambda i: (i, 0)
    tab = lambda i: (i % period, 0)
    fixed = lambda i: (0, 0)
    return pl.pallas_call(
        _qkv_kernel,
        grid=(t // tm,),
        in_specs=[pl.BlockSpec((tm, D_MODEL), row),
                  pl.BlockSpec((D_MODEL, QKV_DIM), fixed),
                  pl.BlockSpec((1, QKV_DIM), fixed),
                  pl.BlockSpec((tm, LANES), tab),
                  pl.BlockSpec((tm, LANES), tab),
                  pl.BlockSpec((tm, LANES), tab)],
        out_specs=[pl.BlockSpec((tm, D_MODEL), row),
                   pl.BlockSpec((tm, KV_DIM), row),
                   pl.BlockSpec((tm, KV_DIM), row)],
        out_shape=[jax.ShapeDtypeStruct((t, D_MODEL), BF16),
                   jax.ShapeDtypeStruct((t, KV_DIM), F32),
                   jax.ShapeDtypeStruct((t, KV_DIM), F32)],
        compiler_params=_cparams(1),
        name="qkv_rope",
    )(x, w_bf, b, cos_t, sa_t, sb_t)


def _qkv_t_kernel(x_ref, wqt_ref, bq_ref, wkv_ref, bkv_ref, wvt_ref, bv_ref, c_ref, a_ref, s_ref,
                  ct_ref, st_ref, qt_ref, k_ref, v_ref, vt_ref):
    x = x_ref[...].astype(BF16)
    nt = (((1,), (1,)), ((), ()))
    qt = lax.dot_general(wqt_ref[...], x, nt, preferred_element_type=F32) + bq_ref[...]
    ct = ct_ref[...]
    st = st_ref[...]
    half = ROT_DIM // 2
    for h in range(N_HEADS):
        r0 = h * HEAD_DIM
        x1 = qt[r0:r0 + half]
        x2 = qt[r0 + half:r0 + ROT_DIM]
        rot = jnp.concatenate([x1 * ct - x2 * st, x2 * ct + x1 * st], axis=0)
        qt_ref[r0:r0 + ROT_DIM, :] = (rot * Q_SCALE).astype(BF16)
        qt_ref[r0 + ROT_DIM:r0 + HEAD_DIM, :] = (qt[r0 + ROT_DIM:r0 + HEAD_DIM] * Q_SCALE).astype(BF16)
    vt = lax.dot_general(wvt_ref[...], x, nt, preferred_element_type=F32) + bv_ref[...]
    vt_ref[...] = vt.astype(BF16)

    kv = jnp.dot(x, wkv_ref[...], preferred_element_type=F32) + bkv_ref[...]
    c = c_ref[...]
    a = a_ref[...]
    s = s_ref[...]
    for j in range(KV_DIM // LANES):
        t = kv[:, j * LANES:(j + 1) * LANES]
        k_ref[:, j * LANES:(j + 1) * LANES] = (
            t * c + pltpu.roll(t, LANES - half, 1) * a + pltpu.roll(t, half, 1) * s)
    v_ref[...] = kv[:, KV_DIM:]


def _qkv_rope_t(x, w_bf, b, cos_t, sa_t, sb_t, cos_tt, sin_tt, tm):
    t = x.shape[0]
    period = cos_t.shape[0] // tm
    row = lambda i: (i, 0)
    col = lambda i: (0, i)
    tab = lambda i: (i % period, 0)
    tabt = lambda i: (0, i % period)
    fixed = lambda i: (0, 0)
    wqt = w_bf[:, :D_MODEL].T
    wvt = w_bf[:, D_MODEL + KV_DIM:].T
    half = ROT_DIM // 2
    return pl.pallas_call(
        _qkv_t_kernel,
        grid=(t // tm,),
        in_specs=[pl.BlockSpec((tm, D_MODEL), row),
                  pl.BlockSpec((D_MODEL, D_MODEL), fixed),
                  pl.BlockSpec((D_MODEL, 1), fixed),
                  pl.BlockSpec((D_MODEL, 2 * KV_DIM), fixed),
                  pl.BlockSpec((1, 2 * KV_DIM), fixed),
                  pl.BlockSpec((KV_DIM, D_MODEL), fixed),
                  pl.BlockSpec((KV_DIM, 1), fixed),
                  pl.BlockSpec((tm, LANES), tab),
                  pl.BlockSpec((tm, LANES), tab),
                  pl.BlockSpec((tm, LANES), tab),
                  pl.BlockSpec((half, tm), tabt),
                  pl.BlockSpec((half, tm), tabt)],
        out_specs=[pl.BlockSpec((D_MODEL, tm), col),
                   pl.BlockSpec((tm, KV_DIM), row),
                   pl.BlockSpec((tm, KV_DIM), row),
                   pl.BlockSpec((KV_DIM, tm), col)],
        out_shape=[jax.ShapeDtypeStruct((D_MODEL, t), BF16),
                   jax.ShapeDtypeStruct((t, KV_DIM), F32),
                   jax.ShapeDtypeStruct((t, KV_DIM), F32),
                   jax.ShapeDtypeStruct((KV_DIM, t), BF16)],
        compiler_params=_cparams(1),
        name="qkv_rope_t",
    )(x, wqt, b[:, :D_MODEL].reshape(D_MODEL, 1), w_bf[:, D_MODEL:], b[:, D_MODEL:],
      wvt, b[:, D_MODEL + KV_DIM:].reshape(KV_DIM, 1), cos_t, sa_t, sb_t, cos_tt, sin_tt)


def _attend(q, kcat, vcat, bias, sink_ref, write):
    kgs = [kcat[:, g * HEAD_DIM:(g + 1) * HEAD_DIM] for g in range(N_KV_HEADS)]

    def scores(h):
        qh = q[:, h * HEAD_DIM:(h + 1) * HEAD_DIM]
        return lax.dot_general(qh, kgs[h // GROUP], (((1,), (1,)), ((), ())), preferred_element_type=F32) + bias

    s_next = scores(0)
    for h in range(N_HEADS):
        s = s_next
        if h + 1 < N_HEADS:
            s_next = scores(h + 1)
        vg = vcat[:, (h // GROUP) * HEAD_DIM:(h // GROUP + 1) * HEAD_DIM]
        sink = sink_ref[h] * LOG2_E
        m = jnp.maximum(jnp.max(s, axis=-1, keepdims=True), sink)
        p = jnp.exp2(s - m)
        denom = jnp.sum(p, axis=-1, keepdims=True) + jnp.exp2(sink - m)
        o = jnp.dot(p.astype(BF16), vg, preferred_element_type=F32)
        write(h, o / denom)


def _attn_prompt_kernel(sink_ref, qt_ref, kp_ref, ko_ref, km_ref, vtp_ref, vto_ref, vtm_ref, o_ref, acc_ref):
    i = pl.program_id(1)
    kcat = jnp.concatenate([kp_ref[...], ko_ref[...], km_ref[...]], axis=0).astype(BF16)
    vt = jnp.concatenate([vtp_ref[...], vto_ref[...], vtm_ref[...]], axis=1)
    nk = 2 * ATTN_BLOCK + META_LEN
    key = lax.broadcasted_iota(jnp.int32, (nk, ATTN_BLOCK), 0)
    qry = lax.broadcasted_iota(jnp.int32, (nk, ATTN_BLOCK), 1)
    first = jnp.where(i > 0, 0, 2 * ATTN_BLOCK)
    prev_ok = (key < ATTN_BLOCK) & (key >= qry + first)
    own_ok = (key >= ATTN_BLOCK) & (key - ATTN_BLOCK <= qry)
    mask = prev_ok | own_ok | (key >= 2 * ATTN_BLOCK)
    bias = jnp.where(mask, 0.0, NEG_INF)
    bias = jnp.concatenate([bias] * HEADS_PER_DOT, axis=1)
    lane = lax.broadcasted_iota(jnp.int32, (1, HEADS_PER_DOT * ATTN_BLOCK), 1)
    kgs = [kcat[:, g * HEAD_DIM:(g + 1) * HEAD_DIM] for g in range(N_KV_HEADS)]

    def scores(u):
        h0 = u * HEADS_PER_DOT
        rhs = jnp.concatenate([qt_ref[(h0 + j) * HEAD_DIM:(h0 + j + 1) * HEAD_DIM, :]
                               for j in range(HEADS_PER_DOT)], axis=1)
        return jnp.dot(kgs[h0 // GROUP], rhs, preferred_element_type=F32) + bias

    n_units = N_HEADS // HEADS_PER_DOT
    s_next = scores(0)
    for u in range(n_units):
        s = s_next
        if u + 1 < n_units:
            s_next = scores(u + 1)
        h0 = u * HEADS_PER_DOT
        g = h0 // GROUP
        sink = sink_ref[h0] * LOG2_E
        for j in range(1, HEADS_PER_DOT):
            sink = jnp.where(lane >= j * ATTN_BLOCK, sink_ref[h0 + j] * LOG2_E, sink)
        m = jnp.maximum(jnp.max(s, axis=0, keepdims=True), sink)
        p = jnp.exp2(s - m)
        denom = jnp.sum(p, axis=0, keepdims=True) + jnp.exp2(sink - m)
        o = jnp.dot(vt[g * HEAD_DIM:(g + 1) * HEAD_DIM, :], p.astype(BF16), preferred_element_type=F32)
        o = o * (1.0 / denom)
        for j in range(HEADS_PER_DOT):
            acc_ref[(h0 + j) * HEAD_DIM:(h0 + j + 1) * HEAD_DIM, :] = o[:, j * ATTN_BLOCK:(j + 1) * ATTN_BLOCK]
    o_ref[...] = acc_ref[...].T.astype(o_ref.dtype)


def _attn_prompt(qt, k, vt, k_meta, vt_meta, sinks, n_batch, seq):
    nblk = seq // ATTN_BLOCK
    own = lambda b, i: b * nblk + i
    prev = lambda b, i: b * nblk + jnp.maximum(i - 1, 0)
    kspec = lambda m: pl.BlockSpec((ATTN_BLOCK, KV_DIM), lambda b, i: (m(b, i), 0))
    vspec = lambda m: pl.BlockSpec((KV_DIM, ATTN_BLOCK), lambda b, i: (0, m(b, i)))
    return pl.pallas_call(
        _attn_prompt_kernel,
        grid=(n_batch, nblk),
        in_specs=[pl.BlockSpec(memory_space=pltpu.SMEM),
                  pl.BlockSpec((D_MODEL, ATTN_BLOCK), lambda b, i: (0, own(b, i))),
                  kspec(prev), kspec(own),
                  pl.BlockSpec((META_LEN, KV_DIM), lambda b, i: (b, 0)),
                  vspec(prev), vspec(own),
                  pl.BlockSpec((None, KV_DIM, META_LEN), lambda b, i: (b, 0, 0))],
        out_specs=pl.BlockSpec((ATTN_BLOCK, D_MODEL), lambda b, i: (own(b, i), 0)),
        out_shape=jax.ShapeDtypeStruct((n_batch * seq, D_MODEL), BF16),
        scratch_shapes=[pltpu.VMEM((D_MODEL, ATTN_BLOCK), F32)],
        compiler_params=_cparams(2),
        name="attn_prompt",
    )(sinks, qt, k, k, k_meta, vt, vt, vt_meta)


def _attn_meta_kernel(sink_ref, q_ref, k_ref, v_ref, o_ref):
    r = lax.broadcasted_iota(jnp.int32, (META_LEN, META_LEN), 0)
    c = lax.broadcasted_iota(jnp.int32, (META_LEN, META_LEN), 1)

    def write(h, o):
        o_ref[:, h * HEAD_DIM:(h + 1) * HEAD_DIM] = o.astype(o_ref.dtype)

    bias = jnp.where(c <= r, 0.0, NEG_INF)
    _attend(q_ref[...], k_ref[...].astype(BF16), v_ref[...].astype(BF16), bias, sink_ref, write)


def _attn_meta(q, k, v, sinks, n_batch):
    imap = lambda b: (b, 0)
    return pl.pallas_call(
        _attn_meta_kernel,
        grid=(n_batch,),
        in_specs=[pl.BlockSpec(memory_space=pltpu.SMEM),
                  pl.BlockSpec((META_LEN, D_MODEL), imap),
                  pl.BlockSpec((META_LEN, KV_DIM), imap),
                  pl.BlockSpec((META_LEN, KV_DIM), imap)],
        out_specs=pl.BlockSpec((META_LEN, D_MODEL), lambda b: (b, 0)),
        out_shape=jax.ShapeDtypeStruct((n_batch * META_LEN, D_MODEL), BF16),
        compiler_params=_cparams(1),
        name="attn_meta",
    )(sinks, q, k, v)


SAMPLE_GROUP = 8


def _attn_sample_kernel(sink_ref, bias_ref, q_ref, kn_ref, vn_ref, cmk_ref, cmv_ref, cwk_ref, cwv_ref,
                        o_ref, *, dec_seq, group):
    def keys(cm_ref, cw_ref, new):
        parts = []
        for j in range(group):
            parts += [cm_ref[j], cw_ref[j], new[j * dec_seq:(j + 1) * dec_seq]]
        return jnp.concatenate(parts, axis=0).astype(BF16)

    kcat = keys(cmk_ref, cwk_ref, kn_ref[...])
    vcat = keys(cmv_ref, cwv_ref, vn_ref[...])

    def write(h, o):
        o_ref[:, h * HEAD_DIM:(h + 1) * HEAD_DIM] = o.astype(o_ref.dtype)

    _attend(q_ref[...], kcat, vcat, bias_ref[...], sink_ref, write)


def _sample_bias(group, dec_seq):
    per = META_LEN + WINDOW + dec_seq
    r = np.arange(group * dec_seq)[:, None]
    c = np.arange(group * per)[None, :]
    tq, ck = r % dec_seq, c % per
    win_ok = (ck >= META_LEN) & (ck < META_LEN + WINDOW) & (ck - META_LEN >= tq)
    new_ok = (ck >= META_LEN + WINDOW) & (ck - (META_LEN + WINDOW) <= tq)
    vis = (r // dec_seq == c // per) & ((ck < META_LEN) | win_ok | new_ok)
    return np.where(vis, 0.0, NEG_INF).astype(np.float32)


def _attn_sample(q, k, v, cmk, cmv, cwk, cwv, sinks, row0, dec_batch, dec_seq):
    group = SAMPLE_GROUP
    nq = group * dec_seq
    blk0 = row0 // nq
    bias = jnp.asarray(_sample_bias(group, dec_seq))
    qmap = lambda n: (blk0 + n, 0)
    cmap = lambda n: (n, 0, 0)
    return pl.pallas_call(
        functools.partial(_attn_sample_kernel, dec_seq=dec_seq, group=group),
        grid=(dec_batch // group,),
        in_specs=[pl.BlockSpec(memory_space=pltpu.SMEM),
                  pl.BlockSpec(bias.shape, lambda n: (0, 0)),
                  pl.BlockSpec((nq, D_MODEL), qmap),
                  pl.BlockSpec((nq, KV_DIM), qmap),
                  pl.BlockSpec((nq, KV_DIM), qmap),
                  pl.BlockSpec((group, META_LEN, KV_DIM), cmap),
                  pl.BlockSpec((group, META_LEN, KV_DIM), cmap),
                  pl.BlockSpec((group, WINDOW, KV_DIM), cmap),
                  pl.BlockSpec((group, WINDOW, KV_DIM), cmap)],
        out_specs=pl.BlockSpec((nq, D_MODEL), lambda n: (n, 0)),
        out_shape=jax.ShapeDtypeStruct((dec_batch * dec_seq, D_MODEL), BF16),
        compiler_params=_cparams(1),
        name="attn_sample",
    )(sinks, bias, q, k, v, cmk, cmv, cwk, cwv)


def _proj_ln_route_kernel(a_ref, w_ref, b_ref, x_ref, g_ref, bb_ref, wr_ref, br_ref, tri_ref, cin_ref,
                          x1_ref, ids_ref, gates_ref, rank_ref, cnt_ref, carry_ref):
    @pl.when(pl.program_id(0) == 0)
    def _():
        carry_ref[...] = cin_ref[...].astype(F32)

    y = jnp.dot(a_ref[...], w_ref[...], preferred_element_type=F32) + b_ref[...]
    x1 = _layer_norm(DEEPNORM_ALPHA * x_ref[...] + y, g_ref[...], bb_ref[...])
    for j in range(N_SLAB):
        x1_ref[pl.ds(j, x1.shape[0], stride=N_SLAB), :] = x1[:, j * LANES:(j + 1) * LANES]

    logits = lax.dot_general(wr_ref[...], x1.astype(BF16), (((1,), (1,)), ((), ())),
                             preferred_element_type=F32) + br_ref[...]
    tm = logits.shape[1]
    eidx = lax.broadcasted_iota(jnp.int32, (N_EXPERTS, tm), 0).astype(F32)
    cur = logits
    vals, idxs, sels = [], [], []
    for _ in range(TOP_K):
        m = jnp.max(cur, axis=0, keepdims=True)
        idx = jnp.min(jnp.where(cur == m, eidx, float(N_EXPERTS)), axis=0, keepdims=True)
        sel = eidx == idx
        vals.append(m)
        idxs.append(idx)
        sels.append(sel)
        cur = jnp.where(sel, -jnp.inf, cur)
    exps = [jnp.exp(v - vals[0]) for v in vals]
    tot = exps[0] + exps[1] + exps[2] + exps[3]
    gates_ref[...] = jnp.concatenate([e / tot for e in exps], axis=0)
    ids_ref[...] = jnp.concatenate(idxs, axis=0).astype(jnp.int32)

    chosen = jnp.where(sels[0] | sels[1] | sels[2] | sels[3], 1.0, 0.0)
    before = jnp.dot(chosen.astype(BF16), tri_ref[...], preferred_element_type=F32)
    before = before + carry_ref[:, 0:1]
    ranks = [jnp.sum(jnp.where(s, before, 0.0), axis=0, keepdims=True) for s in sels]
    rank_ref[...] = jnp.concatenate(ranks, axis=0).astype(jnp.int32)
    carry_ref[...] = carry_ref[...] + jnp.sum(chosen, axis=1, keepdims=True)
    cnt_ref[...] = carry_ref[...].astype(jnp.int32)


def _proj_ln_route(a_bf, w_bf, b, x, g, bb, wr_t_bf, br_col, counts_in, tm):
    t = x.shape[0]
    row = lambda i: (i, 0)
    col = lambda i: (0, i)
    fixed = lambda i: (0, 0)
    tri = (jnp.arange(tm)[:, None] < jnp.arange(tm)[None, :]).astype(BF16)
    return pl.pallas_call(
        _proj_ln_route_kernel,
        grid=(t // tm,),
        in_specs=[pl.BlockSpec((tm, D_MODEL), row),
                  pl.BlockSpec((D_MODEL, D_MODEL), fixed),
                  pl.BlockSpec((1, D_MODEL), fixed),
                  pl.BlockSpec((tm, D_MODEL), row),
                  pl.BlockSpec((1, D_MODEL), fixed),
                  pl.BlockSpec((1, D_MODEL), fixed),
                  pl.BlockSpec((N_EXPERTS, D_MODEL), fixed),
                  pl.BlockSpec((N_EXPERTS, 1), fixed),
                  pl.BlockSpec((tm, tm), fixed),
                  pl.BlockSpec((N_EXPERTS, LANES), fixed)],
        out_specs=[pl.BlockSpec((tm * N_SLAB, LANES), row),
                   pl.BlockSpec((TOP_K, tm), col),
                   pl.BlockSpec((TOP_K, tm), col),
                   pl.BlockSpec((TOP_K, tm), col),
                   pl.BlockSpec((N_EXPERTS, LANES), fixed)],
        out_shape=[jax.ShapeDtypeStruct((t * N_SLAB, LANES), F32),
                   jax.ShapeDtypeStruct((TOP_K, t), jnp.int32),
                   jax.ShapeDtypeStruct((TOP_K, t), F32),
                   jax.ShapeDtypeStruct((TOP_K, t), jnp.int32),
                   jax.ShapeDtypeStruct((N_EXPERTS, LANES), jnp.int32)],
        scratch_shapes=[pltpu.VMEM((N_EXPERTS, LANES), F32)],
        compiler_params=_cparams(1),
        name="proj_ln_route",
    )(a_bf, w_bf, b, x, g, bb, wr_t_bf, br_col, tri, counts_in)


def _idx_copy(dest_hbm, dsm, sem, step, stride):
    slot = step % 2
    return pltpu.make_async_copy(dest_hbm.at[pl.ds(pl.multiple_of(step * stride, stride), stride)],
                                 dsm.at[pl.ds(pl.multiple_of(slot * stride, stride), stride)],
                                 sem)


def _dispatch_kernel(pstart_ref, padded_ref, nact_ref, dest_hbm, *rest, tiles, steps, stride, n_tiles):
    x_refs = rest[:len(tiles)]
    xs_hbm, dsm, zbuf, sems = rest[len(tiles):]
    i = pl.program_id(0)
    nsteps = pl.num_programs(0)

    def zero_tile(row0):
        n = GMM_TILE * N_SLAB
        return pltpu.make_async_copy(zbuf, xs_hbm.at[pl.ds(pl.multiple_of(row0 * N_SLAB, n), n)], sems.at[2])

    def token(ref, r):
        return ref.at[pl.ds(pl.multiple_of(r * N_SLAB, N_SLAB), N_SLAB)]

    @pl.when(i == 0)
    def _():
        zbuf[...] = jnp.zeros_like(zbuf)

        def each_expert(fn):
            def body(e, _):
                @pl.when(padded_ref[e] > 0)
                def _():
                    fn(zero_tile(pstart_ref[e] + padded_ref[e] - GMM_TILE))
                return 0
            lax.fori_loop(0, N_EXPERTS, body, 0)

        def each_tail(fn):
            def body(t, _):
                fn(zero_tile(t * GMM_TILE))
                return 0
            lax.fori_loop(nact_ref[0], n_tiles, body, 0)

        each_expert(lambda cp: cp.start())
        each_tail(lambda cp: cp.start())
        each_expert(lambda cp: cp.wait())
        each_tail(lambda cp: cp.wait())
        _idx_copy(dest_hbm, dsm, sems.at[0], i, stride).start()

    _idx_copy(dest_hbm, dsm, sems.at[0], i, stride).wait()

    @pl.when(i + 1 < nsteps)
    def _():
        _idx_copy(dest_hbm, dsm, sems.at[0], i + 1, stride).start()

    base = (i % 2) * stride
    step0 = 0
    for x_ref, tm, n in zip(x_refs, tiles, steps):
        @pl.when((i >= step0) & (i < step0 + n))
        def _(x_ref=x_ref, tm=tm):
            def row_body(r, _):
                for k in range(TOP_K):
                    d = dsm[base + k * tm + r]
                    pltpu.make_async_copy(token(x_ref, r), token(xs_hbm, d), sems.at[1]).start(priority=k % 2)
                return 0

            lax.fori_loop(0, tm, row_body, 0, unroll=4)
            for k in range(TOP_K):
                pltpu.make_async_copy(x_ref, xs_hbm.at[pl.ds(0, tm * N_SLAB)], sems.at[1]).wait()
        step0 += n


def _seg_map(step0, n):
    return lambda i, *_: (jnp.clip(i - step0, 0, n - 1), 0)


def _dispatch(xs_list, tiles, dest_steps, pstart, padded, nact, stride, n_tiles):
    steps = [x.shape[0] // (tm * N_SLAB) for x, tm in zip(xs_list, tiles)]
    in_specs = [pl.BlockSpec(memory_space=pl.ANY)]
    step0 = 0
    for tm, n in zip(tiles, steps):
        in_specs.append(pl.BlockSpec((tm * N_SLAB, LANES), _seg_map(step0, n)))
        step0 += n
    grid_spec = pltpu.PrefetchScalarGridSpec(
        num_scalar_prefetch=3,
        grid=(sum(steps),),
        in_specs=in_specs,
        out_specs=pl.BlockSpec(memory_space=pl.ANY),
        scratch_shapes=[pltpu.SMEM((2 * stride,), jnp.int32),
                        pltpu.VMEM((GMM_TILE * N_SLAB, LANES), F32),
                        pltpu.SemaphoreType.DMA((3,))],
    )
    return pl.pallas_call(
        functools.partial(_dispatch_kernel, tiles=tuple(tiles), steps=tuple(steps), stride=stride,
                          n_tiles=n_tiles),
        grid_spec=grid_spec,
        out_shape=jax.ShapeDtypeStruct((n_tiles * GMM_TILE * N_SLAB, LANES), F32),
        compiler_params=_cparams(1),
        name="moe_dispatch",
    )(pstart, padded, nact, dest_steps, *xs_list)


def _gmm_kernel(te_ref, tsrc_ref, tfirst_ref, nact_ref, x_ref, wgu_ref, bgu_ref, wdn_ref, bdn_ref, y_ref,
                wgu_bf, wdn_bf):
    i = pl.program_id(0)

    @pl.when(tfirst_ref[i] == 1)
    def _():
        wgu_bf[...] = wgu_ref[...].astype(BF16)
        wdn_bf[...] = wdn_ref[...].astype(BF16)

    @pl.when(i < nact_ref[0])
    def _():
        x = jnp.concatenate([x_ref[pl.ds(j, GMM_TILE, stride=N_SLAB), :] for j in range(N_SLAB)], axis=1)
        gu = jnp.dot(x.astype(BF16), wgu_bf[...], preferred_element_type=F32) + bgu_ref[...]
        gate = jnp.minimum(gu[:, :MOE_FF], SWIGLU_LIMIT)
        up = jnp.clip(gu[:, MOE_FF:], -SWIGLU_LIMIT, SWIGLU_LIMIT)
        glu = gate * jax.nn.sigmoid(SWIGLU_ALPHA * gate)
        h = ((up + 1.0) * glu).astype(BF16)
        y = jnp.dot(h, wdn_bf[...], preferred_element_type=F32) + bdn_ref[...]
        for j in range(N_SLAB):
            y_ref[pl.ds(j, GMM_TILE, stride=N_SLAB), :] = y[:, j * LANES:(j + 1) * LANES]

    @pl.when(i >= nact_ref[0])
    def _():
        y_ref[...] = jnp.zeros_like(y_ref)


def _gmm(xs, te, tsrc, tfirst, nact, wgu, bgu, wdn, bdn, n_tiles):
    emap = lambda i, te, *_: (te[i], 0, 0)
    grid_spec = pltpu.PrefetchScalarGridSpec(
        num_scalar_prefetch=4,
        grid=(n_tiles,),
        in_specs=[pl.BlockSpec((GMM_TILE * N_SLAB, LANES), lambda i, te, ts, *_: (ts[i], 0)),
                  pl.BlockSpec((None, D_MODEL, 2 * MOE_FF), emap),
                  pl.BlockSpec((None, 1, 2 * MOE_FF), emap),
                  pl.BlockSpec((None, MOE_FF, D_MODEL), emap),
                  pl.BlockSpec((None, 1, D_MODEL), emap)],
        out_specs=pl.BlockSpec((GMM_TILE * N_SLAB, LANES), lambda i, *_: (i, 0)),
        scratch_shapes=[pltpu.VMEM((D_MODEL, 2 * MOE_FF), BF16),
                        pltpu.VMEM((MOE_FF, D_MODEL), BF16)],
    )
    return pl.pallas_call(
        _gmm_kernel,
        grid_spec=grid_spec,
        out_shape=jax.ShapeDtypeStruct((n_tiles * GMM_TILE * N_SLAB, LANES), F32),
        compiler_params=_cparams(1),
        name="moe_gmm",
    )(te, tsrc, tfirst, nact, xs, wgu, bgu, wdn, bdn)


def _combine_kernel(dest_hbm, gate_hbm, ys_hbm, x_ref, g_ref, bb_ref, o_ref, dsm, gsm, buf, sems,
                    *, tm, stride):
    i = pl.program_id(0)
    nsteps = pl.num_programs(0)

    def idx_copies(step):
        return (_idx_copy(dest_hbm, dsm, sems.at[0], step, stride),
                _idx_copy(gate_hbm, gsm, sems.at[2], step, stride))

    @pl.when(i == 0)
    def _():
        for cp in idx_copies(i):
            cp.start()

    for cp in idx_copies(i):
        cp.wait()

    @pl.when(i + 1 < nsteps)
    def _():
        for cp in idx_copies(i + 1):
            cp.start()

    base = (i % 2) * stride

    def token(ref, r):
        return ref.at[pl.ds(pl.multiple_of(r * N_SLAB, N_SLAB), N_SLAB)]

    def row_body(r, _):
        for k in range(TOP_K):
            d = dsm[base + k * tm + r]
            pltpu.make_async_copy(token(ys_hbm, d), token(buf.at[k], r), sems.at[1]).start(priority=k % 2)
        return 0

    lax.fori_loop(0, tm, row_body, 0, unroll=4)
    for k in range(TOP_K):
        pltpu.make_async_copy(ys_hbm.at[pl.ds(0, tm * N_SLAB)], buf.at[k], sems.at[1]).wait()

    def token_sum(r, _):
        rows = pl.ds(pl.multiple_of(r * N_SLAB, N_SLAB), N_SLAB)
        h = DEEPNORM_ALPHA * x_ref[rows, :]
        for k in range(TOP_K):
            h = h + gsm[base + k * tm + r] * buf[k, rows, :]
        buf[0, rows, :] = h
        return 0

    lax.fori_loop(0, tm, token_sum, 0, unroll=8)

    def block(b, _):
        r0 = pl.multiple_of(b * COMBINE_ROWS, COMBINE_ROWS)
        hs = [buf[0, pl.ds(r0 * N_SLAB + j, COMBINE_ROWS, stride=N_SLAB), :] for j in range(N_SLAB)]
        tot = hs[0]
        for j in range(1, N_SLAB):
            tot = tot + hs[j]
        mu = jnp.sum(tot, axis=1, keepdims=True) * (1.0 / D_MODEL)
        cen = [h - mu for h in hs]
        sq = cen[0] * cen[0]
        for j in range(1, N_SLAB):
            sq = sq + cen[j] * cen[j]
        inv = lax.rsqrt(jnp.sum(sq, axis=1, keepdims=True) * (1.0 / D_MODEL) + LN_EPS)
        for j in range(N_SLAB):
            o_ref[pl.ds(r0, COMBINE_ROWS), j * LANES:(j + 1) * LANES] = (
                cen[j] * inv * g_ref[j:j + 1, :] + bb_ref[j:j + 1, :])
        return 0

    lax.fori_loop(0, tm // COMBINE_ROWS, block, 0, unroll=2)


def _combine(ys, dest_steps, gate_steps, x_tiles, g3, bb3, tm, stride):
    t = x_tiles.shape[0] // N_SLAB
    row = lambda i: (i, 0)
    fixed = lambda i: (0, 0)
    return pl.pallas_call(
        functools.partial(_combine_kernel, tm=tm, stride=stride),
        grid=(t // tm,),
        in_specs=[pl.BlockSpec(memory_space=pl.ANY),
                  pl.BlockSpec(memory_space=pl.ANY),
                  pl.BlockSpec(memory_space=pl.ANY),
                  pl.BlockSpec((tm * N_SLAB, LANES), row),
                  pl.BlockSpec((N_SLAB, LANES), fixed),
                  pl.BlockSpec((N_SLAB, LANES), fixed)],
        out_specs=pl.BlockSpec((tm, D_MODEL), row),
        out_shape=jax.ShapeDtypeStruct((t, D_MODEL), F32),
        scratch_shapes=[pltpu.SMEM((2 * stride,), jnp.int32),
                        pltpu.SMEM((2 * stride,), F32),
                        pltpu.VMEM((TOP_K, tm * N_SLAB, LANES), F32),
                        pltpu.SemaphoreType.DMA((3,))],
        compiler_params=_cparams(1),
        name="moe_combine",
    )(dest_steps, gate_steps, ys, x_tiles, g3, bb3)


def _moe(segs, counts, expert0, wgu, bgu, wdn, bdn, g, bb):
    t = sum(seg[1].shape[1] for seg in segs)
    n_tiles = -(-(t * TOP_K) // GMM_TILE) + N_EXPERTS
    counts = counts[:, 0]
    padded = ((counts + GMM_TILE - 1) // GMM_TILE) * GMM_TILE
    pend = jnp.cumsum(padded)
    pstart = (pend - padded).astype(jnp.int32)
    padded = padded.astype(jnp.int32)
    nact = (pend[-1] // GMM_TILE).astype(jnp.int32).reshape(1)
    tile = jnp.arange(n_tiles, dtype=jnp.int32)
    tsrc = jnp.minimum(tile, jnp.maximum(nact[0] - 1, 0))
    te = jnp.sum((pend[None, :] <= (tsrc * GMM_TILE)[:, None]).astype(jnp.int32), axis=1)
    te = jnp.minimum(te, N_EXPERTS - 1).astype(jnp.int32)
    tfirst = jnp.concatenate([jnp.ones((1,), jnp.int32), (te[1:] != te[:-1]).astype(jnp.int32)])
    eids = jnp.arange(N_EXPERTS, dtype=jnp.int32)[:, None, None]

    stride = -(-(TOP_K * max(seg[4] for seg in segs)) // 1024) * 1024
    def per_step(a, tm):
        n = a.shape[1]
        steps = a.reshape(TOP_K, n // tm, tm).transpose(1, 0, 2).reshape(n // tm, TOP_K * tm)
        return jnp.pad(steps, ((0, 0), (0, stride - TOP_K * tm))).reshape(-1)

    plans = []
    for x1, ids, gates, rank, tm in segs:
        dest = jnp.sum(jnp.where(ids[None] == eids, pstart[:, None, None], 0), axis=0) + rank
        plans.append((per_step(dest, tm), per_step(gates, tm)))
    xs = _dispatch([seg[0] for seg in segs], [seg[4] for seg in segs],
                   jnp.concatenate([p[0] for p in plans]), pstart, padded, nact, stride, n_tiles)
    ys = _gmm(xs, te + expert0, tsrc, tfirst, nact, wgu, bgu, wdn, bdn, n_tiles)
    return [_combine(ys, dest_steps, gate_steps, seg[0], g, bb, seg[4], stride)
            for seg, (dest_steps, gate_steps) in zip(segs, plans)]


def _pw1_glu_kernel(x_ref, w_ref, b_ref, u_ref):
    a = jnp.dot(x_ref[...].astype(BF16), w_ref[...], preferred_element_type=F32) + b_ref[...]
    u_ref[...] = a[:, :D_MODEL] * jax.nn.sigmoid(a[:, D_MODEL:])


def _pw1_glu(x, w_bf, b, tm):
    t = x.shape[0]
    row = lambda i: (i, 0)
    fixed = lambda i: (0, 0)
    return pl.pallas_call(
        _pw1_glu_kernel,
        grid=(t // tm,),
        in_specs=[pl.BlockSpec((tm, D_MODEL), row),
                  pl.BlockSpec((D_MODEL, 2 * D_MODEL), fixed),
                  pl.BlockSpec((1, 2 * D_MODEL), fixed)],
        out_specs=pl.BlockSpec((tm, D_MODEL), row),
        out_shape=jax.ShapeDtypeStruct((t, D_MODEL), F32),
        compiler_params=_cparams(1),
        name="pw1_glu",
    )(x, w_bf, b)


def _conv_rows(win_ref, zs_ref, base, n_sets, w_ref, b_ref):
    shift = CONV_HALO - CONV_CTX
    offs = tuple(range(n_sets))

    def out_rows(o):
        return pl.ds(base + o, SUBLANES, stride=n_sets)

    def slab(c, _):
        def window(t):
            return win_ref[c, pl.ds(base + (shift + t), SUBLANES, stride=n_sets), :]

        bias = b_ref[pl.ds(c, 1), :]
        acc = [bias] * n_sets
        wins = [window(t) for t in range(n_sets - 1)]
        for j in range(CONV_WIDTH):
            w = w_ref[c, j:j + 1, :]
            wins.append(window(j + n_sets - 1))
            acc = [acc[o] + w * wins[o] for o in offs]
            wins.pop(0)
        for o in offs:
            zs_ref[c, out_rows(o), :] = acc[o]
        return 0

    lax.fori_loop(0, N_SLAB, slab, 0)


def _ln_silu_rows(zs_ref, rows, g_ref, bb_ref):
    acc = [zs_ref[c, rows, :] for c in range(N_SLAB)]
    tot = acc[0]
    for c in range(1, N_SLAB):
        tot = tot + acc[c]
    mu = jnp.sum(tot, axis=1, keepdims=True) * (1.0 / D_MODEL)
    cen = [a - mu for a in acc]
    sq = cen[0] * cen[0]
    for c in range(1, N_SLAB):
        sq = sq + cen[c] * cen[c]
    inv = lax.rsqrt(jnp.sum(sq, axis=1, keepdims=True) * (1.0 / D_MODEL) + LN_EPS)
    out = []
    for c in range(N_SLAB):
        z = cen[c] * inv * g_ref[c:c + 1, :] + bb_ref[c:c + 1, :]
        out.append(z * jax.nn.sigmoid(z))
    return out


def _conv_prompt_kernel(um_ref, up_ref, uc_ref, w_ref, b_ref, g_ref, bb_ref, z_ref, win_ref, zs_ref, *, tr):
    i = pl.program_id(1)
    gap = CONV_HALO - META_LEN

    @pl.when(i == 0)
    def _():
        for c in range(N_SLAB):
            win_ref[c, 0:gap, :] = jnp.zeros((gap, LANES), F32)
            win_ref[c, gap:CONV_HALO, :] = um_ref[:, c * LANES:(c + 1) * LANES]

    @pl.when(i > 0)
    def _():
        for c in range(N_SLAB):
            win_ref[c, 0:CONV_HALO, :] = up_ref[:, c * LANES:(c + 1) * LANES]

    for c in range(N_SLAB):
        win_ref[c, CONV_HALO:, :] = uc_ref[:, c * LANES:(c + 1) * LANES]

    def block(bi, _):
        base = pl.multiple_of(bi * CONV_BLOCK, CONV_BLOCK)
        _conv_rows(win_ref, zs_ref, base, CONV_BLOCK // SUBLANES, w_ref, b_ref)
        return 0

    lax.fori_loop(0, tr // CONV_BLOCK, block, 0)

    def norm(bi, _):
        rows = pl.ds(pl.multiple_of(bi * LN_ROWS, LN_ROWS), LN_ROWS)
        for c, z in enumerate(_ln_silu_rows(zs_ref, rows, g_ref, bb_ref)):
            z_ref[rows, c * LANES:(c + 1) * LANES] = z.astype(z_ref.dtype)
        return 0

    lax.fori_loop(0, tr // LN_ROWS, norm, 0, unroll=4)


def _conv_prompt(u, u_meta, w3, b3, g3, bb3, n_batch, seq, tr):
    nt = seq // tr
    cur = lambda b, i: (b * nt + i, 0)
    prev = lambda b, i: (jnp.maximum((b * seq + i * tr) // CONV_HALO - 1, 0), 0)
    fixed2 = lambda b, i: (0, 0)
    fixed3 = lambda b, i: (0, 0, 0)
    return pl.pallas_call(
        functools.partial(_conv_prompt_kernel, tr=tr),
        grid=(n_batch, nt),
        in_specs=[pl.BlockSpec((META_LEN, D_MODEL), lambda b, i: (b, 0)),
                  pl.BlockSpec((CONV_HALO, D_MODEL), prev),
                  pl.BlockSpec((tr, D_MODEL), cur),
                  pl.BlockSpec((N_SLAB, CONV_HALO, LANES), fixed3),
                  pl.BlockSpec((N_SLAB, LANES), fixed2),
                  pl.BlockSpec((N_SLAB, LANES), fixed2),
                  pl.BlockSpec((N_SLAB, LANES), fixed2)],
        out_specs=pl.BlockSpec((tr, D_MODEL), cur),
        out_shape=jax.ShapeDtypeStruct((n_batch * seq, D_MODEL), BF16),
        scratch_shapes=[pltpu.VMEM((N_SLAB, CONV_HALO + tr, LANES), F32),
                        pltpu.VMEM((N_SLAB, tr, LANES), F32)],
        compiler_params=_cparams(2),
        name="conv_prompt",
    )(u_meta, u, u, w3, b3, g3, bb3)


SHORT_ROWS = 16


def _conv_short_kernel(ctx_ref, u_ref, w_ref, b_ref, g_ref, bb_ref, z_ref, win_ref, zs_ref, *, n_seq, t_len):
    for n in range(n_seq):
        for c in range(N_SLAB):
            sl = slice(c * LANES, (c + 1) * LANES)
            win_ref[c, 0:CONV_HALO, :] = ctx_ref[n, :, sl]
            win_ref[c, CONV_HALO:CONV_HALO + t_len, :] = u_ref[n * t_len:(n + 1) * t_len, sl]
            if t_len < SHORT_ROWS:
                win_ref[c, CONV_HALO + t_len:, :] = jnp.zeros((SHORT_ROWS - t_len, LANES), F32)
        _conv_rows(win_ref, zs_ref, 0, SHORT_ROWS // SUBLANES, w_ref, b_ref)
        for c, z in enumerate(_ln_silu_rows(zs_ref, slice(0, t_len), g_ref, bb_ref)):
            z_ref[n * t_len:(n + 1) * t_len, c * LANES:(c + 1) * LANES] = z


def _conv_short(ctx_pad, u, row0, w3, b3, g3, bb3, t_len, n_seq):
    n_total = ctx_pad.shape[0]
    rows = n_seq * t_len
    blk0 = row0 // rows
    fixed2 = lambda n: (0, 0)
    fixed3 = lambda n: (0, 0, 0)
    return pl.pallas_call(
        functools.partial(_conv_short_kernel, n_seq=n_seq, t_len=t_len),
        grid=(n_total // n_seq,),
        in_specs=[pl.BlockSpec((n_seq, CONV_HALO, D_MODEL), lambda n: (n, 0, 0)),
                  pl.BlockSpec((rows, D_MODEL), lambda n: (blk0 + n, 0)),
                  pl.BlockSpec((N_SLAB, CONV_HALO, LANES), fixed3),
                  pl.BlockSpec((N_SLAB, LANES), fixed2),
                  pl.BlockSpec((N_SLAB, LANES), fixed2),
                  pl.BlockSpec((N_SLAB, LANES), fixed2)],
        out_specs=pl.BlockSpec((rows, D_MODEL), lambda n: (n, 0)),
        out_shape=jax.ShapeDtypeStruct((n_total * t_len, D_MODEL), F32),
        scratch_shapes=[pltpu.VMEM((N_SLAB, CONV_HALO + SHORT_ROWS, LANES), F32),
                        pltpu.VMEM((N_SLAB, SHORT_ROWS, LANES), F32)],
        compiler_params=_cparams(1),
        name="conv_short",
    )(ctx_pad, u, w3, b3, g3, bb3)


def _rope_angles(pos):
    inv = 1.0 / (ROPE_THETA ** (jnp.arange(0, ROT_DIM, 2, dtype=F32) / ROT_DIM))
    ang = pos.astype(F32)[:, None] * inv[None, :]
    return jnp.cos(ang), jnp.sin(ang)


def _rope_tables_t(pos):
    cos, sin = _rope_angles(pos)
    return cos.T, sin.T


def _rope_tables(pos):
    half = ROT_DIM // 2
    cos, sin = _rope_angles(pos)
    n = pos.shape[0]
    ones = jnp.ones((n, HEAD_DIM - ROT_DIM), F32)
    zeros = jnp.zeros((n, HEAD_DIM - ROT_DIM), F32)
    zh = jnp.zeros((n, half), F32)
    c = jnp.concatenate([cos, cos, ones], axis=1)
    a = jnp.concatenate([-sin, zh, zeros], axis=1)
    s = jnp.concatenate([zh, sin, zeros], axis=1)
    rep = LANES // HEAD_DIM
    return jnp.tile(c, (1, rep)), jnp.tile(a, (1, rep)), jnp.tile(s, (1, rep))


def kernel(x_prompt, x_sample, cache_attn_meta_k, cache_attn_meta_v, cache_attn_win_k, cache_attn_win_v, state_conv, meta_tokens, attn_w_qkv, attn_b_qkv, attn_sinks, attn_w_o, attn_b_o, conv_w_pw1, conv_b_pw1, conv_w_dw, conv_b_dw, conv_ln_g, conv_ln_b, conv_w_pw2, conv_b_pw2, ln_mix_g, ln_mix_b, ln_ffn_g, ln_ffn_b, moe_w_router, moe_b_router, moe_w_gate_up, moe_b_gate_up, moe_w_down, moe_b_down):
    n_batch, seq, _ = x_prompt.shape
    dec_batch, dec_seq, _ = x_sample.shape
    n_real = n_batch * seq
    n_meta = n_batch * META_LEN
    n_samp = dec_batch * dec_seq
    n_small = n_meta + n_samp
    tb = _pick_tile(seq, 512)
    ts = _pick_tile(n_small, 512)
    row2 = lambda v: v.reshape(1, -1)

    meta_rows = jnp.broadcast_to(meta_tokens[None], (n_batch, META_LEN, D_MODEL)).reshape(n_meta, D_MODEL)
    xb = x_prompt.reshape(n_real, D_MODEL)
    xs = jnp.concatenate([meta_rows.astype(F32), x_sample.reshape(n_samp, D_MODEL)], axis=0)

    def moe_layer(i, a_big, a_small, w, b, xb, xs):
        w_bf = w.astype(BF16)
        wr = moe_w_router[i].T.astype(BF16)
        br = moe_b_router[i].reshape(N_EXPERTS, 1)
        lng, lnb = row2(ln_mix_g[i]), row2(ln_mix_b[i])
        zero_counts = jnp.zeros((N_EXPERTS, LANES), jnp.int32)
        x1b, idb, gab, rab, cnt = _proj_ln_route(a_big, w_bf, row2(b), xb, lng, lnb, wr, br, zero_counts, tb)
        x1s, ids_, gas, ras, cnt = _proj_ln_route(a_small, w_bf, row2(b), xs, lng, lnb, wr, br, cnt, ts)
        n_all = moe_w_gate_up.shape[0] * N_EXPERTS
        return _moe([(x1b, idb, gab, rab, tb), (x1s, ids_, gas, ras, ts)], cnt, i * N_EXPERTS,
                    moe_w_gate_up.reshape(n_all, D_MODEL, 2 * MOE_FF),
                    moe_b_gate_up.reshape(n_all, 1, 2 * MOE_FF),
                    moe_w_down.reshape(n_all, MOE_FF, D_MODEL),
                    moe_b_down.reshape(n_all, 1, D_MODEL),
                    ln_ffn_g[i].reshape(N_SLAB, LANES), ln_ffn_b[i].reshape(N_SLAB, LANES))

    w_qkv = attn_w_qkv[0].astype(BF16)
    b_qkv = row2(attn_b_qkv[0])
    pos_small = jnp.concatenate([jnp.tile(jnp.arange(META_LEN), n_batch),
                                 jnp.tile(PAST_LEN + jnp.arange(dec_seq), dec_batch)])
    pos_big = META_LEN + jnp.arange(seq)
    qtb, kb, vb, vtb = _qkv_rope_t(xb, w_qkv, b_qkv, *_rope_tables(pos_big), *_rope_tables_t(pos_big), tb)
    qs, ks, vs = _qkv_rope(xs, w_qkv, b_qkv, *_rope_tables(pos_small), ts)
    sinks = attn_sinks[0]
    cmk = cache_attn_meta_k[0].reshape(dec_batch, META_LEN, KV_DIM)
    cmv = cache_attn_meta_v[0].reshape(dec_batch, META_LEN, KV_DIM)
    cwk = cache_attn_win_k[0].reshape(dec_batch, WINDOW, KV_DIM)
    cwv = cache_attn_win_v[0].reshape(dec_batch, WINDOW, KV_DIM)
    vt_meta = vs[:n_meta].reshape(n_batch, META_LEN, KV_DIM).transpose(0, 2, 1).astype(BF16)
    o_big = _attn_prompt(qtb, kb, vtb, ks, vt_meta, sinks, n_batch, seq)
    o_small = jnp.concatenate([
        _attn_meta(qs, ks, vs, sinks, n_batch),
        _attn_sample(qs, ks, vs, cmk, cmv, cwk, cwv, sinks, n_meta, dec_batch, dec_seq)], axis=0)
    xb, xs = moe_layer(0, o_big, o_small, attn_w_o[0], attn_b_o[0], xb, xs)

    kv4 = lambda a, n, t_len: a.reshape(n, t_len, N_KV_HEADS, HEAD_DIM)
    p_meta_k = kv4(ks[:n_meta], n_batch, META_LEN)[None]
    p_meta_v = kv4(vs[:n_meta], n_batch, META_LEN)[None]
    p_win_k = kv4(kb, n_batch, seq)[:, seq - WINDOW:][None]
    p_win_v = kv4(vb, n_batch, seq)[:, seq - WINDOW:][None]
    k_new = kv4(ks[n_meta:], dec_batch, dec_seq)
    v_new = kv4(vs[n_meta:], dec_batch, dec_seq)
    s_win_k = jnp.concatenate([cache_attn_win_k[0], k_new], axis=1)[:, -WINDOW:][None]
    s_win_v = jnp.concatenate([cache_attn_win_v[0], v_new], axis=1)[:, -WINDOW:][None]

    w_pw1 = conv_w_pw1[0].astype(BF16)
    ub = _pw1_glu(xb, w_pw1, row2(conv_b_pw1[0]), tb)
    us = _pw1_glu(xs, w_pw1, row2(conv_b_pw1[0]), ts)
    slab = lambda v: v.reshape(N_SLAB, LANES)
    w_dw = jnp.pad(conv_w_dw[0], ((0, CONV_HALO - CONV_WIDTH), (0, 0)))
    w3 = w_dw.reshape(CONV_HALO, N_SLAB, LANES).transpose(1, 0, 2)
    conv_args = (w3, slab(conv_b_dw[0]), slab(conv_ln_g[0]), slab(conv_ln_b[0]))
    z_big = _conv_prompt(ub, us, *conv_args, n_batch, seq, tb)
    z_meta = _conv_short(jnp.zeros((n_batch, CONV_HALO, D_MODEL), F32), us, 0, *conv_args, META_LEN, 1)
    ctx_pad = jnp.pad(state_conv[0], ((0, 0), (CONV_HALO - CONV_CTX, 0), (0, 0)))
    z_samp = _conv_short(ctx_pad, us, n_meta, *conv_args, dec_seq, 2)
    z_small = jnp.concatenate([z_meta, z_samp], axis=0).astype(BF16)
    xb, xs = moe_layer(1, z_big, z_small, conv_w_pw2[0], conv_b_pw2[0], xb, xs)

    p_conv = ub.reshape(n_batch, seq, D_MODEL)[:, seq - CONV_CTX:][None]
    u_samp = us[n_meta:].reshape(dec_batch, dec_seq, D_MODEL)
    s_conv = jnp.concatenate([state_conv[0], u_samp], axis=1)[:, -CONV_CTX:][None]

    y_prompt = xb.reshape(n_batch, seq, D_MODEL)
    y_sample = xs[n_meta:].reshape(dec_batch, dec_seq, D_MODEL)
    return (y_prompt, y_sample, p_meta_k, p_meta_v, p_win_k, p_win_v, p_conv,
            s_win_k, s_win_v, s_conv)
```

```python
import functools

import numpy as np
import jax
import jax.numpy as jnp
from jax import lax
from jax.experimental import pallas as pl
from jax.experimental.pallas import tpu as pltpu

F32 = jnp.float32
BF16 = jnp.bfloat16

D_MODEL = 1024
HEAD_DIM = 64
N_HEADS = 16
N_KV_HEADS = 4
GROUP = N_HEADS // N_KV_HEADS
KV_DIM = N_KV_HEADS * HEAD_DIM
QKV_DIM = D_MODEL + 2 * KV_DIM
ROT_DIM = 16
ROPE_THETA = 500000.0
WINDOW = 128
ATTN_BLOCK = 128
ATTN_SCALE = HEAD_DIM ** -0.5
LOG2_E = 1.4426950408889634
Q_SCALE = ATTN_SCALE * LOG2_E
META_LEN = 16
CONV_WIDTH = 31
CONV_CTX = CONV_WIDTH - 1
N_EXPERTS = 32
TOP_K = 4
MOE_FF = 1024
SWIGLU_LIMIT = 7.0
SWIGLU_ALPHA = 1.702
LN_EPS = 1e-5
DEPTH = 2
DEEPNORM_ALPHA = (2 * DEPTH) ** 0.25
PAST_LEN = 16384
NEG_INF = -1e30

LANES = 128
SUBLANES = 8
GMM_TILE = 512
CONV_HALO = 32
CONV_BLOCK = 32
LN_ROWS = 16
N_SLAB = D_MODEL // LANES
COMBINE_ROWS = 32
IDX_SLOTS = 3
HEADS_PER_DOT = 4
VMEM_LIMIT = 56 * 1024 * 1024


def _cparams(n_axes):
    return pltpu.CompilerParams(dimension_semantics=("arbitrary",) * n_axes,
                                vmem_limit_bytes=VMEM_LIMIT)


def _pick_tile(n, cap):
    best = None
    t = LANES
    while t <= cap:
        if n % t == 0:
            best = t
        t += LANES
    assert best is not None, n
    return best


def _layer_norm(h, g, b):
    mu = jnp.mean(h, axis=-1, keepdims=True)
    hc = h - mu
    var = jnp.mean(hc * hc, axis=-1, keepdims=True)
    return hc * lax.rsqrt(var + LN_EPS) * g + b


def _qkv_kernel(x_ref, w_ref, b_ref, c_ref, a_ref, s_ref, q_ref, k_ref, v_ref):
    x = x_ref[...].astype(BF16)
    acc = jnp.dot(x, w_ref[...], preferred_element_type=F32) + b_ref[...]
    c = c_ref[...]
    a = a_ref[...]
    s = s_ref[...]

    def rope(t):
        return t * c + pltpu.roll(t, LANES - ROT_DIM // 2, 1) * a + pltpu.roll(t, ROT_DIM // 2, 1) * s

    for j in range(D_MODEL // LANES):
        sl = slice(j * LANES, (j + 1) * LANES)
        q_ref[:, sl] = (rope(acc[:, sl]) * Q_SCALE).astype(BF16)
    for j in range(KV_DIM // LANES):
        sl = slice(D_MODEL + j * LANES, D_MODEL + (j + 1) * LANES)
        k_ref[:, j * LANES:(j + 1) * LANES] = rope(acc[:, sl])
    v_ref[...] = acc[:, D_MODEL + KV_DIM:]


def _qkv_rope(x, w_bf, b, cos_t, sa_t, sb_t, tm):
    t = x.shape[0]
    period = cos_t.shape[0] // tm
    row = lambda i: (i, 0)
    tab = lambda i: (i % period, 0)
    fixed = lambda i: (0, 0)
    return pl.pallas_call(
        _qkv_kernel,
        grid=(t // tm,),
        in_specs=[pl.BlockSpec((tm, D_MODEL), row),
                  pl.BlockSpec((D_MODEL, QKV_DIM), fixed),
                  pl.BlockSpec((1, QKV_DIM), fixed),
                  pl.BlockSpec((tm, LANES), tab),
                  pl.BlockSpec((tm, LANES), tab),
                  pl.BlockSpec((tm, LANES), tab)],
        out_specs=[pl.BlockSpec((tm, D_MODEL), row),
                   pl.BlockSpec((tm, KV_DIM), row),
                   pl.BlockSpec((tm, KV_DIM), row)],
        out_shape=[jax.ShapeDtypeStruct((t, D_MODEL), BF16),
                   jax.ShapeDtypeStruct((t, KV_DIM), F32),
                   jax.ShapeDtypeStruct((t, KV_DIM), F32)],
        compiler_params=_cparams(1),
        name="qkv_rope",
    )(x, w_bf, b, cos_t, sa_t, sb_t)


def _qkv_t_kernel(x_ref, wqt_ref, bq_ref, wkv_ref, bkv_ref, wvt_ref, bv_ref, c_ref, a_ref, s_ref,
                  ct_ref, st_ref, qt_ref, k_ref, v_ref, vt_ref):
    x = x_ref[...].astype(BF16)
    nt = (((1,), (1,)), ((), ()))
    qt = lax.dot_general(wqt_ref[...], x, nt, preferred_element_type=F32) + bq_ref[...]
    ct = ct_ref[...]
    st = st_ref[...]
    half = ROT_DIM // 2
    for h in range(N_HEADS):
        r0 = h * HEAD_DIM
        x1 = qt[r0:r0 + half]
        x2 = qt[r0 + half:r0 + ROT_DIM]
        rot = jnp.concatenate([x1 * ct - x2 * st, x2 * ct + x1 * st], axis=0)
        qt_ref[r0:r0 + ROT_DIM, :] = (rot * Q_SCALE).astype(BF16)
        qt_ref[r0 + ROT_DIM:r0 + HEAD_DIM, :] = (qt[r0 + ROT_DIM:r0 + HEAD_DIM] * Q_SCALE).astype(BF16)
    vt = lax.dot_general(wvt_ref[...], x, nt, preferred_element_type=F32) + bv_ref[...]
    vt_ref[...] = vt.astype(BF16)

    kv = jnp.dot(x, wkv_ref[...], preferred_element_type=F32) + bkv_ref[...]
    c = c_ref[...]
    a = a_ref[...]
    s = s_ref[...]
    for j in range(KV_DIM // LANES):
        t = kv[:, j * LANES:(j + 1) * LANES]
        k_ref[:, j * LANES:(j + 1) * LANES] = (
            t * c + pltpu.roll(t, LANES - half, 1) * a + pltpu.roll(t, half, 1) * s)
    v_ref[...] = kv[:, KV_DIM:]


def _qkv_rope_t(x, w_bf, b, cos_t, sa_t, sb_t, cos_tt, sin_tt, tm):
    t = x.shape[0]
    period = cos_t.shape[0] // tm
    row = lambda i: (i, 0)
    col = lambda i: (0, i)
    tab = lambda i: (i % period, 0)
    tabt = lambda i: (0, i % period)
    fixed = lambda i: (0, 0)
    wqt = w_bf[:, :D_MODEL].T
    wvt = w_bf[:, D_MODEL + KV_DIM:].T
    half = ROT_DIM // 2
    return pl.pallas_call(
        _qkv_t_kernel,
        grid=(t // tm,),
        in_specs=[pl.BlockSpec((tm, D_MODEL), row),
                  pl.BlockSpec((D_MODEL, D_MODEL), fixed),
                  pl.BlockSpec((D_MODEL, 1), fixed),
                  pl.BlockSpec((D_MODEL, 2 * KV_DIM), fixed),
                  pl.BlockSpec((1, 2 * KV_DIM), fixed),
                  pl.BlockSpec((KV_DIM, D_MODEL), fixed),
                  pl.BlockSpec((KV_DIM, 1), fixed),
                  pl.BlockSpec((tm, LANES), tab),
                  pl.BlockSpec((tm, LANES), tab),
                  pl.BlockSpec((tm, LANES), tab),
                  pl.BlockSpec((half, tm), tabt),
                  pl.BlockSpec((half, tm), tabt)],
        out_specs=[pl.BlockSpec((D_MODEL, tm), col),
                   pl.BlockSpec((tm, KV_DIM), row),
                   pl.BlockSpec((tm, KV_DIM), row),
                   pl.BlockSpec((KV_DIM, tm), col)],
        out_shape=[jax.ShapeDtypeStruct((D_MODEL, t), BF16),
                   jax.ShapeDtypeStruct((t, KV_DIM), F32),
                   jax.ShapeDtypeStruct((t, KV_DIM), F32),
                   jax.ShapeDtypeStruct((KV_DIM, t), BF16)],
        compiler_params=_cparams(1),
        name="qkv_rope_t",
    )(x, wqt, b[:, :D_MODEL].reshape(D_MODEL, 1), w_bf[:, D_MODEL:], b[:, D_MODEL:],
      wvt, b[:, D_MODEL + KV_DIM:].reshape(KV_DIM, 1), cos_t, sa_t, sb_t, cos_tt, sin_tt)


def _attend(q, kcat, vcat, bias, sink_ref, write):
    kgs = [kcat[:, g * HEAD_DIM:(g + 1) * HEAD_DIM] for g in range(N_KV_HEADS)]

    def scores(h):
        qh = q[:, h * HEAD_DIM:(h + 1) * HEAD_DIM]
        return lax.dot_general(qh, kgs[h // GROUP], (((1,), (1,)), ((), ())), preferred_element_type=F32) + bias

    s_next = scores(0)
    for h in range(N_HEADS):
        s = s_next
        if h + 1 < N_HEADS:
            s_next = scores(h + 1)
        vg = vcat[:, (h // GROUP) * HEAD_DIM:(h // GROUP + 1) * HEAD_DIM]
        sink = sink_ref[h] * LOG2_E
        m = jnp.maximum(jnp.max(s, axis=-1, keepdims=True), sink)
        p = jnp.exp2(s - m)
        denom = jnp.sum(p, axis=-1, keepdims=True) + jnp.exp2(sink - m)
        o = jnp.dot(p.astype(BF16), vg, preferred_element_type=F32)
        write(h, o / denom)


def _attn_prompt_kernel(sink_ref, qt_ref, kp_ref, ko_ref, km_ref, vtp_ref, vto_ref, vtm_ref, o_ref, acc_ref):
    i = pl.program_id(1)
    kcat = jnp.concatenate([kp_ref[...], ko_ref[...], km_ref[...]], axis=0).astype(BF16)
    vt = jnp.concatenate([vtp_ref[...], vto_ref[...], vtm_ref[...]], axis=1)
    nk = 2 * ATTN_BLOCK + META_LEN
    key = lax.broadcasted_iota(jnp.int32, (nk, ATTN_BLOCK), 0)
    qry = lax.broadcasted_iota(jnp.int32, (nk, ATTN_BLOCK), 1)
    first = jnp.where(i > 0, 0, 2 * ATTN_BLOCK)
    prev_ok = (key < ATTN_BLOCK) & (key >= qry + first)
    own_ok = (key >= ATTN_BLOCK) & (key - ATTN_BLOCK <= qry)
    mask = prev_ok | own_ok | (key >= 2 * ATTN_BLOCK)
    bias = jnp.where(mask, 0.0, NEG_INF)
    bias = jnp.concatenate([bias] * HEADS_PER_DOT, axis=1)
    lane = lax.broadcasted_iota(jnp.int32, (1, HEADS_PER_DOT * ATTN_BLOCK), 1)
    kgs = [kcat[:, g * HEAD_DIM:(g + 1) * HEAD_DIM] for g in range(N_KV_HEADS)]

    def scores(u):
        h0 = u * HEADS_PER_DOT
        rhs = jnp.concatenate([qt_ref[(h0 + j) * HEAD_DIM:(h0 + j + 1) * HEAD_DIM, :]
                               for j in range(HEADS_PER_DOT)], axis=1)
        return jnp.dot(kgs[h0 // GROUP], rhs, preferred_element_type=F32) + bias

    n_units = N_HEADS // HEADS_PER_DOT
    s_next = scores(0)
    for u in range(n_units):
        s = s_next
        if u + 1 < n_units:
            s_next = scores(u + 1)
        h0 = u * HEADS_PER_DOT
        g = h0 // GROUP
        sink = sink_ref[h0] * LOG2_E
        for j in range(1, HEADS_PER_DOT):
            sink = jnp.where(lane >= j * ATTN_BLOCK, sink_ref[h0 + j] * LOG2_E, sink)
        m = jnp.maximum(jnp.max(s, axis=0, keepdims=True), sink)
        p = jnp.exp2(s - m)
        denom = jnp.sum(p, axis=0, keepdims=True) + jnp.exp2(sink - m)
        o = jnp.dot(vt[g * HEAD_DIM:(g + 1) * HEAD_DIM, :], p.astype(BF16), preferred_element_type=F32)
        o = o * (1.0 / denom)
        for j in range(HEADS_PER_DOT):
            acc_ref[(h0 + j) * HEAD_DIM:(h0 + j + 1) * HEAD_DIM, :] = o[:, j * ATTN_BLOCK:(j + 1) * ATTN_BLOCK]
    o_ref[...] = acc_ref[...].T.astype(o_ref.dtype)


def _attn_prompt(qt, k, vt, k_meta, vt_meta, sinks, n_batch, seq):
    nblk = seq // ATTN_BLOCK
    own = lambda b, i: b * nblk + i
    prev = lambda b, i: b * nblk + jnp.maximum(i - 1, 0)
    kspec = lambda m: pl.BlockSpec((ATTN_BLOCK, KV_DIM), lambda b, i: (m(b, i), 0))
    vspec = lambda m: pl.BlockSpec((KV_DIM, ATTN_BLOCK), lambda b, i: (0, m(b, i)))
    return pl.pallas_call(
        _attn_prompt_kernel,
        grid=(n_batch, nblk),
        in_specs=[pl.BlockSpec(memory_space=pltpu.SMEM),
                  pl.BlockSpec((D_MODEL, ATTN_BLOCK), lambda b, i: (0, own(b, i))),
                  kspec(prev), kspec(own),
                  pl.BlockSpec((META_LEN, KV_DIM), lambda b, i: (b, 0)),
                  vspec(prev), vspec(own),
                  pl.BlockSpec((None, KV_DIM, META_LEN), lambda b, i: (b, 0, 0))],
        out_specs=pl.BlockSpec((ATTN_BLOCK, D_MODEL), lambda b, i: (own(b, i), 0)),
        out_shape=jax.ShapeDtypeStruct((n_batch * seq, D_MODEL), BF16),
        scratch_shapes=[pltpu.VMEM((D_MODEL, ATTN_BLOCK), F32)],
        compiler_params=_cparams(2),
        name="attn_prompt",
    )(sinks, qt, k, k, k_meta, vt, vt, vt_meta)


def _attn_meta_kernel(sink_ref, q_ref, k_ref, v_ref, o_ref):
    r = lax.broadcasted_iota(jnp.int32, (META_LEN, META_LEN), 0)
    c = lax.broadcasted_iota(jnp.int32, (META_LEN, META_LEN), 1)

    def write(h, o):
        o_ref[:, h * HEAD_DIM:(h + 1) * HEAD_DIM] = o.astype(o_ref.dtype)

    bias = jnp.where(c <= r, 0.0, NEG_INF)
    _attend(q_ref[...], k_ref[...].astype(BF16), v_ref[...].astype(BF16), bias, sink_ref, write)


def _attn_meta(q, k, v, sinks, n_batch):
    imap = lambda b: (b, 0)
    return pl.pallas_call(
        _attn_meta_kernel,
        grid=(n_batch,),
        in_specs=[pl.BlockSpec(memory_space=pltpu.SMEM),
                  pl.BlockSpec((META_LEN, D_MODEL), imap),
                  pl.BlockSpec((META_LEN, KV_DIM), imap),
                  pl.BlockSpec((META_LEN, KV_DIM), imap)],
        out_specs=pl.BlockSpec((META_LEN, D_MODEL), lambda b: (b, 0)),
        out_shape=jax.ShapeDtypeStruct((n_batch * META_LEN, D_MODEL), BF16),
        compiler_params=_cparams(1),
        name="attn_meta",
    )(sinks, q, k, v)


SAMPLE_GROUP = 8


def _attn_sample_kernel(sink_ref, bias_ref, q_ref, kn_ref, vn_ref, cmk_ref, cmv_ref, cwk_ref, cwv_ref,
                        o_ref, *, dec_seq, group):
    def keys(cm_ref, cw_ref, new):
        parts = []
        for j in range(group):
            parts += [cm_ref[j], cw_ref[j], new[j * dec_seq:(j + 1) * dec_seq]]
        return jnp.concatenate(parts, axis=0).astype(BF16)

    kcat = keys(cmk_ref, cwk_ref, kn_ref[...])
    vcat = keys(cmv_ref, cwv_ref, vn_ref[...])

    def write(h, o):
        o_ref[:, h * HEAD_DIM:(h + 1) * HEAD_DIM] = o.astype(o_ref.dtype)

    _attend(q_ref[...], kcat, vcat, bias_ref[...], sink_ref, write)


def _sample_bias(group, dec_seq):
    per = META_LEN + WINDOW + dec_seq
    r = np.arange(group * dec_seq)[:, None]
    c = np.arange(group * per)[None, :]
    tq, ck = r % dec_seq, c % per
    win_ok = (ck >= META_LEN) & (ck < META_LEN + WINDOW) & (ck - META_LEN >= tq)
    new_ok = (ck >= META_LEN + WINDOW) & (ck - (META_LEN + WINDOW) <= tq)
    vis = (r // dec_seq == c // per) & ((ck < META_LEN) | win_ok | new_ok)
    return np.where(vis, 0.0, NEG_INF).astype(np.float32)


def _attn_sample(q, k, v, cmk, cmv, cwk, cwv, sinks, row0, dec_batch, dec_seq):
    group = SAMPLE_GROUP
    nq = group * dec_seq
    blk0 = row0 // nq
    bias = jnp.asarray(_sample_bias(group, dec_seq))
    qmap = lambda n: (blk0 + n, 0)
    cmap = lambda n: (n, 0, 0)
    return pl.pallas_call(
        functools.partial(_attn_sample_kernel, dec_seq=dec_seq, group=group),
        grid=(dec_batch // group,),
        in_specs=[pl.BlockSpec(memory_space=pltpu.SMEM),
                  pl.BlockSpec(bias.shape, lambda n: (0, 0)),
                  pl.BlockSpec((nq, D_MODEL), qmap),
                  pl.BlockSpec((nq, KV_DIM), qmap),
                  pl.BlockSpec((nq, KV_DIM), qmap),
                  pl.BlockSpec((group, META_LEN, KV_DIM), cmap),
                  pl.BlockSpec((group, META_LEN, KV_DIM), cmap),
                  pl.BlockSpec((group, WINDOW, KV_DIM), cmap),
                  pl.BlockSpec((group, WINDOW, KV_DIM), cmap)],
        out_specs=pl.BlockSpec((nq, D_MODEL), lambda n: (n, 0)),
        out_shape=jax.ShapeDtypeStruct((dec_batch * dec_seq, D_MODEL), BF16),
        compiler_params=_cparams(1),
        name="attn_sample",
    )(sinks, bias, q, k, v, cmk, cmv, cwk, cwv)


def _proj_ln_route_kernel(a_ref, w_ref, b_ref, x_ref, g_ref, bb_ref, wr_ref, br_ref, tri_ref, cin_ref,
                          x1_ref, ids_ref, gates_ref, rank_ref, cnt_ref, carry_ref):
    @pl.when(pl.program_id(0) == 0)
    def _():
        carry_ref[...] = cin_ref[...].astype(F32)

    y = jnp.dot(a_ref[...], w_ref[...], preferred_element_type=F32) + b_ref[...]
    x1 = _layer_norm(DEEPNORM_ALPHA * x_ref[...] + y, g_ref[...], bb_ref[...])
    for j in range(N_SLAB):
        x1_ref[pl.ds(j, x1.shape[0], stride=N_SLAB), :] = x1[:, j * LANES:(j + 1) * LANES]

    logits = lax.dot_general(wr_ref[...], x1.astype(BF16), (((1,), (1,)), ((), ())),
                             preferred_element_type=F32) + br_ref[...]
    tm = logits.shape[1]
    eidx = lax.broadcasted_iota(jnp.int32, (N_EXPERTS, tm), 0).astype(F32)
    cur = logits
    vals, idxs, sels = [], [], []
    for _ in range(TOP_K):
        m = jnp.max(cur, axis=0, keepdims=True)
        idx = jnp.min(jnp.where(cur == m, eidx, float(N_EXPERTS)), axis=0, keepdims=True)
        sel = eidx == idx
        vals.append(m)
        idxs.append(idx)
        sels.append(sel)
        cur = jnp.where(sel, -jnp.inf, cur)
    exps = [jnp.exp(v - vals[0]) for v in vals]
    tot = exps[0] + exps[1] + exps[2] + exps[3]
    gates_ref[...] = jnp.concatenate([e / tot for e in exps], axis=0)
    ids_ref[...] = jnp.concatenate(idxs, axis=0).astype(jnp.int32)

    chosen = jnp.where(sels[0] | sels[1] | sels[2] | sels[3], 1.0, 0.0)
    before = jnp.dot(chosen.astype(BF16), tri_ref[...], preferred_element_type=F32)
    before = before + carry_ref[:, 0:1]
    ranks = [jnp.sum(jnp.where(s, before, 0.0), axis=0, keepdims=True) for s in sels]
    rank_ref[...] = jnp.concatenate(ranks, axis=0).astype(jnp.int32)
    carry_ref[...] = carry_ref[...] + jnp.sum(chosen, axis=1, keepdims=True)
    cnt_ref[...] = carry_ref[...].astype(jnp.int32)


def _proj_ln_route(a_bf, w_bf, b, x, g, bb, wr_t_bf, br_col, counts_in, tm):
    t = x.shape[0]
    row = lambda i: (i, 0)
    col = lambda i: (0, i)
    fixed = lambda i: (0, 0)
    tri = (jnp.arange(tm)[:, None] < jnp.arange(tm)[None, :]).astype(BF16)
    return pl.pallas_call(
        _proj_ln_route_kernel,
        grid=(t // tm,),
        in_specs=[pl.BlockSpec((tm, D_MODEL), row),
                  pl.BlockSpec((D_MODEL, D_MODEL), fixed),
                  pl.BlockSpec((1, D_MODEL), fixed),
                  pl.BlockSpec((tm, D_MODEL), row),
                  pl.BlockSpec((1, D_MODEL), fixed),
                  pl.BlockSpec((1, D_MODEL), fixed),
                  pl.BlockSpec((N_EXPERTS, D_MODEL), fixed),
                  pl.BlockSpec((N_EXPERTS, 1), fixed),
                  pl.BlockSpec((tm, tm), fixed),
                  pl.BlockSpec((N_EXPERTS, LANES), fixed)],
        out_specs=[pl.BlockSpec((tm * N_SLAB, LANES), row),
                   pl.BlockSpec((TOP_K, tm), col),
                   pl.BlockSpec((TOP_K, tm), col),
                   pl.BlockSpec((TOP_K, tm), col),
                   pl.BlockSpec((N_EXPERTS, LANES), fixed)],
        out_shape=[jax.ShapeDtypeStruct((t * N_SLAB, LANES), F32),
                   jax.ShapeDtypeStruct((TOP_K, t), jnp.int32),
                   jax.ShapeDtypeStruct((TOP_K, t), F32),
                   jax.ShapeDtypeStruct((TOP_K, t), jnp.int32),
                   jax.ShapeDtypeStruct((N_EXPERTS, LANES), jnp.int32)],
        scratch_shapes=[pltpu.VMEM((N_EXPERTS, LANES), F32)],
        compiler_params=_cparams(1),
        name="proj_ln_route",
    )(a_bf, w_bf, b, x, g, bb, wr_t_bf, br_col, tri, counts_in)


def _idx_copy(dest_hbm, dsm, sem, step, stride, n_slots=2):
    slot = step % n_slots
    return pltpu.make_async_copy(dest_hbm.at[pl.ds(pl.multiple_of(step * stride, stride), stride)],
                                 dsm.at[pl.ds(pl.multiple_of(slot * stride, stride), stride)],
                                 sem)


def _dispatch_kernel(pstart_ref, padded_ref, nact_ref, dest_hbm, *rest, tiles, steps, stride, n_tiles):
    x_refs = rest[:len(tiles)]
    xs_hbm, dsm, zbuf, sems = rest[len(tiles):]
    i = pl.program_id(0)
    nsteps = pl.num_programs(0)

    def zero_tile(row0):
        n = GMM_TILE * N_SLAB
        return pltpu.make_async_copy(zbuf, xs_hbm.at[pl.ds(pl.multiple_of(row0 * N_SLAB, n), n)], sems.at[2])

    def token(ref, r):
        return ref.at[pl.ds(pl.multiple_of(r * N_SLAB, N_SLAB), N_SLAB)]

    @pl.when(i == 0)
    def _():
        zbuf[...] = jnp.zeros_like(zbuf)

        def each_expert(fn):
            def body(e, _):
                @pl.when(padded_ref[e] > 0)
                def _():
                    fn(zero_tile(pstart_ref[e] + padded_ref[e] - GMM_TILE))
                return 0
            lax.fori_loop(0, N_EXPERTS, body, 0)

        def each_tail(fn):
            def body(t, _):
                fn(zero_tile(t * GMM_TILE))
                return 0
            lax.fori_loop(nact_ref[0], n_tiles, body, 0)

        each_expert(lambda cp: cp.start())
        each_tail(lambda cp: cp.start())
        each_expert(lambda cp: cp.wait())
        each_tail(lambda cp: cp.wait())
        _idx_copy(dest_hbm, dsm, sems.at[0], i, stride).start()

    _idx_copy(dest_hbm, dsm, sems.at[0], i, stride).wait()

    @pl.when(i + 1 < nsteps)
    def _():
        _idx_copy(dest_hbm, dsm, sems.at[0], i + 1, stride).start()

    base = (i % 2) * stride
    step0 = 0
    for x_ref, tm, n in zip(x_refs, tiles, steps):
        @pl.when((i >= step0) & (i < step0 + n))
        def _(x_ref=x_ref, tm=tm):
            def row_body(r, _):
                for k in range(TOP_K):
                    d = dsm[base + k * tm + r]
                    pltpu.make_async_copy(token(x_ref, r), token(xs_hbm, d), sems.at[1]).start(priority=k % 2)
                return 0

            lax.fori_loop(0, tm, row_body, 0, unroll=4)
            for k in range(TOP_K):
                pltpu.make_async_copy(x_ref, xs_hbm.at[pl.ds(0, tm * N_SLAB)], sems.at[1]).wait()
        step0 += n


def _seg_map(step0, n):
    return lambda i, *_: (jnp.clip(i - step0, 0, n - 1), 0)


def _dispatch(xs_list, tiles, dest_steps, pstart, padded, nact, stride, n_tiles):
    steps = [x.shape[0] // (tm * N_SLAB) for x, tm in zip(xs_list, tiles)]
    in_specs = [pl.BlockSpec(memory_space=pl.ANY)]
    step0 = 0
    for tm, n in zip(tiles, steps):
        in_specs.append(pl.BlockSpec((tm * N_SLAB, LANES), _seg_map(step0, n)))
        step0 += n
    grid_spec = pltpu.PrefetchScalarGridSpec(
        num_scalar_prefetch=3,
        grid=(sum(steps),),
        in_specs=in_specs,
        out_specs=pl.BlockSpec(memory_space=pl.ANY),
        scratch_shapes=[pltpu.SMEM((2 * stride,), jnp.int32),
                        pltpu.VMEM((GMM_TILE * N_SLAB, LANES), F32),
                        pltpu.SemaphoreType.DMA((3,))],
    )
    return pl.pallas_call(
        functools.partial(_dispatch_kernel, tiles=tuple(tiles), steps=tuple(steps), stride=stride,
                          n_tiles=n_tiles),
        grid_spec=grid_spec,
        out_shape=jax.ShapeDtypeStruct((n_tiles * GMM_TILE * N_SLAB, LANES), F32),
        compiler_params=_cparams(1),
        name="moe_dispatch",
    )(pstart, padded, nact, dest_steps, *xs_list)


def _gmm_kernel(te_ref, tsrc_ref, tfirst_ref, nact_ref, x_ref, wgu_ref, bgu_ref, wdn_ref, bdn_ref, y_ref,
                wgu_bf, wdn_bf):
    i = pl.program_id(0)

    @pl.when(tfirst_ref[i] == 1)
    def _():
        wgu_bf[...] = wgu_ref[...].astype(BF16)
        wdn_bf[...] = wdn_ref[...].astype(BF16)

    @pl.when(i < nact_ref[0])
    def _():
        x = jnp.concatenate([x_ref[pl.ds(j, GMM_TILE, stride=N_SLAB), :] for j in range(N_SLAB)], axis=1)
        gu = jnp.dot(x.astype(BF16), wgu_bf[...], preferred_element_type=F32) + bgu_ref[...]
        gate = jnp.minimum(gu[:, :MOE_FF], SWIGLU_LIMIT)
        up = jnp.clip(gu[:, MOE_FF:], -SWIGLU_LIMIT, SWIGLU_LIMIT)
        glu = gate * jax.nn.sigmoid(SWIGLU_ALPHA * gate)
        h = ((up + 1.0) * glu).astype(BF16)
        y = jnp.dot(h, wdn_bf[...], preferred_element_type=F32) + bdn_ref[...]
        for j in range(N_SLAB):
            y_ref[pl.ds(j, GMM_TILE, stride=N_SLAB), :] = y[:, j * LANES:(j + 1) * LANES]

    @pl.when(i >= nact_ref[0])
    def _():
        y_ref[...] = jnp.zeros_like(y_ref)


def _gmm(xs, te, tsrc, tfirst, nact, wgu, bgu, wdn, bdn, n_tiles):
    emap = lambda i, te, *_: (te[i], 0, 0)
    grid_spec = pltpu.PrefetchScalarGridSpec(
        num_scalar_prefetch=4,
        grid=(n_tiles,),
        in_specs=[pl.BlockSpec((GMM_TILE * N_SLAB, LANES), lambda i, te, ts, *_: (ts[i], 0)),
                  pl.BlockSpec((None, D_MODEL, 2 * MOE_FF), emap),
                  pl.BlockSpec((None, 1, 2 * MOE_FF), emap),
                  pl.BlockSpec((None, MOE_FF, D_MODEL), emap),
                  pl.BlockSpec((None, 1, D_MODEL), emap)],
        out_specs=pl.BlockSpec((GMM_TILE * N_SLAB, LANES), lambda i, *_: (i, 0)),
        scratch_shapes=[pltpu.VMEM((D_MODEL, 2 * MOE_FF), BF16),
                        pltpu.VMEM((MOE_FF, D_MODEL), BF16)],
    )
    return pl.pallas_call(
        _gmm_kernel,
        grid_spec=grid_spec,
        out_shape=jax.ShapeDtypeStruct((n_tiles * GMM_TILE * N_SLAB, LANES), F32),
        compiler_params=_cparams(1),
        name="moe_gmm",
    )(te, tsrc, tfirst, nact, xs, wgu, bgu, wdn, bdn)


def _combine_kernel(dest_hbm, gate_hbm, ys_hbm, x_ref, g_ref, bb_ref, o_ref, dsm, gsm, buf, sems,
                    *, tm, stride):
    i = pl.program_id(0)
    nsteps = pl.num_programs(0)

    row_sem = lambda step: sems.at[2 * IDX_SLOTS + step % 2]

    def idx_copies(step):
        slot = step % IDX_SLOTS
        return (_idx_copy(dest_hbm, dsm, sems.at[slot], step, stride, IDX_SLOTS),
                _idx_copy(gate_hbm, gsm, sems.at[IDX_SLOTS + slot], step, stride, IDX_SLOTS))

    def token(ref, r):
        return ref.at[pl.ds(pl.multiple_of(r * N_SLAB, N_SLAB), N_SLAB)]

    def issue_row(step, r):
        base = (step % IDX_SLOTS) * stride
        half = (step % 2) * TOP_K
        for k in range(TOP_K):
            d = dsm[base + k * tm + r]
            pltpu.make_async_copy(token(ys_hbm, d), token(buf.at[half + k], r),
                                  row_sem(step)).start(priority=k % 2)

    @pl.when(i == 0)
    def _():
        for cp in idx_copies(0):
            cp.start()

        @pl.when(nsteps > 1)
        def _():
            for cp in idx_copies(1):
                cp.start()

        for cp in idx_copies(0):
            cp.wait()

        def first_rows(r, _):
            issue_row(0, r)
            return 0

        lax.fori_loop(0, tm, first_rows, 0, unroll=4)

    @pl.when(i + 1 < nsteps)
    def _():
        for cp in idx_copies(i + 1):
            cp.wait()

    @pl.when(i + 2 < nsteps)
    def _():
        for cp in idx_copies(i + 2):
            cp.start()

    @pl.when(i + 1 < nsteps)
    def _():
        def next_rows(r, _):
            issue_row(i + 1, r)
            return 0

        lax.fori_loop(0, tm, next_rows, 0, unroll=4)

    half = (i % 2) * TOP_K
    for k in range(TOP_K):
        pltpu.make_async_copy(ys_hbm.at[pl.ds(0, tm * N_SLAB)], buf.at[half + k], row_sem(i)).wait()
    gbase = (i % IDX_SLOTS) * stride

    def token_sum(r, _):
        rows = pl.ds(pl.multiple_of(r * N_SLAB, N_SLAB), N_SLAB)
        h = DEEPNORM_ALPHA * x_ref[rows, :]
        for k in range(TOP_K):
            h = h + gsm[gbase + k * tm + r] * buf[half + k, rows, :]
        buf[half, rows, :] = h
        return 0

    lax.fori_loop(0, tm, token_sum, 0, unroll=8)

    def block(b, _):
        r0 = pl.multiple_of(b * COMBINE_ROWS, COMBINE_ROWS)
        hs = [buf[half, pl.ds(r0 * N_SLAB + j, COMBINE_ROWS, stride=N_SLAB), :] for j in range(N_SLAB)]
        tot = hs[0]
        for j in range(1, N_SLAB):
            tot = tot + hs[j]
        mu = jnp.sum(tot, axis=1, keepdims=True) * (1.0 / D_MODEL)
        cen = [h - mu for h in hs]
        sq = cen[0] * cen[0]
        for j in range(1, N_SLAB):
            sq = sq + cen[j] * cen[j]
        inv = lax.rsqrt(jnp.sum(sq, axis=1, keepdims=True) * (1.0 / D_MODEL) + LN_EPS)
        for j in range(N_SLAB):
            o_ref[pl.ds(r0, COMBINE_ROWS), j * LANES:(j + 1) * LANES] = (
                cen[j] * inv * g_ref[j:j + 1, :] + bb_ref[j:j + 1, :])
        return 0

    lax.fori_loop(0, tm // COMBINE_ROWS, block, 0, unroll=2)


def _combine(ys, dest_steps, gate_steps, x_tiles, g3, bb3, tm, stride):
    t = x_tiles.shape[0] // N_SLAB
    row = lambda i: (i, 0)
    fixed = lambda i: (0, 0)
    return pl.pallas_call(
        functools.partial(_combine_kernel, tm=tm, stride=stride),
        grid=(t // tm,),
        in_specs=[pl.BlockSpec(memory_space=pl.ANY),
                  pl.BlockSpec(memory_space=pl.ANY),
                  pl.BlockSpec(memory_space=pl.ANY),
                  pl.BlockSpec((tm * N_SLAB, LANES), row),
                  pl.BlockSpec((N_SLAB, LANES), fixed),
                  pl.BlockSpec((N_SLAB, LANES), fixed)],
        out_specs=pl.BlockSpec((tm, D_MODEL), row),
        out_shape=jax.ShapeDtypeStruct((t, D_MODEL), F32),
        scratch_shapes=[pltpu.SMEM((IDX_SLOTS * stride,), jnp.int32),
                        pltpu.SMEM((IDX_SLOTS * stride,), F32),
                        pltpu.VMEM((2 * TOP_K, tm * N_SLAB, LANES), F32),
                        pltpu.SemaphoreType.DMA((2 * IDX_SLOTS + 2,))],
        compiler_params=_cparams(1),
        name="moe_combine",
    )(dest_steps, gate_steps, ys, x_tiles, g3, bb3)


def _moe(segs, counts, expert0, wgu, bgu, wdn, bdn, g, bb):
    t = sum(seg[1].shape[1] for seg in segs)
    n_tiles = -(-(t * TOP_K) // GMM_TILE) + N_EXPERTS
    counts = counts[:, 0]
    padded = ((counts + GMM_TILE - 1) // GMM_TILE) * GMM_TILE
    pend = jnp.cumsum(padded)
    pstart = (pend - padded).astype(jnp.int32)
    padded = padded.astype(jnp.int32)
    nact = (pend[-1] // GMM_TILE).astype(jnp.int32).reshape(1)
    tile = jnp.arange(n_tiles, dtype=jnp.int32)
    tsrc = jnp.minimum(tile, jnp.maximum(nact[0] - 1, 0))
    te = jnp.sum((pend[None, :] <= (tsrc * GMM_TILE)[:, None]).astype(jnp.int32), axis=1)
    te = jnp.minimum(te, N_EXPERTS - 1).astype(jnp.int32)
    tfirst = jnp.concatenate([jnp.ones((1,), jnp.int32), (te[1:] != te[:-1]).astype(jnp.int32)])
    eids = jnp.arange(N_EXPERTS, dtype=jnp.int32)[:, None, None]

    stride = -(-(TOP_K * max(seg[4] for seg in segs)) // 1024) * 1024
    def per_step(a, tm):
        n = a.shape[1]
        steps = a.reshape(TOP_K, n // tm, tm).transpose(1, 0, 2).reshape(n // tm, TOP_K * tm)
        return jnp.pad(steps, ((0, 0), (0, stride - TOP_K * tm))).reshape(-1)

    plans = []
    for x1, ids, gates, rank, tm in segs:
        dest = jnp.sum(jnp.where(ids[None] == eids, pstart[:, None, None], 0), axis=0) + rank
        plans.append((per_step(dest, tm), per_step(gates, tm)))
    xs = _dispatch([seg[0] for seg in segs], [seg[4] for seg in segs],
                   jnp.concatenate([p[0] for p in plans]), pstart, padded, nact, stride, n_tiles)
    ys = _gmm(xs, te + expert0, tsrc, tfirst, nact, wgu, bgu, wdn, bdn, n_tiles)
    return [_combine(ys, dest_steps, gate_steps, seg[0], g, bb, seg[4], stride)
            for seg, (dest_steps, gate_steps) in zip(segs, plans)]


def _pw1_glu_kernel(x_ref, w_ref, b_ref, u_ref):
    a = jnp.dot(x_ref[...].astype(BF16), w_ref[...], preferred_element_type=F32) + b_ref[...]
    u_ref[...] = a[:, :D_MODEL] * jax.nn.sigmoid(a[:, D_MODEL:])


def _pw1_glu(x, w_bf, b, tm):
    t = x.shape[0]
    row = lambda i: (i, 0)
    fixed = lambda i: (0, 0)
    return pl.pallas_call(
        _pw1_glu_kernel,
        grid=(t // tm,),
        in_specs=[pl.BlockSpec((tm, D_MODEL), row),
                  pl.BlockSpec((D_MODEL, 2 * D_MODEL), fixed),
                  pl.BlockSpec((1, 2 * D_MODEL), fixed)],
        out_specs=pl.BlockSpec((tm, D_MODEL), row),
        out_shape=jax.ShapeDtypeStruct((t, D_MODEL), F32),
        compiler_params=_cparams(1),
        name="pw1_glu",
    )(x, w_bf, b)


def _conv_rows(win_ref, zs_ref, base, n_sets, w_ref, b_ref):
    shift = CONV_HALO - CONV_CTX
    offs = tuple(range(n_sets))

    def out_rows(o):
        return pl.ds(base + o, SUBLANES, stride=n_sets)

    def slab(c, _):
        def window(t):
            return win_ref[c, pl.ds(base + (shift + t), SUBLANES, stride=n_sets), :]

        bias = b_ref[pl.ds(c, 1), :]
        acc = [bias] * n_sets
        wins = [window(t) for t in range(n_sets - 1)]
        for j in range(CONV_WIDTH):
            w = w_ref[c, j:j + 1, :]
            wins.append(window(j + n_sets - 1))
            acc = [acc[o] + w * wins[o] for o in offs]
            wins.pop(0)
        for o in offs:
            zs_ref[c, out_rows(o), :] = acc[o]
        return 0

    lax.fori_loop(0, N_SLAB, slab, 0)


def _ln_silu_rows(zs_ref, rows, g_ref, bb_ref):
    acc = [zs_ref[c, rows, :] for c in range(N_SLAB)]
    tot = acc[0]
    for c in range(1, N_SLAB):
        tot = tot + acc[c]
    mu = jnp.sum(tot, axis=1, keepdims=True) * (1.0 / D_MODEL)
    cen = [a - mu for a in acc]
    sq = cen[0] * cen[0]
    for c in range(1, N_SLAB):
        sq = sq + cen[c] * cen[c]
    inv = lax.rsqrt(jnp.sum(sq, axis=1, keepdims=True) * (1.0 / D_MODEL) + LN_EPS)
    out = []
    for c in range(N_SLAB):
        z = cen[c] * inv * g_ref[c:c + 1, :] + bb_ref[c:c + 1, :]
        out.append(z * jax.nn.sigmoid(z))
    return out


def _conv_prompt_kernel(um_ref, up_ref, uc_ref, w_ref, b_ref, g_ref, bb_ref, z_ref, win_ref, zs_ref, *, tr):
    i = pl.program_id(1)
    gap = CONV_HALO - META_LEN

    @pl.when(i == 0)
    def _():
        for c in range(N_SLAB):
            win_ref[c, 0:gap, :] = jnp.zeros((gap, LANES), F32)
            win_ref[c, gap:CONV_HALO, :] = um_ref[:, c * LANES:(c + 1) * LANES]

    @pl.when(i > 0)
    def _():
        for c in range(N_SLAB):
            win_ref[c, 0:CONV_HALO, :] = up_ref[:, c * LANES:(c + 1) * LANES]

    for c in range(N_SLAB):
        win_ref[c, CONV_HALO:, :] = uc_ref[:, c * LANES:(c + 1) * LANES]

    def block(bi, _):
        base = pl.multiple_of(bi * CONV_BLOCK, CONV_BLOCK)
        _conv_rows(win_ref, zs_ref, base, CONV_BLOCK // SUBLANES, w_ref, b_ref)
        return 0

    lax.fori_loop(0, tr // CONV_BLOCK, block, 0)

    def norm(bi, _):
        rows = pl.ds(pl.multiple_of(bi * LN_ROWS, LN_ROWS), LN_ROWS)
        for c, z in enumerate(_ln_silu_rows(zs_ref, rows, g_ref, bb_ref)):
            z_ref[rows, c * LANES:(c + 1) * LANES] = z.astype(z_ref.dtype)
        return 0

    lax.fori_loop(0, tr // LN_ROWS, norm, 0, unroll=4)


def _conv_prompt(u, u_meta, w3, b3, g3, bb3, n_batch, seq, tr):
    nt = seq // tr
    cur = lambda b, i: (b * nt + i, 0)
    prev = lambda b, i: (jnp.maximum((b * seq + i * tr) // CONV_HALO - 1, 0), 0)
    fixed2 = lambda b, i: (0, 0)
    fixed3 = lambda b, i: (0, 0, 0)
    return pl.pallas_call(
        functools.partial(_conv_prompt_kernel, tr=tr),
        grid=(n_batch, nt),
        in_specs=[pl.BlockSpec((META_LEN, D_MODEL), lambda b, i: (b, 0)),
                  pl.BlockSpec((CONV_HALO, D_MODEL), prev),
                  pl.BlockSpec((tr, D_MODEL), cur),
                  pl.BlockSpec((N_SLAB, CONV_HALO, LANES), fixed3),
                  pl.BlockSpec((N_SLAB, LANES), fixed2),
                  pl.BlockSpec((N_SLAB, LANES), fixed2),
                  pl.BlockSpec((N_SLAB, LANES), fixed2)],
        out_specs=pl.BlockSpec((tr, D_MODEL), cur),
        out_shape=jax.ShapeDtypeStruct((n_batch * seq, D_MODEL), BF16),
        scratch_shapes=[pltpu.VMEM((N_SLAB, CONV_HALO + tr, LANES), F32),
                        pltpu.VMEM((N_SLAB, tr, LANES), F32)],
        compiler_params=_cparams(2),
        name="conv_prompt",
    )(u_meta, u, u, w3, b3, g3, bb3)


SHORT_ROWS = 16


def _conv_short_kernel(ctx_ref, u_ref, w_ref, b_ref, g_ref, bb_ref, z_ref, win_ref, zs_ref, *, n_seq, t_len):
    for n in range(n_seq):
        for c in range(N_SLAB):
            sl = slice(c * LANES, (c + 1) * LANES)
            win_ref[c, 0:CONV_HALO, :] = ctx_ref[n, :, sl]
            win_ref[c, CONV_HALO:CONV_HALO + t_len, :] = u_ref[n * t_len:(n + 1) * t_len, sl]
            if t_len < SHORT_ROWS:
                win_ref[c, CONV_HALO + t_len:, :] = jnp.zeros((SHORT_ROWS - t_len, LANES), F32)
        _conv_rows(win_ref, zs_ref, 0, SHORT_ROWS // SUBLANES, w_ref, b_ref)
        for c, z in enumerate(_ln_silu_rows(zs_ref, slice(0, t_len), g_ref, bb_ref)):
            z_ref[n * t_len:(n + 1) * t_len, c * LANES:(c + 1) * LANES] = z


def _conv_short(ctx_pad, u, row0, w3, b3, g3, bb3, t_len, n_seq):
    n_total = ctx_pad.shape[0]
    rows = n_seq * t_len
    blk0 = row0 // rows
    fixed2 = lambda n: (0, 0)
    fixed3 = lambda n: (0, 0, 0)
    return pl.pallas_call(
        functools.partial(_conv_short_kernel, n_seq=n_seq, t_len=t_len),
        grid=(n_total // n_seq,),
        in_specs=[pl.BlockSpec((n_seq, CONV_HALO, D_MODEL), lambda n: (n, 0, 0)),
                  pl.BlockSpec((rows, D_MODEL), lambda n: (blk0 + n, 0)),
                  pl.BlockSpec((N_SLAB, CONV_HALO, LANES), fixed3),
                  pl.BlockSpec((N_SLAB, LANES), fixed2),
                  pl.BlockSpec((N_SLAB, LANES), fixed2),
                  pl.BlockSpec((N_SLAB, LANES), fixed2)],
        out_specs=pl.BlockSpec((rows, D_MODEL), lambda n: (n, 0)),
        out_shape=jax.ShapeDtypeStruct((n_total * t_len, D_MODEL), F32),
        scratch_shapes=[pltpu.VMEM((N_SLAB, CONV_HALO + SHORT_ROWS, LANES), F32),
                        pltpu.VMEM((N_SLAB, SHORT_ROWS, LANES), F32)],
        compiler_params=_cparams(1),
        name="conv_short",
    )(ctx_pad, u, w3, b3, g3, bb3)


def _rope_angles(pos):
    inv = 1.0 / (ROPE_THETA ** (jnp.arange(0, ROT_DIM, 2, dtype=F32) / ROT_DIM))
    ang = pos.astype(F32)[:, None] * inv[None, :]
    return jnp.cos(ang), jnp.sin(ang)


def _rope_tables_t(pos):
    cos, sin = _rope_angles(pos)
    return cos.T, sin.T


def _rope_tables(pos):
    half = ROT_DIM // 2
    cos, sin = _rope_angles(pos)
    n = pos.shape[0]
    ones = jnp.ones((n, HEAD_DIM - ROT_DIM), F32)
    zeros = jnp.zeros((n, HEAD_DIM - ROT_DIM), F32)
    zh = jnp.zeros((n, half), F32)
    c = jnp.concatenate([cos, cos, ones], axis=1)
    a = jnp.concatenate([-sin, zh, zeros], axis=1)
    s = jnp.concatenate([zh, sin, zeros], axis=1)
    rep = LANES // HEAD_DIM
    return jnp.tile(c, (1, rep)), jnp.tile(a, (1, rep)), jnp.tile(s, (1, rep))


def kernel(x_prompt, x_sample, cache_attn_meta_k, cache_attn_meta_v, cache_attn_win_k, cache_attn_win_v, state_conv, meta_tokens, attn_w_qkv, attn_b_qkv, attn_sinks, attn_w_o, attn_b_o, conv_w_pw1, conv_b_pw1, conv_w_dw, conv_b_dw, conv_ln_g, conv_ln_b, conv_w_pw2, conv_b_pw2, ln_mix_g, ln_mix_b, ln_ffn_g, ln_ffn_b, moe_w_router, moe_b_router, moe_w_gate_up, moe_b_gate_up, moe_w_down, moe_b_down):
    n_batch, seq, _ = x_prompt.shape
    dec_batch, dec_seq, _ = x_sample.shape
    n_real = n_batch * seq
    n_meta = n_batch * META_LEN
    n_samp = dec_batch * dec_seq
    n_small = n_meta + n_samp
    tb = _pick_tile(seq, 512)
    ts = _pick_tile(n_small, 512)
    row2 = lambda v: v.reshape(1, -1)

    meta_rows = jnp.broadcast_to(meta_tokens[None], (n_batch, META_LEN, D_MODEL)).reshape(n_meta, D_MODEL)
    xb = x_prompt.reshape(n_real, D_MODEL)
    xs = jnp.concatenate([meta_rows.astype(F32), x_sample.reshape(n_samp, D_MODEL)], axis=0)

    def moe_layer(i, a_big, a_small, w, b, xb, xs):
        w_bf = w.astype(BF16)
        wr = moe_w_router[i].T.astype(BF16)
        br = moe_b_router[i].reshape(N_EXPERTS, 1)
        lng, lnb = row2(ln_mix_g[i]), row2(ln_mix_b[i])
        zero_counts = jnp.zeros((N_EXPERTS, LANES), jnp.int32)
        x1b, idb, gab, rab, cnt = _proj_ln_route(a_big, w_bf, row2(b), xb, lng, lnb, wr, br, zero_counts, tb)
        x1s, ids_, gas, ras, cnt = _proj_ln_route(a_small, w_bf, row2(b), xs, lng, lnb, wr, br, cnt, ts)
        n_all = moe_w_gate_up.shape[0] * N_EXPERTS
        return _moe([(x1b, idb, gab, rab, tb), (x1s, ids_, gas, ras, ts)], cnt, i * N_EXPERTS,
                    moe_w_gate_up.reshape(n_all, D_MODEL, 2 * MOE_FF),
                    moe_b_gate_up.reshape(n_all, 1, 2 * MOE_FF),
                    moe_w_down.reshape(n_all, MOE_FF, D_MODEL),
                    moe_b_down.reshape(n_all, 1, D_MODEL),
                    ln_ffn_g[i].reshape(N_SLAB, LANES), ln_ffn_b[i].reshape(N_SLAB, LANES))

    w_qkv = attn_w_qkv[0].astype(BF16)
    b_qkv = row2(attn_b_qkv[0])
    pos_small = jnp.concatenate([jnp.tile(jnp.arange(META_LEN), n_batch),
                                 jnp.tile(PAST_LEN + jnp.arange(dec_seq), dec_batch)])
    pos_big = META_LEN + jnp.arange(seq)
    qtb, kb, vb, vtb = _qkv_rope_t(xb, w_qkv, b_qkv, *_rope_tables(pos_big), *_rope_tables_t(pos_big), tb)
    qs, ks, vs = _qkv_rope(xs, w_qkv, b_qkv, *_rope_tables(pos_small), ts)
    sinks = attn_sinks[0]
    cmk = cache_attn_meta_k[0].reshape(dec_batch, META_LEN, KV_DIM)
    cmv = cache_attn_meta_v[0].reshape(dec_batch, META_LEN, KV_DIM)
    cwk = cache_attn_win_k[0].reshape(dec_batch, WINDOW, KV_DIM)
    cwv = cache_attn_win_v[0].reshape(dec_batch, WINDOW, KV_DIM)
    vt_meta = vs[:n_meta].reshape(n_batch, META_LEN, KV_DIM).transpose(0, 2, 1).astype(BF16)
    o_big = _attn_prompt(qtb, kb, vtb, ks, vt_meta, sinks, n_batch, seq)
    o_small = jnp.concatenate([
        _attn_meta(qs, ks, vs, sinks, n_batch),
        _attn_sample(qs, ks, vs, cmk, cmv, cwk, cwv, sinks, n_meta, dec_batch, dec_seq)], axis=0)
    xb, xs = moe_layer(0, o_big, o_small, attn_w_o[0], attn_b_o[0], xb, xs)

    kv4 = lambda a, n, t_len: a.reshape(n, t_len, N_KV_HEADS, HEAD_DIM)
    p_meta_k = kv4(ks[:n_meta], n_batch, META_LEN)[None]
    p_meta_v = kv4(vs[:n_meta], n_batch, META_LEN)[None]
    p_win_k = kv4(kb, n_batch, seq)[:, seq - WINDOW:][None]
    p_win_v = kv4(vb, n_batch, seq)[:, seq - WINDOW:][None]
    k_new = kv4(ks[n_meta:], dec_batch, dec_seq)
    v_new = kv4(vs[n_meta:], dec_batch, dec_seq)
    s_win_k = jnp.concatenate([cache_attn_win_k[0], k_new], axis=1)[:, -WINDOW:][None]
    s_win_v = jnp.concatenate([cache_attn_win_v[0], v_new], axis=1)[:, -WINDOW:][None]

    w_pw1 = conv_w_pw1[0].astype(BF16)
    ub = _pw1_glu(xb, w_pw1, row2(conv_b_pw1[0]), tb)
    us = _pw1_glu(xs, w_pw1, row2(conv_b_pw1[0]), ts)
    slab = lambda v: v.reshape(N_SLAB, LANES)
    w_dw = jnp.pad(conv_w_dw[0], ((0, CONV_HALO - CONV_WIDTH), (0, 0)))
    w3 = w_dw.reshape(CONV_HALO, N_SLAB, LANES).transpose(1, 0, 2)
    conv_args = (w3, slab(conv_b_dw[0]), slab(conv_ln_g[0]), slab(conv_ln_b[0]))
    z_big = _conv_prompt(ub, us, *conv_args, n_batch, seq, tb)
    z_meta = _conv_short(jnp.zeros((n_batch, CONV_HALO, D_MODEL), F32), us, 0, *conv_args, META_LEN, 1)
    ctx_pad = jnp.pad(state_conv[0], ((0, 0), (CONV_HALO - CONV_CTX, 0), (0, 0)))
    z_samp = _conv_short(ctx_pad, us, n_meta, *conv_args, dec_seq, 2)
    z_small = jnp.concatenate([z_meta, z_samp], axis=0).astype(BF16)
    xb, xs = moe_layer(1, z_big, z_small, conv_w_pw2[0], conv_b_pw2[0], xb, xs)

    p_conv = ub.reshape(n_batch, seq, D_MODEL)[:, seq - CONV_CTX:][None]
    u_samp = us[n_meta:].reshape(dec_batch, dec_seq, D_MODEL)
    s_conv = jnp.concatenate([state_conv[0], u_samp], axis=1)[:, -CONV_CTX:][None]

    y_prompt = xb.reshape(n_batch, seq, D_MODEL)
    y_sample = xs[n_meta:].reshape(dec_batch, dec_seq, D_MODEL)
    return (y_prompt, y_sample, p_meta_k, p_meta_v, p_win_k, p_win_v, p_conv,
            s_win_k, s_win_v, s_conv)
```

```python
import functools

import numpy as np
import jax
import jax.numpy as jnp
from jax import lax
from jax.experimental import pallas as pl
from jax.experimental.pallas import tpu as pltpu

F32 = jnp.float32
BF16 = jnp.bfloat16

D_MODEL = 1024
HEAD_DIM = 64
N_HEADS = 16
N_KV_HEADS = 4
GROUP = N_HEADS // N_KV_HEADS
KV_DIM = N_KV_HEADS * HEAD_DIM
QKV_DIM = D_MODEL + 2 * KV_DIM
ROT_DIM = 16
ROPE_THETA = 500000.0
WINDOW = 128
ATTN_BLOCK = 128
ATTN_SCALE = HEAD_DIM ** -0.5
LOG2_E = 1.4426950408889634
Q_SCALE = ATTN_SCALE * LOG2_E
META_LEN = 16
CONV_WIDTH = 31
CONV_CTX = CONV_WIDTH - 1
N_EXPERTS = 32
TOP_K = 4
MOE_FF = 1024
SWIGLU_LIMIT = 7.0
SWIGLU_ALPHA = 1.702
LN_EPS = 1e-5
DEPTH = 2
DEEPNORM_ALPHA = (2 * DEPTH) ** 0.25
PAST_LEN = 16384
NEG_INF = -1e30

LANES = 128
SUBLANES = 8
GMM_TILE = 512
CONV_HALO = 32
CONV_BLOCK = 32
LN_ROWS = 16
N_SLAB = D_MODEL // LANES
COMBINE_ROWS = 32
IDX_SLOTS = 3
HEADS_PER_DOT = 4
VMEM_LIMIT = 56 * 1024 * 1024


def _cparams(n_axes):
    return pltpu.CompilerParams(dimension_semantics=("arbitrary",) * n_axes,
                                vmem_limit_bytes=VMEM_LIMIT)


def _pick_tile(n, cap):
    best = None
    t = LANES
    while t <= cap:
        if n % t == 0:
            best = t
        t += LANES
    assert best is not None, n
    return best


def _layer_norm(h, g, b):
    mu = jnp.mean(h, axis=-1, keepdims=True)
    hc = h - mu
    var = jnp.mean(hc * hc, axis=-1, keepdims=True)
    return hc * lax.rsqrt(var + LN_EPS) * g + b


def _qkv_kernel(x_ref, w_ref, b_ref, c_ref, a_ref, s_ref, q_ref, k_ref, v_ref):
    x = x_ref[...].astype(BF16)
    acc = jnp.dot(x, w_ref[...], preferred_element_type=F32) + b_ref[...]
    c = c_ref[...]
    a = a_ref[...]
    s = s_ref[...]

    def rope(t):
        return t * c + pltpu.roll(t, LANES - ROT_DIM // 2, 1) * a + pltpu.roll(t, ROT_DIM // 2, 1) * s

    for j in range(D_MODEL // LANES):
        sl = slice(j * LANES, (j + 1) * LANES)
        q_ref[:, sl] = (rope(acc[:, sl]) * Q_SCALE).astype(BF16)
    for j in range(KV_DIM // LANES):
        sl = slice(D_MODEL + j * LANES, D_MODEL + (j + 1) * LANES)
        k_ref[:, j * LANES:(j + 1) * LANES] = rope(acc[:, sl])
    v_ref[...] = acc[:, D_MODEL + KV_DIM:]


def _qkv_rope(x, w_bf, b, cos_t, sa_t, sb_t, tm):
    t = x.shape[0]
    period = cos_t.shape[0] // tm
    row = lambda i: (i, 0)
    tab = lambda i: (i % period, 0)
    fixed = lambda i: (0, 0)
    return pl.pallas_call(
        _qkv_kernel,
        grid=(t // tm,),
        in_specs=[pl.BlockSpec((tm, D_MODEL), row),
                  pl.BlockSpec((D_MODEL, QKV_DIM), fixed),
                  pl.BlockSpec((1, QKV_DIM), fixed),
                  pl.BlockSpec((tm, LANES), tab),
                  pl.BlockSpec((tm, LANES), tab),
                  pl.BlockSpec((tm, LANES), tab)],
        out_specs=[pl.BlockSpec((tm, D_MODEL), row),
                   pl.BlockSpec((tm, KV_DIM), row),
                   pl.BlockSpec((tm, KV_DIM), row)],
        out_shape=[jax.ShapeDtypeStruct((t, D_MODEL), BF16),
                   jax.ShapeDtypeStruct((t, KV_DIM), F32),
                   jax.ShapeDtypeStruct((t, KV_DIM), F32)],
        compiler_params=_cparams(1),
        name="qkv_rope",
    )(x, w_bf, b, cos_t, sa_t, sb_t)


def _qkv_t_kernel(x_ref, wqt_ref, bq_ref, wkv_ref, bkv_ref, wvt_ref, bv_ref, c_ref, a_ref, s_ref,
                  ct_ref, st_ref, qt_ref, k_ref, v_ref, vt_ref):
    x = x_ref[...].astype(BF16)
    nt = (((1,), (1,)), ((), ()))
    qt = lax.dot_general(wqt_ref[...], x, nt, preferred_element_type=F32) + bq_ref[...]
    ct = ct_ref[...]
    st = st_ref[...]
    half = ROT_DIM // 2
    for h in range(N_HEADS):
        r0 = h * HEAD_DIM
        x1 = qt[r0:r0 + half]
        x2 = qt[r0 + half:r0 + ROT_DIM]
        rot = jnp.concatenate([x1 * ct - x2 * st, x2 * ct + x1 * st], axis=0)
        qt_ref[r0:r0 + ROT_DIM, :] = (rot * Q_SCALE).astype(BF16)
        qt_ref[r0 + ROT_DIM:r0 + HEAD_DIM, :] = (qt[r0 + ROT_DIM:r0 + HEAD_DIM] * Q_SCALE).astype(BF16)
    vt = lax.dot_general(wvt_ref[...], x, nt, preferred_element_type=F32) + bv_ref[...]
    vt_ref[...] = vt.astype(BF16)

    kv = jnp.dot(x, wkv_ref[...], preferred_element_type=F32) + bkv_ref[...]
    c = c_ref[...]
    a = a_ref[...]
    s = s_ref[...]
    for j in range(KV_DIM // LANES):
        t = kv[:, j * LANES:(j + 1) * LANES]
        k_ref[:, j * LANES:(j + 1) * LANES] = (
            t * c + pltpu.roll(t, LANES - half, 1) * a + pltpu.roll(t, half, 1) * s)
    v_ref[...] = kv[:, KV_DIM:]


def _qkv_rope_t(x, w_bf, b, cos_t, sa_t, sb_t, cos_tt, sin_tt, tm):
    t = x.shape[0]
    period = cos_t.shape[0] // tm
    row = lambda i: (i, 0)
    col = lambda i: (0, i)
    tab = lambda i: (i % period, 0)
    tabt = lambda i: (0, i % period)
    fixed = lambda i: (0, 0)
    wqt = w_bf[:, :D_MODEL].T
    wvt = w_bf[:, D_MODEL + KV_DIM:].T
    half = ROT_DIM // 2
    return pl.pallas_call(
        _qkv_t_kernel,
        grid=(t // tm,),
        in_specs=[pl.BlockSpec((tm, D_MODEL), row),
                  pl.BlockSpec((D_MODEL, D_MODEL), fixed),
                  pl.BlockSpec((D_MODEL, 1), fixed),
                  pl.BlockSpec((D_MODEL, 2 * KV_DIM), fixed),
                  pl.BlockSpec((1, 2 * KV_DIM), fixed),
                  pl.BlockSpec((KV_DIM, D_MODEL), fixed),
                  pl.BlockSpec((KV_DIM, 1), fixed),
                  pl.BlockSpec((tm, LANES), tab),
                  pl.BlockSpec((tm, LANES), tab),
                  pl.BlockSpec((tm, LANES), tab),
                  pl.BlockSpec((half, tm), tabt),
                  pl.BlockSpec((half, tm), tabt)],
        out_specs=[pl.BlockSpec((D_MODEL, tm), col),
                   pl.BlockSpec((tm, KV_DIM), row),
                   pl.BlockSpec((tm, KV_DIM), row),
                   pl.BlockSpec((KV_DIM, tm), col)],
        out_shape=[jax.ShapeDtypeStruct((D_MODEL, t), BF16),
                   jax.ShapeDtypeStruct((t, KV_DIM), F32),
                   jax.ShapeDtypeStruct((t, KV_DIM), F32),
                   jax.ShapeDtypeStruct((KV_DIM, t), BF16)],
        compiler_params=_cparams(1),
        name="qkv_rope_t",
    )(x, wqt, b[:, :D_MODEL].reshape(D_MODEL, 1), w_bf[:, D_MODEL:], b[:, D_MODEL:],
      wvt, b[:, D_MODEL + KV_DIM:].reshape(KV_DIM, 1), cos_t, sa_t, sb_t, cos_tt, sin_tt)


def _attend(q, kcat, vcat, bias, sink_ref, write):
    kgs = [kcat[:, g * HEAD_DIM:(g + 1) * HEAD_DIM] for g in range(N_KV_HEADS)]

    def scores(h):
        qh = q[:, h * HEAD_DIM:(h + 1) * HEAD_DIM]
        return lax.dot_general(qh, kgs[h // GROUP], (((1,), (1,)), ((), ())), preferred_element_type=F32) + bias

    s_next = scores(0)
    for h in range(N_HEADS):
        s = s_next
        if h + 1 < N_HEADS:
            s_next = scores(h + 1)
        vg = vcat[:, (h // GROUP) * HEAD_DIM:(h // GROUP + 1) * HEAD_DIM]
        sink = sink_ref[h] * LOG2_E
        m = jnp.maximum(jnp.max(s, axis=-1, keepdims=True), sink)
        p = jnp.exp2(s - m)
        denom = jnp.sum(p, axis=-1, keepdims=True) + jnp.exp2(sink - m)
        o = jnp.dot(p.astype(BF16), vg, preferred_element_type=F32)
        write(h, o / denom)


def _attn_prompt_kernel(sink_ref, qt_ref, kp_ref, ko_ref, km_ref, vtp_ref, vto_ref, vtm_ref, o_ref, acc_ref):
    i = pl.program_id(1)
    kcat = jnp.concatenate([kp_ref[...], ko_ref[...], km_ref[...]], axis=0).astype(BF16)
    vt = jnp.concatenate([vtp_ref[...], vto_ref[...], vtm_ref[...]], axis=1)
    nk = 2 * ATTN_BLOCK + META_LEN
    key = lax.broadcasted_iota(jnp.int32, (nk, ATTN_BLOCK), 0)
    qry = lax.broadcasted_iota(jnp.int32, (nk, ATTN_BLOCK), 1)
    first = jnp.where(i > 0, 0, 2 * ATTN_BLOCK)
    prev_ok = (key < ATTN_BLOCK) & (key >= qry + first)
    own_ok = (key >= ATTN_BLOCK) & (key - ATTN_BLOCK <= qry)
    mask = prev_ok | own_ok | (key >= 2 * ATTN_BLOCK)
    bias = jnp.where(mask, 0.0, NEG_INF)
    bias = jnp.concatenate([bias] * HEADS_PER_DOT, axis=1)
    lane = lax.broadcasted_iota(jnp.int32, (1, HEADS_PER_DOT * ATTN_BLOCK), 1)
    kgs = [kcat[:, g * HEAD_DIM:(g + 1) * HEAD_DIM] for g in range(N_KV_HEADS)]

    def scores(u):
        h0 = u * HEADS_PER_DOT
        rhs = jnp.concatenate([qt_ref[(h0 + j) * HEAD_DIM:(h0 + j + 1) * HEAD_DIM, :]
                               for j in range(HEADS_PER_DOT)], axis=1)
        return jnp.dot(kgs[h0 // GROUP], rhs, preferred_element_type=F32) + bias

    n_units = N_HEADS // HEADS_PER_DOT
    s_next = scores(0)
    for u in range(n_units):
        s = s_next
        if u + 1 < n_units:
            s_next = scores(u + 1)
        h0 = u * HEADS_PER_DOT
        g = h0 // GROUP
        sink = sink_ref[h0] * LOG2_E
        for j in range(1, HEADS_PER_DOT):
            sink = jnp.where(lane >= j * ATTN_BLOCK, sink_ref[h0 + j] * LOG2_E, sink)
        m = jnp.maximum(jnp.max(s, axis=0, keepdims=True), sink)
        p = jnp.exp2(s - m)
        denom = jnp.sum(p, axis=0, keepdims=True) + jnp.exp2(sink - m)
        o = jnp.dot(vt[g * HEAD_DIM:(g + 1) * HEAD_DIM, :], p.astype(BF16), preferred_element_type=F32)
        o = o * (1.0 / denom)
        for j in range(HEADS_PER_DOT):
            acc_ref[(h0 + j) * HEAD_DIM:(h0 + j + 1) * HEAD_DIM, :] = o[:, j * ATTN_BLOCK:(j + 1) * ATTN_BLOCK]
    o_ref[...] = acc_ref[...].T.astype(o_ref.dtype)


def _attn_prompt(qt, k, vt, k_meta, vt_meta, sinks, n_batch, seq):
    nblk = seq // ATTN_BLOCK
    own = lambda b, i: b * nblk + i
    prev = lambda b, i: b * nblk + jnp.maximum(i - 1, 0)
    kspec = lambda m: pl.BlockSpec((ATTN_BLOCK, KV_DIM), lambda b, i: (m(b, i), 0))
    vspec = lambda m: pl.BlockSpec((KV_DIM, ATTN_BLOCK), lambda b, i: (0, m(b, i)))
    return pl.pallas_call(
        _attn_prompt_kernel,
        grid=(n_batch, nblk),
        in_specs=[pl.BlockSpec(memory_space=pltpu.SMEM),
                  pl.BlockSpec((D_MODEL, ATTN_BLOCK), lambda b, i: (0, own(b, i))),
                  kspec(prev), kspec(own),
                  pl.BlockSpec((META_LEN, KV_DIM), lambda b, i: (b, 0)),
                  vspec(prev), vspec(own),
                  pl.BlockSpec((None, KV_DIM, META_LEN), lambda b, i: (b, 0, 0))],
        out_specs=pl.BlockSpec((ATTN_BLOCK, D_MODEL), lambda b, i: (own(b, i), 0)),
        out_shape=jax.ShapeDtypeStruct((n_batch * seq, D_MODEL), BF16),
        scratch_shapes=[pltpu.VMEM((D_MODEL, ATTN_BLOCK), F32)],
        compiler_params=_cparams(2),
        name="attn_prompt",
    )(sinks, qt, k, k, k_meta, vt, vt, vt_meta)


def _attn_meta_kernel(sink_ref, q_ref, k_ref, v_ref, o_ref):
    r = lax.broadcasted_iota(jnp.int32, (META_LEN, META_LEN), 0)
    c = lax.broadcasted_iota(jnp.int32, (META_LEN, META_LEN), 1)

    def write(h, o):
        o_ref[:, h * HEAD_DIM:(h + 1) * HEAD_DIM] = o.astype(o_ref.dtype)

    bias = jnp.where(c <= r, 0.0, NEG_INF)
    _attend(q_ref[...], k_ref[...].astype(BF16), v_ref[...].astype(BF16), bias, sink_ref, write)


def _attn_meta(q, k, v, sinks, n_batch):
    imap = lambda b: (b, 0)
    return pl.pallas_call(
        _attn_meta_kernel,
        grid=(n_batch,),
        in_specs=[pl.BlockSpec(memory_space=pltpu.SMEM),
                  pl.BlockSpec((META_LEN, D_MODEL), imap),
                  pl.BlockSpec((META_LEN, KV_DIM), imap),
                  pl.BlockSpec((META_LEN, KV_DIM), imap)],
        out_specs=pl.BlockSpec((META_LEN, D_MODEL), lambda b: (b, 0)),
        out_shape=jax.ShapeDtypeStruct((n_batch * META_LEN, D_MODEL), BF16),
        compiler_params=_cparams(1),
        name="attn_meta",
    )(sinks, q, k, v)


SAMPLE_GROUP = 8


def _attn_sample_kernel(sink_ref, bias_ref, q_ref, kn_ref, vn_ref, cmk_ref, cmv_ref, cwk_ref, cwv_ref,
                        o_ref, *, dec_seq, group):
    def keys(cm_ref, cw_ref, new):
        parts = []
        for j in range(group):
            parts += [cm_ref[j], cw_ref[j], new[j * dec_seq:(j + 1) * dec_seq]]
        return jnp.concatenate(parts, axis=0).astype(BF16)

    kcat = keys(cmk_ref, cwk_ref, kn_ref[...])
    vcat = keys(cmv_ref, cwv_ref, vn_ref[...])

    def write(h, o):
        o_ref[:, h * HEAD_DIM:(h + 1) * HEAD_DIM] = o.astype(o_ref.dtype)

    _attend(q_ref[...], kcat, vcat, bias_ref[...], sink_ref, write)


def _sample_bias(group, dec_seq):
    per = META_LEN + WINDOW + dec_seq
    r = np.arange(group * dec_seq)[:, None]
    c = np.arange(group * per)[None, :]
    tq, ck = r % dec_seq, c % per
    win_ok = (ck >= META_LEN) & (ck < META_LEN + WINDOW) & (ck - META_LEN >= tq)
    new_ok = (ck >= META_LEN + WINDOW) & (ck - (META_LEN + WINDOW) <= tq)
    vis = (r // dec_seq == c // per) & ((ck < META_LEN) | win_ok | new_ok)
    return np.where(vis, 0.0, NEG_INF).astype(np.float32)


def _attn_sample(q, k, v, cmk, cmv, cwk, cwv, sinks, row0, dec_batch, dec_seq):
    group = SAMPLE_GROUP
    nq = group * dec_seq
    blk0 = row0 // nq
    bias = jnp.asarray(_sample_bias(group, dec_seq))
    qmap = lambda n: (blk0 + n, 0)
    cmap = lambda n: (n, 0, 0)
    return pl.pallas_call(
        functools.partial(_attn_sample_kernel, dec_seq=dec_seq, group=group),
        grid=(dec_batch // group,),
        in_specs=[pl.BlockSpec(memory_space=pltpu.SMEM),
                  pl.BlockSpec(bias.shape, lambda n: (0, 0)),
                  pl.BlockSpec((nq, D_MODEL), qmap),
                  pl.BlockSpec((nq, KV_DIM), qmap),
                  pl.BlockSpec((nq, KV_DIM), qmap),
                  pl.BlockSpec((group, META_LEN, KV_DIM), cmap),
                  pl.BlockSpec((group, META_LEN, KV_DIM), cmap),
                  pl.BlockSpec((group, WINDOW, KV_DIM), cmap),
                  pl.BlockSpec((group, WINDOW, KV_DIM), cmap)],
        out_specs=pl.BlockSpec((nq, D_MODEL), lambda n: (n, 0)),
        out_shape=jax.ShapeDtypeStruct((dec_batch * dec_seq, D_MODEL), BF16),
        compiler_params=_cparams(1),
        name="attn_sample",
    )(sinks, bias, q, k, v, cmk, cmv, cwk, cwv)


def _proj_ln_route_kernel(a_ref, w_ref, b_ref, x_ref, g_ref, bb_ref, wr_ref, br_ref, tri_ref, cin_ref,
                          x1_ref, ids_ref, gates_ref, rank_ref, cnt_ref, carry_ref):
    @pl.when(pl.program_id(0) == 0)
    def _():
        carry_ref[...] = cin_ref[...].astype(F32)

    y = jnp.dot(a_ref[...], w_ref[...], preferred_element_type=F32) + b_ref[...]
    x1 = _layer_norm(DEEPNORM_ALPHA * x_ref[...] + y, g_ref[...], bb_ref[...])
    for j in range(N_SLAB):
        x1_ref[pl.ds(j, x1.shape[0], stride=N_SLAB), :] = x1[:, j * LANES:(j + 1) * LANES]

    logits = lax.dot_general(wr_ref[...], x1.astype(BF16), (((1,), (1,)), ((), ())),
                             preferred_element_type=F32) + br_ref[...]
    tm = logits.shape[1]
    eidx = lax.broadcasted_iota(jnp.int32, (N_EXPERTS, tm), 0).astype(F32)
    cur = logits
    vals, idxs, sels = [], [], []
    for _ in range(TOP_K):
        m = jnp.max(cur, axis=0, keepdims=True)
        idx = jnp.min(jnp.where(cur == m, eidx, float(N_EXPERTS)), axis=0, keepdims=True)
        sel = eidx == idx
        vals.append(m)
        idxs.append(idx)
        sels.append(sel)
        cur = jnp.where(sel, -jnp.inf, cur)
    exps = [jnp.exp(v - vals[0]) for v in vals]
    tot = exps[0] + exps[1] + exps[2] + exps[3]
    gates_ref[...] = jnp.concatenate([e / tot for e in exps], axis=0)
    ids_ref[...] = jnp.concatenate(idxs, axis=0).astype(jnp.int32)

    chosen = jnp.where(sels[0] | sels[1] | sels[2] | sels[3], 1.0, 0.0)
    before = jnp.dot(chosen.astype(BF16), tri_ref[...], preferred_element_type=F32)
    before = before + carry_ref[:, 0:1]
    ranks = [jnp.sum(jnp.where(s, before, 0.0), axis=0, keepdims=True) for s in sels]
    rank_ref[...] = jnp.concatenate(ranks, axis=0).astype(jnp.int32)
    carry_ref[...] = carry_ref[...] + jnp.sum(chosen, axis=1, keepdims=True)
    cnt_ref[...] = carry_ref[...].astype(jnp.int32)


def _proj_ln_route(a_bf, w_bf, b, x, g, bb, wr_t_bf, br_col, counts_in, tm):
    t = x.shape[0]
    row = lambda i: (i, 0)
    col = lambda i: (0, i)
    fixed = lambda i: (0, 0)
    tri = (jnp.arange(tm)[:, None] < jnp.arange(tm)[None, :]).astype(BF16)
    return pl.pallas_call(
        _proj_ln_route_kernel,
        grid=(t // tm,),
        in_specs=[pl.BlockSpec((tm, D_MODEL), row),
                  pl.BlockSpec((D_MODEL, D_MODEL), fixed),
                  pl.BlockSpec((1, D_MODEL), fixed),
                  pl.BlockSpec((tm, D_MODEL), row),
                  pl.BlockSpec((1, D_MODEL), fixed),
                  pl.BlockSpec((1, D_MODEL), fixed),
                  pl.BlockSpec((N_EXPERTS, D_MODEL), fixed),
                  pl.BlockSpec((N_EXPERTS, 1), fixed),
                  pl.BlockSpec((tm, tm), fixed),
                  pl.BlockSpec((N_EXPERTS, LANES), fixed)],
        out_specs=[pl.BlockSpec((tm * N_SLAB, LANES), row),
                   pl.BlockSpec((TOP_K, tm), col),
                   pl.BlockSpec((TOP_K, tm), col),
                   pl.BlockSpec((TOP_K, tm), col),
                   pl.BlockSpec((N_EXPERTS, LANES), fixed)],
        out_shape=[jax.ShapeDtypeStruct((t * N_SLAB, LANES), F32),
                   jax.ShapeDtypeStruct((TOP_K, t), jnp.int32),
                   jax.ShapeDtypeStruct((TOP_K, t), F32),
                   jax.ShapeDtypeStruct((TOP_K, t), jnp.int32),
                   jax.ShapeDtypeStruct((N_EXPERTS, LANES), jnp.int32)],
        scratch_shapes=[pltpu.VMEM((N_EXPERTS, LANES), F32)],
        compiler_params=_cparams(1),
        name="proj_ln_route",
    )(a_bf, w_bf, b, x, g, bb, wr_t_bf, br_col, tri, counts_in)


def _idx_copy(dest_hbm, dsm, sem, step, stride, n_slots=2):
    slot = step % n_slots
    return pltpu.make_async_copy(dest_hbm.at[pl.ds(pl.multiple_of(step * stride, stride), stride)],
                                 dsm.at[pl.ds(pl.multiple_of(slot * stride, stride), stride)],
                                 sem)


def _dispatch_kernel(pstart_ref, padded_ref, nact_ref, dest_hbm, *rest, tiles, steps, stride, n_tiles):
    x_refs = rest[:len(tiles)]
    xs_hbm, dsm, zbuf, sems = rest[len(tiles):]
    i = pl.program_id(0)
    nsteps = pl.num_programs(0)

    def zero_tile(row0):
        n = GMM_TILE * N_SLAB
        return pltpu.make_async_copy(zbuf, xs_hbm.at[pl.ds(pl.multiple_of(row0 * N_SLAB, n), n)], sems.at[2])

    def token(ref, r):
        return ref.at[pl.ds(pl.multiple_of(r * N_SLAB, N_SLAB), N_SLAB)]

    @pl.when(i == 0)
    def _():
        zbuf[...] = jnp.zeros_like(zbuf)

        def each_expert(fn):
            def body(e, _):
                @pl.when(padded_ref[e] > 0)
                def _():
                    fn(zero_tile(pstart_ref[e] + padded_ref[e] - GMM_TILE))
                return 0
            lax.fori_loop(0, N_EXPERTS, body, 0)

        def each_tail(fn):
            def body(t, _):
                fn(zero_tile(t * GMM_TILE))
                return 0
            lax.fori_loop(nact_ref[0], n_tiles, body, 0)

        each_expert(lambda cp: cp.start())
        each_tail(lambda cp: cp.start())
        each_expert(lambda cp: cp.wait())
        each_tail(lambda cp: cp.wait())
        _idx_copy(dest_hbm, dsm, sems.at[0], i, stride).start()

    _idx_copy(dest_hbm, dsm, sems.at[0], i, stride).wait()

    @pl.when(i + 1 < nsteps)
    def _():
        _idx_copy(dest_hbm, dsm, sems.at[0], i + 1, stride).start()

    base = (i % 2) * stride
    step0 = 0
    for x_ref, tm, n in zip(x_refs, tiles, steps):
        @pl.when((i >= step0) & (i < step0 + n))
        def _(x_ref=x_ref, tm=tm):
            def row_body(r, _):
                for k in range(TOP_K):
                    d = dsm[base + k * tm + r]
                    pltpu.make_async_copy(token(x_ref, r), token(xs_hbm, d), sems.at[1]).start(priority=k % 2)
                return 0

            lax.fori_loop(0, tm, row_body, 0, unroll=4)
            for k in range(TOP_K):
                pltpu.make_async_copy(x_ref, xs_hbm.at[pl.ds(0, tm * N_SLAB)], sems.at[1]).wait()
        step0 += n


def _seg_map(step0, n):
    return lambda i, *_: (jnp.clip(i - step0, 0, n - 1), 0)


def _dispatch(xs_list, tiles, dest_steps, pstart, padded, nact, stride, n_tiles):
    steps = [x.shape[0] // (tm * N_SLAB) for x, tm in zip(xs_list, tiles)]
    in_specs = [pl.BlockSpec(memory_space=pl.ANY)]
    step0 = 0
    for tm, n in zip(tiles, steps):
        in_specs.append(pl.BlockSpec((tm * N_SLAB, LANES), _seg_map(step0, n)))
        step0 += n
    grid_spec = pltpu.PrefetchScalarGridSpec(
        num_scalar_prefetch=3,
        grid=(sum(steps),),
        in_specs=in_specs,
        out_specs=pl.BlockSpec(memory_space=pl.ANY),
        scratch_shapes=[pltpu.SMEM((2 * stride,), jnp.int32),
                        pltpu.VMEM((GMM_TILE * N_SLAB, LANES), F32),
                        pltpu.SemaphoreType.DMA((3,))],
    )
    return pl.pallas_call(
        functools.partial(_dispatch_kernel, tiles=tuple(tiles), steps=tuple(steps), stride=stride,
                          n_tiles=n_tiles),
        grid_spec=grid_spec,
        out_shape=jax.ShapeDtypeStruct((n_tiles * GMM_TILE * N_SLAB, LANES), F32),
        compiler_params=_cparams(1),
        name="moe_dispatch",
    )(pstart, padded, nact, dest_steps, *xs_list)


def _gmm_kernel(te_ref, tsrc_ref, tfirst_ref, nact_ref, x_ref, wgu_ref, bgu_ref, wdn_ref, bdn_ref, y_ref,
                wgu_bf, wdn_bf):
    i = pl.program_id(0)

    @pl.when(tfirst_ref[i] == 1)
    def _():
        wgu_bf[...] = wgu_ref[...].astype(BF16)
        wdn_bf[...] = wdn_ref[...].astype(BF16)

    @pl.when(i < nact_ref[0])
    def _():
        x = jnp.concatenate([x_ref[pl.ds(j, GMM_TILE, stride=N_SLAB), :] for j in range(N_SLAB)], axis=1)
        gu = jnp.dot(x.astype(BF16), wgu_bf[...], preferred_element_type=F32) + bgu_ref[...]
        gate = jnp.minimum(gu[:, :MOE_FF], SWIGLU_LIMIT)
        up = jnp.clip(gu[:, MOE_FF:], -SWIGLU_LIMIT, SWIGLU_LIMIT)
        glu = gate * jax.nn.sigmoid(SWIGLU_ALPHA * gate)
        h = ((up + 1.0) * glu).astype(BF16)
        y = jnp.dot(h, wdn_bf[...], preferred_element_type=F32) + bdn_ref[...]
        for j in range(N_SLAB):
            y_ref[pl.ds(j, GMM_TILE, stride=N_SLAB), :] = y[:, j * LANES:(j + 1) * LANES]

    @pl.when(i >= nact_ref[0])
    def _():
        y_ref[...] = jnp.zeros_like(y_ref)


def _gmm(xs, te, tsrc, tfirst, nact, wgu, bgu, wdn, bdn, n_tiles):
    emap = lambda i, te, *_: (te[i], 0, 0)
    grid_spec = pltpu.PrefetchScalarGridSpec(
        num_scalar_prefetch=4,
        grid=(n_tiles,),
        in_specs=[pl.BlockSpec((GMM_TILE * N_SLAB, LANES), lambda i, te, ts, *_: (ts[i], 0)),
                  pl.BlockSpec((None, D_MODEL, 2 * MOE_FF), emap),
                  pl.BlockSpec((None, 1, 2 * MOE_FF), emap),
                  pl.BlockSpec((None, MOE_FF, D_MODEL), emap),
                  pl.BlockSpec((None, 1, D_MODEL), emap)],
        out_specs=pl.BlockSpec((GMM_TILE * N_SLAB, LANES), lambda i, *_: (i, 0)),
        scratch_shapes=[pltpu.VMEM((D_MODEL, 2 * MOE_FF), BF16),
                        pltpu.VMEM((MOE_FF, D_MODEL), BF16)],
    )
    return pl.pallas_call(
        _gmm_kernel,
        grid_spec=grid_spec,
        out_shape=jax.ShapeDtypeStruct((n_tiles * GMM_TILE * N_SLAB, LANES), F32),
        compiler_params=_cparams(1),
        name="moe_gmm",
    )(te, tsrc, tfirst, nact, xs, wgu, bgu, wdn, bdn)


def _combine_kernel(dest_hbm, gate_hbm, ys_hbm, x_ref, g_ref, bb_ref, o_ref, dsm, gsm, buf, sems,
                    *, tm, stride):
    i = pl.program_id(0)
    nsteps = pl.num_programs(0)

    def idx_copies(step):
        slot = step % IDX_SLOTS
        return (_idx_copy(dest_hbm, dsm, sems.at[slot], step, stride, IDX_SLOTS),
                _idx_copy(gate_hbm, gsm, sems.at[IDX_SLOTS + slot], step, stride, IDX_SLOTS))

    def token(ref, r):
        return ref.at[pl.ds(pl.multiple_of(r * N_SLAB, N_SLAB), N_SLAB)]

    def issue_rows(step, parity):
        base = (step % IDX_SLOTS) * stride

        def body(r, _):
            for k in range(TOP_K):
                d = dsm[base + k * tm + r]
                pltpu.make_async_copy(token(ys_hbm, d), token(buf.at[parity * TOP_K + k], r),
                                      sems.at[2 * IDX_SLOTS + parity]).start(priority=k % 2)
            return 0

        lax.fori_loop(0, tm, body, 0, unroll=4)

    @pl.when(i == 0)
    def _():
        for cp in idx_copies(0):
            cp.start()

        @pl.when(nsteps > 1)
        def _():
            for cp in idx_copies(1):
                cp.start()

        for cp in idx_copies(0):
            cp.wait()

        issue_rows(0, 0)

    @pl.when(i + 1 < nsteps)
    def _():
        for cp in idx_copies(i + 1):
            cp.wait()

    @pl.when(i + 2 < nsteps)
    def _():
        for cp in idx_copies(i + 2):
            cp.start()

    for parity in range(2):
        @pl.when((i + 1 < nsteps) & ((i + 1) % 2 == parity))
        def _(parity=parity):
            issue_rows(i + 1, parity)

    gbase = (i % IDX_SLOTS) * stride

    def compute(parity):
        half = parity * TOP_K
        for k in range(TOP_K):
            pltpu.make_async_copy(ys_hbm.at[pl.ds(0, tm * N_SLAB)], buf.at[half + k],
                                  sems.at[2 * IDX_SLOTS + parity]).wait()

        def token_sum(r, _):
            rows = pl.ds(pl.multiple_of(r * N_SLAB, N_SLAB), N_SLAB)
            h = DEEPNORM_ALPHA * x_ref[rows, :]
            for k in range(TOP_K):
                h = h + gsm[gbase + k * tm + r] * buf[half + k, rows, :]
            buf[half, rows, :] = h
            return 0

        lax.fori_loop(0, tm, token_sum, 0, unroll=8)

        def block(b, _):
            r0 = pl.multiple_of(b * COMBINE_ROWS, COMBINE_ROWS)
            hs = [buf[half, pl.ds(r0 * N_SLAB + j, COMBINE_ROWS, stride=N_SLAB), :] for j in range(N_SLAB)]
            tot = hs[0]
            for j in range(1, N_SLAB):
                tot = tot + hs[j]
            mu = jnp.sum(tot, axis=1, keepdims=True) * (1.0 / D_MODEL)
            cen = [h - mu for h in hs]
            sq = cen[0] * cen[0]
            for j in range(1, N_SLAB):
                sq = sq + cen[j] * cen[j]
            inv = lax.rsqrt(jnp.sum(sq, axis=1, keepdims=True) * (1.0 / D_MODEL) + LN_EPS)
            for j in range(N_SLAB):
                o_ref[pl.ds(r0, COMBINE_ROWS), j * LANES:(j + 1) * LANES] = (
                    cen[j] * inv * g_ref[j:j + 1, :] + bb_ref[j:j + 1, :])
            return 0

        lax.fori_loop(0, tm // COMBINE_ROWS, block, 0, unroll=2)

    for parity in range(2):
        @pl.when(i % 2 == parity)
        def _(parity=parity):
            compute(parity)


def _combine(ys, dest_steps, gate_steps, x_tiles, g3, bb3, tm, stride):
    t = x_tiles.shape[0] // N_SLAB
    row = lambda i: (i, 0)
    fixed = lambda i: (0, 0)
    return pl.pallas_call(
        functools.partial(_combine_kernel, tm=tm, stride=stride),
        grid=(t // tm,),
        in_specs=[pl.BlockSpec(memory_space=pl.ANY),
                  pl.BlockSpec(memory_space=pl.ANY),
                  pl.BlockSpec(memory_space=pl.ANY),
                  pl.BlockSpec((tm * N_SLAB, LANES), row),
                  pl.BlockSpec((N_SLAB, LANES), fixed),
                  pl.BlockSpec((N_SLAB, LANES), fixed)],
        out_specs=pl.BlockSpec((tm, D_MODEL), row),
        out_shape=jax.ShapeDtypeStruct((t, D_MODEL), F32),
        scratch_shapes=[pltpu.SMEM((IDX_SLOTS * stride,), jnp.int32),
                        pltpu.SMEM((IDX_SLOTS * stride,), F32),
                        pltpu.VMEM((2 * TOP_K, tm * N_SLAB, LANES), F32),
                        pltpu.SemaphoreType.DMA((2 * IDX_SLOTS + 2,))],
        compiler_params=_cparams(1),
        name="moe_combine",
    )(dest_steps, gate_steps, ys, x_tiles, g3, bb3)


def _moe(segs, counts, expert0, wgu, bgu, wdn, bdn, g, bb):
    t = sum(seg[1].shape[1] for seg in segs)
    n_tiles = -(-(t * TOP_K) // GMM_TILE) + N_EXPERTS
    counts = counts[:, 0]
    padded = ((counts + GMM_TILE - 1) // GMM_TILE) * GMM_TILE
    pend = jnp.cumsum(padded)
    pstart = (pend - padded).astype(jnp.int32)
    padded = padded.astype(jnp.int32)
    nact = (pend[-1] // GMM_TILE).astype(jnp.int32).reshape(1)
    tile = jnp.arange(n_tiles, dtype=jnp.int32)
    tsrc = jnp.minimum(tile, jnp.maximum(nact[0] - 1, 0))
    te = jnp.sum((pend[None, :] <= (tsrc * GMM_TILE)[:, None]).astype(jnp.int32), axis=1)
    te = jnp.minimum(te, N_EXPERTS - 1).astype(jnp.int32)
    tfirst = jnp.concatenate([jnp.ones((1,), jnp.int32), (te[1:] != te[:-1]).astype(jnp.int32)])
    eids = jnp.arange(N_EXPERTS, dtype=jnp.int32)[:, None, None]

    stride = -(-(TOP_K * max(seg[4] for seg in segs)) // 1024) * 1024
    def per_step(a, tm):
        n = a.shape[1]
        steps = a.reshape(TOP_K, n // tm, tm).transpose(1, 0, 2).reshape(n // tm, TOP_K * tm)
        return jnp.pad(steps, ((0, 0), (0, stride - TOP_K * tm))).reshape(-1)

    plans = []
    for x1, ids, gates, rank, tm in segs:
        dest = jnp.sum(jnp.where(ids[None] == eids, pstart[:, None, None], 0), axis=0) + rank
        plans.append((per_step(dest, tm), per_step(gates, tm)))
    xs = _dispatch([seg[0] for seg in segs], [seg[4] for seg in segs],
                   jnp.concatenate([p[0] for p in plans]), pstart, padded, nact, stride, n_tiles)
    ys = _gmm(xs, te + expert0, tsrc, tfirst, nact, wgu, bgu, wdn, bdn, n_tiles)
    return [_combine(ys, dest_steps, gate_steps, seg[0], g, bb, seg[4], stride)
            for seg, (dest_steps, gate_steps) in zip(segs, plans)]


def _pw1_glu_kernel(x_ref, w_ref, b_ref, u_ref):
    a = jnp.dot(x_ref[...].astype(BF16), w_ref[...], preferred_element_type=F32) + b_ref[...]
    u_ref[...] = a[:, :D_MODEL] * jax.nn.sigmoid(a[:, D_MODEL:])


def _pw1_glu(x, w_bf, b, tm):
    t = x.shape[0]
    row = lambda i: (i, 0)
    fixed = lambda i: (0, 0)
    return pl.pallas_call(
        _pw1_glu_kernel,
        grid=(t // tm,),
        in_specs=[pl.BlockSpec((tm, D_MODEL), row),
                  pl.BlockSpec((D_MODEL, 2 * D_MODEL), fixed),
                  pl.BlockSpec((1, 2 * D_MODEL), fixed)],
        out_specs=pl.BlockSpec((tm, D_MODEL), row),
        out_shape=jax.ShapeDtypeStruct((t, D_MODEL), F32),
        compiler_params=_cparams(1),
        name="pw1_glu",
    )(x, w_bf, b)


def _conv_rows(win_ref, zs_ref, base, n_sets, w_ref, b_ref):
    shift = CONV_HALO - CONV_CTX
    offs = tuple(range(n_sets))

    def out_rows(o):
        return pl.ds(base + o, SUBLANES, stride=n_sets)

    def slab(c, _):
        def window(t):
            return win_ref[c, pl.ds(base + (shift + t), SUBLANES, stride=n_sets), :]

        bias = b_ref[pl.ds(c, 1), :]
        acc = [bias] * n_sets
        wins = [window(t) for t in range(n_sets - 1)]
        for j in range(CONV_WIDTH):
            w = w_ref[c, j:j + 1, :]
            wins.append(window(j + n_sets - 1))
            acc = [acc[o] + w * wins[o] for o in offs]
            wins.pop(0)
        for o in offs:
            zs_ref[c, out_rows(o), :] = acc[o]
        return 0

    lax.fori_loop(0, N_SLAB, slab, 0)


def _ln_silu_rows(zs_ref, rows, g_ref, bb_ref):
    acc = [zs_ref[c, rows, :] for c in range(N_SLAB)]
    tot = acc[0]
    for c in range(1, N_SLAB):
        tot = tot + acc[c]
    mu = jnp.sum(tot, axis=1, keepdims=True) * (1.0 / D_MODEL)
    cen = [a - mu for a in acc]
    sq = cen[0] * cen[0]
    for c in range(1, N_SLAB):
        sq = sq + cen[c] * cen[c]
    inv = lax.rsqrt(jnp.sum(sq, axis=1, keepdims=True) * (1.0 / D_MODEL) + LN_EPS)
    out = []
    for c in range(N_SLAB):
        z = cen[c] * inv * g_ref[c:c + 1, :] + bb_ref[c:c + 1, :]
        out.append(z * jax.nn.sigmoid(z))
    return out


def _conv_prompt_kernel(um_ref, up_ref, uc_ref, w_ref, b_ref, g_ref, bb_ref, z_ref, win_ref, zs_ref, *, tr):
    i = pl.program_id(1)
    gap = CONV_HALO - META_LEN

    @pl.when(i == 0)
    def _():
        for c in range(N_SLAB):
            win_ref[c, 0:gap, :] = jnp.zeros((gap, LANES), F32)
            win_ref[c, gap:CONV_HALO, :] = um_ref[:, c * LANES:(c + 1) * LANES]

    @pl.when(i > 0)
    def _():
        for c in range(N_SLAB):
            win_ref[c, 0:CONV_HALO, :] = up_ref[:, c * LANES:(c + 1) * LANES]

    for c in range(N_SLAB):
        win_ref[c, CONV_HALO:, :] = uc_ref[:, c * LANES:(c + 1) * LANES]

    def block(bi, _):
        base = pl.multiple_of(bi * CONV_BLOCK, CONV_BLOCK)
        _conv_rows(win_ref, zs_ref, base, CONV_BLOCK // SUBLANES, w_ref, b_ref)
        return 0

    lax.fori_loop(0, tr // CONV_BLOCK, block, 0)

    def norm(bi, _):
        rows = pl.ds(pl.multiple_of(bi * LN_ROWS, LN_ROWS), LN_ROWS)
        for c, z in enumerate(_ln_silu_rows(zs_ref, rows, g_ref, bb_ref)):
            z_ref[rows, c * LANES:(c + 1) * LANES] = z.astype(z_ref.dtype)
        return 0

    lax.fori_loop(0, tr // LN_ROWS, norm, 0, unroll=4)


def _conv_prompt(u, u_meta, w3, b3, g3, bb3, n_batch, seq, tr):
    nt = seq // tr
    cur = lambda b, i: (b * nt + i, 0)
    prev = lambda b, i: (jnp.maximum((b * seq + i * tr) // CONV_HALO - 1, 0), 0)
    fixed2 = lambda b, i: (0, 0)
    fixed3 = lambda b, i: (0, 0, 0)
    return pl.pallas_call(
        functools.partial(_conv_prompt_kernel, tr=tr),
        grid=(n_batch, nt),
        in_specs=[pl.BlockSpec((META_LEN, D_MODEL), lambda b, i: (b, 0)),
                  pl.BlockSpec((CONV_HALO, D_MODEL), prev),
                  pl.BlockSpec((tr, D_MODEL), cur),
                  pl.BlockSpec((N_SLAB, CONV_HALO, LANES), fixed3),
                  pl.BlockSpec((N_SLAB, LANES), fixed2),
                  pl.BlockSpec((N_SLAB, LANES), fixed2),
                  pl.BlockSpec((N_SLAB, LANES), fixed2)],
        out_specs=pl.BlockSpec((tr, D_MODEL), cur),
        out_shape=jax.ShapeDtypeStruct((n_batch * seq, D_MODEL), BF16),
        scratch_shapes=[pltpu.VMEM((N_SLAB, CONV_HALO + tr, LANES), F32),
                        pltpu.VMEM((N_SLAB, tr, LANES), F32)],
        compiler_params=_cparams(2),
        name="conv_prompt",
    )(u_meta, u, u, w3, b3, g3, bb3)


SHORT_ROWS = 16


def _conv_short_kernel(ctx_ref, u_ref, w_ref, b_ref, g_ref, bb_ref, z_ref, win_ref, zs_ref, *, n_seq, t_len):
    for n in range(n_seq):
        for c in range(N_SLAB):
            sl = slice(c * LANES, (c + 1) * LANES)
            win_ref[c, 0:CONV_HALO, :] = ctx_ref[n, :, sl]
            win_ref[c, CONV_HALO:CONV_HALO + t_len, :] = u_ref[n * t_len:(n + 1) * t_len, sl]
            if t_len < SHORT_ROWS:
                win_ref[c, CONV_HALO + t_len:, :] = jnp.zeros((SHORT_ROWS - t_len, LANES), F32)
        _conv_rows(win_ref, zs_ref, 0, SHORT_ROWS // SUBLANES, w_ref, b_ref)
        for c, z in enumerate(_ln_silu_rows(zs_ref, slice(0, t_len), g_ref, bb_ref)):
            z_ref[n * t_len:(n + 1) * t_len, c * LANES:(c + 1) * LANES] = z


def _conv_short(ctx_pad, u, row0, w3, b3, g3, bb3, t_len, n_seq):
    n_total = ctx_pad.shape[0]
    rows = n_seq * t_len
    blk0 = row0 // rows
    fixed2 = lambda n: (0, 0)
    fixed3 = lambda n: (0, 0, 0)
    return pl.pallas_call(
        functools.partial(_conv_short_kernel, n_seq=n_seq, t_len=t_len),
        grid=(n_total // n_seq,),
        in_specs=[pl.BlockSpec((n_seq, CONV_HALO, D_MODEL), lambda n: (n, 0, 0)),
                  pl.BlockSpec((rows, D_MODEL), lambda n: (blk0 + n, 0)),
                  pl.BlockSpec((N_SLAB, CONV_HALO, LANES), fixed3),
                  pl.BlockSpec((N_SLAB, LANES), fixed2),
                  pl.BlockSpec((N_SLAB, LANES), fixed2),
                  pl.BlockSpec((N_SLAB, LANES), fixed2)],
        out_specs=pl.BlockSpec((rows, D_MODEL), lambda n: (n, 0)),
        out_shape=jax.ShapeDtypeStruct((n_total * t_len, D_MODEL), F32),
        scratch_shapes=[pltpu.VMEM((N_SLAB, CONV_HALO + SHORT_ROWS, LANES), F32),
                        pltpu.VMEM((N_SLAB, SHORT_ROWS, LANES), F32)],
        compiler_params=_cparams(1),
        name="conv_short",
    )(ctx_pad, u, w3, b3, g3, bb3)


def _rope_angles(pos):
    inv = 1.0 / (ROPE_THETA ** (jnp.arange(0, ROT_DIM, 2, dtype=F32) / ROT_DIM))
    ang = pos.astype(F32)[:, None] * inv[None, :]
    return jnp.cos(ang), jnp.sin(ang)


def _rope_tables_t(pos):
    cos, sin = _rope_angles(pos)
    return cos.T, sin.T


def _rope_tables(pos):
    half = ROT_DIM // 2
    cos, sin = _rope_angles(pos)
    n = pos.shape[0]
    ones = jnp.ones((n, HEAD_DIM - ROT_DIM), F32)
    zeros = jnp.zeros((n, HEAD_DIM - ROT_DIM), F32)
    zh = jnp.zeros((n, half), F32)
    c = jnp.concatenate([cos, cos, ones], axis=1)
    a = jnp.concatenate([-sin, zh, zeros], axis=1)
    s = jnp.concatenate([zh, sin, zeros], axis=1)
    rep = LANES // HEAD_DIM
    return jnp.tile(c, (1, rep)), jnp.tile(a, (1, rep)), jnp.tile(s, (1, rep))


def kernel(x_prompt, x_sample, cache_attn_meta_k, cache_attn_meta_v, cache_attn_win_k, cache_attn_win_v, state_conv, meta_tokens, attn_w_qkv, attn_b_qkv, attn_sinks, attn_w_o, attn_b_o, conv_w_pw1, conv_b_pw1, conv_w_dw, conv_b_dw, conv_ln_g, conv_ln_b, conv_w_pw2, conv_b_pw2, ln_mix_g, ln_mix_b, ln_ffn_g, ln_ffn_b, moe_w_router, moe_b_router, moe_w_gate_up, moe_b_gate_up, moe_w_down, moe_b_down):
    n_batch, seq, _ = x_prompt.shape
    dec_batch, dec_seq, _ = x_sample.shape
    n_real = n_batch * seq
    n_meta = n_batch * META_LEN
    n_samp = dec_batch * dec_seq
    n_small = n_meta + n_samp
    tb = _pick_tile(seq, 512)
    td = _pick_tile(seq, 1024)
    ts = _pick_tile(n_small, 512)
    row2 = lambda v: v.reshape(1, -1)

    meta_rows = jnp.broadcast_to(meta_tokens[None], (n_batch, META_LEN, D_MODEL)).reshape(n_meta, D_MODEL)
    xb = x_prompt.reshape(n_real, D_MODEL)
    xs = jnp.concatenate([meta_rows.astype(F32), x_sample.reshape(n_samp, D_MODEL)], axis=0)

    def moe_layer(i, a_big, a_small, w, b, xb, xs):
        w_bf = w.astype(BF16)
        wr = moe_w_router[i].T.astype(BF16)
        br = moe_b_router[i].reshape(N_EXPERTS, 1)
        lng, lnb = row2(ln_mix_g[i]), row2(ln_mix_b[i])
        zero_counts = jnp.zeros((N_EXPERTS, LANES), jnp.int32)
        x1b, idb, gab, rab, cnt = _proj_ln_route(a_big, w_bf, row2(b), xb, lng, lnb, wr, br, zero_counts, td)
        x1s, ids_, gas, ras, cnt = _proj_ln_route(a_small, w_bf, row2(b), xs, lng, lnb, wr, br, cnt, ts)
        n_all = moe_w_gate_up.shape[0] * N_EXPERTS
        return _moe([(x1b, idb, gab, rab, tb), (x1s, ids_, gas, ras, ts)], cnt, i * N_EXPERTS,
                    moe_w_gate_up.reshape(n_all, D_MODEL, 2 * MOE_FF),
                    moe_b_gate_up.reshape(n_all, 1, 2 * MOE_FF),
                    moe_w_down.reshape(n_all, MOE_FF, D_MODEL),
                    moe_b_down.reshape(n_all, 1, D_MODEL),
                    ln_ffn_g[i].reshape(N_SLAB, LANES), ln_ffn_b[i].reshape(N_SLAB, LANES))

    w_qkv = attn_w_qkv[0].astype(BF16)
    b_qkv = row2(attn_b_qkv[0])
    pos_small = jnp.concatenate([jnp.tile(jnp.arange(META_LEN), n_batch),
                                 jnp.tile(PAST_LEN + jnp.arange(dec_seq), dec_batch)])
    pos_big = META_LEN + jnp.arange(seq)
    qtb, kb, vb, vtb = _qkv_rope_t(xb, w_qkv, b_qkv, *_rope_tables(pos_big), *_rope_tables_t(pos_big), td)
    qs, ks, vs = _qkv_rope(xs, w_qkv, b_qkv, *_rope_tables(pos_small), ts)
    sinks = attn_sinks[0]
    cmk = cache_attn_meta_k[0].reshape(dec_batch, META_LEN, KV_DIM)
    cmv = cache_attn_meta_v[0].reshape(dec_batch, META_LEN, KV_DIM)
    cwk = cache_attn_win_k[0].reshape(dec_batch, WINDOW, KV_DIM)
    cwv = cache_attn_win_v[0].reshape(dec_batch, WINDOW, KV_DIM)
    vt_meta = vs[:n_meta].reshape(n_batch, META_LEN, KV_DIM).transpose(0, 2, 1).astype(BF16)
    o_big = _attn_prompt(qtb, kb, vtb, ks, vt_meta, sinks, n_batch, seq)
    o_small = jnp.concatenate([
        _attn_meta(qs, ks, vs, sinks, n_batch),
        _attn_sample(qs, ks, vs, cmk, cmv, cwk, cwv, sinks, n_meta, dec_batch, dec_seq)], axis=0)
    xb, xs = moe_layer(0, o_big, o_small, attn_w_o[0], attn_b_o[0], xb, xs)

    kv4 = lambda a, n, t_len: a.reshape(n, t_len, N_KV_HEADS, HEAD_DIM)
    p_meta_k = kv4(ks[:n_meta], n_batch, META_LEN)[None]
    p_meta_v = kv4(vs[:n_meta], n_batch, META_LEN)[None]
    p_win_k = kv4(kb, n_batch, seq)[:, seq - WINDOW:][None]
    p_win_v = kv4(vb, n_batch, seq)[:, seq - WINDOW:][None]
    k_new = kv4(ks[n_meta:], dec_batch, dec_seq)
    v_new = kv4(vs[n_meta:], dec_batch, dec_seq)
    s_win_k = jnp.concatenate([cache_attn_win_k[0], k_new], axis=1)[:, -WINDOW:][None]
    s_win_v = jnp.concatenate([cache_attn_win_v[0], v_new], axis=1)[:, -WINDOW:][None]

    w_pw1 = conv_w_pw1[0].astype(BF16)
    ub = _pw1_glu(xb, w_pw1, row2(conv_b_pw1[0]), td)
    us = _pw1_glu(xs, w_pw1, row2(conv_b_pw1[0]), ts)
    slab = lambda v: v.reshape(N_SLAB, LANES)
    w_dw = jnp.pad(conv_w_dw[0], ((0, CONV_HALO - CONV_WIDTH), (0, 0)))
    w3 = w_dw.reshape(CONV_HALO, N_SLAB, LANES).transpose(1, 0, 2)
    conv_args = (w3, slab(conv_b_dw[0]), slab(conv_ln_g[0]), slab(conv_ln_b[0]))
    z_big = _conv_prompt(ub, us, *conv_args, n_batch, seq, tb)
    z_meta = _conv_short(jnp.zeros((n_batch, CONV_HALO, D_MODEL), F32), us, 0, *conv_args, META_LEN, 1)
    ctx_pad = jnp.pad(state_conv[0], ((0, 0), (CONV_HALO - CONV_CTX, 0), (0, 0)))
    z_samp = _conv_short(ctx_pad, us, n_meta, *conv_args, dec_seq, 2)
    z_small = jnp.concatenate([z_meta, z_samp], axis=0).astype(BF16)
    xb, xs = moe_layer(1, z_big, z_small, conv_w_pw2[0], conv_b_pw2[0], xb, xs)

    p_conv = ub.reshape(n_batch, seq, D_MODEL)[:, seq - CONV_CTX:][None]
    u_samp = us[n_meta:].reshape(dec_batch, dec_seq, D_MODEL)
    s_conv = jnp.concatenate([state_conv[0], u_samp], axis=1)[:, -CONV_CTX:][None]

    y_prompt = xb.reshape(n_batch, seq, D_MODEL)
    y_sample = xs[n_meta:].reshape(dec_batch, dec_seq, D_MODEL)
    return (y_prompt, y_sample, p_meta_k, p_meta_v, p_win_k, p_win_v, p_conv,
            s_win_k, s_win_v, s_conv)
```

```python
import functools

import numpy as np
import jax
import jax.numpy as jnp
from jax import lax
from jax.experimental import pallas as pl
from jax.experimental.pallas import tpu as pltpu

F32 = jnp.float32
BF16 = jnp.bfloat16

D_MODEL = 1024
HEAD_DIM = 64
N_HEADS = 16
N_KV_HEADS = 4
GROUP = N_HEADS // N_KV_HEADS
KV_DIM = N_KV_HEADS * HEAD_DIM
QKV_DIM = D_MODEL + 2 * KV_DIM
ROT_DIM = 16
ROPE_THETA = 500000.0
WINDOW = 128
ATTN_BLOCK = 128
ATTN_SCALE = HEAD_DIM ** -0.5
LOG2_E = 1.4426950408889634
Q_SCALE = ATTN_SCALE * LOG2_E
META_LEN = 16
CONV_WIDTH = 31
CONV_CTX = CONV_WIDTH - 1
N_EXPERTS = 32
TOP_K = 4
MOE_FF = 1024
SWIGLU_LIMIT = 7.0
SWIGLU_ALPHA = 1.702
LN_EPS = 1e-5
DEPTH = 2
DEEPNORM_ALPHA = (2 * DEPTH) ** 0.25
PAST_LEN = 16384
NEG_INF = -1e30

LANES = 128
SUBLANES = 8
GMM_TILE = 512
CONV_HALO = 32
CONV_BLOCK = 32
LN_ROWS = 16
N_SLAB = D_MODEL // LANES
COMBINE_ROWS = 32
IDX_SLOTS = 3
HEADS_PER_DOT = 4
VMEM_LIMIT = 56 * 1024 * 1024


def _cparams(n_axes):
    return pltpu.CompilerParams(dimension_semantics=("arbitrary",) * n_axes,
                                vmem_limit_bytes=VMEM_LIMIT)


def _pick_tile(n, cap):
    best = None
    t = LANES
    while t <= cap:
        if n % t == 0:
            best = t
        t += LANES
    assert best is not None, n
    return best


def _layer_norm(h, g, b):
    mu = jnp.mean(h, axis=-1, keepdims=True)
    hc = h - mu
    var = jnp.mean(hc * hc, axis=-1, keepdims=True)
    return hc * lax.rsqrt(var + LN_EPS) * g + b


def _qkv_kernel(x_ref, w_ref, b_ref, c_ref, a_ref, s_ref, q_ref, k_ref, v_ref):
    x = x_ref[...].astype(BF16)
    acc = jnp.dot(x, w_ref[...], preferred_element_type=F32) + b_ref[...]
    c = c_ref[...]
    a = a_ref[...]
    s = s_ref[...]

    def rope(t):
        return t * c + pltpu.roll(t, LANES - ROT_DIM // 2, 1) * a + pltpu.roll(t, ROT_DIM // 2, 1) * s

    for j in range(D_MODEL // LANES):
        sl = slice(j * LANES, (j + 1) * LANES)
        q_ref[:, sl] = (rope(acc[:, sl]) * Q_SCALE).astype(BF16)
    for j in range(KV_DIM // LANES):
        sl = slice(D_MODEL + j * LANES, D_MODEL + (j + 1) * LANES)
        k_ref[:, j * LANES:(j + 1) * LANES] = rope(acc[:, sl])
    v_ref[...] = acc[:, D_MODEL + KV_DIM:]


def _qkv_rope(x, w_bf, b, cos_t, sa_t, sb_t, tm):
    t = x.shape[0]
    period = cos_t.shape[0] // tm
    row = lambda i: (i, 0)
    tab = lambda i: (i % period, 0)
    fixed = lambda i: (0, 0)
    return pl.pallas_call(
        _qkv_kernel,
        grid=(t // tm,),
        in_specs=[pl.BlockSpec((tm, D_MODEL), row),
                  pl.BlockSpec((D_MODEL, QKV_DIM), fixed),
                  pl.BlockSpec((1, QKV_DIM), fixed),
                  pl.BlockSpec((tm, LANES), tab),
                  pl.BlockSpec((tm, LANES), tab),
                  pl.BlockSpec((tm, LANES), tab)],
        out_specs=[pl.BlockSpec((tm, D_MODEL), row),
                   pl.BlockSpec((tm, KV_DIM), row),
                   pl.BlockSpec((tm, KV_DIM), row)],
        out_shape=[jax.ShapeDtypeStruct((t, D_MODEL), BF16),
                   jax.ShapeDtypeStruct((t, KV_DIM), F32),
                   jax.ShapeDtypeStruct((t, KV_DIM), F32)],
        compiler_params=_cparams(1),
        name="qkv_rope",
    )(x, w_bf, b, cos_t, sa_t, sb_t)


def _qkv_t_kernel(x_ref, wqt_ref, bq_ref, wkv_ref, bkv_ref, wvt_ref, bv_ref, c_ref, a_ref, s_ref,
                  ct_ref, st_ref, qt_ref, k_ref, v_ref, vt_ref):
    x = x_ref[...].astype(BF16)
    nt = (((1,), (1,)), ((), ()))
    qt = lax.dot_general(wqt_ref[...], x, nt, preferred_element_type=F32) + bq_ref[...]
    ct = ct_ref[...]
    st = st_ref[...]
    half = ROT_DIM // 2
    for h in range(N_HEADS):
        r0 = h * HEAD_DIM
        x1 = qt[r0:r0 + half]
        x2 = qt[r0 + half:r0 + ROT_DIM]
        rot = jnp.concatenate([x1 * ct - x2 * st, x2 * ct + x1 * st], axis=0)
        qt_ref[r0:r0 + ROT_DIM, :] = (rot * Q_SCALE).astype(BF16)
        qt_ref[r0 + ROT_DIM:r0 + HEAD_DIM, :] = (qt[r0 + ROT_DIM:r0 + HEAD_DIM] * Q_SCALE).astype(BF16)
    vt = lax.dot_general(wvt_ref[...], x, nt, preferred_element_type=F32) + bv_ref[...]
    vt_ref[...] = vt.astype(BF16)

    kv = jnp.dot(x, wkv_ref[...], preferred_element_type=F32) + bkv_ref[...]
    c = c_ref[...]
    a = a_ref[...]
    s = s_ref[...]
    for j in range(KV_DIM // LANES):
        t = kv[:, j * LANES:(j + 1) * LANES]
        k_ref[:, j * LANES:(j + 1) * LANES] = (
            t * c + pltpu.roll(t, LANES - half, 1) * a + pltpu.roll(t, half, 1) * s)
    v_ref[...] = kv[:, KV_DIM:]


def _qkv_rope_t(x, w_bf, b, cos_t, sa_t, sb_t, cos_tt, sin_tt, tm):
    t = x.shape[0]
    period = cos_t.shape[0] // tm
    row = lambda i: (i, 0)
    col = lambda i: (0, i)
    tab = lambda i: (i % period, 0)
    tabt = lambda i: (0, i % period)
    fixed = lambda i: (0, 0)
    wqt = w_bf[:, :D_MODEL].T
    wvt = w_bf[:, D_MODEL + KV_DIM:].T
    half = ROT_DIM // 2
    return pl.pallas_call(
        _qkv_t_kernel,
        grid=(t // tm,),
        in_specs=[pl.BlockSpec((tm, D_MODEL), row),
                  pl.BlockSpec((D_MODEL, D_MODEL), fixed),
                  pl.BlockSpec((D_MODEL, 1), fixed),
                  pl.BlockSpec((D_MODEL, 2 * KV_DIM), fixed),
                  pl.BlockSpec((1, 2 * KV_DIM), fixed),
                  pl.BlockSpec((KV_DIM, D_MODEL), fixed),
                  pl.BlockSpec((KV_DIM, 1), fixed),
                  pl.BlockSpec((tm, LANES), tab),
                  pl.BlockSpec((tm, LANES), tab),
                  pl.BlockSpec((tm, LANES), tab),
                  pl.BlockSpec((half, tm), tabt),
                  pl.BlockSpec((half, tm), tabt)],
        out_specs=[pl.BlockSpec((D_MODEL, tm), col),
                   pl.BlockSpec((tm, KV_DIM), row),
                   pl.BlockSpec((tm, KV_DIM), row),
                   pl.BlockSpec((KV_DIM, tm), col)],
        out_shape=[jax.ShapeDtypeStruct((D_MODEL, t), BF16),
                   jax.ShapeDtypeStruct((t, KV_DIM), F32),
                   jax.ShapeDtypeStruct((t, KV_DIM), F32),
                   jax.ShapeDtypeStruct((KV_DIM, t), BF16)],
        compiler_params=_cparams(1),
        name="qkv_rope_t",
    )(x, wqt, b[:, :D_MODEL].reshape(D_MODEL, 1), w_bf[:, D_MODEL:], b[:, D_MODEL:],
      wvt, b[:, D_MODEL + KV_DIM:].reshape(KV_DIM, 1), cos_t, sa_t, sb_t, cos_tt, sin_tt)


def _attend(q, kcat, vcat, bias, sink_ref, write):
    kgs = [kcat[:, g * HEAD_DIM:(g + 1) * HEAD_DIM] for g in range(N_KV_HEADS)]

    def scores(h):
        qh = q[:, h * HEAD_DIM:(h + 1) * HEAD_DIM]
        return lax.dot_general(qh, kgs[h // GROUP], (((1,), (1,)), ((), ())), preferred_element_type=F32) + bias

    s_next = scores(0)
    for h in range(N_HEADS):
        s = s_next
        if h + 1 < N_HEADS:
            s_next = scores(h + 1)
        vg = vcat[:, (h // GROUP) * HEAD_DIM:(h // GROUP + 1) * HEAD_DIM]
        sink = sink_ref[h] * LOG2_E
        m = jnp.maximum(jnp.max(s, axis=-1, keepdims=True), sink)
        p = jnp.exp2(s - m)
        denom = jnp.sum(p, axis=-1, keepdims=True) + jnp.exp2(sink - m)
        o = jnp.dot(p.astype(BF16), vg, preferred_element_type=F32)
        write(h, o / denom)


def _attn_prompt_kernel(sink_ref, qt_ref, kp_ref, ko_ref, km_ref, vtp_ref, vto_ref, vtm_ref, o_ref, acc_ref):
    i = pl.program_id(1)
    kcat = jnp.concatenate([kp_ref[...], ko_ref[...], km_ref[...]], axis=0).astype(BF16)
    vt = jnp.concatenate([vtp_ref[...], vto_ref[...], vtm_ref[...]], axis=1)
    nk = 2 * ATTN_BLOCK + META_LEN
    key = lax.broadcasted_iota(jnp.int32, (nk, ATTN_BLOCK), 0)
    qry = lax.broadcasted_iota(jnp.int32, (nk, ATTN_BLOCK), 1)
    first = jnp.where(i > 0, 0, 2 * ATTN_BLOCK)
    prev_ok = (key < ATTN_BLOCK) & (key >= qry + first)
    own_ok = (key >= ATTN_BLOCK) & (key - ATTN_BLOCK <= qry)
    mask = prev_ok | own_ok | (key >= 2 * ATTN_BLOCK)
    bias = jnp.where(mask, 0.0, NEG_INF)
    bias = jnp.concatenate([bias] * HEADS_PER_DOT, axis=1)
    lane = lax.broadcasted_iota(jnp.int32, (1, HEADS_PER_DOT * ATTN_BLOCK), 1)
    kgs = [kcat[:, g * HEAD_DIM:(g + 1) * HEAD_DIM] for g in range(N_KV_HEADS)]

    def scores(u):
        h0 = u * HEADS_PER_DOT
        rhs = jnp.concatenate([qt_ref[(h0 + j) * HEAD_DIM:(h0 + j + 1) * HEAD_DIM, :]
                               for j in range(HEADS_PER_DOT)], axis=1)
        return jnp.dot(kgs[h0 // GROUP], rhs, preferred_element_type=F32) + bias

    n_units = N_HEADS // HEADS_PER_DOT
    s_next = scores(0)
    for u in range(n_units):
        s = s_next
        if u + 1 < n_units:
            s_next = scores(u + 1)
        h0 = u * HEADS_PER_DOT
        g = h0 // GROUP
        sink = sink_ref[h0] * LOG2_E
        for j in range(1, HEADS_PER_DOT):
            sink = jnp.where(lane >= j * ATTN_BLOCK, sink_ref[h0 + j] * LOG2_E, sink)
        m = jnp.maximum(jnp.max(s, axis=0, keepdims=True), sink)
        p = jnp.exp2(s - m)
        denom = jnp.sum(p, axis=0, keepdims=True) + jnp.exp2(sink - m)
        o = jnp.dot(vt[g * HEAD_DIM:(g + 1) * HEAD_DIM, :], p.astype(BF16), preferred_element_type=F32)
        o = o * (1.0 / denom)
        for j in range(HEADS_PER_DOT):
            acc_ref[(h0 + j) * HEAD_DIM:(h0 + j + 1) * HEAD_DIM, :] = o[:, j * ATTN_BLOCK:(j + 1) * ATTN_BLOCK]
    o_ref[...] = acc_ref[...].T.astype(o_ref.dtype)


def _attn_prompt(qt, k, vt, k_meta, vt_meta, sinks, n_batch, seq):
    nblk = seq // ATTN_BLOCK
    own = lambda b, i: b * nblk + i
    prev = lambda b, i: b * nblk + jnp.maximum(i - 1, 0)
    kspec = lambda m: pl.BlockSpec((ATTN_BLOCK, KV_DIM), lambda b, i: (m(b, i), 0))
    vspec = lambda m: pl.BlockSpec((KV_DIM, ATTN_BLOCK), lambda b, i: (0, m(b, i)))
    return pl.pallas_call(
        _attn_prompt_kernel,
        grid=(n_batch, nblk),
        in_specs=[pl.BlockSpec(memory_space=pltpu.SMEM),
                  pl.BlockSpec((D_MODEL, ATTN_BLOCK), lambda b, i: (0, own(b, i))),
                  kspec(prev), kspec(own),
                  pl.BlockSpec((META_LEN, KV_DIM), lambda b, i: (b, 0)),
                  vspec(prev), vspec(own),
                  pl.BlockSpec((None, KV_DIM, META_LEN), lambda b, i: (b, 0, 0))],
        out_specs=pl.BlockSpec((ATTN_BLOCK, D_MODEL), lambda b, i: (own(b, i), 0)),
        out_shape=jax.ShapeDtypeStruct((n_batch * seq, D_MODEL), BF16),
        scratch_shapes=[pltpu.VMEM((D_MODEL, ATTN_BLOCK), F32)],
        compiler_params=_cparams(2),
        name="attn_prompt",
    )(sinks, qt, k, k, k_meta, vt, vt, vt_meta)


def _attn_meta_kernel(sink_ref, q_ref, k_ref, v_ref, o_ref):
    r = lax.broadcasted_iota(jnp.int32, (META_LEN, META_LEN), 0)
    c = lax.broadcasted_iota(jnp.int32, (META_LEN, META_LEN), 1)

    def write(h, o):
        o_ref[:, h * HEAD_DIM:(h + 1) * HEAD_DIM] = o.astype(o_ref.dtype)

    bias = jnp.where(c <= r, 0.0, NEG_INF)
    _attend(q_ref[...], k_ref[...].astype(BF16), v_ref[...].astype(BF16), bias, sink_ref, write)


def _attn_meta(q, k, v, sinks, n_batch):
    imap = lambda b: (b, 0)
    return pl.pallas_call(
        _attn_meta_kernel,
        grid=(n_batch,),
        in_specs=[pl.BlockSpec(memory_space=pltpu.SMEM),
                  pl.BlockSpec((META_LEN, D_MODEL), imap),
                  pl.BlockSpec((META_LEN, KV_DIM), imap),
                  pl.BlockSpec((META_LEN, KV_DIM), imap)],
        out_specs=pl.BlockSpec((META_LEN, D_MODEL), lambda b: (b, 0)),
        out_shape=jax.ShapeDtypeStruct((n_batch * META_LEN, D_MODEL), BF16),
        compiler_params=_cparams(1),
        name="attn_meta",
    )(sinks, q, k, v)


SAMPLE_GROUP = 8


def _attn_sample_kernel(sink_ref, bias_ref, q_ref, kn_ref, vn_ref, cmk_ref, cmv_ref, cwk_ref, cwv_ref,
                        o_ref, *, dec_seq, group):
    def keys(cm_ref, cw_ref, new):
        parts = []
        for j in range(group):
            parts += [cm_ref[j], cw_ref[j], new[j * dec_seq:(j + 1) * dec_seq]]
        return jnp.concatenate(parts, axis=0).astype(BF16)

    kcat = keys(cmk_ref, cwk_ref, kn_ref[...])
    vcat = keys(cmv_ref, cwv_ref, vn_ref[...])

    def write(h, o):
        o_ref[:, h * HEAD_DIM:(h + 1) * HEAD_DIM] = o.astype(o_ref.dtype)

    _attend(q_ref[...], kcat, vcat, bias_ref[...], sink_ref, write)


def _sample_bias(group, dec_seq):
    per = META_LEN + WINDOW + dec_seq
    r = np.arange(group * dec_seq)[:, None]
    c = np.arange(group * per)[None, :]
    tq, ck = r % dec_seq, c % per
    win_ok = (ck >= META_LEN) & (ck < META_LEN + WINDOW) & (ck - META_LEN >= tq)
    new_ok = (ck >= META_LEN + WINDOW) & (ck - (META_LEN + WINDOW) <= tq)
    vis = (r // dec_seq == c // per) & ((ck < META_LEN) | win_ok | new_ok)
    return np.where(vis, 0.0, NEG_INF).astype(np.float32)


def _attn_sample(q, k, v, cmk, cmv, cwk, cwv, sinks, row0, dec_batch, dec_seq):
    group = SAMPLE_GROUP
    nq = group * dec_seq
    blk0 = row0 // nq
    bias = jnp.asarray(_sample_bias(group, dec_seq))
    qmap = lambda n: (blk0 + n, 0)
    cmap = lambda n: (n, 0, 0)
    return pl.pallas_call(
        functools.partial(_attn_sample_kernel, dec_seq=dec_seq, group=group),
        grid=(dec_batch // group,),
        in_specs=[pl.BlockSpec(memory_space=pltpu.SMEM),
                  pl.BlockSpec(bias.shape, lambda n: (0, 0)),
                  pl.BlockSpec((nq, D_MODEL), qmap),
                  pl.BlockSpec((nq, KV_DIM), qmap),
                  pl.BlockSpec((nq, KV_DIM), qmap),
                  pl.BlockSpec((group, META_LEN, KV_DIM), cmap),
                  pl.BlockSpec((group, META_LEN, KV_DIM), cmap),
                  pl.BlockSpec((group, WINDOW, KV_DIM), cmap),
                  pl.BlockSpec((group, WINDOW, KV_DIM), cmap)],
        out_specs=pl.BlockSpec((nq, D_MODEL), lambda n: (n, 0)),
        out_shape=jax.ShapeDtypeStruct((dec_batch * dec_seq, D_MODEL), BF16),
        compiler_params=_cparams(1),
        name="attn_sample",
    )(sinks, bias, q, k, v, cmk, cmv, cwk, cwv)


def _proj_ln_route_kernel(a_ref, w_ref, b_ref, x_ref, g_ref, bb_ref, wr_ref, br_ref, tri_ref, cin_ref,
                          x1_ref, ids_ref, gates_ref, rank_ref, cnt_ref, carry_ref):
    @pl.when(pl.program_id(0) == 0)
    def _():
        carry_ref[...] = cin_ref[...].astype(F32)

    y = jnp.dot(a_ref[...], w_ref[...], preferred_element_type=F32) + b_ref[...]
    x1 = _layer_norm(DEEPNORM_ALPHA * x_ref[...] + y, g_ref[...], bb_ref[...])
    for j in range(N_SLAB):
        x1_ref[pl.ds(j, x1.shape[0], stride=N_SLAB), :] = x1[:, j * LANES:(j + 1) * LANES]

    logits = lax.dot_general(wr_ref[...], x1.astype(BF16), (((1,), (1,)), ((), ())),
                             preferred_element_type=F32) + br_ref[...]
    tm = logits.shape[1]
    eidx = lax.broadcasted_iota(jnp.int32, (N_EXPERTS, tm), 0).astype(F32)
    cur = logits
    vals, idxs, sels = [], [], []
    for _ in range(TOP_K):
        m = jnp.max(cur, axis=0, keepdims=True)
        idx = jnp.min(jnp.where(cur == m, eidx, float(N_EXPERTS)), axis=0, keepdims=True)
        sel = eidx == idx
        vals.append(m)
        idxs.append(idx)
        sels.append(sel)
        cur = jnp.where(sel, -jnp.inf, cur)
    exps = [jnp.exp(v - vals[0]) for v in vals]
    tot = exps[0] + exps[1] + exps[2] + exps[3]
    gates_ref[...] = jnp.concatenate([e / tot for e in exps], axis=0)
    ids_ref[...] = jnp.concatenate(idxs, axis=0).astype(jnp.int32)

    chosen = jnp.where(sels[0] | sels[1] | sels[2] | sels[3], 1.0, 0.0)
    before = jnp.dot(chosen.astype(BF16), tri_ref[...], preferred_element_type=F32)
    before = before + carry_ref[:, 0:1]
    ranks = [jnp.sum(jnp.where(s, before, 0.0), axis=0, keepdims=True) for s in sels]
    rank_ref[...] = jnp.concatenate(ranks, axis=0).astype(jnp.int32)
    carry_ref[...] = carry_ref[...] + jnp.sum(chosen, axis=1, keepdims=True)
    cnt_ref[...] = carry_ref[...].astype(jnp.int32)


def _proj_ln_route(a_bf, w_bf, b, x, g, bb, wr_t_bf, br_col, counts_in, tm):
    t = x.shape[0]
    row = lambda i: (i, 0)
    col = lambda i: (0, i)
    fixed = lambda i: (0, 0)
    tri = (jnp.arange(tm)[:, None] < jnp.arange(tm)[None, :]).astype(BF16)
    return pl.pallas_call(
        _proj_ln_route_kernel,
        grid=(t // tm,),
        in_specs=[pl.BlockSpec((tm, D_MODEL), row),
                  pl.BlockSpec((D_MODEL, D_MODEL), fixed),
                  pl.BlockSpec((1, D_MODEL), fixed),
                  pl.BlockSpec((tm, D_MODEL), row),
                  pl.BlockSpec((1, D_MODEL), fixed),
                  pl.BlockSpec((1, D_MODEL), fixed),
                  pl.BlockSpec((N_EXPERTS, D_MODEL), fixed),
                  pl.BlockSpec((N_EXPERTS, 1), fixed),
                  pl.BlockSpec((tm, tm), fixed),
                  pl.BlockSpec((N_EXPERTS, LANES), fixed)],
        out_specs=[pl.BlockSpec((tm * N_SLAB, LANES), row),
                   pl.BlockSpec((TOP_K, tm), col),
                   pl.BlockSpec((TOP_K, tm), col),
                   pl.BlockSpec((TOP_K, tm), col),
                   pl.BlockSpec((N_EXPERTS, LANES), fixed)],
        out_shape=[jax.ShapeDtypeStruct((t * N_SLAB, LANES), F32),
                   jax.ShapeDtypeStruct((TOP_K, t), jnp.int32),
                   jax.ShapeDtypeStruct((TOP_K, t), F32),
                   jax.ShapeDtypeStruct((TOP_K, t), jnp.int32),
                   jax.ShapeDtypeStruct((N_EXPERTS, LANES), jnp.int32)],
        scratch_shapes=[pltpu.VMEM((N_EXPERTS, LANES), F32)],
        compiler_params=_cparams(1),
        name="proj_ln_route",
    )(a_bf, w_bf, b, x, g, bb, wr_t_bf, br_col, tri, counts_in)


def _idx_copy(dest_hbm, dsm, sem, step, stride, n_slots=2):
    slot = step % n_slots
    return pltpu.make_async_copy(dest_hbm.at[pl.ds(pl.multiple_of(step * stride, stride), stride)],
                                 dsm.at[pl.ds(pl.multiple_of(slot * stride, stride), stride)],
                                 sem)


def _dispatch_kernel(pstart_ref, padded_ref, nact_ref, dest_hbm, *rest, tiles, steps, stride, n_tiles):
    x_refs = rest[:len(tiles)]
    xs_hbm, dsm, zbuf, sems = rest[len(tiles):]
    i = pl.program_id(0)
    nsteps = pl.num_programs(0)

    def zero_tile(row0):
        n = GMM_TILE * N_SLAB
        return pltpu.make_async_copy(zbuf, xs_hbm.at[pl.ds(pl.multiple_of(row0 * N_SLAB, n), n)], sems.at[2])

    def token(ref, r):
        return ref.at[pl.ds(pl.multiple_of(r * N_SLAB, N_SLAB), N_SLAB)]

    @pl.when(i == 0)
    def _():
        zbuf[...] = jnp.zeros_like(zbuf)

        def each_expert(fn):
            def body(e, _):
                @pl.when(padded_ref[e] > 0)
                def _():
                    fn(zero_tile(pstart_ref[e] + padded_ref[e] - GMM_TILE))
                return 0
            lax.fori_loop(0, N_EXPERTS, body, 0)

        def each_tail(fn):
            def body(t, _):
                fn(zero_tile(t * GMM_TILE))
                return 0
            lax.fori_loop(nact_ref[0], n_tiles, body, 0)

        each_expert(lambda cp: cp.start())
        each_tail(lambda cp: cp.start())
        each_expert(lambda cp: cp.wait())
        each_tail(lambda cp: cp.wait())
        _idx_copy(dest_hbm, dsm, sems.at[0], i, stride).start()

    _idx_copy(dest_hbm, dsm, sems.at[0], i, stride).wait()

    @pl.when(i + 1 < nsteps)
    def _():
        _idx_copy(dest_hbm, dsm, sems.at[0], i + 1, stride).start()

    base = (i % 2) * stride
    step0 = 0
    for x_ref, tm, n in zip(x_refs, tiles, steps):
        @pl.when((i >= step0) & (i < step0 + n))
        def _(x_ref=x_ref, tm=tm):
            def row_body(r, _):
                for k in range(TOP_K):
                    d = dsm[base + k * tm + r]
                    pltpu.make_async_copy(token(x_ref, r), token(xs_hbm, d), sems.at[1]).start(priority=k % 2)
                return 0

            lax.fori_loop(0, tm, row_body, 0, unroll=4)
            for k in range(TOP_K):
                pltpu.make_async_copy(x_ref, xs_hbm.at[pl.ds(0, tm * N_SLAB)], sems.at[1]).wait()
        step0 += n


def _seg_map(step0, n):
    return lambda i, *_: (jnp.clip(i - step0, 0, n - 1), 0)


def _dispatch(xs_list, tiles, dest_steps, pstart, padded, nact, stride, n_tiles):
    steps = [x.shape[0] // (tm * N_SLAB) for x, tm in zip(xs_list, tiles)]
    in_specs = [pl.BlockSpec(memory_space=pl.ANY)]
    step0 = 0
    for tm, n in zip(tiles, steps):
        in_specs.append(pl.BlockSpec((tm * N_SLAB, LANES), _seg_map(step0, n)))
        step0 += n
    grid_spec = pltpu.PrefetchScalarGridSpec(
        num_scalar_prefetch=3,
        grid=(sum(steps),),
        in_specs=in_specs,
        out_specs=pl.BlockSpec(memory_space=pl.ANY),
        scratch_shapes=[pltpu.SMEM((2 * stride,), jnp.int32),
                        pltpu.VMEM((GMM_TILE * N_SLAB, LANES), F32),
                        pltpu.SemaphoreType.DMA((3,))],
    )
    return pl.pallas_call(
        functools.partial(_dispatch_kernel, tiles=tuple(tiles), steps=tuple(steps), stride=stride,
                          n_tiles=n_tiles),
        grid_spec=grid_spec,
        out_shape=jax.ShapeDtypeStruct((n_tiles * GMM_TILE * N_SLAB, LANES), F32),
        compiler_params=_cparams(1),
        name="moe_dispatch",
    )(pstart, padded, nact, dest_steps, *xs_list)


def _gmm_kernel(te_ref, tsrc_ref, tfirst_ref, nact_ref, x_ref, wgu_ref, bgu_ref, wdn_ref, bdn_ref, y_ref,
                wgu_bf, wdn_bf):
    i = pl.program_id(0)

    @pl.when(tfirst_ref[i] == 1)
    def _():
        wgu_bf[...] = wgu_ref[...].astype(BF16)
        wdn_bf[...] = wdn_ref[...].astype(BF16)

    @pl.when(i < nact_ref[0])
    def _():
        x = jnp.concatenate([x_ref[pl.ds(j, GMM_TILE, stride=N_SLAB), :] for j in range(N_SLAB)], axis=1)
        gu = jnp.dot(x.astype(BF16), wgu_bf[...], preferred_element_type=F32) + bgu_ref[...]
        gate = jnp.minimum(gu[:, :MOE_FF], SWIGLU_LIMIT)
        up = jnp.clip(gu[:, MOE_FF:], -SWIGLU_LIMIT, SWIGLU_LIMIT)
        glu = gate * jax.nn.sigmoid(SWIGLU_ALPHA * gate)
        h = ((up + 1.0) * glu).astype(BF16)
        y = jnp.dot(h, wdn_bf[...], preferred_element_type=F32) + bdn_ref[...]
        for j in range(N_SLAB):
            y_ref[pl.ds(j, GMM_TILE, stride=N_SLAB), :] = y[:, j * LANES:(j + 1) * LANES]

    @pl.when(i >= nact_ref[0])
    def _():
        y_ref[...] = jnp.zeros_like(y_ref)


def _gmm(xs, te, tsrc, tfirst, nact, wgu, bgu, wdn, bdn, n_tiles):
    emap = lambda i, te, *_: (te[i], 0, 0)
    grid_spec = pltpu.PrefetchScalarGridSpec(
        num_scalar_prefetch=4,
        grid=(n_tiles,),
        in_specs=[pl.BlockSpec((GMM_TILE * N_SLAB, LANES), lambda i, te, ts, *_: (ts[i], 0)),
                  pl.BlockSpec((None, D_MODEL, 2 * MOE_FF), emap),
                  pl.BlockSpec((None, 1, 2 * MOE_FF), emap),
                  pl.BlockSpec((None, MOE_FF, D_MODEL), emap),
                  pl.BlockSpec((None, 1, D_MODEL), emap)],
        out_specs=pl.BlockSpec((GMM_TILE * N_SLAB, LANES), lambda i, *_: (i, 0)),
        scratch_shapes=[pltpu.VMEM((D_MODEL, 2 * MOE_FF), BF16),
                        pltpu.VMEM((MOE_FF, D_MODEL), BF16)],
    )
    return pl.pallas_call(
        _gmm_kernel,
        grid_spec=grid_spec,
        out_shape=jax.ShapeDtypeStruct((n_tiles * GMM_TILE * N_SLAB, LANES), F32),
        compiler_params=_cparams(1),
        name="moe_gmm",
    )(te, tsrc, tfirst, nact, xs, wgu, bgu, wdn, bdn)


def _combine_kernel(dest_hbm, gate_hbm, ys_hbm, x_ref, g_ref, bb_ref, o_ref, dsm, gsm, buf, sems,
                    *, tm, stride):
    i = pl.program_id(0)
    nsteps = pl.num_programs(0)

    def idx_copies(step):
        slot = step % IDX_SLOTS
        return (_idx_copy(dest_hbm, dsm, sems.at[slot], step, stride, IDX_SLOTS),
                _idx_copy(gate_hbm, gsm, sems.at[IDX_SLOTS + slot], step, stride, IDX_SLOTS))

    def token(ref, r):
        return ref.at[pl.ds(pl.multiple_of(r * N_SLAB, N_SLAB), N_SLAB)]

    def issue_rows(step, parity):
        base = (step % IDX_SLOTS) * stride

        def body(r, _):
            for k in range(TOP_K):
                d = dsm[base + k * tm + r]
                pltpu.make_async_copy(token(ys_hbm, d), token(buf.at[parity * TOP_K + k], r),
                                      sems.at[2 * IDX_SLOTS + parity]).start(priority=k % 2)
            return 0

        lax.fori_loop(0, tm, body, 0, unroll=4)

    @pl.when(i == 0)
    def _():
        for cp in idx_copies(0):
            cp.start()

        @pl.when(nsteps > 1)
        def _():
            for cp in idx_copies(1):
                cp.start()

        for cp in idx_copies(0):
            cp.wait()

        issue_rows(0, 0)

    @pl.when(i + 1 < nsteps)
    def _():
        for cp in idx_copies(i + 1):
            cp.wait()

    @pl.when(i + 2 < nsteps)
    def _():
        for cp in idx_copies(i + 2):
            cp.start()

    for parity in range(2):
        @pl.when((i + 1 < nsteps) & ((i + 1) % 2 == parity))
        def _(parity=parity):
            issue_rows(i + 1, parity)

    gbase = (i % IDX_SLOTS) * stride

    def compute(parity):
        half = parity * TOP_K
        for k in range(TOP_K):
            pltpu.make_async_copy(ys_hbm.at[pl.ds(0, tm * N_SLAB)], buf.at[half + k],
                                  sems.at[2 * IDX_SLOTS + parity]).wait()

        def token_sum(r, _):
            rows = pl.ds(pl.multiple_of(r * N_SLAB, N_SLAB), N_SLAB)
            h = DEEPNORM_ALPHA * x_ref[rows, :]
            for k in range(TOP_K):
                h = h + gsm[gbase + k * tm + r] * buf[half + k, rows, :]
            buf[half, rows, :] = h
            return 0

        lax.fori_loop(0, tm, token_sum, 0, unroll=8)

        def block(b, _):
            r0 = pl.multiple_of(b * COMBINE_ROWS, COMBINE_ROWS)
            hs = [buf[half, pl.ds(r0 * N_SLAB + j, COMBINE_ROWS, stride=N_SLAB), :] for j in range(N_SLAB)]
            tot = hs[0]
            for j in range(1, N_SLAB):
                tot = tot + hs[j]
            mu = jnp.sum(tot, axis=1, keepdims=True) * (1.0 / D_MODEL)
            cen = [h - mu for h in hs]
            sq = cen[0] * cen[0]
            for j in range(1, N_SLAB):
                sq = sq + cen[j] * cen[j]
            inv = lax.rsqrt(jnp.sum(sq, axis=1, keepdims=True) * (1.0 / D_MODEL) + LN_EPS)
            for j in range(N_SLAB):
                o_ref[pl.ds(r0, COMBINE_ROWS), j * LANES:(j + 1) * LANES] = (
                    cen[j] * inv * g_ref[j:j + 1, :] + bb_ref[j:j + 1, :])
            return 0

        lax.fori_loop(0, tm // COMBINE_ROWS, block, 0, unroll=4)

    for parity in range(2):
        @pl.when(i % 2 == parity)
        def _(parity=parity):
            compute(parity)


def _combine(ys, dest_steps, gate_steps, x_tiles, g3, bb3, tm, stride):
    t = x_tiles.shape[0] // N_SLAB
    row = lambda i: (i, 0)
    fixed = lambda i: (0, 0)
    return pl.pallas_call(
        functools.partial(_combine_kernel, tm=tm, stride=stride),
        grid=(t // tm,),
        in_specs=[pl.BlockSpec(memory_space=pl.ANY),
                  pl.BlockSpec(memory_space=pl.ANY),
                  pl.BlockSpec(memory_space=pl.ANY),
                  pl.BlockSpec((tm * N_SLAB, LANES), row),
                  pl.BlockSpec((N_SLAB, LANES), fixed),
                  pl.BlockSpec((N_SLAB, LANES), fixed)],
        out_specs=pl.BlockSpec((tm, D_MODEL), row),
        out_shape=jax.ShapeDtypeStruct((t, D_MODEL), F32),
        scratch_shapes=[pltpu.SMEM((IDX_SLOTS * stride,), jnp.int32),
                        pltpu.SMEM((IDX_SLOTS * stride,), F32),
                        pltpu.VMEM((2 * TOP_K, tm * N_SLAB, LANES), F32),
                        pltpu.SemaphoreType.DMA((2 * IDX_SLOTS + 2,))],
        compiler_params=_cparams(1),
        name="moe_combine",
    )(dest_steps, gate_steps, ys, x_tiles, g3, bb3)


def _moe(segs, counts, expert0, wgu, bgu, wdn, bdn, g, bb):
    t = sum(seg[1].shape[1] for seg in segs)
    n_tiles = -(-(t * TOP_K) // GMM_TILE) + N_EXPERTS
    counts = counts[:, 0]
    padded = ((counts + GMM_TILE - 1) // GMM_TILE) * GMM_TILE
    pend = jnp.cumsum(padded)
    pstart = (pend - padded).astype(jnp.int32)
    padded = padded.astype(jnp.int32)
    nact = (pend[-1] // GMM_TILE).astype(jnp.int32).reshape(1)
    tile = jnp.arange(n_tiles, dtype=jnp.int32)
    tsrc = jnp.minimum(tile, jnp.maximum(nact[0] - 1, 0))
    te = jnp.sum((pend[None, :] <= (tsrc * GMM_TILE)[:, None]).astype(jnp.int32), axis=1)
    te = jnp.minimum(te, N_EXPERTS - 1).astype(jnp.int32)
    tfirst = jnp.concatenate([jnp.ones((1,), jnp.int32), (te[1:] != te[:-1]).astype(jnp.int32)])
    eids = jnp.arange(N_EXPERTS, dtype=jnp.int32)[:, None, None]

    stride = -(-(TOP_K * max(seg[4] for seg in segs)) // 1024) * 1024
    def per_step(a, tm):
        n = a.shape[1]
        steps = a.reshape(TOP_K, n // tm, tm).transpose(1, 0, 2).reshape(n // tm, TOP_K * tm)
        return jnp.pad(steps, ((0, 0), (0, stride - TOP_K * tm))).reshape(-1)

    plans = []
    for x1, ids, gates, rank, tm in segs:
        dest = jnp.sum(jnp.where(ids[None] == eids, pstart[:, None, None], 0), axis=0) + rank
        plans.append((per_step(dest, tm), per_step(gates, tm)))
    xs = _dispatch([seg[0] for seg in segs], [seg[4] for seg in segs],
                   jnp.concatenate([p[0] for p in plans]), pstart, padded, nact, stride, n_tiles)
    ys = _gmm(xs, te + expert0, tsrc, tfirst, nact, wgu, bgu, wdn, bdn, n_tiles)
    return [_combine(ys, dest_steps, gate_steps, seg[0], g, bb, seg[4], stride)
            for seg, (dest_steps, gate_steps) in zip(segs, plans)]


def _pw1_glu_kernel(x_ref, w_ref, b_ref, u_ref):
    a = jnp.dot(x_ref[...].astype(BF16), w_ref[...], preferred_element_type=F32) + b_ref[...]
    u_ref[...] = a[:, :D_MODEL] * jax.nn.sigmoid(a[:, D_MODEL:])


def _pw1_glu(x, w_bf, b, tm):
    t = x.shape[0]
    row = lambda i: (i, 0)
    fixed = lambda i: (0, 0)
    return pl.pallas_call(
        _pw1_glu_kernel,
        grid=(t // tm,),
        in_specs=[pl.BlockSpec((tm, D_MODEL), row),
                  pl.BlockSpec((D_MODEL, 2 * D_MODEL), fixed),
                  pl.BlockSpec((1, 2 * D_MODEL), fixed)],
        out_specs=pl.BlockSpec((tm, D_MODEL), row),
        out_shape=jax.ShapeDtypeStruct((t, D_MODEL), F32),
        compiler_params=_cparams(1),
        name="pw1_glu",
    )(x, w_bf, b)


def _conv_rows(win_ref, zs_ref, base, n_sets, w_ref, b_ref):
    shift = CONV_HALO - CONV_CTX
    offs = tuple(range(n_sets))

    def out_rows(o):
        return pl.ds(base + o, SUBLANES, stride=n_sets)

    def slab(c, _):
        def window(t):
            return win_ref[c, pl.ds(base + (shift + t), SUBLANES, stride=n_sets), :]

        bias = b_ref[pl.ds(c, 1), :]
        acc = [bias] * n_sets
        wins = [window(t) for t in range(n_sets - 1)]
        for j in range(CONV_WIDTH):
            w = w_ref[c, j:j + 1, :]
            wins.append(window(j + n_sets - 1))
            acc = [acc[o] + w * wins[o] for o in offs]
            wins.pop(0)
        for o in offs:
            zs_ref[c, out_rows(o), :] = acc[o]
        return 0

    lax.fori_loop(0, N_SLAB, slab, 0)


def _ln_silu_rows(zs_ref, rows, g_ref, bb_ref):
    acc = [zs_ref[c, rows, :] for c in range(N_SLAB)]
    tot = acc[0]
    for c in range(1, N_SLAB):
        tot = tot + acc[c]
    mu = jnp.sum(tot, axis=1, keepdims=True) * (1.0 / D_MODEL)
    cen = [a - mu for a in acc]
    sq = cen[0] * cen[0]
    for c in range(1, N_SLAB):
        sq = sq + cen[c] * cen[c]
    inv = lax.rsqrt(jnp.sum(sq, axis=1, keepdims=True) * (1.0 / D_MODEL) + LN_EPS)
    out = []
    for c in range(N_SLAB):
        z = cen[c] * inv * g_ref[c:c + 1, :] + bb_ref[c:c + 1, :]
        out.append(z * jax.nn.sigmoid(z))
    return out


def _conv_prompt_kernel(um_ref, up_ref, uc_ref, w_ref, b_ref, g_ref, bb_ref, z_ref, win_ref, zs_ref, *, tr):
    i = pl.program_id(1)
    gap = CONV_HALO - META_LEN

    @pl.when(i == 0)
    def _():
        for c in range(N_SLAB):
            win_ref[c, 0:gap, :] = jnp.zeros((gap, LANES), F32)
            win_ref[c, gap:CONV_HALO, :] = um_ref[:, c * LANES:(c + 1) * LANES]

    @pl.when(i > 0)
    def _():
        for c in range(N_SLAB):
            win_ref[c, 0:CONV_HALO, :] = up_ref[:, c * LANES:(c + 1) * LANES]

    for c in range(N_SLAB):
        win_ref[c, CONV_HALO:, :] = uc_ref[:, c * LANES:(c + 1) * LANES]

    def block(bi, _):
        base = pl.multiple_of(bi * CONV_BLOCK, CONV_BLOCK)
        _conv_rows(win_ref, zs_ref, base, CONV_BLOCK // SUBLANES, w_ref, b_ref)
        return 0

    lax.fori_loop(0, tr // CONV_BLOCK, block, 0)

    def norm(bi, _):
        rows = pl.ds(pl.multiple_of(bi * LN_ROWS, LN_ROWS), LN_ROWS)
        for c, z in enumerate(_ln_silu_rows(zs_ref, rows, g_ref, bb_ref)):
            z_ref[rows, c * LANES:(c + 1) * LANES] = z.astype(z_ref.dtype)
        return 0

    lax.fori_loop(0, tr // LN_ROWS, norm, 0, unroll=8)


def _conv_prompt(u, u_meta, w3, b3, g3, bb3, n_batch, seq, tr):
    nt = seq // tr
    cur = lambda b, i: (b * nt + i, 0)
    prev = lambda b, i: (jnp.maximum((b * seq + i * tr) // CONV_HALO - 1, 0), 0)
    fixed2 = lambda b, i: (0, 0)
    fixed3 = lambda b, i: (0, 0, 0)
    return pl.pallas_call(
        functools.partial(_conv_prompt_kernel, tr=tr),
        grid=(n_batch, nt),
        in_specs=[pl.BlockSpec((META_LEN, D_MODEL), lambda b, i: (b, 0)),
                  pl.BlockSpec((CONV_HALO, D_MODEL), prev),
                  pl.BlockSpec((tr, D_MODEL), cur),
                  pl.BlockSpec((N_SLAB, CONV_HALO, LANES), fixed3),
                  pl.BlockSpec((N_SLAB, LANES), fixed2),
                  pl.BlockSpec((N_SLAB, LANES), fixed2),
                  pl.BlockSpec((N_SLAB, LANES), fixed2)],
        out_specs=pl.BlockSpec((tr, D_MODEL), cur),
        out_shape=jax.ShapeDtypeStruct((n_batch * seq, D_MODEL), BF16),
        scratch_shapes=[pltpu.VMEM((N_SLAB, CONV_HALO + tr, LANES), F32),
                        pltpu.VMEM((N_SLAB, tr, LANES), F32)],
        compiler_params=_cparams(2),
        name="conv_prompt",
    )(u_meta, u, u, w3, b3, g3, bb3)


SHORT_ROWS = 16


def _conv_short_kernel(ctx_ref, u_ref, w_ref, b_ref, g_ref, bb_ref, z_ref, win_ref, zs_ref, *, n_seq, t_len):
    for n in range(n_seq):
        for c in range(N_SLAB):
            sl = slice(c * LANES, (c + 1) * LANES)
            win_ref[c, 0:CONV_HALO, :] = ctx_ref[n, :, sl]
            win_ref[c, CONV_HALO:CONV_HALO + t_len, :] = u_ref[n * t_len:(n + 1) * t_len, sl]
            if t_len < SHORT_ROWS:
                win_ref[c, CONV_HALO + t_len:, :] = jnp.zeros((SHORT_ROWS - t_len, LANES), F32)
        _conv_rows(win_ref, zs_ref, 0, SHORT_ROWS // SUBLANES, w_ref, b_ref)
        for c, z in enumerate(_ln_silu_rows(zs_ref, slice(0, t_len), g_ref, bb_ref)):
            z_ref[n * t_len:(n + 1) * t_len, c * LANES:(c + 1) * LANES] = z


def _conv_short(ctx_pad, u, row0, w3, b3, g3, bb3, t_len, n_seq):
    n_total = ctx_pad.shape[0]
    rows = n_seq * t_len
    blk0 = row0 // rows
    fixed2 = lambda n: (0, 0)
    fixed3 = lambda n: (0, 0, 0)
    return pl.pallas_call(
        functools.partial(_conv_short_kernel, n_seq=n_seq, t_len=t_len),
        grid=(n_total // n_seq,),
        in_specs=[pl.BlockSpec((n_seq, CONV_HALO, D_MODEL), lambda n: (n, 0, 0)),
                  pl.BlockSpec((rows, D_MODEL), lambda n: (blk0 + n, 0)),
                  pl.BlockSpec((N_SLAB, CONV_HALO, LANES), fixed3),
                  pl.BlockSpec((N_SLAB, LANES), fixed2),
                  pl.BlockSpec((N_SLAB, LANES), fixed2),
                  pl.BlockSpec((N_SLAB, LANES), fixed2)],
        out_specs=pl.BlockSpec((rows, D_MODEL), lambda n: (n, 0)),
        out_shape=jax.ShapeDtypeStruct((n_total * t_len, D_MODEL), F32),
        scratch_shapes=[pltpu.VMEM((N_SLAB, CONV_HALO + SHORT_ROWS, LANES), F32),
                        pltpu.VMEM((N_SLAB, SHORT_ROWS, LANES), F32)],
        compiler_params=_cparams(1),
        name="conv_short",
    )(ctx_pad, u, w3, b3, g3, bb3)


def _rope_angles(pos):
    inv = 1.0 / (ROPE_THETA ** (jnp.arange(0, ROT_DIM, 2, dtype=F32) / ROT_DIM))
    ang = pos.astype(F32)[:, None] * inv[None, :]
    return jnp.cos(ang), jnp.sin(ang)


def _rope_tables_t(pos):
    cos, sin = _rope_angles(pos)
    return cos.T, sin.T


def _rope_tables(pos):
    half = ROT_DIM // 2
    cos, sin = _rope_angles(pos)
    n = pos.shape[0]
    ones = jnp.ones((n, HEAD_DIM - ROT_DIM), F32)
    zeros = jnp.zeros((n, HEAD_DIM - ROT_DIM), F32)
    zh = jnp.zeros((n, half), F32)
    c = jnp.concatenate([cos, cos, ones], axis=1)
    a = jnp.concatenate([-sin, zh, zeros], axis=1)
    s = jnp.concatenate([zh, sin, zeros], axis=1)
    rep = LANES // HEAD_DIM
    return jnp.tile(c, (1, rep)), jnp.tile(a, (1, rep)), jnp.tile(s, (1, rep))


def kernel(x_prompt, x_sample, cache_attn_meta_k, cache_attn_meta_v, cache_attn_win_k, cache_attn_win_v, state_conv, meta_tokens, attn_w_qkv, attn_b_qkv, attn_sinks, attn_w_o, attn_b_o, conv_w_pw1, conv_b_pw1, conv_w_dw, conv_b_dw, conv_ln_g, conv_ln_b, conv_w_pw2, conv_b_pw2, ln_mix_g, ln_mix_b, ln_ffn_g, ln_ffn_b, moe_w_router, moe_b_router, moe_w_gate_up, moe_b_gate_up, moe_w_down, moe_b_down):
    n_batch, seq, _ = x_prompt.shape
    dec_batch, dec_seq, _ = x_sample.shape
    n_real = n_batch * seq
    n_meta = n_batch * META_LEN
    n_samp = dec_batch * dec_seq
    n_small = n_meta + n_samp
    tb = _pick_tile(seq, 512)
    td = _pick_tile(seq, 1024)
    ts = _pick_tile(n_small, 512)
    row2 = lambda v: v.reshape(1, -1)

    meta_rows = jnp.broadcast_to(meta_tokens[None], (n_batch, META_LEN, D_MODEL)).reshape(n_meta, D_MODEL)
    xb = x_prompt.reshape(n_real, D_MODEL)
    xs = jnp.concatenate([meta_rows.astype(F32), x_sample.reshape(n_samp, D_MODEL)], axis=0)

    def moe_layer(i, a_big, a_small, w, b, xb, xs):
        w_bf = w.astype(BF16)
        wr = moe_w_router[i].T.astype(BF16)
        br = moe_b_router[i].reshape(N_EXPERTS, 1)
        lng, lnb = row2(ln_mix_g[i]), row2(ln_mix_b[i])
        zero_counts = jnp.zeros((N_EXPERTS, LANES), jnp.int32)
        x1b, idb, gab, rab, cnt = _proj_ln_route(a_big, w_bf, row2(b), xb, lng, lnb, wr, br, zero_counts, td)
        x1s, ids_, gas, ras, cnt = _proj_ln_route(a_small, w_bf, row2(b), xs, lng, lnb, wr, br, cnt, ts)
        n_all = moe_w_gate_up.shape[0] * N_EXPERTS
        return _moe([(x1b, idb, gab, rab, tb), (x1s, ids_, gas, ras, ts)], cnt, i * N_EXPERTS,
                    moe_w_gate_up.reshape(n_all, D_MODEL, 2 * MOE_FF),
                    moe_b_gate_up.reshape(n_all, 1, 2 * MOE_FF),
                    moe_w_down.reshape(n_all, MOE_FF, D_MODEL),
                    moe_b_down.reshape(n_all, 1, D_MODEL),
                    ln_ffn_g[i].reshape(N_SLAB, LANES), ln_ffn_b[i].reshape(N_SLAB, LANES))

    w_qkv = attn_w_qkv[0].astype(BF16)
    b_qkv = row2(attn_b_qkv[0])
    pos_small = jnp.concatenate([jnp.tile(jnp.arange(META_LEN), n_batch),
                                 jnp.tile(PAST_LEN + jnp.arange(dec_seq), dec_batch)])
    pos_big = META_LEN + jnp.arange(seq)
    qtb, kb, vb, vtb = _qkv_rope_t(xb, w_qkv, b_qkv, *_rope_tables(pos_big), *_rope_tables_t(pos_big), td)
    qs, ks, vs = _qkv_rope(xs, w_qkv, b_qkv, *_rope_tables(pos_small), ts)
    sinks = attn_sinks[0]
    cmk = cache_attn_meta_k[0].reshape(dec_batch, META_LEN, KV_DIM)
    cmv = cache_attn_meta_v[0].reshape(dec_batch, META_LEN, KV_DIM)
    cwk = cache_attn_win_k[0].reshape(dec_batch, WINDOW, KV_DIM)
    cwv = cache_attn_win_v[0].reshape(dec_batch, WINDOW, KV_DIM)
    vt_meta = vs[:n_meta].reshape(n_batch, META_LEN, KV_DIM).transpose(0, 2, 1).astype(BF16)
    o_big = _attn_prompt(qtb, kb, vtb, ks, vt_meta, sinks, n_batch, seq)
    o_small = jnp.concatenate([
        _attn_meta(qs, ks, vs, sinks, n_batch),
        _attn_sample(qs, ks, vs, cmk, cmv, cwk, cwv, sinks, n_meta, dec_batch, dec_seq)], axis=0)
    xb, xs = moe_layer(0, o_big, o_small, attn_w_o[0], attn_b_o[0], xb, xs)

    kv4 = lambda a, n, t_len: a.reshape(n, t_len, N_KV_HEADS, HEAD_DIM)
    p_meta_k = kv4(ks[:n_meta], n_batch, META_LEN)[None]
    p_meta_v = kv4(vs[:n_meta], n_batch, META_LEN)[None]
    last_win = lambda a: a.reshape(n_batch, seq, KV_DIM)[:, seq - WINDOW:].reshape(n_batch * WINDOW, KV_DIM)
    p_win_k = kv4(last_win(kb), n_batch, WINDOW)[None]
    p_win_v = kv4(last_win(vb), n_batch, WINDOW)[None]
    k_new = kv4(ks[n_meta:], dec_batch, dec_seq)
    v_new = kv4(vs[n_meta:], dec_batch, dec_seq)
    s_win_k = jnp.concatenate([cache_attn_win_k[0], k_new], axis=1)[:, -WINDOW:][None]
    s_win_v = jnp.concatenate([cache_attn_win_v[0], v_new], axis=1)[:, -WINDOW:][None]

    w_pw1 = conv_w_pw1[0].astype(BF16)
    ub = _pw1_glu(xb, w_pw1, row2(conv_b_pw1[0]), td)
    us = _pw1_glu(xs, w_pw1, row2(conv_b_pw1[0]), ts)
    slab = lambda v: v.reshape(N_SLAB, LANES)
    w_dw = jnp.pad(conv_w_dw[0], ((0, CONV_HALO - CONV_WIDTH), (0, 0)))
    w3 = w_dw.reshape(CONV_HALO, N_SLAB, LANES).transpose(1, 0, 2)
    conv_args = (w3, slab(conv_b_dw[0]), slab(conv_ln_g[0]), slab(conv_ln_b[0]))
    z_big = _conv_prompt(ub, us, *conv_args, n_batch, seq, tb)
    z_meta = _conv_short(jnp.zeros((n_batch, CONV_HALO, D_MODEL), F32), us, 0, *conv_args, META_LEN, 1)
    ctx_pad = jnp.pad(state_conv[0], ((0, 0), (CONV_HALO - CONV_CTX, 0), (0, 0)))
    z_samp = _conv_short(ctx_pad, us, n_meta, *conv_args, dec_seq, 2)
    z_small = jnp.concatenate([z_meta, z_samp], axis=0).astype(BF16)
    xb, xs = moe_layer(1, z_big, z_small, conv_w_pw2[0], conv_b_pw2[0], xb, xs)

    p_conv = ub.reshape(n_batch, seq, D_MODEL)[:, seq - CONV_CTX:][None]
    u_samp = us[n_meta:].reshape(dec_batch, dec_seq, D_MODEL)
    s_conv = jnp.concatenate([state_conv[0], u_samp], axis=1)[:, -CONV_CTX:][None]

    y_prompt = xb.reshape(n_batch, seq, D_MODEL)
    y_sample = xs[n_meta:].reshape(dec_batch, dec_seq, D_MODEL)
    return (y_prompt, y_sample, p_meta_k, p_meta_v, p_win_k, p_win_v, p_conv,
            s_win_k, s_win_v, s_conv)
```

```python
import functools

import numpy as np
import jax
import jax.numpy as jnp
from jax import lax
from jax.experimental import pallas as pl
from jax.experimental.pallas import tpu as pltpu

F32 = jnp.float32
BF16 = jnp.bfloat16

D_MODEL = 1024
HEAD_DIM = 64
N_HEADS = 16
N_KV_HEADS = 4
GROUP = N_HEADS // N_KV_HEADS
KV_DIM = N_KV_HEADS * HEAD_DIM
QKV_DIM = D_MODEL + 2 * KV_DIM
ROT_DIM = 16
ROPE_THETA = 500000.0
WINDOW = 128
ATTN_BLOCK = 128
ATTN_SCALE = HEAD_DIM ** -0.5
LOG2_E = 1.4426950408889634
Q_SCALE = ATTN_SCALE * LOG2_E
META_LEN = 16
CONV_WIDTH = 31
CONV_CTX = CONV_WIDTH - 1
N_EXPERTS = 32
TOP_K = 4
MOE_FF = 1024
SWIGLU_LIMIT = 7.0
SWIGLU_ALPHA = 1.702
LN_EPS = 1e-5
DEPTH = 2
DEEPNORM_ALPHA = (2 * DEPTH) ** 0.25
PAST_LEN = 16384
NEG_INF = -1e30

LANES = 128
SUBLANES = 8
GMM_TILE = 512
CONV_HALO = 32
CONV_BLOCK = 32
LN_ROWS = 16
N_SLAB = D_MODEL // LANES
COMBINE_ROWS = 32
SUM_GROUP = 16
IDX_SLOTS = 3
HEADS_PER_DOT = 4
VMEM_LIMIT = 56 * 1024 * 1024


def _cparams(n_axes):
    return pltpu.CompilerParams(dimension_semantics=("arbitrary",) * n_axes,
                                vmem_limit_bytes=VMEM_LIMIT)


def _pick_tile(n, cap):
    best = None
    t = LANES
    while t <= cap:
        if n % t == 0:
            best = t
        t += LANES
    assert best is not None, n
    return best


def _layer_norm(h, g, b):
    mu = jnp.mean(h, axis=-1, keepdims=True)
    hc = h - mu
    var = jnp.mean(hc * hc, axis=-1, keepdims=True)
    return hc * lax.rsqrt(var + LN_EPS) * g + b


def _qkv_kernel(x_ref, w_ref, b_ref, c_ref, a_ref, s_ref, q_ref, k_ref, v_ref):
    x = x_ref[...].astype(BF16)
    acc = jnp.dot(x, w_ref[...], preferred_element_type=F32) + b_ref[...]
    c = c_ref[...]
    a = a_ref[...]
    s = s_ref[...]

    def rope(t):
        return t * c + pltpu.roll(t, LANES - ROT_DIM // 2, 1) * a + pltpu.roll(t, ROT_DIM // 2, 1) * s

    for j in range(D_MODEL // LANES):
        sl = slice(j * LANES, (j + 1) * LANES)
        q_ref[:, sl] = (rope(acc[:, sl]) * Q_SCALE).astype(BF16)
    for j in range(KV_DIM // LANES):
        sl = slice(D_MODEL + j * LANES, D_MODEL + (j + 1) * LANES)
        k_ref[:, j * LANES:(j + 1) * LANES] = rope(acc[:, sl])
    v_ref[...] = acc[:, D_MODEL + KV_DIM:]


def _qkv_rope(x, w_bf, b, cos_t, sa_t, sb_t, tm):
    t = x.shape[0]
    period = cos_t.shape[0] // tm
    row = lambda i: (i, 0)
    tab = lambda i: (i % period, 0)
    fixed = lambda i: (0, 0)
    return pl.pallas_call(
        _qkv_kernel,
        grid=(t // tm,),
        in_specs=[pl.BlockSpec((tm, D_MODEL), row),
                  pl.BlockSpec((D_MODEL, QKV_DIM), fixed),
                  pl.BlockSpec((1, QKV_DIM), fixed),
                  pl.BlockSpec((tm, LANES), tab),
                  pl.BlockSpec((tm, LANES), tab),
                  pl.BlockSpec((tm, LANES), tab)],
        out_specs=[pl.BlockSpec((tm, D_MODEL), row),
                   pl.BlockSpec((tm, KV_DIM), row),
                   pl.BlockSpec((tm, KV_DIM), row)],
        out_shape=[jax.ShapeDtypeStruct((t, D_MODEL), BF16),
                   jax.ShapeDtypeStruct((t, KV_DIM), F32),
                   jax.ShapeDtypeStruct((t, KV_DIM), F32)],
        compiler_params=_cparams(1),
        name="qkv_rope",
    )(x, w_bf, b, cos_t, sa_t, sb_t)


def _qkv_t_kernel(x_ref, wqt_ref, bq_ref, wkv_ref, bkv_ref, wvt_ref, bv_ref, c_ref, a_ref, s_ref,
                  ct_ref, st_ref, qt_ref, k_ref, v_ref, vt_ref):
    x = x_ref[...].astype(BF16)
    nt = (((1,), (1,)), ((), ()))
    qt = lax.dot_general(wqt_ref[...], x, nt, preferred_element_type=F32) + bq_ref[...]
    ct = ct_ref[...]
    st = st_ref[...]
    half = ROT_DIM // 2
    for h in range(N_HEADS):
        r0 = h * HEAD_DIM
        x1 = qt[r0:r0 + half]
        x2 = qt[r0 + half:r0 + ROT_DIM]
        rot = jnp.concatenate([x1 * ct - x2 * st, x2 * ct + x1 * st], axis=0)
        qt_ref[r0:r0 + ROT_DIM, :] = (rot * Q_SCALE).astype(BF16)
        qt_ref[r0 + ROT_DIM:r0 + HEAD_DIM, :] = (qt[r0 + ROT_DIM:r0 + HEAD_DIM] * Q_SCALE).astype(BF16)
    vt = lax.dot_general(wvt_ref[...], x, nt, preferred_element_type=F32) + bv_ref[...]
    vt_ref[...] = vt.astype(BF16)

    kv = jnp.dot(x, wkv_ref[...], preferred_element_type=F32) + bkv_ref[...]
    c = c_ref[...]
    a = a_ref[...]
    s = s_ref[...]
    for j in range(KV_DIM // LANES):
        t = kv[:, j * LANES:(j + 1) * LANES]
        k_ref[:, j * LANES:(j + 1) * LANES] = (
            t * c + pltpu.roll(t, LANES - half, 1) * a + pltpu.roll(t, half, 1) * s)
    v_ref[...] = kv[:, KV_DIM:]


def _qkv_rope_t(x, w_bf, b, cos_t, sa_t, sb_t, cos_tt, sin_tt, tm):
    t = x.shape[0]
    period = cos_t.shape[0] // tm
    row = lambda i: (i, 0)
    col = lambda i: (0, i)
    tab = lambda i: (i % period, 0)
    tabt = lambda i: (0, i % period)
    fixed = lambda i: (0, 0)
    wqt = w_bf[:, :D_MODEL].T
    wvt = w_bf[:, D_MODEL + KV_DIM:].T
    half = ROT_DIM // 2
    return pl.pallas_call(
        _qkv_t_kernel,
        grid=(t // tm,),
        in_specs=[pl.BlockSpec((tm, D_MODEL), row),
                  pl.BlockSpec((D_MODEL, D_MODEL), fixed),
                  pl.BlockSpec((D_MODEL, 1), fixed),
                  pl.BlockSpec((D_MODEL, 2 * KV_DIM), fixed),
                  pl.BlockSpec((1, 2 * KV_DIM), fixed),
                  pl.BlockSpec((KV_DIM, D_MODEL), fixed),
                  pl.BlockSpec((KV_DIM, 1), fixed),
                  pl.BlockSpec((tm, LANES), tab),
                  pl.BlockSpec((tm, LANES), tab),
                  pl.BlockSpec((tm, LANES), tab),
                  pl.BlockSpec((half, tm), tabt),
                  pl.BlockSpec((half, tm), tabt)],
        out_specs=[pl.BlockSpec((D_MODEL, tm), col),
                   pl.BlockSpec((tm, KV_DIM), row),
                   pl.BlockSpec((tm, KV_DIM), row),
                   pl.BlockSpec((KV_DIM, tm), col)],
        out_shape=[jax.ShapeDtypeStruct((D_MODEL, t), BF16),
                   jax.ShapeDtypeStruct((t, KV_DIM), F32),
                   jax.ShapeDtypeStruct((t, KV_DIM), F32),
                   jax.ShapeDtypeStruct((KV_DIM, t), BF16)],
        compiler_params=_cparams(1),
        name="qkv_rope_t",
    )(x, wqt, b[:, :D_MODEL].reshape(D_MODEL, 1), w_bf[:, D_MODEL:], b[:, D_MODEL:],
      wvt, b[:, D_MODEL + KV_DIM:].reshape(KV_DIM, 1), cos_t, sa_t, sb_t, cos_tt, sin_tt)


def _attend(q, kcat, vcat, bias, sink_ref, write):
    kgs = [kcat[:, g * HEAD_DIM:(g + 1) * HEAD_DIM] for g in range(N_KV_HEADS)]

    def scores(h):
        qh = q[:, h * HEAD_DIM:(h + 1) * HEAD_DIM]
        return lax.dot_general(qh, kgs[h // GROUP], (((1,), (1,)), ((), ())), preferred_element_type=F32) + bias

    s_next = scores(0)
    for h in range(N_HEADS):
        s = s_next
        if h + 1 < N_HEADS:
            s_next = scores(h + 1)
        vg = vcat[:, (h // GROUP) * HEAD_DIM:(h // GROUP + 1) * HEAD_DIM]
        sink = sink_ref[h] * LOG2_E
        m = jnp.maximum(jnp.max(s, axis=-1, keepdims=True), sink)
        p = jnp.exp2(s - m)
        denom = jnp.sum(p, axis=-1, keepdims=True) + jnp.exp2(sink - m)
        o = jnp.dot(p.astype(BF16), vg, preferred_element_type=F32)
        write(h, o / denom)


def _attn_prompt_kernel(sink_ref, qt_ref, kp_ref, ko_ref, km_ref, vtp_ref, vto_ref, vtm_ref, o_ref, acc_ref):
    i = pl.program_id(1)
    kcat = jnp.concatenate([kp_ref[...], ko_ref[...], km_ref[...]], axis=0).astype(BF16)
    vt = jnp.concatenate([vtp_ref[...], vto_ref[...], vtm_ref[...]], axis=1)
    nk = 2 * ATTN_BLOCK + META_LEN
    key = lax.broadcasted_iota(jnp.int32, (nk, ATTN_BLOCK), 0)
    qry = lax.broadcasted_iota(jnp.int32, (nk, ATTN_BLOCK), 1)
    first = jnp.where(i > 0, 0, 2 * ATTN_BLOCK)
    prev_ok = (key < ATTN_BLOCK) & (key >= qry + first)
    own_ok = (key >= ATTN_BLOCK) & (key - ATTN_BLOCK <= qry)
    mask = prev_ok | own_ok | (key >= 2 * ATTN_BLOCK)
    bias = jnp.where(mask, 0.0, NEG_INF)
    bias = jnp.concatenate([bias] * HEADS_PER_DOT, axis=1)
    lane = lax.broadcasted_iota(jnp.int32, (1, HEADS_PER_DOT * ATTN_BLOCK), 1)
    kgs = [kcat[:, g * HEAD_DIM:(g + 1) * HEAD_DIM] for g in range(N_KV_HEADS)]

    def scores(u):
        h0 = u * HEADS_PER_DOT
        rhs = jnp.concatenate([qt_ref[(h0 + j) * HEAD_DIM:(h0 + j + 1) * HEAD_DIM, :]
                               for j in range(HEADS_PER_DOT)], axis=1)
        return jnp.dot(kgs[h0 // GROUP], rhs, preferred_element_type=F32) + bias

    n_units = N_HEADS // HEADS_PER_DOT
    s_next = scores(0)
    for u in range(n_units):
        s = s_next
        if u + 1 < n_units:
            s_next = scores(u + 1)
        h0 = u * HEADS_PER_DOT
        g = h0 // GROUP
        sink = sink_ref[h0] * LOG2_E
        for j in range(1, HEADS_PER_DOT):
            sink = jnp.where(lane >= j * ATTN_BLOCK, sink_ref[h0 + j] * LOG2_E, sink)
        m = jnp.maximum(jnp.max(s, axis=0, keepdims=True), sink)
        p = jnp.exp2(s - m)
        denom = jnp.sum(p, axis=0, keepdims=True) + jnp.exp2(sink - m)
        o = jnp.dot(vt[g * HEAD_DIM:(g + 1) * HEAD_DIM, :], p.astype(BF16), preferred_element_type=F32)
        o = o * (1.0 / denom)
        for j in range(HEADS_PER_DOT):
            acc_ref[(h0 + j) * HEAD_DIM:(h0 + j + 1) * HEAD_DIM, :] = o[:, j * ATTN_BLOCK:(j + 1) * ATTN_BLOCK]
    o_ref[...] = acc_ref[...].T.astype(o_ref.dtype)


def _attn_prompt(qt, k, vt, k_meta, vt_meta, sinks, n_batch, seq):
    nblk = seq // ATTN_BLOCK
    own = lambda b, i: b * nblk + i
    prev = lambda b, i: b * nblk + jnp.maximum(i - 1, 0)
    kspec = lambda m: pl.BlockSpec((ATTN_BLOCK, KV_DIM), lambda b, i: (m(b, i), 0))
    vspec = lambda m: pl.BlockSpec((KV_DIM, ATTN_BLOCK), lambda b, i: (0, m(b, i)))
    return pl.pallas_call(
        _attn_prompt_kernel,
        grid=(n_batch, nblk),
        in_specs=[pl.BlockSpec(memory_space=pltpu.SMEM),
                  pl.BlockSpec((D_MODEL, ATTN_BLOCK), lambda b, i: (0, own(b, i))),
                  kspec(prev), kspec(own),
                  pl.BlockSpec((META_LEN, KV_DIM), lambda b, i: (b, 0)),
                  vspec(prev), vspec(own),
                  pl.BlockSpec((None, KV_DIM, META_LEN), lambda b, i: (b, 0, 0))],
        out_specs=pl.BlockSpec((ATTN_BLOCK, D_MODEL), lambda b, i: (own(b, i), 0)),
        out_shape=jax.ShapeDtypeStruct((n_batch * seq, D_MODEL), BF16),
        scratch_shapes=[pltpu.VMEM((D_MODEL, ATTN_BLOCK), F32)],
        compiler_params=_cparams(2),
        name="attn_prompt",
    )(sinks, qt, k, k, k_meta, vt, vt, vt_meta)


def _attn_meta_kernel(sink_ref, q_ref, k_ref, v_ref, o_ref):
    r = lax.broadcasted_iota(jnp.int32, (META_LEN, META_LEN), 0)
    c = lax.broadcasted_iota(jnp.int32, (META_LEN, META_LEN), 1)

    def write(h, o):
        o_ref[:, h * HEAD_DIM:(h + 1) * HEAD_DIM] = o.astype(o_ref.dtype)

    bias = jnp.where(c <= r, 0.0, NEG_INF)
    _attend(q_ref[...], k_ref[...].astype(BF16), v_ref[...].astype(BF16), bias, sink_ref, write)


def _attn_meta(q, k, v, sinks, n_batch):
    imap = lambda b: (b, 0)
    return pl.pallas_call(
        _attn_meta_kernel,
        grid=(n_batch,),
        in_specs=[pl.BlockSpec(memory_space=pltpu.SMEM),
                  pl.BlockSpec((META_LEN, D_MODEL), imap),
                  pl.BlockSpec((META_LEN, KV_DIM), imap),
                  pl.BlockSpec((META_LEN, KV_DIM), imap)],
        out_specs=pl.BlockSpec((META_LEN, D_MODEL), lambda b: (b, 0)),
        out_shape=jax.ShapeDtypeStruct((n_batch * META_LEN, D_MODEL), BF16),
        compiler_params=_cparams(1),
        name="attn_meta",
    )(sinks, q, k, v)


SAMPLE_GROUP = 8


def _attn_sample_kernel(sink_ref, bias_ref, q_ref, kn_ref, vn_ref, cmk_ref, cmv_ref, cwk_ref, cwv_ref,
                        o_ref, *, dec_seq, group):
    def keys(cm_ref, cw_ref, new):
        parts = []
        for j in range(group):
            parts += [cm_ref[j], cw_ref[j], new[j * dec_seq:(j + 1) * dec_seq]]
        return jnp.concatenate(parts, axis=0).astype(BF16)

    kcat = keys(cmk_ref, cwk_ref, kn_ref[...])
    vcat = keys(cmv_ref, cwv_ref, vn_ref[...])

    def write(h, o):
        o_ref[:, h * HEAD_DIM:(h + 1) * HEAD_DIM] = o.astype(o_ref.dtype)

    _attend(q_ref[...], kcat, vcat, bias_ref[...], sink_ref, write)


def _sample_bias(group, dec_seq):
    per = META_LEN + WINDOW + dec_seq
    r = np.arange(group * dec_seq)[:, None]
    c = np.arange(group * per)[None, :]
    tq, ck = r % dec_seq, c % per
    win_ok = (ck >= META_LEN) & (ck < META_LEN + WINDOW) & (ck - META_LEN >= tq)
    new_ok = (ck >= META_LEN + WINDOW) & (ck - (META_LEN + WINDOW) <= tq)
    vis = (r // dec_seq == c // per) & ((ck < META_LEN) | win_ok | new_ok)
    return np.where(vis, 0.0, NEG_INF).astype(np.float32)


def _attn_sample(q, k, v, cmk, cmv, cwk, cwv, sinks, row0, dec_batch, dec_seq):
    group = SAMPLE_GROUP
    nq = group * dec_seq
    blk0 = row0 // nq
    bias = jnp.asarray(_sample_bias(group, dec_seq))
    qmap = lambda n: (blk0 + n, 0)
    cmap = lambda n: (n, 0, 0)
    return pl.pallas_call(
        functools.partial(_attn_sample_kernel, dec_seq=dec_seq, group=group),
        grid=(dec_batch // group,),
        in_specs=[pl.BlockSpec(memory_space=pltpu.SMEM),
                  pl.BlockSpec(bias.shape, lambda n: (0, 0)),
                  pl.BlockSpec((nq, D_MODEL), qmap),
                  pl.BlockSpec((nq, KV_DIM), qmap),
                  pl.BlockSpec((nq, KV_DIM), qmap),
                  pl.BlockSpec((group, META_LEN, KV_DIM), cmap),
                  pl.BlockSpec((group, META_LEN, KV_DIM), cmap),
                  pl.BlockSpec((group, WINDOW, KV_DIM), cmap),
                  pl.BlockSpec((group, WINDOW, KV_DIM), cmap)],
        out_specs=pl.BlockSpec((nq, D_MODEL), lambda n: (n, 0)),
        out_shape=jax.ShapeDtypeStruct((dec_batch * dec_seq, D_MODEL), BF16),
        compiler_params=_cparams(1),
        name="attn_sample",
    )(sinks, bias, q, k, v, cmk, cmv, cwk, cwv)


def _proj_ln_route_kernel(a_ref, w_ref, b_ref, x_ref, g_ref, bb_ref, wr_ref, br_ref, tri_ref, cin_ref,
                          x1_ref, ids_ref, gates_ref, rank_ref, cnt_ref, carry_ref):
    @pl.when(pl.program_id(0) == 0)
    def _():
        carry_ref[...] = cin_ref[...].astype(F32)

    y = jnp.dot(a_ref[...], w_ref[...], preferred_element_type=F32) + b_ref[...]
    x1 = _layer_norm(DEEPNORM_ALPHA * x_ref[...] + y, g_ref[...], bb_ref[...])
    for j in range(N_SLAB):
        x1_ref[pl.ds(j, x1.shape[0], stride=N_SLAB), :] = x1[:, j * LANES:(j + 1) * LANES]

    logits = lax.dot_general(wr_ref[...], x1.astype(BF16), (((1,), (1,)), ((), ())),
                             preferred_element_type=F32) + br_ref[...]
    tm = logits.shape[1]
    eidx = lax.broadcasted_iota(jnp.int32, (N_EXPERTS, tm), 0).astype(F32)
    cur = logits
    vals, idxs, sels = [], [], []
    for _ in range(TOP_K):
        m = jnp.max(cur, axis=0, keepdims=True)
        idx = jnp.min(jnp.where(cur == m, eidx, float(N_EXPERTS)), axis=0, keepdims=True)
        sel = eidx == idx
        vals.append(m)
        idxs.append(idx)
        sels.append(sel)
        cur = jnp.where(sel, -jnp.inf, cur)
    exps = [jnp.exp(v - vals[0]) for v in vals]
    tot = exps[0] + exps[1] + exps[2] + exps[3]
    gates_ref[...] = jnp.concatenate([e / tot for e in exps], axis=0)
    ids_ref[...] = jnp.concatenate(idxs, axis=0).astype(jnp.int32)

    chosen = jnp.where(sels[0] | sels[1] | sels[2] | sels[3], 1.0, 0.0)
    before = jnp.dot(chosen.astype(BF16), tri_ref[...], preferred_element_type=F32)
    before = before + carry_ref[:, 0:1]
    ranks = [jnp.sum(jnp.where(s, before, 0.0), axis=0, keepdims=True) for s in sels]
    rank_ref[...] = jnp.concatenate(ranks, axis=0).astype(jnp.int32)
    carry_ref[...] = carry_ref[...] + jnp.sum(chosen, axis=1, keepdims=True)
    cnt_ref[...] = carry_ref[...].astype(jnp.int32)


def _proj_ln_route(a_bf, w_bf, b, x, g, bb, wr_t_bf, br_col, counts_in, tm):
    t = x.shape[0]
    row = lambda i: (i, 0)
    col = lambda i: (0, i)
    fixed = lambda i: (0, 0)
    tri = (jnp.arange(tm)[:, None] < jnp.arange(tm)[None, :]).astype(BF16)
    return pl.pallas_call(
        _proj_ln_route_kernel,
        grid=(t // tm,),
        in_specs=[pl.BlockSpec((tm, D_MODEL), row),
                  pl.BlockSpec((D_MODEL, D_MODEL), fixed),
                  pl.BlockSpec((1, D_MODEL), fixed),
                  pl.BlockSpec((tm, D_MODEL), row),
                  pl.BlockSpec((1, D_MODEL), fixed),
                  pl.BlockSpec((1, D_MODEL), fixed),
                  pl.BlockSpec((N_EXPERTS, D_MODEL), fixed),
                  pl.BlockSpec((N_EXPERTS, 1), fixed),
                  pl.BlockSpec((tm, tm), fixed),
                  pl.BlockSpec((N_EXPERTS, LANES), fixed)],
        out_specs=[pl.BlockSpec((tm * N_SLAB, LANES), row),
                   pl.BlockSpec((TOP_K, tm), col),
                   pl.BlockSpec((TOP_K, tm), col),
                   pl.BlockSpec((TOP_K, tm), col),
                   pl.BlockSpec((N_EXPERTS, LANES), fixed)],
        out_shape=[jax.ShapeDtypeStruct((t * N_SLAB, LANES), F32),
                   jax.ShapeDtypeStruct((TOP_K, t), jnp.int32),
                   jax.ShapeDtypeStruct((TOP_K, t), F32),
                   jax.ShapeDtypeStruct((TOP_K, t), jnp.int32),
                   jax.ShapeDtypeStruct((N_EXPERTS, LANES), jnp.int32)],
        scratch_shapes=[pltpu.VMEM((N_EXPERTS, LANES), F32)],
        compiler_params=_cparams(1),
        name="proj_ln_route",
    )(a_bf, w_bf, b, x, g, bb, wr_t_bf, br_col, tri, counts_in)


def _idx_copy(dest_hbm, dsm, sem, step, stride, n_slots=2):
    slot = step % n_slots
    return pltpu.make_async_copy(dest_hbm.at[pl.ds(pl.multiple_of(step * stride, stride), stride)],
                                 dsm.at[pl.ds(pl.multiple_of(slot * stride, stride), stride)],
                                 sem)


def _dispatch_kernel(pstart_ref, padded_ref, nact_ref, dest_hbm, *rest, tiles, steps, stride, n_tiles):
    x_refs = rest[:len(tiles)]
    xs_hbm, dsm, zbuf, sems = rest[len(tiles):]
    i = pl.program_id(0)
    nsteps = pl.num_programs(0)

    def zero_tile(row0):
        n = GMM_TILE * N_SLAB
        return pltpu.make_async_copy(zbuf, xs_hbm.at[pl.ds(pl.multiple_of(row0 * N_SLAB, n), n)], sems.at[2])

    def token(ref, r):
        return ref.at[pl.ds(pl.multiple_of(r * N_SLAB, N_SLAB), N_SLAB)]

    @pl.when(i == 0)
    def _():
        zbuf[...] = jnp.zeros_like(zbuf)

        def each_expert(fn):
            def body(e, _):
                @pl.when(padded_ref[e] > 0)
                def _():
                    fn(zero_tile(pstart_ref[e] + padded_ref[e] - GMM_TILE))
                return 0
            lax.fori_loop(0, N_EXPERTS, body, 0)

        def each_tail(fn):
            def body(t, _):
                fn(zero_tile(t * GMM_TILE))
                return 0
            lax.fori_loop(nact_ref[0], n_tiles, body, 0)

        each_expert(lambda cp: cp.start())
        each_tail(lambda cp: cp.start())
        each_expert(lambda cp: cp.wait())
        each_tail(lambda cp: cp.wait())
        _idx_copy(dest_hbm, dsm, sems.at[0], i, stride).start()

    _idx_copy(dest_hbm, dsm, sems.at[0], i, stride).wait()

    @pl.when(i + 1 < nsteps)
    def _():
        _idx_copy(dest_hbm, dsm, sems.at[0], i + 1, stride).start()

    base = (i % 2) * stride
    step0 = 0
    for x_ref, tm, n in zip(x_refs, tiles, steps):
        @pl.when((i >= step0) & (i < step0 + n))
        def _(x_ref=x_ref, tm=tm):
            def row_body(r, _):
                for k in range(TOP_K):
                    d = dsm[base + k * tm + r]
                    pltpu.make_async_copy(token(x_ref, r), token(xs_hbm, d), sems.at[1]).start(priority=k % 2)
                return 0

            lax.fori_loop(0, tm, row_body, 0, unroll=4)
            for k in range(TOP_K):
                pltpu.make_async_copy(x_ref, xs_hbm.at[pl.ds(0, tm * N_SLAB)], sems.at[1]).wait()
        step0 += n


def _seg_map(step0, n):
    return lambda i, *_: (jnp.clip(i - step0, 0, n - 1), 0)


def _dispatch(xs_list, tiles, dest_steps, pstart, padded, nact, stride, n_tiles):
    steps = [x.shape[0] // (tm * N_SLAB) for x, tm in zip(xs_list, tiles)]
    in_specs = [pl.BlockSpec(memory_space=pl.ANY)]
    step0 = 0
    for tm, n in zip(tiles, steps):
        in_specs.append(pl.BlockSpec((tm * N_SLAB, LANES), _seg_map(step0, n)))
        step0 += n
    grid_spec = pltpu.PrefetchScalarGridSpec(
        num_scalar_prefetch=3,
        grid=(sum(steps),),
        in_specs=in_specs,
        out_specs=pl.BlockSpec(memory_space=pl.ANY),
        scratch_shapes=[pltpu.SMEM((2 * stride,), jnp.int32),
                        pltpu.VMEM((GMM_TILE * N_SLAB, LANES), F32),
                        pltpu.SemaphoreType.DMA((3,))],
    )
    return pl.pallas_call(
        functools.partial(_dispatch_kernel, tiles=tuple(tiles), steps=tuple(steps), stride=stride,
                          n_tiles=n_tiles),
        grid_spec=grid_spec,
        out_shape=jax.ShapeDtypeStruct((n_tiles * GMM_TILE * N_SLAB, LANES), F32),
        compiler_params=_cparams(1),
        name="moe_dispatch",
    )(pstart, padded, nact, dest_steps, *xs_list)


def _gmm_kernel(te_ref, tsrc_ref, tfirst_ref, nact_ref, x_ref, wgu_ref, bgu_ref, wdn_ref, bdn_ref, y_ref,
                wgu_bf, wdn_bf):
    i = pl.program_id(0)

    @pl.when(tfirst_ref[i] == 1)
    def _():
        wgu_bf[...] = wgu_ref[...].astype(BF16)
        wdn_bf[...] = wdn_ref[...].astype(BF16)

    @pl.when(i < nact_ref[0])
    def _():
        x = jnp.concatenate([x_ref[pl.ds(j, GMM_TILE, stride=N_SLAB), :] for j in range(N_SLAB)], axis=1)
        gu = jnp.dot(x.astype(BF16), wgu_bf[...], preferred_element_type=F32) + bgu_ref[...]
        gate = jnp.minimum(gu[:, :MOE_FF], SWIGLU_LIMIT)
        up = jnp.clip(gu[:, MOE_FF:], -SWIGLU_LIMIT, SWIGLU_LIMIT)
        glu = gate * jax.nn.sigmoid(SWIGLU_ALPHA * gate)
        h = ((up + 1.0) * glu).astype(BF16)
        y = jnp.dot(h, wdn_bf[...], preferred_element_type=F32) + bdn_ref[...]
        for j in range(N_SLAB):
            y_ref[pl.ds(j, GMM_TILE, stride=N_SLAB), :] = y[:, j * LANES:(j + 1) * LANES]

    @pl.when(i >= nact_ref[0])
    def _():
        y_ref[...] = jnp.zeros_like(y_ref)


def _gmm(xs, te, tsrc, tfirst, nact, wgu, bgu, wdn, bdn, n_tiles):
    emap = lambda i, te, *_: (te[i], 0, 0)
    grid_spec = pltpu.PrefetchScalarGridSpec(
        num_scalar_prefetch=4,
        grid=(n_tiles,),
        in_specs=[pl.BlockSpec((GMM_TILE * N_SLAB, LANES), lambda i, te, ts, *_: (ts[i], 0)),
                  pl.BlockSpec((None, D_MODEL, 2 * MOE_FF), emap),
                  pl.BlockSpec((None, 1, 2 * MOE_FF), emap),
                  pl.BlockSpec((None, MOE_FF, D_MODEL), emap),
                  pl.BlockSpec((None, 1, D_MODEL), emap)],
        out_specs=pl.BlockSpec((GMM_TILE * N_SLAB, LANES), lambda i, *_: (i, 0)),
        scratch_shapes=[pltpu.VMEM((D_MODEL, 2 * MOE_FF), BF16),
                        pltpu.VMEM((MOE_FF, D_MODEL), BF16)],
    )
    return pl.pallas_call(
        _gmm_kernel,
        grid_spec=grid_spec,
        out_shape=jax.ShapeDtypeStruct((n_tiles * GMM_TILE * N_SLAB, LANES), F32),
        compiler_params=_cparams(1),
        name="moe_gmm",
    )(te, tsrc, tfirst, nact, xs, wgu, bgu, wdn, bdn)


def _combine_kernel(dest_hbm, gate_hbm, ys_hbm, x_ref, g_ref, bb_ref, o_ref, dsm, gsm, buf, sems,
                    *, tm, stride):
    i = pl.program_id(0)
    nsteps = pl.num_programs(0)

    def idx_copies(step):
        slot = step % IDX_SLOTS
        return (_idx_copy(dest_hbm, dsm, sems.at[slot], step, stride, IDX_SLOTS),
                _idx_copy(gate_hbm, gsm, sems.at[IDX_SLOTS + slot], step, stride, IDX_SLOTS))

    def token(ref, r):
        return ref.at[pl.ds(pl.multiple_of(r * N_SLAB, N_SLAB), N_SLAB)]

    def issue_rows(step, parity):
        base = (step % IDX_SLOTS) * stride

        def body(r, _):
            for k in range(TOP_K):
                d = dsm[base + k * tm + r]
                pltpu.make_async_copy(token(ys_hbm, d), token(buf.at[parity * TOP_K + k], r),
                                      sems.at[2 * IDX_SLOTS + parity]).start(priority=k % 2)
            return 0

        lax.fori_loop(0, tm, body, 0, unroll=4)

    @pl.when(i == 0)
    def _():
        for cp in idx_copies(0):
            cp.start()

        @pl.when(nsteps > 1)
        def _():
            for cp in idx_copies(1):
                cp.start()

        for cp in idx_copies(0):
            cp.wait()

        issue_rows(0, 0)

    @pl.when(i + 1 < nsteps)
    def _():
        for cp in idx_copies(i + 1):
            cp.wait()

    @pl.when(i + 2 < nsteps)
    def _():
        for cp in idx_copies(i + 2):
            cp.start()

    for parity in range(2):
        @pl.when((i + 1 < nsteps) & ((i + 1) % 2 == parity))
        def _(parity=parity):
            issue_rows(i + 1, parity)

    gbase = (i % IDX_SLOTS) * stride

    def compute(parity):
        half = parity * TOP_K
        for k in range(TOP_K):
            pltpu.make_async_copy(ys_hbm.at[pl.ds(0, tm * N_SLAB)], buf.at[half + k],
                                  sems.at[2 * IDX_SLOTS + parity]).wait()

        def token_sum(g, _):
            r0 = g * SUM_GROUP
            row0 = pl.multiple_of(r0 * N_SLAB, SUM_GROUP * N_SLAB)
            for t in range(SUM_GROUP):
                rows = pl.ds(row0 + t * N_SLAB, N_SLAB)
                h = DEEPNORM_ALPHA * x_ref[rows, :]
                for k in range(TOP_K):
                    h = h + gsm[gbase + r0 + (k * tm + t)] * buf[half + k, rows, :]
                buf[half, rows, :] = h
            return 0

        lax.fori_loop(0, tm // SUM_GROUP, token_sum, 0)

        def block(b, _):
            r0 = pl.multiple_of(b * COMBINE_ROWS, COMBINE_ROWS)
            hs = [buf[half, pl.ds(r0 * N_SLAB + j, COMBINE_ROWS, stride=N_SLAB), :] for j in range(N_SLAB)]
            tot = hs[0]
            for j in range(1, N_SLAB):
                tot = tot + hs[j]
            mu = jnp.sum(tot, axis=1, keepdims=True) * (1.0 / D_MODEL)
            cen = [h - mu for h in hs]
            sq = cen[0] * cen[0]
            for j in range(1, N_SLAB):
                sq = sq + cen[j] * cen[j]
            inv = lax.rsqrt(jnp.sum(sq, axis=1, keepdims=True) * (1.0 / D_MODEL) + LN_EPS)
            for j in range(N_SLAB):
                o_ref[pl.ds(r0, COMBINE_ROWS), j * LANES:(j + 1) * LANES] = (
                    cen[j] * inv * g_ref[j:j + 1, :] + bb_ref[j:j + 1, :])
            return 0

        lax.fori_loop(0, tm // COMBINE_ROWS, block, 0, unroll=4)

    for parity in range(2):
        @pl.when(i % 2 == parity)
        def _(parity=parity):
            compute(parity)


def _combine(ys, dest_steps, gate_steps, x_tiles, g3, bb3, tm, stride):
    t = x_tiles.shape[0] // N_SLAB
    row = lambda i: (i, 0)
    fixed = lambda i: (0, 0)
    return pl.pallas_call(
        functools.partial(_combine_kernel, tm=tm, stride=stride),
        grid=(t // tm,),
        in_specs=[pl.BlockSpec(memory_space=pl.ANY),
                  pl.BlockSpec(memory_space=pl.ANY),
                  pl.BlockSpec(memory_space=pl.ANY),
                  pl.BlockSpec((tm * N_SLAB, LANES), row),
                  pl.BlockSpec((N_SLAB, LANES), fixed),
                  pl.BlockSpec((N_SLAB, LANES), fixed)],
        out_specs=pl.BlockSpec((tm, D_MODEL), row),
        out_shape=jax.ShapeDtypeStruct((t, D_MODEL), F32),
        scratch_shapes=[pltpu.SMEM((IDX_SLOTS * stride,), jnp.int32),
                        pltpu.SMEM((IDX_SLOTS * stride,), F32),
                        pltpu.VMEM((2 * TOP_K, tm * N_SLAB, LANES), F32),
                        pltpu.SemaphoreType.DMA((2 * IDX_SLOTS + 2,))],
        compiler_params=_cparams(1),
        name="moe_combine",
    )(dest_steps, gate_steps, ys, x_tiles, g3, bb3)


def _moe(segs, counts, expert0, wgu, bgu, wdn, bdn, g, bb):
    t = sum(seg[1].shape[1] for seg in segs)
    n_tiles = -(-(t * TOP_K) // GMM_TILE) + N_EXPERTS
    counts = counts[:, 0]
    padded = ((counts + GMM_TILE - 1) // GMM_TILE) * GMM_TILE
    pend = jnp.cumsum(padded)
    pstart = (pend - padded).astype(jnp.int32)
    padded = padded.astype(jnp.int32)
    nact = (pend[-1] // GMM_TILE).astype(jnp.int32).reshape(1)
    tile = jnp.arange(n_tiles, dtype=jnp.int32)
    tsrc = jnp.minimum(tile, jnp.maximum(nact[0] - 1, 0))
    te = jnp.sum((pend[None, :] <= (tsrc * GMM_TILE)[:, None]).astype(jnp.int32), axis=1)
    te = jnp.minimum(te, N_EXPERTS - 1).astype(jnp.int32)
    tfirst = jnp.concatenate([jnp.ones((1,), jnp.int32), (te[1:] != te[:-1]).astype(jnp.int32)])
    eids = jnp.arange(N_EXPERTS, dtype=jnp.int32)[:, None, None]

    stride = -(-(TOP_K * max(seg[4] for seg in segs)) // 1024) * 1024
    def per_step(a, tm):
        n = a.shape[1]
        steps = a.reshape(TOP_K, n // tm, tm).transpose(1, 0, 2).reshape(n // tm, TOP_K * tm)
        return jnp.pad(steps, ((0, 0), (0, stride - TOP_K * tm))).reshape(-1)

    plans = []
    for x1, ids, gates, rank, tm in segs:
        dest = jnp.sum(jnp.where(ids[None] == eids, pstart[:, None, None], 0), axis=0) + rank
        plans.append((per_step(dest, tm), per_step(gates, tm)))
    xs = _dispatch([seg[0] for seg in segs], [seg[4] for seg in segs],
                   jnp.concatenate([p[0] for p in plans]), pstart, padded, nact, stride, n_tiles)
    ys = _gmm(xs, te + expert0, tsrc, tfirst, nact, wgu, bgu, wdn, bdn, n_tiles)
    return [_combine(ys, dest_steps, gate_steps, seg[0], g, bb, seg[4], stride)
            for seg, (dest_steps, gate_steps) in zip(segs, plans)]


def _pw1_glu_kernel(x_ref, w_ref, b_ref, u_ref):
    a = jnp.dot(x_ref[...].astype(BF16), w_ref[...], preferred_element_type=F32) + b_ref[...]
    u_ref[...] = a[:, :D_MODEL] * jax.nn.sigmoid(a[:, D_MODEL:])


def _pw1_glu(x, w_bf, b, tm):
    t = x.shape[0]
    row = lambda i: (i, 0)
    fixed = lambda i: (0, 0)
    return pl.pallas_call(
        _pw1_glu_kernel,
        grid=(t // tm,),
        in_specs=[pl.BlockSpec((tm, D_MODEL), row),
                  pl.BlockSpec((D_MODEL, 2 * D_MODEL), fixed),
                  pl.BlockSpec((1, 2 * D_MODEL), fixed)],
        out_specs=pl.BlockSpec((tm, D_MODEL), row),
        out_shape=jax.ShapeDtypeStruct((t, D_MODEL), F32),
        compiler_params=_cparams(1),
        name="pw1_glu",
    )(x, w_bf, b)


def _conv_rows(win_ref, zs_ref, base, n_sets, w_ref, b_ref):
    shift = CONV_HALO - CONV_CTX
    offs = tuple(range(n_sets))

    def out_rows(o):
        return pl.ds(base + o, SUBLANES, stride=n_sets)

    def slab(c, _):
        def window(t):
            return win_ref[c, pl.ds(base + (shift + t), SUBLANES, stride=n_sets), :]

        bias = b_ref[pl.ds(c, 1), :]
        acc = [bias] * n_sets
        wins = [window(t) for t in range(n_sets - 1)]
        for j in range(CONV_WIDTH):
            w = w_ref[c, j:j + 1, :]
            wins.append(window(j + n_sets - 1))
            acc = [acc[o] + w * wins[o] for o in offs]
            wins.pop(0)
        for o in offs:
            zs_ref[c, out_rows(o), :] = acc[o]
        return 0

    lax.fori_loop(0, N_SLAB, slab, 0)


def _ln_silu_rows(zs_ref, rows, g_ref, bb_ref):
    acc = [zs_ref[c, rows, :] for c in range(N_SLAB)]
    tot = acc[0]
    for c in range(1, N_SLAB):
        tot = tot + acc[c]
    mu = jnp.sum(tot, axis=1, keepdims=True) * (1.0 / D_MODEL)
    cen = [a - mu for a in acc]
    sq = cen[0] * cen[0]
    for c in range(1, N_SLAB):
        sq = sq + cen[c] * cen[c]
    inv = lax.rsqrt(jnp.sum(sq, axis=1, keepdims=True) * (1.0 / D_MODEL) + LN_EPS)
    out = []
    for c in range(N_SLAB):
        z = cen[c] * inv * g_ref[c:c + 1, :] + bb_ref[c:c + 1, :]
        out.append(z * jax.nn.sigmoid(z))
    return out


def _conv_prompt_kernel(um_ref, up_ref, uc_ref, w_ref, b_ref, g_ref, bb_ref, z_ref, win_ref, zs_ref, *, tr):
    i = pl.program_id(1)
    gap = CONV_HALO - META_LEN

    @pl.when(i == 0)
    def _():
        for c in range(N_SLAB):
            win_ref[c, 0:gap, :] = jnp.zeros((gap, LANES), F32)
            win_ref[c, gap:CONV_HALO, :] = um_ref[:, c * LANES:(c + 1) * LANES]

    @pl.when(i > 0)
    def _():
        for c in range(N_SLAB):
            win_ref[c, 0:CONV_HALO, :] = up_ref[:, c * LANES:(c + 1) * LANES]

    for c in range(N_SLAB):
        win_ref[c, CONV_HALO:, :] = uc_ref[:, c * LANES:(c + 1) * LANES]

    def block(bi, _):
        base = pl.multiple_of(bi * CONV_BLOCK, CONV_BLOCK)
        _conv_rows(win_ref, zs_ref, base, CONV_BLOCK // SUBLANES, w_ref, b_ref)
        return 0

    lax.fori_loop(0, tr // CONV_BLOCK, block, 0)

    def norm(bi, _):
        rows = pl.ds(pl.multiple_of(bi * LN_ROWS, LN_ROWS), LN_ROWS)
        for c, z in enumerate(_ln_silu_rows(zs_ref, rows, g_ref, bb_ref)):
            z_ref[rows, c * LANES:(c + 1) * LANES] = z.astype(z_ref.dtype)
        return 0

    lax.fori_loop(0, tr // LN_ROWS, norm, 0, unroll=8)


def _conv_prompt(u, u_meta, w3, b3, g3, bb3, n_batch, seq, tr):
    nt = seq // tr
    cur = lambda b, i: (b * nt + i, 0)
    prev = lambda b, i: (jnp.maximum((b * seq + i * tr) // CONV_HALO - 1, 0), 0)
    fixed2 = lambda b, i: (0, 0)
    fixed3 = lambda b, i: (0, 0, 0)
    return pl.pallas_call(
        functools.partial(_conv_prompt_kernel, tr=tr),
        grid=(n_batch, nt),
        in_specs=[pl.BlockSpec((META_LEN, D_MODEL), lambda b, i: (b, 0)),
                  pl.BlockSpec((CONV_HALO, D_MODEL), prev),
                  pl.BlockSpec((tr, D_MODEL), cur),
                  pl.BlockSpec((N_SLAB, CONV_HALO, LANES), fixed3),
                  pl.BlockSpec((N_SLAB, LANES), fixed2),
                  pl.BlockSpec((N_SLAB, LANES), fixed2),
                  pl.BlockSpec((N_SLAB, LANES), fixed2)],
        out_specs=pl.BlockSpec((tr, D_MODEL), cur),
        out_shape=jax.ShapeDtypeStruct((n_batch * seq, D_MODEL), BF16),
        scratch_shapes=[pltpu.VMEM((N_SLAB, CONV_HALO + tr, LANES), F32),
                        pltpu.VMEM((N_SLAB, tr, LANES), F32)],
        compiler_params=_cparams(2),
        name="conv_prompt",
    )(u_meta, u, u, w3, b3, g3, bb3)


SHORT_ROWS = 16


def _conv_short_kernel(ctx_ref, u_ref, w_ref, b_ref, g_ref, bb_ref, z_ref, win_ref, zs_ref, *, n_seq, t_len):
    for n in range(n_seq):
        for c in range(N_SLAB):
            sl = slice(c * LANES, (c + 1) * LANES)
            win_ref[c, 0:CONV_HALO, :] = ctx_ref[n, :, sl]
            win_ref[c, CONV_HALO:CONV_HALO + t_len, :] = u_ref[n * t_len:(n + 1) * t_len, sl]
            if t_len < SHORT_ROWS:
                win_ref[c, CONV_HALO + t_len:, :] = jnp.zeros((SHORT_ROWS - t_len, LANES), F32)
        _conv_rows(win_ref, zs_ref, 0, SHORT_ROWS // SUBLANES, w_ref, b_ref)
        for c, z in enumerate(_ln_silu_rows(zs_ref, slice(0, t_len), g_ref, bb_ref)):
            z_ref[n * t_len:(n + 1) * t_len, c * LANES:(c + 1) * LANES] = z


def _conv_short(ctx_pad, u, row0, w3, b3, g3, bb3, t_len, n_seq):
    n_total = ctx_pad.shape[0]
    rows = n_seq * t_len
    blk0 = row0 // rows
    fixed2 = lambda n: (0, 0)
    fixed3 = lambda n: (0, 0, 0)
    return pl.pallas_call(
        functools.partial(_conv_short_kernel, n_seq=n_seq, t_len=t_len),
        grid=(n_total // n_seq,),
        in_specs=[pl.BlockSpec((n_seq, CONV_HALO, D_MODEL), lambda n: (n, 0, 0)),
                  pl.BlockSpec((rows, D_MODEL), lambda n: (blk0 + n, 0)),
                  pl.BlockSpec((N_SLAB, CONV_HALO, LANES), fixed3),
                  pl.BlockSpec((N_SLAB, LANES), fixed2),
                  pl.BlockSpec((N_SLAB, LANES), fixed2),
                  pl.BlockSpec((N_SLAB, LANES), fixed2)],
        out_specs=pl.BlockSpec((rows, D_MODEL), lambda n: (n, 0)),
        out_shape=jax.ShapeDtypeStruct((n_total * t_len, D_MODEL), F32),
        scratch_shapes=[pltpu.VMEM((N_SLAB, CONV_HALO + SHORT_ROWS, LANES), F32),
                        pltpu.VMEM((N_SLAB, SHORT_ROWS, LANES), F32)],
        compiler_params=_cparams(1),
        name="conv_short",
    )(ctx_pad, u, w3, b3, g3, bb3)


def _rope_angles(pos):
    inv = 1.0 / (ROPE_THETA ** (jnp.arange(0, ROT_DIM, 2, dtype=F32) / ROT_DIM))
    ang = pos.astype(F32)[:, None] * inv[None, :]
    return jnp.cos(ang), jnp.sin(ang)


def _rope_tables_t(pos):
    cos, sin = _rope_angles(pos)
    return cos.T, sin.T


def _rope_tables(pos):
    half = ROT_DIM // 2
    cos, sin = _rope_angles(pos)
    n = pos.shape[0]
    ones = jnp.ones((n, HEAD_DIM - ROT_DIM), F32)
    zeros = jnp.zeros((n, HEAD_DIM - ROT_DIM), F32)
    zh = jnp.zeros((n, half), F32)
    c = jnp.concatenate([cos, cos, ones], axis=1)
    a = jnp.concatenate([-sin, zh, zeros], axis=1)
    s = jnp.concatenate([zh, sin, zeros], axis=1)
    rep = LANES // HEAD_DIM
    return jnp.tile(c, (1, rep)), jnp.tile(a, (1, rep)), jnp.tile(s, (1, rep))


def kernel(x_prompt, x_sample, cache_attn_meta_k, cache_attn_meta_v, cache_attn_win_k, cache_attn_win_v, state_conv, meta_tokens, attn_w_qkv, attn_b_qkv, attn_sinks, attn_w_o, attn_b_o, conv_w_pw1, conv_b_pw1, conv_w_dw, conv_b_dw, conv_ln_g, conv_ln_b, conv_w_pw2, conv_b_pw2, ln_mix_g, ln_mix_b, ln_ffn_g, ln_ffn_b, moe_w_router, moe_b_router, moe_w_gate_up, moe_b_gate_up, moe_w_down, moe_b_down):
    n_batch, seq, _ = x_prompt.shape
    dec_batch, dec_seq, _ = x_sample.shape
    n_real = n_batch * seq
    n_meta = n_batch * META_LEN
    n_samp = dec_batch * dec_seq
    n_small = n_meta + n_samp
    tb = _pick_tile(seq, 512)
    td = _pick_tile(seq, 1024)
    ts = _pick_tile(n_small, 512)
    row2 = lambda v: v.reshape(1, -1)

    meta_rows = jnp.broadcast_to(meta_tokens[None], (n_batch, META_LEN, D_MODEL)).reshape(n_meta, D_MODEL)
    xb = x_prompt.reshape(n_real, D_MODEL)
    xs = jnp.concatenate([meta_rows.astype(F32), x_sample.reshape(n_samp, D_MODEL)], axis=0)

    def moe_layer(i, a_big, a_small, w, b, xb, xs):
        w_bf = w.astype(BF16)
        wr = moe_w_router[i].T.astype(BF16)
        br = moe_b_router[i].reshape(N_EXPERTS, 1)
        lng, lnb = row2(ln_mix_g[i]), row2(ln_mix_b[i])
        zero_counts = jnp.zeros((N_EXPERTS, LANES), jnp.int32)
        x1b, idb, gab, rab, cnt = _proj_ln_route(a_big, w_bf, row2(b), xb, lng, lnb, wr, br, zero_counts, td)
        x1s, ids_, gas, ras, cnt = _proj_ln_route(a_small, w_bf, row2(b), xs, lng, lnb, wr, br, cnt, ts)
        n_all = moe_w_gate_up.shape[0] * N_EXPERTS
        return _moe([(x1b, idb, gab, rab, tb), (x1s, ids_, gas, ras, ts)], cnt, i * N_EXPERTS,
                    moe_w_gate_up.reshape(n_all, D_MODEL, 2 * MOE_FF),
                    moe_b_gate_up.reshape(n_all, 1, 2 * MOE_FF),
                    moe_w_down.reshape(n_all, MOE_FF, D_MODEL),
                    moe_b_down.reshape(n_all, 1, D_MODEL),
                    ln_ffn_g[i].reshape(N_SLAB, LANES), ln_ffn_b[i].reshape(N_SLAB, LANES))

    w_qkv = attn_w_qkv[0].astype(BF16)
    b_qkv = row2(attn_b_qkv[0])
    pos_small = jnp.concatenate([jnp.tile(jnp.arange(META_LEN), n_batch),
                                 jnp.tile(PAST_LEN + jnp.arange(dec_seq), dec_batch)])
    pos_big = META_LEN + jnp.arange(seq)
    qtb, kb, vb, vtb = _qkv_rope_t(xb, w_qkv, b_qkv, *_rope_tables(pos_big), *_rope_tables_t(pos_big), td)
    qs, ks, vs = _qkv_rope(xs, w_qkv, b_qkv, *_rope_tables(pos_small), ts)
    sinks = attn_sinks[0]
    cmk = cache_attn_meta_k[0].reshape(dec_batch, META_LEN, KV_DIM)
    cmv = cache_attn_meta_v[0].reshape(dec_batch, META_LEN, KV_DIM)
    cwk = cache_attn_win_k[0].reshape(dec_batch, WINDOW, KV_DIM)
    cwv = cache_attn_win_v[0].reshape(dec_batch, WINDOW, KV_DIM)
    vt_meta = vs[:n_meta].reshape(n_batch, META_LEN, KV_DIM).transpose(0, 2, 1).astype(BF16)
    o_big = _attn_prompt(qtb, kb, vtb, ks, vt_meta, sinks, n_batch, seq)
    o_small = jnp.concatenate([
        _attn_meta(qs, ks, vs, sinks, n_batch),
        _attn_sample(qs, ks, vs, cmk, cmv, cwk, cwv, sinks, n_meta, dec_batch, dec_seq)], axis=0)
    xb, xs = moe_layer(0, o_big, o_small, attn_w_o[0], attn_b_o[0], xb, xs)

    kv4 = lambda a, n, t_len: a.reshape(n, t_len, N_KV_HEADS, HEAD_DIM)
    p_meta_k = kv4(ks[:n_meta], n_batch, META_LEN)[None]
    p_meta_v = kv4(vs[:n_meta], n_batch, META_LEN)[None]
    last_win = lambda a: a.reshape(n_batch, seq, KV_DIM)[:, seq - WINDOW:].reshape(n_batch * WINDOW, KV_DIM)
    p_win_k = kv4(last_win(kb), n_batch, WINDOW)[None]
    p_win_v = kv4(last_win(vb), n_batch, WINDOW)[None]
    k_new = kv4(ks[n_meta:], dec_batch, dec_seq)
    v_new = kv4(vs[n_meta:], dec_batch, dec_seq)
    s_win_k = jnp.concatenate([cache_attn_win_k[0], k_new], axis=1)[:, -WINDOW:][None]
    s_win_v = jnp.concatenate([cache_attn_win_v[0], v_new], axis=1)[:, -WINDOW:][None]

    w_pw1 = conv_w_pw1[0].astype(BF16)
    ub = _pw1_glu(xb, w_pw1, row2(conv_b_pw1[0]), td)
    us = _pw1_glu(xs, w_pw1, row2(conv_b_pw1[0]), ts)
    slab = lambda v: v.reshape(N_SLAB, LANES)
    w_dw = jnp.pad(conv_w_dw[0], ((0, CONV_HALO - CONV_WIDTH), (0, 0)))
    w3 = w_dw.reshape(CONV_HALO, N_SLAB, LANES).transpose(1, 0, 2)
    conv_args = (w3, slab(conv_b_dw[0]), slab(conv_ln_g[0]), slab(conv_ln_b[0]))
    z_big = _conv_prompt(ub, us, *conv_args, n_batch, seq, td)
    z_meta = _conv_short(jnp.zeros((n_batch, CONV_HALO, D_MODEL), F32), us, 0, *conv_args, META_LEN, 1)
    ctx_pad = jnp.pad(state_conv[0], ((0, 0), (CONV_HALO - CONV_CTX, 0), (0, 0)))
    z_samp = _conv_short(ctx_pad, us, n_meta, *conv_args, dec_seq, 2)
    z_small = jnp.concatenate([z_meta, z_samp], axis=0).astype(BF16)
    xb, xs = moe_layer(1, z_big, z_small, conv_w_pw2[0], conv_b_pw2[0], xb, xs)

    p_conv = ub.reshape(n_batch, seq, D_MODEL)[:, seq - CONV_CTX:][None]
    u_samp = us[n_meta:].reshape(dec_batch, dec_seq, D_MODEL)
    s_conv = jnp.concatenate([state_conv[0], u_samp], axis=1)[:, -CONV_CTX:][None]

    y_prompt = xb.reshape(n_batch, seq, D_MODEL)
    y_sample = xs[n_meta:].reshape(dec_batch, dec_seq, D_MODEL)
    return (y_prompt, y_sample, p_meta_k, p_meta_v, p_win_k, p_win_v, p_conv,
            s_win_k, s_win_v, s_conv)
```

```python
import functools

import numpy as np
import jax
import jax.numpy as jnp
from jax import lax
from jax.experimental import pallas as pl
from jax.experimental.pallas import tpu as pltpu

F32 = jnp.float32
BF16 = jnp.bfloat16

D_MODEL = 1024
HEAD_DIM = 64
N_HEADS = 16
N_KV_HEADS = 4
GROUP = N_HEADS // N_KV_HEADS
KV_DIM = N_KV_HEADS * HEAD_DIM
QKV_DIM = D_MODEL + 2 * KV_DIM
ROT_DIM = 16
ROPE_THETA = 500000.0
WINDOW = 128
ATTN_BLOCK = 128
ATTN_SCALE = HEAD_DIM ** -0.5
LOG2_E = 1.4426950408889634
Q_SCALE = ATTN_SCALE * LOG2_E
META_LEN = 16
CONV_WIDTH = 31
CONV_CTX = CONV_WIDTH - 1
N_EXPERTS = 32
TOP_K = 4
MOE_FF = 1024
SWIGLU_LIMIT = 7.0
SWIGLU_ALPHA = 1.702
LN_EPS = 1e-5
DEPTH = 2
DEEPNORM_ALPHA = (2 * DEPTH) ** 0.25
PAST_LEN = 16384
NEG_INF = -1e30

LANES = 128
SUBLANES = 8
GMM_TILE = 512
CONV_HALO = 32
CONV_BLOCK = 32
LN_ROWS = 16
N_SLAB = D_MODEL // LANES
COMBINE_ROWS = 32
SUM_GROUP = 16
IDX_SLOTS = 3
HEADS_PER_DOT = 4
VMEM_LIMIT = 56 * 1024 * 1024


def _cparams(n_axes):
    return pltpu.CompilerParams(dimension_semantics=("arbitrary",) * n_axes,
                                vmem_limit_bytes=VMEM_LIMIT)


def _pick_tile(n, cap):
    best = None
    t = LANES
    while t <= cap:
        if n % t == 0:
            best = t
        t += LANES
    assert best is not None, n
    return best


def _layer_norm(h, g, b):
    mu = jnp.mean(h, axis=-1, keepdims=True)
    hc = h - mu
    var = jnp.mean(hc * hc, axis=-1, keepdims=True)
    return hc * lax.rsqrt(var + LN_EPS) * g + b


def _qkv_kernel(x_ref, w_ref, b_ref, c_ref, a_ref, s_ref, q_ref, k_ref, v_ref):
    x = x_ref[...].astype(BF16)
    acc = jnp.dot(x, w_ref[...], preferred_element_type=F32) + b_ref[...]
    c = c_ref[...]
    a = a_ref[...]
    s = s_ref[...]

    def rope(t):
        return t * c + pltpu.roll(t, LANES - ROT_DIM // 2, 1) * a + pltpu.roll(t, ROT_DIM // 2, 1) * s

    for j in range(D_MODEL // LANES):
        sl = slice(j * LANES, (j + 1) * LANES)
        q_ref[:, sl] = (rope(acc[:, sl]) * Q_SCALE).astype(BF16)
    for j in range(KV_DIM // LANES):
        sl = slice(D_MODEL + j * LANES, D_MODEL + (j + 1) * LANES)
        k_ref[:, j * LANES:(j + 1) * LANES] = rope(acc[:, sl])
    v_ref[...] = acc[:, D_MODEL + KV_DIM:]


def _qkv_rope(x, w_bf, b, cos_t, sa_t, sb_t, tm):
    t = x.shape[0]
    period = cos_t.shape[0] // tm
    row = lambda i: (i, 0)
    tab = lambda i: (i % period, 0)
    fixed = lambda i: (0, 0)
    return pl.pallas_call(
        _qkv_kernel,
        grid=(t // tm,),
        in_specs=[pl.BlockSpec((tm, D_MODEL), row),
                  pl.BlockSpec((D_MODEL, QKV_DIM), fixed),
                  pl.BlockSpec((1, QKV_DIM), fixed),
                  pl.BlockSpec((tm, LANES), tab),
                  pl.BlockSpec((tm, LANES), tab),
                  pl.BlockSpec((tm, LANES), tab)],
        out_specs=[pl.BlockSpec((tm, D_MODEL), row),
                   pl.BlockSpec((tm, KV_DIM), row),
                   pl.BlockSpec((tm, KV_DIM), row)],
        out_shape=[jax.ShapeDtypeStruct((t, D_MODEL), BF16),
                   jax.ShapeDtypeStruct((t, KV_DIM), F32),
                   jax.ShapeDtypeStruct((t, KV_DIM), F32)],
        compiler_params=_cparams(1),
        name="qkv_rope",
    )(x, w_bf, b, cos_t, sa_t, sb_t)


def _qkv_t_kernel(x_ref, wqt_ref, bq_ref, wkv_ref, bkv_ref, wvt_ref, bv_ref, c_ref, a_ref, s_ref,
                  ct_ref, st_ref, qt_ref, k_ref, v_ref, vt_ref):
    x = x_ref[...].astype(BF16)
    nt = (((1,), (1,)), ((), ()))
    qt = lax.dot_general(wqt_ref[...], x, nt, preferred_element_type=F32) + bq_ref[...]
    ct = ct_ref[...]
    st = st_ref[...]
    half = ROT_DIM // 2
    for h in range(N_HEADS):
        r0 = h * HEAD_DIM
        x1 = qt[r0:r0 + half]
        x2 = qt[r0 + half:r0 + ROT_DIM]
        rot = jnp.concatenate([x1 * ct - x2 * st, x2 * ct + x1 * st], axis=0)
        qt_ref[r0:r0 + ROT_DIM, :] = (rot * Q_SCALE).astype(BF16)
        qt_ref[r0 + ROT_DIM:r0 + HEAD_DIM, :] = (qt[r0 + ROT_DIM:r0 + HEAD_DIM] * Q_SCALE).astype(BF16)
    vt = lax.dot_general(wvt_ref[...], x, nt, preferred_element_type=F32) + bv_ref[...]
    vt_ref[...] = vt.astype(BF16)

    kv = jnp.dot(x, wkv_ref[...], preferred_element_type=F32) + bkv_ref[...]
    c = c_ref[...]
    a = a_ref[...]
    s = s_ref[...]
    for j in range(KV_DIM // LANES):
        t = kv[:, j * LANES:(j + 1) * LANES]
        k_ref[:, j * LANES:(j + 1) * LANES] = (
            t * c + pltpu.roll(t, LANES - half, 1) * a + pltpu.roll(t, half, 1) * s)
    v_ref[...] = kv[:, KV_DIM:]


def _qkv_rope_t(x, w_bf, b, cos_t, sa_t, sb_t, cos_tt, sin_tt, tm):
    t = x.shape[0]
    period = cos_t.shape[0] // tm
    row = lambda i: (i, 0)
    col = lambda i: (0, i)
    tab = lambda i: (i % period, 0)
    tabt = lambda i: (0, i % period)
    fixed = lambda i: (0, 0)
    wqt = w_bf[:, :D_MODEL].T
    wvt = w_bf[:, D_MODEL + KV_DIM:].T
    half = ROT_DIM // 2
    return pl.pallas_call(
        _qkv_t_kernel,
        grid=(t // tm,),
        in_specs=[pl.BlockSpec((tm, D_MODEL), row),
                  pl.BlockSpec((D_MODEL, D_MODEL), fixed),
                  pl.BlockSpec((D_MODEL, 1), fixed),
                  pl.BlockSpec((D_MODEL, 2 * KV_DIM), fixed),
                  pl.BlockSpec((1, 2 * KV_DIM), fixed),
                  pl.BlockSpec((KV_DIM, D_MODEL), fixed),
                  pl.BlockSpec((KV_DIM, 1), fixed),
                  pl.BlockSpec((tm, LANES), tab),
                  pl.BlockSpec((tm, LANES), tab),
                  pl.BlockSpec((tm, LANES), tab),
                  pl.BlockSpec((half, tm), tabt),
                  pl.BlockSpec((half, tm), tabt)],
        out_specs=[pl.BlockSpec((D_MODEL, tm), col),
                   pl.BlockSpec((tm, KV_DIM), row),
                   pl.BlockSpec((tm, KV_DIM), row),
                   pl.BlockSpec((KV_DIM, tm), col)],
        out_shape=[jax.ShapeDtypeStruct((D_MODEL, t), BF16),
                   jax.ShapeDtypeStruct((t, KV_DIM), F32),
                   jax.ShapeDtypeStruct((t, KV_DIM), F32),
                   jax.ShapeDtypeStruct((KV_DIM, t), BF16)],
        compiler_params=_cparams(1),
        name="qkv_rope_t",
    )(x, wqt, b[:, :D_MODEL].reshape(D_MODEL, 1), w_bf[:, D_MODEL:], b[:, D_MODEL:],
      wvt, b[:, D_MODEL + KV_DIM:].reshape(KV_DIM, 1), cos_t, sa_t, sb_t, cos_tt, sin_tt)


def _attend(q, kcat, vcat, bias, sink_ref, write):
    kgs = [kcat[:, g * HEAD_DIM:(g + 1) * HEAD_DIM] for g in range(N_KV_HEADS)]

    def scores(h):
        qh = q[:, h * HEAD_DIM:(h + 1) * HEAD_DIM]
        return lax.dot_general(qh, kgs[h // GROUP], (((1,), (1,)), ((), ())), preferred_element_type=F32) + bias

    s_next = scores(0)
    for h in range(N_HEADS):
        s = s_next
        if h + 1 < N_HEADS:
            s_next = scores(h + 1)
        vg = vcat[:, (h // GROUP) * HEAD_DIM:(h // GROUP + 1) * HEAD_DIM]
        sink = sink_ref[h] * LOG2_E
        m = jnp.maximum(jnp.max(s, axis=-1, keepdims=True), sink)
        p = jnp.exp2(s - m)
        denom = jnp.sum(p, axis=-1, keepdims=True) + jnp.exp2(sink - m)
        o = jnp.dot(p.astype(BF16), vg, preferred_element_type=F32)
        write(h, o / denom)


def _attn_prompt_kernel(sink_ref, qt_ref, kp_ref, ko_ref, km_ref, vtp_ref, vto_ref, vtm_ref, o_ref, acc_ref):
    i = pl.program_id(1)
    kcat = jnp.concatenate([kp_ref[...], ko_ref[...], km_ref[...]], axis=0).astype(BF16)
    vt = jnp.concatenate([vtp_ref[...], vto_ref[...], vtm_ref[...]], axis=1)
    nk = 2 * ATTN_BLOCK + META_LEN
    key = lax.broadcasted_iota(jnp.int32, (nk, ATTN_BLOCK), 0)
    qry = lax.broadcasted_iota(jnp.int32, (nk, ATTN_BLOCK), 1)
    first = jnp.where(i > 0, 0, 2 * ATTN_BLOCK)
    prev_ok = (key < ATTN_BLOCK) & (key >= qry + first)
    own_ok = (key >= ATTN_BLOCK) & (key - ATTN_BLOCK <= qry)
    mask = prev_ok | own_ok | (key >= 2 * ATTN_BLOCK)
    bias = jnp.where(mask, 0.0, NEG_INF)
    bias = jnp.concatenate([bias] * HEADS_PER_DOT, axis=1)
    lane = lax.broadcasted_iota(jnp.int32, (1, HEADS_PER_DOT * ATTN_BLOCK), 1)
    kgs = [kcat[:, g * HEAD_DIM:(g + 1) * HEAD_DIM] for g in range(N_KV_HEADS)]

    def scores(u):
        h0 = u * HEADS_PER_DOT
        rhs = jnp.concatenate([qt_ref[(h0 + j) * HEAD_DIM:(h0 + j + 1) * HEAD_DIM, :]
                               for j in range(HEADS_PER_DOT)], axis=1)
        return jnp.dot(kgs[h0 // GROUP], rhs, preferred_element_type=F32) + bias

    n_units = N_HEADS // HEADS_PER_DOT
    s_next = scores(0)
    for u in range(n_units):
        s = s_next
        if u + 1 < n_units:
            s_next = scores(u + 1)
        h0 = u * HEADS_PER_DOT
        g = h0 // GROUP
        sink = sink_ref[h0] * LOG2_E
        for j in range(1, HEADS_PER_DOT):
            sink = jnp.where(lane >= j * ATTN_BLOCK, sink_ref[h0 + j] * LOG2_E, sink)
        m = jnp.maximum(jnp.max(s, axis=0, keepdims=True), sink)
        p = jnp.exp2(s - m)
        denom = jnp.sum(p, axis=0, keepdims=True) + jnp.exp2(sink - m)
        o = jnp.dot(vt[g * HEAD_DIM:(g + 1) * HEAD_DIM, :], p.astype(BF16), preferred_element_type=F32)
        o = o * (1.0 / denom)
        for j in range(HEADS_PER_DOT):
            acc_ref[(h0 + j) * HEAD_DIM:(h0 + j + 1) * HEAD_DIM, :] = o[:, j * ATTN_BLOCK:(j + 1) * ATTN_BLOCK]
    o_ref[...] = acc_ref[...].T.astype(o_ref.dtype)


def _attn_prompt(qt, k, vt, k_meta, vt_meta, sinks, n_batch, seq):
    nblk = seq // ATTN_BLOCK
    own = lambda b, i: b * nblk + i
    prev = lambda b, i: b * nblk + jnp.maximum(i - 1, 0)
    kspec = lambda m: pl.BlockSpec((ATTN_BLOCK, KV_DIM), lambda b, i: (m(b, i), 0))
    vspec = lambda m: pl.BlockSpec((KV_DIM, ATTN_BLOCK), lambda b, i: (0, m(b, i)))
    return pl.pallas_call(
        _attn_prompt_kernel,
        grid=(n_batch, nblk),
        in_specs=[pl.BlockSpec(memory_space=pltpu.SMEM),
                  pl.BlockSpec((D_MODEL, ATTN_BLOCK), lambda b, i: (0, own(b, i))),
                  kspec(prev), kspec(own),
                  pl.BlockSpec((META_LEN, KV_DIM), lambda b, i: (b, 0)),
                  vspec(prev), vspec(own),
                  pl.BlockSpec((None, KV_DIM, META_LEN), lambda b, i: (b, 0, 0))],
        out_specs=pl.BlockSpec((ATTN_BLOCK, D_MODEL), lambda b, i: (own(b, i), 0)),
        out_shape=jax.ShapeDtypeStruct((n_batch * seq, D_MODEL), BF16),
        scratch_shapes=[pltpu.VMEM((D_MODEL, ATTN_BLOCK), F32)],
        compiler_params=_cparams(2),
        name="attn_prompt",
    )(sinks, qt, k, k, k_meta, vt, vt, vt_meta)


def _attn_meta_kernel(sink_ref, q_ref, k_ref, v_ref, o_ref):
    r = lax.broadcasted_iota(jnp.int32, (META_LEN, META_LEN), 0)
    c = lax.broadcasted_iota(jnp.int32, (META_LEN, META_LEN), 1)

    def write(h, o):
        o_ref[:, h * HEAD_DIM:(h + 1) * HEAD_DIM] = o.astype(o_ref.dtype)

    bias = jnp.where(c <= r, 0.0, NEG_INF)
    _attend(q_ref[...], k_ref[...].astype(BF16), v_ref[...].astype(BF16), bias, sink_ref, write)


def _attn_meta(q, k, v, sinks, n_batch):
    imap = lambda b: (b, 0)
    return pl.pallas_call(
        _attn_meta_kernel,
        grid=(n_batch,),
        in_specs=[pl.BlockSpec(memory_space=pltpu.SMEM),
                  pl.BlockSpec((META_LEN, D_MODEL), imap),
                  pl.BlockSpec((META_LEN, KV_DIM), imap),
                  pl.BlockSpec((META_LEN, KV_DIM), imap)],
        out_specs=pl.BlockSpec((META_LEN, D_MODEL), lambda b: (b, 0)),
        out_shape=jax.ShapeDtypeStruct((n_batch * META_LEN, D_MODEL), BF16),
        compiler_params=_cparams(1),
        name="attn_meta",
    )(sinks, q, k, v)


SAMPLE_GROUP = 8


def _attn_sample_kernel(sink_ref, bias_ref, q_ref, kn_ref, vn_ref, cmk_ref, cmv_ref, cwk_ref, cwv_ref,
                        o_ref, *, dec_seq, group):
    def keys(cm_ref, cw_ref, new):
        parts = []
        for j in range(group):
            parts += [cm_ref[j], cw_ref[j], new[j * dec_seq:(j + 1) * dec_seq]]
        return jnp.concatenate(parts, axis=0).astype(BF16)

    kcat = keys(cmk_ref, cwk_ref, kn_ref[...])
    vcat = keys(cmv_ref, cwv_ref, vn_ref[...])

    def write(h, o):
        o_ref[:, h * HEAD_DIM:(h + 1) * HEAD_DIM] = o.astype(o_ref.dtype)

    _attend(q_ref[...], kcat, vcat, bias_ref[...], sink_ref, write)


def _sample_bias(group, dec_seq):
    per = META_LEN + WINDOW + dec_seq
    r = np.arange(group * dec_seq)[:, None]
    c = np.arange(group * per)[None, :]
    tq, ck = r % dec_seq, c % per
    win_ok = (ck >= META_LEN) & (ck < META_LEN + WINDOW) & (ck - META_LEN >= tq)
    new_ok = (ck >= META_LEN + WINDOW) & (ck - (META_LEN + WINDOW) <= tq)
    vis = (r // dec_seq == c // per) & ((ck < META_LEN) | win_ok | new_ok)
    return np.where(vis, 0.0, NEG_INF).astype(np.float32)


def _attn_sample(q, k, v, cmk, cmv, cwk, cwv, sinks, row0, dec_batch, dec_seq):
    group = SAMPLE_GROUP
    nq = group * dec_seq
    blk0 = row0 // nq
    bias = jnp.asarray(_sample_bias(group, dec_seq))
    qmap = lambda n: (blk0 + n, 0)
    cmap = lambda n: (n, 0, 0)
    return pl.pallas_call(
        functools.partial(_attn_sample_kernel, dec_seq=dec_seq, group=group),
        grid=(dec_batch // group,),
        in_specs=[pl.BlockSpec(memory_space=pltpu.SMEM),
                  pl.BlockSpec(bias.shape, lambda n: (0, 0)),
                  pl.BlockSpec((nq, D_MODEL), qmap),
                  pl.BlockSpec((nq, KV_DIM), qmap),
                  pl.BlockSpec((nq, KV_DIM), qmap),
                  pl.BlockSpec((group, META_LEN, KV_DIM), cmap),
                  pl.BlockSpec((group, META_LEN, KV_DIM), cmap),
                  pl.BlockSpec((group, WINDOW, KV_DIM), cmap),
                  pl.BlockSpec((group, WINDOW, KV_DIM), cmap)],
        out_specs=pl.BlockSpec((nq, D_MODEL), lambda n: (n, 0)),
        out_shape=jax.ShapeDtypeStruct((dec_batch * dec_seq, D_MODEL), BF16),
        compiler_params=_cparams(1),
        name="attn_sample",
    )(sinks, bias, q, k, v, cmk, cmv, cwk, cwv)


def _proj_ln_route_kernel(a_ref, w_ref, b_ref, x_ref, g_ref, bb_ref, wr_ref, br_ref, tri_ref, cin_ref,
                          x1_ref, ids_ref, gates_ref, rank_ref, cnt_ref, carry_ref):
    @pl.when(pl.program_id(0) == 0)
    def _():
        carry_ref[...] = cin_ref[...].astype(F32)

    y = jnp.dot(a_ref[...], w_ref[...], preferred_element_type=F32) + b_ref[...]
    x1 = _layer_norm(DEEPNORM_ALPHA * x_ref[...] + y, g_ref[...], bb_ref[...])
    for j in range(N_SLAB):
        x1_ref[pl.ds(j, x1.shape[0], stride=N_SLAB), :] = x1[:, j * LANES:(j + 1) * LANES]

    logits = lax.dot_general(wr_ref[...], x1.astype(BF16), (((1,), (1,)), ((), ())),
                             preferred_element_type=F32) + br_ref[...]
    tm = logits.shape[1]
    eidx = lax.broadcasted_iota(jnp.int32, (N_EXPERTS, tm), 0).astype(F32)
    cur = logits
    vals, idxs, sels = [], [], []
    for _ in range(TOP_K):
        m = jnp.max(cur, axis=0, keepdims=True)
        idx = jnp.min(jnp.where(cur == m, eidx, float(N_EXPERTS)), axis=0, keepdims=True)
        sel = eidx == idx
        vals.append(m)
        idxs.append(idx)
        sels.append(sel)
        cur = jnp.where(sel, -jnp.inf, cur)
    exps = [jnp.exp(v - vals[0]) for v in vals]
    tot = exps[0] + exps[1] + exps[2] + exps[3]
    gates_ref[...] = jnp.concatenate([e / tot for e in exps], axis=0)
    ids_ref[...] = jnp.concatenate(idxs, axis=0).astype(jnp.int32)

    chosen = jnp.where(sels[0] | sels[1] | sels[2] | sels[3], 1.0, 0.0)
    before = jnp.dot(chosen.astype(BF16), tri_ref[...], preferred_element_type=F32)
    before = before + carry_ref[:, 0:1]
    ranks = [jnp.sum(jnp.where(s, before, 0.0), axis=0, keepdims=True) for s in sels]
    rank_ref[...] = jnp.concatenate(ranks, axis=0).astype(jnp.int32)
    carry_ref[...] = carry_ref[...] + jnp.sum(chosen, axis=1, keepdims=True)
    cnt_ref[...] = carry_ref[...].astype(jnp.int32)


def _proj_ln_route(a_bf, w_bf, b, x, g, bb, wr_t_bf, br_col, counts_in, tm):
    t = x.shape[0]
    row = lambda i: (i, 0)
    col = lambda i: (0, i)
    fixed = lambda i: (0, 0)
    tri = (jnp.arange(tm)[:, None] < jnp.arange(tm)[None, :]).astype(BF16)
    return pl.pallas_call(
        _proj_ln_route_kernel,
        grid=(t // tm,),
        in_specs=[pl.BlockSpec((tm, D_MODEL), row),
                  pl.BlockSpec((D_MODEL, D_MODEL), fixed),
                  pl.BlockSpec((1, D_MODEL), fixed),
                  pl.BlockSpec((tm, D_MODEL), row),
                  pl.BlockSpec((1, D_MODEL), fixed),
                  pl.BlockSpec((1, D_MODEL), fixed),
                  pl.BlockSpec((N_EXPERTS, D_MODEL), fixed),
                  pl.BlockSpec((N_EXPERTS, 1), fixed),
                  pl.BlockSpec((tm, tm), fixed),
                  pl.BlockSpec((N_EXPERTS, LANES), fixed)],
        out_specs=[pl.BlockSpec((tm * N_SLAB, LANES), row),
                   pl.BlockSpec((TOP_K, tm), col),
                   pl.BlockSpec((TOP_K, tm), col),
                   pl.BlockSpec((TOP_K, tm), col),
                   pl.BlockSpec((N_EXPERTS, LANES), fixed)],
        out_shape=[jax.ShapeDtypeStruct((t * N_SLAB, LANES), F32),
                   jax.ShapeDtypeStruct((TOP_K, t), jnp.int32),
                   jax.ShapeDtypeStruct((TOP_K, t), F32),
                   jax.ShapeDtypeStruct((TOP_K, t), jnp.int32),
                   jax.ShapeDtypeStruct((N_EXPERTS, LANES), jnp.int32)],
        scratch_shapes=[pltpu.VMEM((N_EXPERTS, LANES), F32)],
        compiler_params=_cparams(1),
        name="proj_ln_route",
    )(a_bf, w_bf, b, x, g, bb, wr_t_bf, br_col, tri, counts_in)


def _idx_copy(dest_hbm, dsm, sem, step, stride, n_slots=2):
    slot = step % n_slots
    return pltpu.make_async_copy(dest_hbm.at[pl.ds(pl.multiple_of(step * stride, stride), stride)],
                                 dsm.at[pl.ds(pl.multiple_of(slot * stride, stride), stride)],
                                 sem)


def _dispatch_kernel(pstart_ref, padded_ref, nact_ref, dest_hbm, *rest, tiles, steps, stride, n_tiles):
    x_refs = rest[:len(tiles)]
    xs_hbm, dsm, zbuf, sems = rest[len(tiles):]
    i = pl.program_id(0)
    nsteps = pl.num_programs(0)

    def zero_tile(row0):
        n = GMM_TILE * N_SLAB
        return pltpu.make_async_copy(zbuf, xs_hbm.at[pl.ds(pl.multiple_of(row0 * N_SLAB, n), n)], sems.at[2])

    def token(ref, r):
        return ref.at[pl.ds(pl.multiple_of(r * N_SLAB, N_SLAB), N_SLAB)]

    @pl.when(i == 0)
    def _():
        zbuf[...] = jnp.zeros_like(zbuf)

        def each_expert(fn):
            def body(e, _):
                @pl.when(padded_ref[e] > 0)
                def _():
                    fn(zero_tile(pstart_ref[e] + padded_ref[e] - GMM_TILE))
                return 0
            lax.fori_loop(0, N_EXPERTS, body, 0)

        def each_tail(fn):
            def body(t, _):
                fn(zero_tile(t * GMM_TILE))
                return 0
            lax.fori_loop(nact_ref[0], n_tiles, body, 0)

        each_expert(lambda cp: cp.start())
        each_tail(lambda cp: cp.start())
        each_expert(lambda cp: cp.wait())
        each_tail(lambda cp: cp.wait())
        _idx_copy(dest_hbm, dsm, sems.at[0], i, stride).start()

    _idx_copy(dest_hbm, dsm, sems.at[0], i, stride).wait()

    @pl.when(i + 1 < nsteps)
    def _():
        _idx_copy(dest_hbm, dsm, sems.at[0], i + 1, stride).start()

    base = (i % 2) * stride
    step0 = 0
    for x_ref, tm, n in zip(x_refs, tiles, steps):
        @pl.when((i >= step0) & (i < step0 + n))
        def _(x_ref=x_ref, tm=tm):
            def row_body(r, _):
                for k in range(TOP_K):
                    d = dsm[base + k * tm + r]
                    pltpu.make_async_copy(token(x_ref, r), token(xs_hbm, d), sems.at[1]).start(priority=k % 2)
                return 0

            lax.fori_loop(0, tm, row_body, 0, unroll=4)
            for k in range(TOP_K):
                pltpu.make_async_copy(x_ref, xs_hbm.at[pl.ds(0, tm * N_SLAB)], sems.at[1]).wait()
        step0 += n


def _seg_map(step0, n):
    return lambda i, *_: (jnp.clip(i - step0, 0, n - 1), 0)


def _dispatch(xs_list, tiles, dest_steps, pstart, padded, nact, stride, n_tiles):
    steps = [x.shape[0] // (tm * N_SLAB) for x, tm in zip(xs_list, tiles)]
    in_specs = [pl.BlockSpec(memory_space=pl.ANY)]
    step0 = 0
    for tm, n in zip(tiles, steps):
        in_specs.append(pl.BlockSpec((tm * N_SLAB, LANES), _seg_map(step0, n)))
        step0 += n
    grid_spec = pltpu.PrefetchScalarGridSpec(
        num_scalar_prefetch=3,
        grid=(sum(steps),),
        in_specs=in_specs,
        out_specs=pl.BlockSpec(memory_space=pl.ANY),
        scratch_shapes=[pltpu.SMEM((2 * stride,), jnp.int32),
                        pltpu.VMEM((GMM_TILE * N_SLAB, LANES), F32),
                        pltpu.SemaphoreType.DMA((3,))],
    )
    return pl.pallas_call(
        functools.partial(_dispatch_kernel, tiles=tuple(tiles), steps=tuple(steps), stride=stride,
                          n_tiles=n_tiles),
        grid_spec=grid_spec,
        out_shape=jax.ShapeDtypeStruct((n_tiles * GMM_TILE * N_SLAB, LANES), F32),
        compiler_params=_cparams(1),
        name="moe_dispatch",
    )(pstart, padded, nact, dest_steps, *xs_list)


def _gmm_kernel(te_ref, tsrc_ref, tfirst_ref, nact_ref, x_ref, wgu_ref, bgu_ref, wdn_ref, bdn_ref, y_ref,
                wgu_bf, wdn_bf):
    i = pl.program_id(0)

    @pl.when(tfirst_ref[i] == 1)
    def _():
        wgu_bf[...] = wgu_ref[...].astype(BF16)
        wdn_bf[...] = wdn_ref[...].astype(BF16)

    @pl.when(i < nact_ref[0])
    def _():
        x = jnp.concatenate([x_ref[pl.ds(j, GMM_TILE, stride=N_SLAB), :] for j in range(N_SLAB)], axis=1)
        gu = jnp.dot(x.astype(BF16), wgu_bf[...], preferred_element_type=F32) + bgu_ref[...]
        gate = jnp.minimum(gu[:, :MOE_FF], SWIGLU_LIMIT)
        up = jnp.clip(gu[:, MOE_FF:], -SWIGLU_LIMIT, SWIGLU_LIMIT)
        glu = gate * jax.nn.sigmoid(SWIGLU_ALPHA * gate)
        h = ((up + 1.0) * glu).astype(BF16)
        y = jnp.dot(h, wdn_bf[...], preferred_element_type=F32) + bdn_ref[...]
        for j in range(N_SLAB):
            y_ref[pl.ds(j, GMM_TILE, stride=N_SLAB), :] = y[:, j * LANES:(j + 1) * LANES]

    @pl.when(i >= nact_ref[0])
    def _():
        y_ref[...] = jnp.zeros_like(y_ref)


def _gmm(xs, te, tsrc, tfirst, nact, wgu, bgu, wdn, bdn, n_tiles):
    emap = lambda i, te, *_: (te[i], 0, 0)
    grid_spec = pltpu.PrefetchScalarGridSpec(
        num_scalar_prefetch=4,
        grid=(n_tiles,),
        in_specs=[pl.BlockSpec((GMM_TILE * N_SLAB, LANES), lambda i, te, ts, *_: (ts[i], 0)),
                  pl.BlockSpec((None, D_MODEL, 2 * MOE_FF), emap),
                  pl.BlockSpec((None, 1, 2 * MOE_FF), emap),
                  pl.BlockSpec((None, MOE_FF, D_MODEL), emap),
                  pl.BlockSpec((None, 1, D_MODEL), emap)],
        out_specs=pl.BlockSpec((GMM_TILE * N_SLAB, LANES), lambda i, *_: (i, 0)),
        scratch_shapes=[pltpu.VMEM((D_MODEL, 2 * MOE_FF), BF16),
                        pltpu.VMEM((MOE_FF, D_MODEL), BF16)],
    )
    return pl.pallas_call(
        _gmm_kernel,
        grid_spec=grid_spec,
        out_shape=jax.ShapeDtypeStruct((n_tiles * GMM_TILE * N_SLAB, LANES), F32),
        compiler_params=_cparams(1),
        name="moe_gmm",
    )(te, tsrc, tfirst, nact, xs, wgu, bgu, wdn, bdn)


def _combine_kernel(dest_hbm, gate_hbm, ys_hbm, x_ref, g_ref, bb_ref, o_ref, dsm, gsm, buf, sems,
                    *, tm, stride):
    i = pl.program_id(0)
    nsteps = pl.num_programs(0)

    def idx_copies(step):
        slot = step % IDX_SLOTS
        return (_idx_copy(dest_hbm, dsm, sems.at[slot], step, stride, IDX_SLOTS),
                _idx_copy(gate_hbm, gsm, sems.at[IDX_SLOTS + slot], step, stride, IDX_SLOTS))

    def token(ref, r):
        return ref.at[pl.ds(pl.multiple_of(r * N_SLAB, N_SLAB), N_SLAB)]

    def issue_rows(step, parity):
        base = (step % IDX_SLOTS) * stride

        def body(r, _):
            for k in range(TOP_K):
                d = dsm[base + k * tm + r]
                pltpu.make_async_copy(token(ys_hbm, d), token(buf.at[parity * TOP_K + k], r),
                                      sems.at[2 * IDX_SLOTS + parity]).start(priority=k % 2)
            return 0

        lax.fori_loop(0, tm, body, 0, unroll=4)

    @pl.when(i == 0)
    def _():
        for cp in idx_copies(0):
            cp.start()

        @pl.when(nsteps > 1)
        def _():
            for cp in idx_copies(1):
                cp.start()

        for cp in idx_copies(0):
            cp.wait()

        issue_rows(0, 0)

    @pl.when(i + 1 < nsteps)
    def _():
        for cp in idx_copies(i + 1):
            cp.wait()

    @pl.when(i + 2 < nsteps)
    def _():
        for cp in idx_copies(i + 2):
            cp.start()

    for parity in range(2):
        @pl.when((i + 1 < nsteps) & ((i + 1) % 2 == parity))
        def _(parity=parity):
            issue_rows(i + 1, parity)

    gbase = (i % IDX_SLOTS) * stride

    def compute(parity):
        half = parity * TOP_K
        for k in range(TOP_K):
            pltpu.make_async_copy(ys_hbm.at[pl.ds(0, tm * N_SLAB)], buf.at[half + k],
                                  sems.at[2 * IDX_SLOTS + parity]).wait()

        def token_sum(g, _):
            r0 = g * SUM_GROUP
            row0 = pl.multiple_of(r0 * N_SLAB, SUM_GROUP * N_SLAB)
            for t in range(SUM_GROUP):
                rows = pl.ds(row0 + t * N_SLAB, N_SLAB)
                h = DEEPNORM_ALPHA * x_ref[rows, :]
                for k in range(TOP_K):
                    h = h + gsm[gbase + r0 + (k * tm + t)] * buf[half + k, rows, :]
                buf[half, rows, :] = h
            return 0

        lax.fori_loop(0, tm // SUM_GROUP, token_sum, 0)

        def block(b, _):
            r0 = pl.multiple_of(b * COMBINE_ROWS, COMBINE_ROWS)
            hs = [buf[half, pl.ds(r0 * N_SLAB + j, COMBINE_ROWS, stride=N_SLAB), :] for j in range(N_SLAB)]
            tot = hs[0]
            for j in range(1, N_SLAB):
                tot = tot + hs[j]
            mu = jnp.sum(tot, axis=1, keepdims=True) * (1.0 / D_MODEL)
            cen = [h - mu for h in hs]
            sq = cen[0] * cen[0]
            for j in range(1, N_SLAB):
                sq = sq + cen[j] * cen[j]
            inv = lax.rsqrt(jnp.sum(sq, axis=1, keepdims=True) * (1.0 / D_MODEL) + LN_EPS)
            for j in range(N_SLAB):
                o_ref[pl.ds(r0, COMBINE_ROWS), j * LANES:(j + 1) * LANES] = (
                    cen[j] * inv * g_ref[j:j + 1, :] + bb_ref[j:j + 1, :])
            return 0

        lax.fori_loop(0, tm // COMBINE_ROWS, block, 0, unroll=4)

    for parity in range(2):
        @pl.when(i % 2 == parity)
        def _(parity=parity):
            compute(parity)


def _combine(ys, dest_steps, gate_steps, x_tiles, g3, bb3, tm, stride):
    t = x_tiles.shape[0] // N_SLAB
    row = lambda i: (i, 0)
    fixed = lambda i: (0, 0)
    return pl.pallas_call(
        functools.partial(_combine_kernel, tm=tm, stride=stride),
        grid=(t // tm,),
        in_specs=[pl.BlockSpec(memory_space=pl.ANY),
                  pl.BlockSpec(memory_space=pl.ANY),
                  pl.BlockSpec(memory_space=pl.ANY),
                  pl.BlockSpec((tm * N_SLAB, LANES), row),
                  pl.BlockSpec((N_SLAB, LANES), fixed),
                  pl.BlockSpec((N_SLAB, LANES), fixed)],
        out_specs=pl.BlockSpec((tm, D_MODEL), row),
        out_shape=jax.ShapeDtypeStruct((t, D_MODEL), F32),
        scratch_shapes=[pltpu.SMEM((IDX_SLOTS * stride,), jnp.int32),
                        pltpu.SMEM((IDX_SLOTS * stride,), F32),
                        pltpu.VMEM((2 * TOP_K, tm * N_SLAB, LANES), F32),
                        pltpu.SemaphoreType.DMA((2 * IDX_SLOTS + 2,))],
        compiler_params=_cparams(1),
        name="moe_combine",
    )(dest_steps, gate_steps, ys, x_tiles, g3, bb3)


def _moe(segs, counts, expert0, wgu, bgu, wdn, bdn, g, bb):
    t = sum(seg[1].shape[1] for seg in segs)
    n_tiles = -(-(t * TOP_K) // GMM_TILE) + N_EXPERTS
    counts = counts[:, 0]
    padded = ((counts + GMM_TILE - 1) // GMM_TILE) * GMM_TILE
    pend = jnp.cumsum(padded)
    pstart = (pend - padded).astype(jnp.int32)
    padded = padded.astype(jnp.int32)
    nact = (pend[-1] // GMM_TILE).astype(jnp.int32).reshape(1)
    tile = jnp.arange(n_tiles, dtype=jnp.int32)
    tsrc = jnp.minimum(tile, jnp.maximum(nact[0] - 1, 0))
    te = jnp.sum((pend[None, :] <= (tsrc * GMM_TILE)[:, None]).astype(jnp.int32), axis=1)
    te = jnp.minimum(te, N_EXPERTS - 1).astype(jnp.int32)
    tfirst = jnp.concatenate([jnp.ones((1,), jnp.int32), (te[1:] != te[:-1]).astype(jnp.int32)])
    eids = jnp.arange(N_EXPERTS, dtype=jnp.int32)[:, None, None]

    stride = -(-(TOP_K * max(seg[4] for seg in segs)) // 1024) * 1024
    def per_step(a, tm):
        n = a.shape[1]
        steps = a.reshape(TOP_K, n // tm, tm).transpose(1, 0, 2).reshape(n // tm, TOP_K * tm)
        return jnp.pad(steps, ((0, 0), (0, stride - TOP_K * tm))).reshape(-1)

    plans = []
    for x1, ids, gates, rank, tm in segs:
        dest = jnp.sum(jnp.where(ids[None] == eids, pstart[:, None, None], 0), axis=0) + rank
        plans.append((per_step(dest, tm), per_step(gates, tm)))
    xs = _dispatch([seg[0] for seg in segs], [seg[4] for seg in segs],
                   jnp.concatenate([p[0] for p in plans]), pstart, padded, nact, stride, n_tiles)
    ys = _gmm(xs, te + expert0, tsrc, tfirst, nact, wgu, bgu, wdn, bdn, n_tiles)
    return [_combine(ys, dest_steps, gate_steps, seg[0], g, bb, seg[4], stride)
            for seg, (dest_steps, gate_steps) in zip(segs, plans)]


def _pw1_glu_kernel(x_ref, w_ref, b_ref, u_ref):
    a = jnp.dot(x_ref[...].astype(BF16), w_ref[...], preferred_element_type=F32) + b_ref[...]
    u_ref[...] = a[:, :D_MODEL] * jax.nn.sigmoid(a[:, D_MODEL:])


def _pw1_glu(x, w_bf, b, tm):
    t = x.shape[0]
    row = lambda i: (i, 0)
    fixed = lambda i: (0, 0)
    return pl.pallas_call(
        _pw1_glu_kernel,
        grid=(t // tm,),
        in_specs=[pl.BlockSpec((tm, D_MODEL), row),
                  pl.BlockSpec((D_MODEL, 2 * D_MODEL), fixed),
                  pl.BlockSpec((1, 2 * D_MODEL), fixed)],
        out_specs=pl.BlockSpec((tm, D_MODEL), row),
        out_shape=jax.ShapeDtypeStruct((t, D_MODEL), F32),
        compiler_params=_cparams(1),
        name="pw1_glu",
    )(x, w_bf, b)


def _conv_rows(win_ref, zs_ref, base, n_sets, w_ref, b_ref):
    shift = CONV_HALO - CONV_CTX
    offs = tuple(range(n_sets))

    def out_rows(o):
        return pl.ds(base + o, SUBLANES, stride=n_sets)

    def slab(c, _):
        def window(t):
            return win_ref[c, pl.ds(base + (shift + t), SUBLANES, stride=n_sets), :]

        bias = b_ref[pl.ds(c, 1), :]
        acc = [bias] * n_sets
        wins = [window(t) for t in range(n_sets - 1)]
        for j in range(CONV_WIDTH):
            w = w_ref[c, j:j + 1, :]
            wins.append(window(j + n_sets - 1))
            acc = [acc[o] + w * wins[o] for o in offs]
            wins.pop(0)
        for o in offs:
            zs_ref[c, out_rows(o), :] = acc[o]
        return 0

    lax.fori_loop(0, N_SLAB, slab, 0)


def _ln_silu_rows(zs_ref, rows, g_ref, bb_ref):
    acc = [zs_ref[c, rows, :] for c in range(N_SLAB)]
    tot = acc[0]
    for c in range(1, N_SLAB):
        tot = tot + acc[c]
    mu = jnp.sum(tot, axis=1, keepdims=True) * (1.0 / D_MODEL)
    cen = [a - mu for a in acc]
    sq = cen[0] * cen[0]
    for c in range(1, N_SLAB):
        sq = sq + cen[c] * cen[c]
    inv = lax.rsqrt(jnp.sum(sq, axis=1, keepdims=True) * (1.0 / D_MODEL) + LN_EPS)
    out = []
    for c in range(N_SLAB):
        z = cen[c] * inv * g_ref[c:c + 1, :] + bb_ref[c:c + 1, :]
        out.append(z * jax.nn.sigmoid(z))
    return out


def _conv_prompt_kernel(um_ref, up_ref, uc_ref, w_ref, b_ref, g_ref, bb_ref, z_ref, win_ref, zs_ref, *, tr):
    i = pl.program_id(1)
    gap = CONV_HALO - META_LEN

    @pl.when(i == 0)
    def _():
        for c in range(N_SLAB):
            win_ref[c, 0:gap, :] = jnp.zeros((gap, LANES), F32)
            win_ref[c, gap:CONV_HALO, :] = um_ref[:, c * LANES:(c + 1) * LANES]

    @pl.when(i > 0)
    def _():
        for c in range(N_SLAB):
            win_ref[c, 0:CONV_HALO, :] = up_ref[:, c * LANES:(c + 1) * LANES]

    for c in range(N_SLAB):
        win_ref[c, CONV_HALO:, :] = uc_ref[:, c * LANES:(c + 1) * LANES]

    def block(bi, _):
        base = pl.multiple_of(bi * CONV_BLOCK, CONV_BLOCK)
        _conv_rows(win_ref, zs_ref, base, CONV_BLOCK // SUBLANES, w_ref, b_ref)
        return 0

    lax.fori_loop(0, tr // CONV_BLOCK, block, 0)

    def norm(bi, _):
        rows = pl.ds(pl.multiple_of(bi * LN_ROWS, LN_ROWS), LN_ROWS)
        for c, z in enumerate(_ln_silu_rows(zs_ref, rows, g_ref, bb_ref)):
            z_ref[rows, c * LANES:(c + 1) * LANES] = z.astype(z_ref.dtype)
        return 0

    lax.fori_loop(0, tr // LN_ROWS, norm, 0, unroll=8)


def _conv_prompt(u, u_meta, w3, b3, g3, bb3, n_batch, seq, tr):
    nt = seq // tr
    cur = lambda b, i: (b * nt + i, 0)
    prev = lambda b, i: (jnp.maximum((b * seq + i * tr) // CONV_HALO - 1, 0), 0)
    fixed2 = lambda b, i: (0, 0)
    fixed3 = lambda b, i: (0, 0, 0)
    return pl.pallas_call(
        functools.partial(_conv_prompt_kernel, tr=tr),
        grid=(n_batch, nt),
        in_specs=[pl.BlockSpec((META_LEN, D_MODEL), lambda b, i: (b, 0)),
                  pl.BlockSpec((CONV_HALO, D_MODEL), prev),
                  pl.BlockSpec((tr, D_MODEL), cur),
                  pl.BlockSpec((N_SLAB, CONV_HALO, LANES), fixed3),
                  pl.BlockSpec((N_SLAB, LANES), fixed2),
                  pl.BlockSpec((N_SLAB, LANES), fixed2),
                  pl.BlockSpec((N_SLAB, LANES), fixed2)],
        out_specs=pl.BlockSpec((tr, D_MODEL), cur),
        out_shape=jax.ShapeDtypeStruct((n_batch * seq, D_MODEL), BF16),
        scratch_shapes=[pltpu.VMEM((N_SLAB, CONV_HALO + tr, LANES), F32),
                        pltpu.VMEM((N_SLAB, tr, LANES), F32)],
        compiler_params=_cparams(2),
        name="conv_prompt",
    )(u_meta, u, u, w3, b3, g3, bb3)


SHORT_ROWS = 16


def _conv_short_kernel(ctx_ref, u_ref, w_ref, b_ref, g_ref, bb_ref, z_ref, win_ref, zs_ref, *, n_seq, t_len):
    for n in range(n_seq):
        for c in range(N_SLAB):
            sl = slice(c * LANES, (c + 1) * LANES)
            win_ref[c, 0:CONV_HALO, :] = ctx_ref[n, :, sl]
            win_ref[c, CONV_HALO:CONV_HALO + t_len, :] = u_ref[n * t_len:(n + 1) * t_len, sl]
            if t_len < SHORT_ROWS:
                win_ref[c, CONV_HALO + t_len:, :] = jnp.zeros((SHORT_ROWS - t_len, LANES), F32)
        _conv_rows(win_ref, zs_ref, 0, SHORT_ROWS // SUBLANES, w_ref, b_ref)
        for c, z in enumerate(_ln_silu_rows(zs_ref, slice(0, t_len), g_ref, bb_ref)):
            z_ref[n * t_len:(n + 1) * t_len, c * LANES:(c + 1) * LANES] = z


def _conv_short(ctx_pad, u, row0, w3, b3, g3, bb3, t_len, n_seq):
    n_total = ctx_pad.shape[0]
    rows = n_seq * t_len
    blk0 = row0 // rows
    fixed2 = lambda n: (0, 0)
    fixed3 = lambda n: (0, 0, 0)
    return pl.pallas_call(
        functools.partial(_conv_short_kernel, n_seq=n_seq, t_len=t_len),
        grid=(n_total // n_seq,),
        in_specs=[pl.BlockSpec((n_seq, CONV_HALO, D_MODEL), lambda n: (n, 0, 0)),
                  pl.BlockSpec((rows, D_MODEL), lambda n: (blk0 + n, 0)),
                  pl.BlockSpec((N_SLAB, CONV_HALO, LANES), fixed3),
                  pl.BlockSpec((N_SLAB, LANES), fixed2),
                  pl.BlockSpec((N_SLAB, LANES), fixed2),
                  pl.BlockSpec((N_SLAB, LANES), fixed2)],
        out_specs=pl.BlockSpec((rows, D_MODEL), lambda n: (n, 0)),
        out_shape=jax.ShapeDtypeStruct((n_total * t_len, D_MODEL), F32),
        scratch_shapes=[pltpu.VMEM((N_SLAB, CONV_HALO + SHORT_ROWS, LANES), F32),
                        pltpu.VMEM((N_SLAB, SHORT_ROWS, LANES), F32)],
        compiler_params=_cparams(1),
        name="conv_short",
    )(ctx_pad, u, w3, b3, g3, bb3)


def _rope_angles(pos):
    inv = 1.0 / (ROPE_THETA ** (jnp.arange(0, ROT_DIM, 2, dtype=F32) / ROT_DIM))
    ang = pos.astype(F32)[:, None] * inv[None, :]
    return jnp.cos(ang), jnp.sin(ang)


def _rope_tables_t(pos):
    cos, sin = _rope_angles(pos)
    return cos.T, sin.T


def _rope_tables(pos):
    half = ROT_DIM // 2
    cos, sin = _rope_angles(pos)
    n = pos.shape[0]
    ones = jnp.ones((n, HEAD_DIM - ROT_DIM), F32)
    zeros = jnp.zeros((n, HEAD_DIM - ROT_DIM), F32)
    zh = jnp.zeros((n, half), F32)
    c = jnp.concatenate([cos, cos, ones], axis=1)
    a = jnp.concatenate([-sin, zh, zeros], axis=1)
    s = jnp.concatenate([zh, sin, zeros], axis=1)
    rep = LANES // HEAD_DIM
    return jnp.tile(c, (1, rep)), jnp.tile(a, (1, rep)), jnp.tile(s, (1, rep))


def kernel(x_prompt, x_sample, cache_attn_meta_k, cache_attn_meta_v, cache_attn_win_k, cache_attn_win_v, state_conv, meta_tokens, attn_w_qkv, attn_b_qkv, attn_sinks, attn_w_o, attn_b_o, conv_w_pw1, conv_b_pw1, conv_w_dw, conv_b_dw, conv_ln_g, conv_ln_b, conv_w_pw2, conv_b_pw2, ln_mix_g, ln_mix_b, ln_ffn_g, ln_ffn_b, moe_w_router, moe_b_router, moe_w_gate_up, moe_b_gate_up, moe_w_down, moe_b_down):
    n_batch, seq, _ = x_prompt.shape
    dec_batch, dec_seq, _ = x_sample.shape
    n_real = n_batch * seq
    n_meta = n_batch * META_LEN
    n_samp = dec_batch * dec_seq
    n_small = n_meta + n_samp
    tb = _pick_tile(seq, 512)
    td = _pick_tile(seq, 1024)
    ts = _pick_tile(n_small, 512)
    row2 = lambda v: v.reshape(1, -1)

    meta_rows = jnp.broadcast_to(meta_tokens[None], (n_batch, META_LEN, D_MODEL)).reshape(n_meta, D_MODEL)
    xb = x_prompt.reshape(n_real, D_MODEL)
    xs = jnp.concatenate([meta_rows.astype(F32), x_sample.reshape(n_samp, D_MODEL)], axis=0)

    def moe_layer(i, a_big, a_small, w, b, xb, xs):
        w_bf = w.astype(BF16)
        wr = moe_w_router[i].T.astype(BF16)
        br = moe_b_router[i].reshape(N_EXPERTS, 1)
        lng, lnb = row2(ln_mix_g[i]), row2(ln_mix_b[i])
        zero_counts = jnp.zeros((N_EXPERTS, LANES), jnp.int32)
        x1b, idb, gab, rab, cnt = _proj_ln_route(a_big, w_bf, row2(b), xb, lng, lnb, wr, br, zero_counts, td)
        x1s, ids_, gas, ras, cnt = _proj_ln_route(a_small, w_bf, row2(b), xs, lng, lnb, wr, br, cnt, ts)
        n_all = moe_w_gate_up.shape[0] * N_EXPERTS
        return _moe([(x1b, idb, gab, rab, tb), (x1s, ids_, gas, ras, ts)], cnt, i * N_EXPERTS,
                    moe_w_gate_up.reshape(n_all, D_MODEL, 2 * MOE_FF),
                    moe_b_gate_up.reshape(n_all, 1, 2 * MOE_FF),
                    moe_w_down.reshape(n_all, MOE_FF, D_MODEL),
                    moe_b_down.reshape(n_all, 1, D_MODEL),
                    ln_ffn_g[i].reshape(N_SLAB, LANES), ln_ffn_b[i].reshape(N_SLAB, LANES))

    w_qkv = attn_w_qkv[0].astype(BF16)
    b_qkv = row2(attn_b_qkv[0])
    pos_small = jnp.concatenate([jnp.tile(jnp.arange(META_LEN), n_batch),
                                 jnp.tile(PAST_LEN + jnp.arange(dec_seq), dec_batch)])
    pos_big = META_LEN + jnp.arange(seq)
    qtb, kb, vb, vtb = _qkv_rope_t(xb, w_qkv, b_qkv, *_rope_tables(pos_big), *_rope_tables_t(pos_big), td)
    qs, ks, vs = _qkv_rope(xs, w_qkv, b_qkv, *_rope_tables(pos_small), ts)
    sinks = attn_sinks[0]
    cmk = cache_attn_meta_k[0].reshape(dec_batch, META_LEN, KV_DIM)
    cmv = cache_attn_meta_v[0].reshape(dec_batch, META_LEN, KV_DIM)
    cwk = cache_attn_win_k[0].reshape(dec_batch, WINDOW, KV_DIM)
    cwv = cache_attn_win_v[0].reshape(dec_batch, WINDOW, KV_DIM)
    vt_meta = vs[:n_meta].reshape(n_batch, META_LEN, KV_DIM).transpose(0, 2, 1).astype(BF16)
    o_big = _attn_prompt(qtb, kb, vtb, ks, vt_meta, sinks, n_batch, seq)
    o_small = jnp.concatenate([
        _attn_meta(qs, ks, vs, sinks, n_batch),
        _attn_sample(qs, ks, vs, cmk, cmv, cwk, cwv, sinks, n_meta, dec_batch, dec_seq)], axis=0)
    xb, xs = moe_layer(0, o_big, o_small, attn_w_o[0], attn_b_o[0], xb, xs)

    kv4 = lambda a, n, t_len: a.reshape(n, t_len, N_KV_HEADS, HEAD_DIM)
    p_meta_k = kv4(ks[:n_meta], n_batch, META_LEN)[None]
    p_meta_v = kv4(vs[:n_meta], n_batch, META_LEN)[None]
    last_win = lambda a: a.reshape(n_batch, seq, KV_DIM)[:, seq - WINDOW:].reshape(n_batch * WINDOW, KV_DIM)
    p_win_k = kv4(last_win(kb), n_batch, WINDOW)[None]
    p_win_v = kv4(last_win(vb), n_batch, WINDOW)[None]
    k_new = kv4(ks[n_meta:], dec_batch, dec_seq)
    v_new = kv4(vs[n_meta:], dec_batch, dec_seq)
    def roll_in(old, new, keep):
        n_new = new.shape[1]
        if n_new >= keep:
            return new[:, n_new - keep:]
        return jnp.concatenate([old[:, old.shape[1] - (keep - n_new):], new], axis=1)

    s_win_k = roll_in(cache_attn_win_k[0], k_new, cache_attn_win_k.shape[2])[None]
    s_win_v = roll_in(cache_attn_win_v[0], v_new, cache_attn_win_v.shape[2])[None]

    w_pw1 = conv_w_pw1[0].astype(BF16)
    ub = _pw1_glu(xb, w_pw1, row2(conv_b_pw1[0]), td)
    us = _pw1_glu(xs, w_pw1, row2(conv_b_pw1[0]), ts)
    slab = lambda v: v.reshape(N_SLAB, LANES)
    w_dw = jnp.pad(conv_w_dw[0], ((0, CONV_HALO - CONV_WIDTH), (0, 0)))
    w3 = w_dw.reshape(CONV_HALO, N_SLAB, LANES).transpose(1, 0, 2)
    conv_args = (w3, slab(conv_b_dw[0]), slab(conv_ln_g[0]), slab(conv_ln_b[0]))
    z_big = _conv_prompt(ub, us, *conv_args, n_batch, seq, td)
    z_meta = _conv_short(jnp.zeros((n_batch, CONV_HALO, D_MODEL), F32), us, 0, *conv_args, META_LEN, 1)
    ctx_pad = jnp.pad(state_conv[0], ((0, 0), (CONV_HALO - CONV_CTX, 0), (0, 0)))
    z_samp = _conv_short(ctx_pad, us, n_meta, *conv_args, dec_seq, 2)
    z_small = jnp.concatenate([z_meta, z_samp], axis=0).astype(BF16)
    xb, xs = moe_layer(1, z_big, z_small, conv_w_pw2[0], conv_b_pw2[0], xb, xs)

    p_conv = ub.reshape(n_batch, seq, D_MODEL)[:, seq - CONV_CTX:][None]
    u_samp = us[n_meta:].reshape(dec_batch, dec_seq, D_MODEL)
    s_conv = roll_in(state_conv[0], u_samp, CONV_CTX)[None]

    y_prompt = xb.reshape(n_batch, seq, D_MODEL)
    y_sample = xs[n_meta:].reshape(dec_batch, dec_seq, D_MODEL)
    return (y_prompt, y_sample, p_meta_k, p_meta_v, p_win_k, p_win_v, p_conv,
            s_win_k, s_win_v, s_conv)
```

```python
import functools

import numpy as np
import jax
import jax.numpy as jnp
from jax import lax
from jax.experimental import pallas as pl
from jax.experimental.pallas import tpu as pltpu

F32 = jnp.float32
BF16 = jnp.bfloat16

D_MODEL = 1024
HEAD_DIM = 64
N_HEADS = 16
N_KV_HEADS = 4
GROUP = N_HEADS // N_KV_HEADS
KV_DIM = N_KV_HEADS * HEAD_DIM
QKV_DIM = D_MODEL + 2 * KV_DIM
ROT_DIM = 16
ROPE_THETA = 500000.0
WINDOW = 128
ATTN_BLOCK = 128
ATTN_SCALE = HEAD_DIM ** -0.5
LOG2_E = 1.4426950408889634
Q_SCALE = ATTN_SCALE * LOG2_E
META_LEN = 16
CONV_WIDTH = 31
CONV_CTX = CONV_WIDTH - 1
N_EXPERTS = 32
TOP_K = 4
MOE_FF = 1024
SWIGLU_LIMIT = 7.0
SWIGLU_ALPHA = 1.702
LN_EPS = 1e-5
DEPTH = 2
DEEPNORM_ALPHA = (2 * DEPTH) ** 0.25
PAST_LEN = 16384
NEG_INF = -1e30

LANES = 128
SUBLANES = 8
GMM_TILE = 512
CONV_HALO = 32
CONV_BLOCK = 32
LN_ROWS = 16
N_SLAB = D_MODEL // LANES
COMBINE_ROWS = 32
SUM_GROUP = 32
IDX_SLOTS = 3
HEADS_PER_DOT = 4
VMEM_LIMIT = 56 * 1024 * 1024


def _cparams(n_axes):
    return pltpu.CompilerParams(dimension_semantics=("arbitrary",) * n_axes,
                                vmem_limit_bytes=VMEM_LIMIT)


def _pick_tile(n, cap):
    best = None
    t = LANES
    while t <= cap:
        if n % t == 0:
            best = t
        t += LANES
    assert best is not None, n
    return best


def _layer_norm(h, g, b):
    mu = jnp.mean(h, axis=-1, keepdims=True)
    hc = h - mu
    var = jnp.mean(hc * hc, axis=-1, keepdims=True)
    return hc * lax.rsqrt(var + LN_EPS) * g + b


def _qkv_kernel(x_ref, w_ref, b_ref, c_ref, a_ref, s_ref, q_ref, k_ref, v_ref):
    x = x_ref[...].astype(BF16)
    acc = jnp.dot(x, w_ref[...], preferred_element_type=F32) + b_ref[...]
    c = c_ref[...]
    a = a_ref[...]
    s = s_ref[...]

    def rope(t):
        return t * c + pltpu.roll(t, LANES - ROT_DIM // 2, 1) * a + pltpu.roll(t, ROT_DIM // 2, 1) * s

    for j in range(D_MODEL // LANES):
        sl = slice(j * LANES, (j + 1) * LANES)
        q_ref[:, sl] = (rope(acc[:, sl]) * Q_SCALE).astype(BF16)
    for j in range(KV_DIM // LANES):
        sl = slice(D_MODEL + j * LANES, D_MODEL + (j + 1) * LANES)
        k_ref[:, j * LANES:(j + 1) * LANES] = rope(acc[:, sl])
    v_ref[...] = acc[:, D_MODEL + KV_DIM:]


def _qkv_rope(x, w_bf, b, cos_t, sa_t, sb_t, tm):
    t = x.shape[0]
    period = cos_t.shape[0] // tm
    row = lambda i: (i, 0)
    tab = lambda i: (i % period, 0)
    fixed = lambda i: (0, 0)
    return pl.pallas_call(
        _qkv_kernel,
        grid=(t // tm,),
        in_specs=[pl.BlockSpec((tm, D_MODEL), row),
                  pl.BlockSpec((D_MODEL, QKV_DIM), fixed),
                  pl.BlockSpec((1, QKV_DIM), fixed),
                  pl.BlockSpec((tm, LANES), tab),
                  pl.BlockSpec((tm, LANES), tab),
                  pl.BlockSpec((tm, LANES), tab)],
        out_specs=[pl.BlockSpec((tm, D_MODEL), row),
                   pl.BlockSpec((tm, KV_DIM), row),
                   pl.BlockSpec((tm, KV_DIM), row)],
        out_shape=[jax.ShapeDtypeStruct((t, D_MODEL), BF16),
                   jax.ShapeDtypeStruct((t, KV_DIM), F32),
                   jax.ShapeDtypeStruct((t, KV_DIM), F32)],
        compiler_params=_cparams(1),
        name="qkv_rope",
    )(x, w_bf, b, cos_t, sa_t, sb_t)


def _qkv_t_kernel(x_ref, wqt_ref, bq_ref, wkv_ref, bkv_ref, wvt_ref, bv_ref, c_ref, a_ref, s_ref,
                  ct_ref, st_ref, qt_ref, k_ref, v_ref, vt_ref):
    x = x_ref[...].astype(BF16)
    nt = (((1,), (1,)), ((), ()))
    qt = lax.dot_general(wqt_ref[...], x, nt, preferred_element_type=F32) + bq_ref[...]
    ct = ct_ref[...]
    st = st_ref[...]
    half = ROT_DIM // 2
    for h in range(N_HEADS):
        r0 = h * HEAD_DIM
        x1 = qt[r0:r0 + half]
        x2 = qt[r0 + half:r0 + ROT_DIM]
        rot = jnp.concatenate([x1 * ct - x2 * st, x2 * ct + x1 * st], axis=0)
        qt_ref[r0:r0 + ROT_DIM, :] = (rot * Q_SCALE).astype(BF16)
        qt_ref[r0 + ROT_DIM:r0 + HEAD_DIM, :] = (qt[r0 + ROT_DIM:r0 + HEAD_DIM] * Q_SCALE).astype(BF16)
    vt = lax.dot_general(wvt_ref[...], x, nt, preferred_element_type=F32) + bv_ref[...]
    vt_ref[...] = vt.astype(BF16)

    kv = jnp.dot(x, wkv_ref[...], preferred_element_type=F32) + bkv_ref[...]
    c = c_ref[...]
    a = a_ref[...]
    s = s_ref[...]
    for j in range(KV_DIM // LANES):
        t = kv[:, j * LANES:(j + 1) * LANES]
        k_ref[:, j * LANES:(j + 1) * LANES] = (
            t * c + pltpu.roll(t, LANES - half, 1) * a + pltpu.roll(t, half, 1) * s)
    v_ref[...] = kv[:, KV_DIM:]


def _qkv_rope_t(x, w_bf, b, cos_t, sa_t, sb_t, cos_tt, sin_tt, tm):
    t = x.shape[0]
    period = cos_t.shape[0] // tm
    row = lambda i: (i, 0)
    col = lambda i: (0, i)
    tab = lambda i: (i % period, 0)
    tabt = lambda i: (0, i % period)
    fixed = lambda i: (0, 0)
    wqt = w_bf[:, :D_MODEL].T
    wvt = w_bf[:, D_MODEL + KV_DIM:].T
    half = ROT_DIM // 2
    return pl.pallas_call(
        _qkv_t_kernel,
        grid=(t // tm,),
        in_specs=[pl.BlockSpec((tm, D_MODEL), row),
                  pl.BlockSpec((D_MODEL, D_MODEL), fixed),
                  pl.BlockSpec((D_MODEL, 1), fixed),
                  pl.BlockSpec((D_MODEL, 2 * KV_DIM), fixed),
                  pl.BlockSpec((1, 2 * KV_DIM), fixed),
                  pl.BlockSpec((KV_DIM, D_MODEL), fixed),
                  pl.BlockSpec((KV_DIM, 1), fixed),
                  pl.BlockSpec((tm, LANES), tab),
                  pl.BlockSpec((tm, LANES), tab),
                  pl.BlockSpec((tm, LANES), tab),
                  pl.BlockSpec((half, tm), tabt),
                  pl.BlockSpec((half, tm), tabt)],
        out_specs=[pl.BlockSpec((D_MODEL, tm), col),
                   pl.BlockSpec((tm, KV_DIM), row),
                   pl.BlockSpec((tm, KV_DIM), row),
                   pl.BlockSpec((KV_DIM, tm), col)],
        out_shape=[jax.ShapeDtypeStruct((D_MODEL, t), BF16),
                   jax.ShapeDtypeStruct((t, KV_DIM), F32),
                   jax.ShapeDtypeStruct((t, KV_DIM), F32),
                   jax.ShapeDtypeStruct((KV_DIM, t), BF16)],
        compiler_params=_cparams(1),
        name="qkv_rope_t",
    )(x, wqt, b[:, :D_MODEL].reshape(D_MODEL, 1), w_bf[:, D_MODEL:], b[:, D_MODEL:],
      wvt, b[:, D_MODEL + KV_DIM:].reshape(KV_DIM, 1), cos_t, sa_t, sb_t, cos_tt, sin_tt)


def _attend(q, kcat, vcat, bias, sink_ref, write):
    kgs = [kcat[:, g * HEAD_DIM:(g + 1) * HEAD_DIM] for g in range(N_KV_HEADS)]

    def scores(h):
        qh = q[:, h * HEAD_DIM:(h + 1) * HEAD_DIM]
        return lax.dot_general(qh, kgs[h // GROUP], (((1,), (1,)), ((), ())), preferred_element_type=F32) + bias

    s_next = scores(0)
    for h in range(N_HEADS):
        s = s_next
        if h + 1 < N_HEADS:
            s_next = scores(h + 1)
        vg = vcat[:, (h // GROUP) * HEAD_DIM:(h // GROUP + 1) * HEAD_DIM]
        sink = sink_ref[h] * LOG2_E
        m = jnp.maximum(jnp.max(s, axis=-1, keepdims=True), sink)
        p = jnp.exp2(s - m)
        denom = jnp.sum(p, axis=-1, keepdims=True) + jnp.exp2(sink - m)
        o = jnp.dot(p.astype(BF16), vg, preferred_element_type=F32)
        write(h, o / denom)


def _attn_prompt_kernel(sink_ref, qt_ref, kp_ref, ko_ref, km_ref, vtp_ref, vto_ref, vtm_ref, o_ref, acc_ref):
    i = pl.program_id(1)
    kcat = jnp.concatenate([kp_ref[...], ko_ref[...], km_ref[...]], axis=0).astype(BF16)
    vt = jnp.concatenate([vtp_ref[...], vto_ref[...], vtm_ref[...]], axis=1)
    nk = 2 * ATTN_BLOCK + META_LEN
    key = lax.broadcasted_iota(jnp.int32, (nk, ATTN_BLOCK), 0)
    qry = lax.broadcasted_iota(jnp.int32, (nk, ATTN_BLOCK), 1)
    first = jnp.where(i > 0, 0, 2 * ATTN_BLOCK)
    prev_ok = (key < ATTN_BLOCK) & (key >= qry + first)
    own_ok = (key >= ATTN_BLOCK) & (key - ATTN_BLOCK <= qry)
    mask = prev_ok | own_ok | (key >= 2 * ATTN_BLOCK)
    bias = jnp.where(mask, 0.0, NEG_INF)
    bias = jnp.concatenate([bias] * HEADS_PER_DOT, axis=1)
    lane = lax.broadcasted_iota(jnp.int32, (1, HEADS_PER_DOT * ATTN_BLOCK), 1)
    kgs = [kcat[:, g * HEAD_DIM:(g + 1) * HEAD_DIM] for g in range(N_KV_HEADS)]

    def scores(u):
        h0 = u * HEADS_PER_DOT
        rhs = jnp.concatenate([qt_ref[(h0 + j) * HEAD_DIM:(h0 + j + 1) * HEAD_DIM, :]
                               for j in range(HEADS_PER_DOT)], axis=1)
        return jnp.dot(kgs[h0 // GROUP], rhs, preferred_element_type=F32) + bias

    n_units = N_HEADS // HEADS_PER_DOT
    s_next = scores(0)
    for u in range(n_units):
        s = s_next
        if u + 1 < n_units:
            s_next = scores(u + 1)
        h0 = u * HEADS_PER_DOT
        g = h0 // GROUP
        sink = sink_ref[h0] * LOG2_E
        for j in range(1, HEADS_PER_DOT):
            sink = jnp.where(lane >= j * ATTN_BLOCK, sink_ref[h0 + j] * LOG2_E, sink)
        m = jnp.maximum(jnp.max(s, axis=0, keepdims=True), sink)
        p = jnp.exp2(s - m)
        denom = jnp.sum(p, axis=0, keepdims=True) + jnp.exp2(sink - m)
        o = jnp.dot(vt[g * HEAD_DIM:(g + 1) * HEAD_DIM, :], p.astype(BF16), preferred_element_type=F32)
        o = o * (1.0 / denom)
        for j in range(HEADS_PER_DOT):
            acc_ref[(h0 + j) * HEAD_DIM:(h0 + j + 1) * HEAD_DIM, :] = o[:, j * ATTN_BLOCK:(j + 1) * ATTN_BLOCK]
    o_ref[...] = acc_ref[...].T.astype(o_ref.dtype)


def _attn_prompt(qt, k, vt, k_meta, vt_meta, sinks, n_batch, seq):
    nblk = seq // ATTN_BLOCK
    own = lambda b, i: b * nblk + i
    prev = lambda b, i: b * nblk + jnp.maximum(i - 1, 0)
    kspec = lambda m: pl.BlockSpec((ATTN_BLOCK, KV_DIM), lambda b, i: (m(b, i), 0))
    vspec = lambda m: pl.BlockSpec((KV_DIM, ATTN_BLOCK), lambda b, i: (0, m(b, i)))
    return pl.pallas_call(
        _attn_prompt_kernel,
        grid=(n_batch, nblk),
        in_specs=[pl.BlockSpec(memory_space=pltpu.SMEM),
                  pl.BlockSpec((D_MODEL, ATTN_BLOCK), lambda b, i: (0, own(b, i))),
                  kspec(prev), kspec(own),
                  pl.BlockSpec((META_LEN, KV_DIM), lambda b, i: (b, 0)),
                  vspec(prev), vspec(own),
                  pl.BlockSpec((None, KV_DIM, META_LEN), lambda b, i: (b, 0, 0))],
        out_specs=pl.BlockSpec((ATTN_BLOCK, D_MODEL), lambda b, i: (own(b, i), 0)),
        out_shape=jax.ShapeDtypeStruct((n_batch * seq, D_MODEL), BF16),
        scratch_shapes=[pltpu.VMEM((D_MODEL, ATTN_BLOCK), F32)],
        compiler_params=_cparams(2),
        name="attn_prompt",
    )(sinks, qt, k, k, k_meta, vt, vt, vt_meta)


def _attn_meta_kernel(sink_ref, q_ref, k_ref, v_ref, o_ref):
    r = lax.broadcasted_iota(jnp.int32, (META_LEN, META_LEN), 0)
    c = lax.broadcasted_iota(jnp.int32, (META_LEN, META_LEN), 1)

    def write(h, o):
        o_ref[:, h * HEAD_DIM:(h + 1) * HEAD_DIM] = o.astype(o_ref.dtype)

    bias = jnp.where(c <= r, 0.0, NEG_INF)
    _attend(q_ref[...], k_ref[...].astype(BF16), v_ref[...].astype(BF16), bias, sink_ref, write)


def _attn_meta(q, k, v, sinks, n_batch):
    imap = lambda b: (b, 0)
    return pl.pallas_call(
        _attn_meta_kernel,
        grid=(n_batch,),
        in_specs=[pl.BlockSpec(memory_space=pltpu.SMEM),
                  pl.BlockSpec((META_LEN, D_MODEL), imap),
                  pl.BlockSpec((META_LEN, KV_DIM), imap),
                  pl.BlockSpec((META_LEN, KV_DIM), imap)],
        out_specs=pl.BlockSpec((META_LEN, D_MODEL), lambda b: (b, 0)),
        out_shape=jax.ShapeDtypeStruct((n_batch * META_LEN, D_MODEL), BF16),
        compiler_params=_cparams(1),
        name="attn_meta",
    )(sinks, q, k, v)


SAMPLE_GROUP = 8


def _attn_sample_kernel(sink_ref, bias_ref, q_ref, kn_ref, vn_ref, cmk_ref, cmv_ref, cwk_ref, cwv_ref,
                        o_ref, *, dec_seq, group):
    def keys(cm_ref, cw_ref, new):
        parts = []
        for j in range(group):
            parts += [cm_ref[j], cw_ref[j], new[j * dec_seq:(j + 1) * dec_seq]]
        return jnp.concatenate(parts, axis=0).astype(BF16)

    kcat = keys(cmk_ref, cwk_ref, kn_ref[...])
    vcat = keys(cmv_ref, cwv_ref, vn_ref[...])

    def write(h, o):
        o_ref[:, h * HEAD_DIM:(h + 1) * HEAD_DIM] = o.astype(o_ref.dtype)

    _attend(q_ref[...], kcat, vcat, bias_ref[...], sink_ref, write)


def _sample_bias(group, dec_seq):
    per = META_LEN + WINDOW + dec_seq
    r = np.arange(group * dec_seq)[:, None]
    c = np.arange(group * per)[None, :]
    tq, ck = r % dec_seq, c % per
    win_ok = (ck >= META_LEN) & (ck < META_LEN + WINDOW) & (ck - META_LEN >= tq)
    new_ok = (ck >= META_LEN + WINDOW) & (ck - (META_LEN + WINDOW) <= tq)
    vis = (r // dec_seq == c // per) & ((ck < META_LEN) | win_ok | new_ok)
    return np.where(vis, 0.0, NEG_INF).astype(np.float32)


def _attn_sample(q, k, v, cmk, cmv, cwk, cwv, sinks, row0, dec_batch, dec_seq):
    group = SAMPLE_GROUP
    nq = group * dec_seq
    blk0 = row0 // nq
    bias = jnp.asarray(_sample_bias(group, dec_seq))
    qmap = lambda n: (blk0 + n, 0)
    cmap = lambda n: (n, 0, 0)
    return pl.pallas_call(
        functools.partial(_attn_sample_kernel, dec_seq=dec_seq, group=group),
        grid=(dec_batch // group,),
        in_specs=[pl.BlockSpec(memory_space=pltpu.SMEM),
                  pl.BlockSpec(bias.shape, lambda n: (0, 0)),
                  pl.BlockSpec((nq, D_MODEL), qmap),
                  pl.BlockSpec((nq, KV_DIM), qmap),
                  pl.BlockSpec((nq, KV_DIM), qmap),
                  pl.BlockSpec((group, META_LEN, KV_DIM), cmap),
                  pl.BlockSpec((group, META_LEN, KV_DIM), cmap),
                  pl.BlockSpec((group, WINDOW, KV_DIM), cmap),
                  pl.BlockSpec((group, WINDOW, KV_DIM), cmap)],
        out_specs=pl.BlockSpec((nq, D_MODEL), lambda n: (n, 0)),
        out_shape=jax.ShapeDtypeStruct((dec_batch * dec_seq, D_MODEL), BF16),
        compiler_params=_cparams(1),
        name="attn_sample",
    )(sinks, bias, q, k, v, cmk, cmv, cwk, cwv)


def _proj_ln_route_kernel(a_ref, w_ref, b_ref, x_ref, g_ref, bb_ref, wr_ref, br_ref, tri_ref, cin_ref,
                          x1_ref, ids_ref, gates_ref, rank_ref, cnt_ref, carry_ref):
    @pl.when(pl.program_id(0) == 0)
    def _():
        carry_ref[...] = cin_ref[...].astype(F32)

    y = jnp.dot(a_ref[...], w_ref[...], preferred_element_type=F32) + b_ref[...]
    x1 = _layer_norm(DEEPNORM_ALPHA * x_ref[...] + y, g_ref[...], bb_ref[...])
    for j in range(N_SLAB):
        x1_ref[pl.ds(j, x1.shape[0], stride=N_SLAB), :] = x1[:, j * LANES:(j + 1) * LANES]

    logits = lax.dot_general(wr_ref[...], x1.astype(BF16), (((1,), (1,)), ((), ())),
                             preferred_element_type=F32) + br_ref[...]
    tm = logits.shape[1]
    eidx = lax.broadcasted_iota(jnp.int32, (N_EXPERTS, tm), 0).astype(F32)
    cur = logits
    vals, idxs, sels = [], [], []
    for _ in range(TOP_K):
        m = jnp.max(cur, axis=0, keepdims=True)
        idx = jnp.min(jnp.where(cur == m, eidx, float(N_EXPERTS)), axis=0, keepdims=True)
        sel = eidx == idx
        vals.append(m)
        idxs.append(idx)
        sels.append(sel)
        cur = jnp.where(sel, -jnp.inf, cur)
    exps = [jnp.exp(v - vals[0]) for v in vals]
    tot = exps[0] + exps[1] + exps[2] + exps[3]
    gates_ref[...] = jnp.concatenate([e / tot for e in exps], axis=0)
    ids_ref[...] = jnp.concatenate(idxs, axis=0).astype(jnp.int32)

    chosen = jnp.where(sels[0] | sels[1] | sels[2] | sels[3], 1.0, 0.0)
    before = jnp.dot(chosen.astype(BF16), tri_ref[...], preferred_element_type=F32)
    before = before + carry_ref[:, 0:1]
    ranks = [jnp.sum(jnp.where(s, before, 0.0), axis=0, keepdims=True) for s in sels]
    rank_ref[...] = jnp.concatenate(ranks, axis=0).astype(jnp.int32)
    carry_ref[...] = carry_ref[...] + jnp.sum(chosen, axis=1, keepdims=True)
    cnt_ref[...] = carry_ref[...].astype(jnp.int32)


def _proj_ln_route(a_bf, w_bf, b, x, g, bb, wr_t_bf, br_col, counts_in, tm):
    t = x.shape[0]
    row = lambda i: (i, 0)
    col = lambda i: (0, i)
    fixed = lambda i: (0, 0)
    tri = (jnp.arange(tm)[:, None] < jnp.arange(tm)[None, :]).astype(BF16)
    return pl.pallas_call(
        _proj_ln_route_kernel,
        grid=(t // tm,),
        in_specs=[pl.BlockSpec((tm, D_MODEL), row),
                  pl.BlockSpec((D_MODEL, D_MODEL), fixed),
                  pl.BlockSpec((1, D_MODEL), fixed),
                  pl.BlockSpec((tm, D_MODEL), row),
                  pl.BlockSpec((1, D_MODEL), fixed),
                  pl.BlockSpec((1, D_MODEL), fixed),
                  pl.BlockSpec((N_EXPERTS, D_MODEL), fixed),
                  pl.BlockSpec((N_EXPERTS, 1), fixed),
                  pl.BlockSpec((tm, tm), fixed),
                  pl.BlockSpec((N_EXPERTS, LANES), fixed)],
        out_specs=[pl.BlockSpec((tm * N_SLAB, LANES), row),
                   pl.BlockSpec((TOP_K, tm), col),
                   pl.BlockSpec((TOP_K, tm), col),
                   pl.BlockSpec((TOP_K, tm), col),
                   pl.BlockSpec((N_EXPERTS, LANES), fixed)],
        out_shape=[jax.ShapeDtypeStruct((t * N_SLAB, LANES), F32),
                   jax.ShapeDtypeStruct((TOP_K, t), jnp.int32),
                   jax.ShapeDtypeStruct((TOP_K, t), F32),
                   jax.ShapeDtypeStruct((TOP_K, t), jnp.int32),
                   jax.ShapeDtypeStruct((N_EXPERTS, LANES), jnp.int32)],
        scratch_shapes=[pltpu.VMEM((N_EXPERTS, LANES), F32)],
        compiler_params=_cparams(1),
        name="proj_ln_route",
    )(a_bf, w_bf, b, x, g, bb, wr_t_bf, br_col, tri, counts_in)


def _idx_copy(dest_hbm, dsm, sem, step, stride, n_slots=2):
    slot = step % n_slots
    return pltpu.make_async_copy(dest_hbm.at[pl.ds(pl.multiple_of(step * stride, stride), stride)],
                                 dsm.at[pl.ds(pl.multiple_of(slot * stride, stride), stride)],
                                 sem)


def _dispatch_kernel(pstart_ref, padded_ref, nact_ref, dest_hbm, *rest, tiles, steps, stride, n_tiles):
    x_refs = rest[:len(tiles)]
    xs_hbm, dsm, zbuf, sems = rest[len(tiles):]
    i = pl.program_id(0)
    nsteps = pl.num_programs(0)

    def zero_tile(row0):
        n = GMM_TILE * N_SLAB
        return pltpu.make_async_copy(zbuf, xs_hbm.at[pl.ds(pl.multiple_of(row0 * N_SLAB, n), n)], sems.at[2])

    def token(ref, r):
        return ref.at[pl.ds(pl.multiple_of(r * N_SLAB, N_SLAB), N_SLAB)]

    @pl.when(i == 0)
    def _():
        zbuf[...] = jnp.zeros_like(zbuf)

        def each_expert(fn):
            def body(e, _):
                @pl.when(padded_ref[e] > 0)
                def _():
                    fn(zero_tile(pstart_ref[e] + padded_ref[e] - GMM_TILE))
                return 0
            lax.fori_loop(0, N_EXPERTS, body, 0)

        def each_tail(fn):
            def body(t, _):
                fn(zero_tile(t * GMM_TILE))
                return 0
            lax.fori_loop(nact_ref[0], n_tiles, body, 0)

        each_expert(lambda cp: cp.start())
        each_tail(lambda cp: cp.start())
        each_expert(lambda cp: cp.wait())
        each_tail(lambda cp: cp.wait())
        _idx_copy(dest_hbm, dsm, sems.at[0], i, stride).start()

    _idx_copy(dest_hbm, dsm, sems.at[0], i, stride).wait()

    @pl.when(i + 1 < nsteps)
    def _():
        _idx_copy(dest_hbm, dsm, sems.at[0], i + 1, stride).start()

    base = (i % 2) * stride
    step0 = 0
    for x_ref, tm, n in zip(x_refs, tiles, steps):
        @pl.when((i >= step0) & (i < step0 + n))
        def _(x_ref=x_ref, tm=tm):
            def row_body(r, _):
                for k in range(TOP_K):
                    d = dsm[base + k * tm + r]
                    pltpu.make_async_copy(token(x_ref, r), token(xs_hbm, d), sems.at[1]).start(priority=k % 2)
                return 0

            lax.fori_loop(0, tm, row_body, 0, unroll=4)
            for k in range(TOP_K):
                pltpu.make_async_copy(x_ref, xs_hbm.at[pl.ds(0, tm * N_SLAB)], sems.at[1]).wait()
        step0 += n


def _seg_map(step0, n):
    return lambda i, *_: (jnp.clip(i - step0, 0, n - 1), 0)


def _dispatch(xs_list, tiles, dest_steps, pstart, padded, nact, stride, n_tiles):
    steps = [x.shape[0] // (tm * N_SLAB) for x, tm in zip(xs_list, tiles)]
    in_specs = [pl.BlockSpec(memory_space=pl.ANY)]
    step0 = 0
    for tm, n in zip(tiles, steps):
        in_specs.append(pl.BlockSpec((tm * N_SLAB, LANES), _seg_map(step0, n)))
        step0 += n
    grid_spec = pltpu.PrefetchScalarGridSpec(
        num_scalar_prefetch=3,
        grid=(sum(steps),),
        in_specs=in_specs,
        out_specs=pl.BlockSpec(memory_space=pl.ANY),
        scratch_shapes=[pltpu.SMEM((2 * stride,), jnp.int32),
                        pltpu.VMEM((GMM_TILE * N_SLAB, LANES), F32),
                        pltpu.SemaphoreType.DMA((3,))],
    )
    return pl.pallas_call(
        functools.partial(_dispatch_kernel, tiles=tuple(tiles), steps=tuple(steps), stride=stride,
                          n_tiles=n_tiles),
        grid_spec=grid_spec,
        out_shape=jax.ShapeDtypeStruct((n_tiles * GMM_TILE * N_SLAB, LANES), F32),
        compiler_params=_cparams(1),
        name="moe_dispatch",
    )(pstart, padded, nact, dest_steps, *xs_list)


def _gmm_kernel(te_ref, tsrc_ref, tfirst_ref, nact_ref, x_ref, wgu_ref, bgu_ref, wdn_ref, bdn_ref, y_ref,
                wgu_bf, wdn_bf):
    i = pl.program_id(0)

    @pl.when(tfirst_ref[i] == 1)
    def _():
        wgu_bf[...] = wgu_ref[...].astype(BF16)
        wdn_bf[...] = wdn_ref[...].astype(BF16)

    @pl.when(i < nact_ref[0])
    def _():
        x = jnp.concatenate([x_ref[pl.ds(j, GMM_TILE, stride=N_SLAB), :] for j in range(N_SLAB)], axis=1)
        gu = jnp.dot(x.astype(BF16), wgu_bf[...], preferred_element_type=F32) + bgu_ref[...]
        gate = jnp.minimum(gu[:, :MOE_FF], SWIGLU_LIMIT)
        up = jnp.clip(gu[:, MOE_FF:], -SWIGLU_LIMIT, SWIGLU_LIMIT)
        glu = gate * jax.nn.sigmoid(SWIGLU_ALPHA * gate)
        h = ((up + 1.0) * glu).astype(BF16)
        y = jnp.dot(h, wdn_bf[...], preferred_element_type=F32) + bdn_ref[...]
        for j in range(N_SLAB):
            y_ref[pl.ds(j, GMM_TILE, stride=N_SLAB), :] = y[:, j * LANES:(j + 1) * LANES]

    @pl.when(i >= nact_ref[0])
    def _():
        y_ref[...] = jnp.zeros_like(y_ref)


def _gmm(xs, te, tsrc, tfirst, nact, wgu, bgu, wdn, bdn, n_tiles):
    emap = lambda i, te, *_: (te[i], 0, 0)
    grid_spec = pltpu.PrefetchScalarGridSpec(
        num_scalar_prefetch=4,
        grid=(n_tiles,),
        in_specs=[pl.BlockSpec((GMM_TILE * N_SLAB, LANES), lambda i, te, ts, *_: (ts[i], 0)),
                  pl.BlockSpec((None, D_MODEL, 2 * MOE_FF), emap),
                  pl.BlockSpec((None, 1, 2 * MOE_FF), emap),
                  pl.BlockSpec((None, MOE_FF, D_MODEL), emap),
                  pl.BlockSpec((None, 1, D_MODEL), emap)],
        out_specs=pl.BlockSpec((GMM_TILE * N_SLAB, LANES), lambda i, *_: (i, 0)),
        scratch_shapes=[pltpu.VMEM((D_MODEL, 2 * MOE_FF), BF16),
                        pltpu.VMEM((MOE_FF, D_MODEL), BF16)],
    )
    return pl.pallas_call(
        _gmm_kernel,
        grid_spec=grid_spec,
        out_shape=jax.ShapeDtypeStruct((n_tiles * GMM_TILE * N_SLAB, LANES), F32),
        compiler_params=_cparams(1),
        name="moe_gmm",
    )(te, tsrc, tfirst, nact, xs, wgu, bgu, wdn, bdn)


def _combine_kernel(dest_hbm, gate_hbm, ys_hbm, x_ref, g_ref, bb_ref, o_ref, dsm, gsm, buf, sems,
                    *, tm, stride):
    i = pl.program_id(0)
    nsteps = pl.num_programs(0)

    def idx_copies(step):
        slot = step % IDX_SLOTS
        return (_idx_copy(dest_hbm, dsm, sems.at[slot], step, stride, IDX_SLOTS),
                _idx_copy(gate_hbm, gsm, sems.at[IDX_SLOTS + slot], step, stride, IDX_SLOTS))

    def token(ref, r):
        return ref.at[pl.ds(pl.multiple_of(r * N_SLAB, N_SLAB), N_SLAB)]

    def issue_rows(step, parity):
        base = (step % IDX_SLOTS) * stride

        def body(r, _):
            for k in range(TOP_K):
                d = dsm[base + k * tm + r]
                pltpu.make_async_copy(token(ys_hbm, d), token(buf.at[parity * TOP_K + k], r),
                                      sems.at[2 * IDX_SLOTS + parity]).start(priority=k % 2)
            return 0

        lax.fori_loop(0, tm, body, 0, unroll=4)

    @pl.when(i == 0)
    def _():
        for cp in idx_copies(0):
            cp.start()

        @pl.when(nsteps > 1)
        def _():
            for cp in idx_copies(1):
                cp.start()

        for cp in idx_copies(0):
            cp.wait()

        issue_rows(0, 0)

    @pl.when(i + 1 < nsteps)
    def _():
        for cp in idx_copies(i + 1):
            cp.wait()

    @pl.when(i + 2 < nsteps)
    def _():
        for cp in idx_copies(i + 2):
            cp.start()

    for parity in range(2):
        @pl.when((i + 1 < nsteps) & ((i + 1) % 2 == parity))
        def _(parity=parity):
            issue_rows(i + 1, parity)

    gbase = (i % IDX_SLOTS) * stride

    def compute(parity):
        half = parity * TOP_K
        for k in range(TOP_K):
            pltpu.make_async_copy(ys_hbm.at[pl.ds(0, tm * N_SLAB)], buf.at[half + k],
                                  sems.at[2 * IDX_SLOTS + parity]).wait()

        def token_sum(g, _):
            r0 = g * SUM_GROUP
            row0 = pl.multiple_of(r0 * N_SLAB, SUM_GROUP * N_SLAB)
            for t in range(SUM_GROUP):
                rows = pl.ds(row0 + t * N_SLAB, N_SLAB)
                h = DEEPNORM_ALPHA * x_ref[rows, :]
                for k in range(TOP_K):
                    h = h + gsm[gbase + r0 + (k * tm + t)] * buf[half + k, rows, :]
                buf[half, rows, :] = h
            return 0

        lax.fori_loop(0, tm // SUM_GROUP, token_sum, 0)

        def block(b, _):
            r0 = pl.multiple_of(b * COMBINE_ROWS, COMBINE_ROWS)
            hs = [buf[half, pl.ds(r0 * N_SLAB + j, COMBINE_ROWS, stride=N_SLAB), :] for j in range(N_SLAB)]
            tot = hs[0]
            for j in range(1, N_SLAB):
                tot = tot + hs[j]
            mu = jnp.sum(tot, axis=1, keepdims=True) * (1.0 / D_MODEL)
            cen = [h - mu for h in hs]
            sq = cen[0] * cen[0]
            for j in range(1, N_SLAB):
                sq = sq + cen[j] * cen[j]
            inv = lax.rsqrt(jnp.sum(sq, axis=1, keepdims=True) * (1.0 / D_MODEL) + LN_EPS)
            for j in range(N_SLAB):
                o_ref[pl.ds(r0, COMBINE_ROWS), j * LANES:(j + 1) * LANES] = (
                    cen[j] * inv * g_ref[j:j + 1, :] + bb_ref[j:j + 1, :])
            return 0

        lax.fori_loop(0, tm // COMBINE_ROWS, block, 0, unroll=4)

    for parity in range(2):
        @pl.when(i % 2 == parity)
        def _(parity=parity):
            compute(parity)


def _combine(ys, dest_steps, gate_steps, x_tiles, g3, bb3, tm, stride):
    t = x_tiles.shape[0] // N_SLAB
    row = lambda i: (i, 0)
    fixed = lambda i: (0, 0)
    return pl.pallas_call(
        functools.partial(_combine_kernel, tm=tm, stride=stride),
        grid=(t // tm,),
        in_specs=[pl.BlockSpec(memory_space=pl.ANY),
                  pl.BlockSpec(memory_space=pl.ANY),
                  pl.BlockSpec(memory_space=pl.ANY),
                  pl.BlockSpec((tm * N_SLAB, LANES), row),
                  pl.BlockSpec((N_SLAB, LANES), fixed),
                  pl.BlockSpec((N_SLAB, LANES), fixed)],
        out_specs=pl.BlockSpec((tm, D_MODEL), row),
        out_shape=jax.ShapeDtypeStruct((t, D_MODEL), F32),
        scratch_shapes=[pltpu.SMEM((IDX_SLOTS * stride,), jnp.int32),
                        pltpu.SMEM((IDX_SLOTS * stride,), F32),
                        pltpu.VMEM((2 * TOP_K, tm * N_SLAB, LANES), F32),
                        pltpu.SemaphoreType.DMA((2 * IDX_SLOTS + 2,))],
        compiler_params=_cparams(1),
        name="moe_combine",
    )(dest_steps, gate_steps, ys, x_tiles, g3, bb3)


def _moe(segs, counts, expert0, wgu, bgu, wdn, bdn, g, bb):
    t = sum(seg[1].shape[1] for seg in segs)
    n_tiles = -(-(t * TOP_K) // GMM_TILE) + N_EXPERTS
    counts = counts[:, 0]
    padded = ((counts + GMM_TILE - 1) // GMM_TILE) * GMM_TILE
    pend = jnp.cumsum(padded)
    pstart = (pend - padded).astype(jnp.int32)
    padded = padded.astype(jnp.int32)
    nact = (pend[-1] // GMM_TILE).astype(jnp.int32).reshape(1)
    tile = jnp.arange(n_tiles, dtype=jnp.int32)
    tsrc = jnp.minimum(tile, jnp.maximum(nact[0] - 1, 0))
    te = jnp.sum((pend[None, :] <= (tsrc * GMM_TILE)[:, None]).astype(jnp.int32), axis=1)
    te = jnp.minimum(te, N_EXPERTS - 1).astype(jnp.int32)
    tfirst = jnp.concatenate([jnp.ones((1,), jnp.int32), (te[1:] != te[:-1]).astype(jnp.int32)])
    eids = jnp.arange(N_EXPERTS, dtype=jnp.int32)[:, None, None]

    stride = -(-(TOP_K * max(seg[4] for seg in segs)) // 1024) * 1024
    def per_step(a, tm):
        n = a.shape[1]
        steps = a.reshape(TOP_K, n // tm, tm).transpose(1, 0, 2).reshape(n // tm, TOP_K * tm)
        return jnp.pad(steps, ((0, 0), (0, stride - TOP_K * tm))).reshape(-1)

    plans = []
    for x1, ids, gates, rank, tm in segs:
        dest = jnp.sum(jnp.where(ids[None] == eids, pstart[:, None, None], 0), axis=0) + rank
        plans.append((per_step(dest, tm), per_step(gates, tm)))
    xs = _dispatch([seg[0] for seg in segs], [seg[4] for seg in segs],
                   jnp.concatenate([p[0] for p in plans]), pstart, padded, nact, stride, n_tiles)
    ys = _gmm(xs, te + expert0, tsrc, tfirst, nact, wgu, bgu, wdn, bdn, n_tiles)
    return [_combine(ys, dest_steps, gate_steps, seg[0], g, bb, seg[4], stride)
            for seg, (dest_steps, gate_steps) in zip(segs, plans)]


def _pw1_glu_kernel(x_ref, w_ref, b_ref, u_ref):
    a = jnp.dot(x_ref[...].astype(BF16), w_ref[...], preferred_element_type=F32) + b_ref[...]
    u_ref[...] = a[:, :D_MODEL] * jax.nn.sigmoid(a[:, D_MODEL:])


def _pw1_glu(x, w_bf, b, tm):
    t = x.shape[0]
    row = lambda i: (i, 0)
    fixed = lambda i: (0, 0)
    return pl.pallas_call(
        _pw1_glu_kernel,
        grid=(t // tm,),
        in_specs=[pl.BlockSpec((tm, D_MODEL), row),
                  pl.BlockSpec((D_MODEL, 2 * D_MODEL), fixed),
                  pl.BlockSpec((1, 2 * D_MODEL), fixed)],
        out_specs=pl.BlockSpec((tm, D_MODEL), row),
        out_shape=jax.ShapeDtypeStruct((t, D_MODEL), F32),
        compiler_params=_cparams(1),
        name="pw1_glu",
    )(x, w_bf, b)


def _conv_rows(win_ref, zs_ref, base, n_sets, w_ref, b_ref):
    shift = CONV_HALO - CONV_CTX
    offs = tuple(range(n_sets))

    def out_rows(o):
        return pl.ds(base + o, SUBLANES, stride=n_sets)

    def slab(c, _):
        def window(t):
            return win_ref[c, pl.ds(base + (shift + t), SUBLANES, stride=n_sets), :]

        bias = b_ref[pl.ds(c, 1), :]
        acc = [bias] * n_sets
        wins = [window(t) for t in range(n_sets - 1)]
        for j in range(CONV_WIDTH):
            w = w_ref[c, j:j + 1, :]
            wins.append(window(j + n_sets - 1))
            acc = [acc[o] + w * wins[o] for o in offs]
            wins.pop(0)
        for o in offs:
            zs_ref[c, out_rows(o), :] = acc[o]
        return 0

    lax.fori_loop(0, N_SLAB, slab, 0)


def _ln_silu_rows(zs_ref, rows, g_ref, bb_ref):
    acc = [zs_ref[c, rows, :] for c in range(N_SLAB)]
    tot = acc[0]
    for c in range(1, N_SLAB):
        tot = tot + acc[c]
    mu = jnp.sum(tot, axis=1, keepdims=True) * (1.0 / D_MODEL)
    cen = [a - mu for a in acc]
    sq = cen[0] * cen[0]
    for c in range(1, N_SLAB):
        sq = sq + cen[c] * cen[c]
    inv = lax.rsqrt(jnp.sum(sq, axis=1, keepdims=True) * (1.0 / D_MODEL) + LN_EPS)
    out = []
    for c in range(N_SLAB):
        z = cen[c] * inv * g_ref[c:c + 1, :] + bb_ref[c:c + 1, :]
        out.append(z * jax.nn.sigmoid(z))
    return out


def _conv_prompt_kernel(um_ref, up_ref, uc_ref, w_ref, b_ref, g_ref, bb_ref, z_ref, win_ref, zs_ref, *, tr):
    i = pl.program_id(1)
    gap = CONV_HALO - META_LEN

    @pl.when(i == 0)
    def _():
        for c in range(N_SLAB):
            win_ref[c, 0:gap, :] = jnp.zeros((gap, LANES), F32)
            win_ref[c, gap:CONV_HALO, :] = um_ref[:, c * LANES:(c + 1) * LANES]

    @pl.when(i > 0)
    def _():
        for c in range(N_SLAB):
            win_ref[c, 0:CONV_HALO, :] = up_ref[:, c * LANES:(c + 1) * LANES]

    for c in range(N_SLAB):
        win_ref[c, CONV_HALO:, :] = uc_ref[:, c * LANES:(c + 1) * LANES]

    def block(bi, _):
        base = pl.multiple_of(bi * CONV_BLOCK, CONV_BLOCK)
        _conv_rows(win_ref, zs_ref, base, CONV_BLOCK // SUBLANES, w_ref, b_ref)
        return 0

    lax.fori_loop(0, tr // CONV_BLOCK, block, 0)

    def norm(bi, _):
        rows = pl.ds(pl.multiple_of(bi * LN_ROWS, LN_ROWS), LN_ROWS)
        for c, z in enumerate(_ln_silu_rows(zs_ref, rows, g_ref, bb_ref)):
            z_ref[rows, c * LANES:(c + 1) * LANES] = z.astype(z_ref.dtype)
        return 0

    lax.fori_loop(0, tr // LN_ROWS, norm, 0, unroll=8)


def _conv_prompt(u, u_meta, w3, b3, g3, bb3, n_batch, seq, tr):
    nt = seq // tr
    cur = lambda b, i: (b * nt + i, 0)
    prev = lambda b, i: (jnp.maximum((b * seq + i * tr) // CONV_HALO - 1, 0), 0)
    fixed2 = lambda b, i: (0, 0)
    fixed3 = lambda b, i: (0, 0, 0)
    return pl.pallas_call(
        functools.partial(_conv_prompt_kernel, tr=tr),
        grid=(n_batch, nt),
        in_specs=[pl.BlockSpec((META_LEN, D_MODEL), lambda b, i: (b, 0)),
                  pl.BlockSpec((CONV_HALO, D_MODEL), prev),
                  pl.BlockSpec((tr, D_MODEL), cur),
                  pl.BlockSpec((N_SLAB, CONV_HALO, LANES), fixed3),
                  pl.BlockSpec((N_SLAB, LANES), fixed2),
                  pl.BlockSpec((N_SLAB, LANES), fixed2),
                  pl.BlockSpec((N_SLAB, LANES), fixed2)],
        out_specs=pl.BlockSpec((tr, D_MODEL), cur),
        out_shape=jax.ShapeDtypeStruct((n_batch * seq, D_MODEL), BF16),
        scratch_shapes=[pltpu.VMEM((N_SLAB, CONV_HALO + tr, LANES), F32),
                        pltpu.VMEM((N_SLAB, tr, LANES), F32)],
        compiler_params=_cparams(2),
        name="conv_prompt",
    )(u_meta, u, u, w3, b3, g3, bb3)


SHORT_ROWS = 16


def _conv_short_kernel(ctx_ref, u_ref, w_ref, b_ref, g_ref, bb_ref, z_ref, win_ref, zs_ref, *, n_seq, t_len):
    for n in range(n_seq):
        for c in range(N_SLAB):
            sl = slice(c * LANES, (c + 1) * LANES)
            win_ref[c, 0:CONV_HALO, :] = ctx_ref[n, :, sl]
            win_ref[c, CONV_HALO:CONV_HALO + t_len, :] = u_ref[n * t_len:(n + 1) * t_len, sl]
            if t_len < SHORT_ROWS:
                win_ref[c, CONV_HALO + t_len:, :] = jnp.zeros((SHORT_ROWS - t_len, LANES), F32)
        _conv_rows(win_ref, zs_ref, 0, SHORT_ROWS // SUBLANES, w_ref, b_ref)
        for c, z in enumerate(_ln_silu_rows(zs_ref, slice(0, t_len), g_ref, bb_ref)):
            z_ref[n * t_len:(n + 1) * t_len, c * LANES:(c + 1) * LANES] = z


def _conv_short(ctx_pad, u, row0, w3, b3, g3, bb3, t_len, n_seq):
    n_total = ctx_pad.shape[0]
    rows = n_seq * t_len
    blk0 = row0 // rows
    fixed2 = lambda n: (0, 0)
    fixed3 = lambda n: (0, 0, 0)
    return pl.pallas_call(
        functools.partial(_conv_short_kernel, n_seq=n_seq, t_len=t_len),
        grid=(n_total // n_seq,),
        in_specs=[pl.BlockSpec((n_seq, CONV_HALO, D_MODEL), lambda n: (n, 0, 0)),
                  pl.BlockSpec((rows, D_MODEL), lambda n: (blk0 + n, 0)),
                  pl.BlockSpec((N_SLAB, CONV_HALO, LANES), fixed3),
                  pl.BlockSpec((N_SLAB, LANES), fixed2),
                  pl.BlockSpec((N_SLAB, LANES), fixed2),
                  pl.BlockSpec((N_SLAB, LANES), fixed2)],
        out_specs=pl.BlockSpec((rows, D_MODEL), lambda n: (n, 0)),
        out_shape=jax.ShapeDtypeStruct((n_total * t_len, D_MODEL), F32),
        scratch_shapes=[pltpu.VMEM((N_SLAB, CONV_HALO + SHORT_ROWS, LANES), F32),
                        pltpu.VMEM((N_SLAB, SHORT_ROWS, LANES), F32)],
        compiler_params=_cparams(1),
        name="conv_short",
    )(ctx_pad, u, w3, b3, g3, bb3)


def _rope_angles(pos):
    inv = 1.0 / (ROPE_THETA ** (jnp.arange(0, ROT_DIM, 2, dtype=F32) / ROT_DIM))
    ang = pos.astype(F32)[:, None] * inv[None, :]
    return jnp.cos(ang), jnp.sin(ang)


def _rope_tables_t(pos):
    cos, sin = _rope_angles(pos)
    return cos.T, sin.T


def _rope_tables(pos):
    half = ROT_DIM // 2
    cos, sin = _rope_angles(pos)
    n = pos.shape[0]
    ones = jnp.ones((n, HEAD_DIM - ROT_DIM), F32)
    zeros = jnp.zeros((n, HEAD_DIM - ROT_DIM), F32)
    zh = jnp.zeros((n, half), F32)
    c = jnp.concatenate([cos, cos, ones], axis=1)
    a = jnp.concatenate([-sin, zh, zeros], axis=1)
    s = jnp.concatenate([zh, sin, zeros], axis=1)
    rep = LANES // HEAD_DIM
    return jnp.tile(c, (1, rep)), jnp.tile(a, (1, rep)), jnp.tile(s, (1, rep))


def kernel(x_prompt, x_sample, cache_attn_meta_k, cache_attn_meta_v, cache_attn_win_k, cache_attn_win_v, state_conv, meta_tokens, attn_w_qkv, attn_b_qkv, attn_sinks, attn_w_o, attn_b_o, conv_w_pw1, conv_b_pw1, conv_w_dw, conv_b_dw, conv_ln_g, conv_ln_b, conv_w_pw2, conv_b_pw2, ln_mix_g, ln_mix_b, ln_ffn_g, ln_ffn_b, moe_w_router, moe_b_router, moe_w_gate_up, moe_b_gate_up, moe_w_down, moe_b_down):
    n_batch, seq, _ = x_prompt.shape
    dec_batch, dec_seq, _ = x_sample.shape
    n_real = n_batch * seq
    n_meta = n_batch * META_LEN
    n_samp = dec_batch * dec_seq
    n_small = n_meta + n_samp
    tb = _pick_tile(seq, 512)
    td = _pick_tile(seq, 1024)
    ts = _pick_tile(n_small, 512)
    row2 = lambda v: v.reshape(1, -1)

    meta_rows = jnp.broadcast_to(meta_tokens[None], (n_batch, META_LEN, D_MODEL)).reshape(n_meta, D_MODEL)
    xb = x_prompt.reshape(n_real, D_MODEL)
    xs = jnp.concatenate([meta_rows.astype(F32), x_sample.reshape(n_samp, D_MODEL)], axis=0)

    def moe_layer(i, a_big, a_small, w, b, xb, xs):
        w_bf = w.astype(BF16)
        wr = moe_w_router[i].T.astype(BF16)
        br = moe_b_router[i].reshape(N_EXPERTS, 1)
        lng, lnb = row2(ln_mix_g[i]), row2(ln_mix_b[i])
        zero_counts = jnp.zeros((N_EXPERTS, LANES), jnp.int32)
        x1b, idb, gab, rab, cnt = _proj_ln_route(a_big, w_bf, row2(b), xb, lng, lnb, wr, br, zero_counts, td)
        x1s, ids_, gas, ras, cnt = _proj_ln_route(a_small, w_bf, row2(b), xs, lng, lnb, wr, br, cnt, ts)
        n_all = moe_w_gate_up.shape[0] * N_EXPERTS
        return _moe([(x1b, idb, gab, rab, tb), (x1s, ids_, gas, ras, ts)], cnt, i * N_EXPERTS,
                    moe_w_gate_up.reshape(n_all, D_MODEL, 2 * MOE_FF),
                    moe_b_gate_up.reshape(n_all, 1, 2 * MOE_FF),
                    moe_w_down.reshape(n_all, MOE_FF, D_MODEL),
                    moe_b_down.reshape(n_all, 1, D_MODEL),
                    ln_ffn_g[i].reshape(N_SLAB, LANES), ln_ffn_b[i].reshape(N_SLAB, LANES))

    w_qkv = attn_w_qkv[0].astype(BF16)
    b_qkv = row2(attn_b_qkv[0])
    pos_small = jnp.concatenate([jnp.tile(jnp.arange(META_LEN), n_batch),
                                 jnp.tile(PAST_LEN + jnp.arange(dec_seq), dec_batch)])
    pos_big = META_LEN + jnp.arange(seq)
    qtb, kb, vb, vtb = _qkv_rope_t(xb, w_qkv, b_qkv, *_rope_tables(pos_big), *_rope_tables_t(pos_big), td)
    qs, ks, vs = _qkv_rope(xs, w_qkv, b_qkv, *_rope_tables(pos_small), ts)
    sinks = attn_sinks[0]
    cmk = cache_attn_meta_k[0].reshape(dec_batch, META_LEN, KV_DIM)
    cmv = cache_attn_meta_v[0].reshape(dec_batch, META_LEN, KV_DIM)
    cwk = cache_attn_win_k[0].reshape(dec_batch, WINDOW, KV_DIM)
    cwv = cache_attn_win_v[0].reshape(dec_batch, WINDOW, KV_DIM)
    vt_meta = vs[:n_meta].reshape(n_batch, META_LEN, KV_DIM).transpose(0, 2, 1).astype(BF16)
    o_big = _attn_prompt(qtb, kb, vtb, ks, vt_meta, sinks, n_batch, seq)
    o_small = jnp.concatenate([
        _attn_meta(qs, ks, vs, sinks, n_batch),
        _attn_sample(qs, ks, vs, cmk, cmv, cwk, cwv, sinks, n_meta, dec_batch, dec_seq)], axis=0)
    xb, xs = moe_layer(0, o_big, o_small, attn_w_o[0], attn_b_o[0], xb, xs)

    kv4 = lambda a, n, t_len: a.reshape(n, t_len, N_KV_HEADS, HEAD_DIM)
    p_meta_k = kv4(ks[:n_meta], n_batch, META_LEN)[None]
    p_meta_v = kv4(vs[:n_meta], n_batch, META_LEN)[None]
    last_win = lambda a: a.reshape(n_batch, seq, KV_DIM)[:, seq - WINDOW:].reshape(n_batch * WINDOW, KV_DIM)
    p_win_k = kv4(last_win(kb), n_batch, WINDOW)[None]
    p_win_v = kv4(last_win(vb), n_batch, WINDOW)[None]
    k_new = kv4(ks[n_meta:], dec_batch, dec_seq)
    v_new = kv4(vs[n_meta:], dec_batch, dec_seq)
    s_win_k = jnp.concatenate([cache_attn_win_k[0], k_new], axis=1)[:, -WINDOW:][None]
    s_win_v = jnp.concatenate([cache_attn_win_v[0], v_new], axis=1)[:, -WINDOW:][None]

    w_pw1 = conv_w_pw1[0].astype(BF16)
    ub = _pw1_glu(xb, w_pw1, row2(conv_b_pw1[0]), td)
    us = _pw1_glu(xs, w_pw1, row2(conv_b_pw1[0]), ts)
    slab = lambda v: v.reshape(N_SLAB, LANES)
    w_dw = jnp.pad(conv_w_dw[0], ((0, CONV_HALO - CONV_WIDTH), (0, 0)))
    w3 = w_dw.reshape(CONV_HALO, N_SLAB, LANES).transpose(1, 0, 2)
    conv_args = (w3, slab(conv_b_dw[0]), slab(conv_ln_g[0]), slab(conv_ln_b[0]))
    z_big = _conv_prompt(ub, us, *conv_args, n_batch, seq, td)
    z_meta = _conv_short(jnp.zeros((n_batch, CONV_HALO, D_MODEL), F32), us, 0, *conv_args, META_LEN, 1)
    ctx_pad = jnp.pad(state_conv[0], ((0, 0), (CONV_HALO - CONV_CTX, 0), (0, 0)))
    z_samp = _conv_short(ctx_pad, us, n_meta, *conv_args, dec_seq, 2)
    z_small = jnp.concatenate([z_meta, z_samp], axis=0).astype(BF16)
    xb, xs = moe_layer(1, z_big, z_small, conv_w_pw2[0], conv_b_pw2[0], xb, xs)

    p_conv = ub.reshape(n_batch, seq, D_MODEL)[:, seq - CONV_CTX:][None]
    u_samp = us[n_meta:].reshape(dec_batch, dec_seq, D_MODEL)
    s_conv = jnp.concatenate([state_conv[0], u_samp], axis=1)[:, -CONV_CTX:][None]

    y_prompt = xb.reshape(n_batch, seq, D_MODEL)
    y_sample = xs[n_meta:].reshape(dec_batch, dec_seq, D_MODEL)
    return (y_prompt, y_sample, p_meta_k, p_meta_v, p_win_k, p_win_v, p_conv,
            s_win_k, s_win_v, s_conv)
```
